```python
import math
import jax, jax.numpy as jnp
from jax import lax
import numpy as np

D_MODEL = 1024
BATCH = 32
SEQ = 2048
DEPTH = 2
DEC_BATCH = 8
DEC_SEQ = 8192
PAST_LEN = 128

RWKV_HEADS = 8
RWKV_HEAD_DIM = 64
RWKV_WIDTH = RWKV_HEADS * RWKV_HEAD_DIM
RWKV_LORA_W = 64
RWKV_LORA_A = 64
RWKV_DECAY_SCALE = 0.6065306597126334
RWKV_GN_EPS = 64e-5
ATTN_HEADS = 8
ATTN_HEAD_DIM = 64
ATTN_WIDTH = ATTN_HEADS * ATTN_HEAD_DIM
DILATED_PATTERNS = ((128, 1), (512, 4), (2048, 16))
ROPE_THETA = 10000.0
NEG_INF = -1e30
EVEN_SHIFT = 3 * RWKV_WIDTH + RWKV_LORA_W + RWKV_LORA_A
EVEN_COLS = EVEN_SHIFT + RWKV_WIDTH + 4 * ATTN_WIDTH
EVEN_MIX = RWKV_WIDTH + ATTN_WIDTH

GLA_HEADS = 4
GLA_KEY_DIM = D_MODEL // 2
GLA_VAL_DIM = D_MODEL
GLA_HK = GLA_KEY_DIM // GLA_HEADS
GLA_HV = GLA_VAL_DIM // GLA_HEADS
GLA_GATE_RANK = 16
GLA_GATE_NORM = 16.0
GLA_CHUNK = 64
ODD_COLS = 2 * GLA_KEY_DIM + GLA_VAL_DIM + GLA_GATE_RANK + GLA_VAL_DIM

RMS_EPS = 1e-6
N_EVEN = (DEPTH + 1) // 2
N_ODD = DEPTH // 2

kernel_name = "hybrid_bidir_rwkv7_dilated_gla_encoder"


def _split(t, sizes):
    out, off = [], 0
    for sz in sizes:
        out.append(t[..., off:off + sz])
        off += sz
    return out


def _rmsnorm(x, w):
    xf = x.astype(jnp.float32)
    y = xf * lax.rsqrt(jnp.mean(xf * xf, axis=-1, keepdims=True) + RMS_EPS)
    return (y * w.astype(jnp.float32)).astype(x.dtype)


def _centred_shift(p, mu_prev, mu_next):
    prev = jnp.pad(p[:, :-1], ((0, 0), (1, 0), (0, 0)))
    nxt = jnp.pad(p[:, 1:], ((0, 0), (0, 1), (0, 0)))
    return p + mu_prev * (prev - p) + mu_next * (nxt - p)


def _rope(x):
    s, dh = x.shape[1], x.shape[-1]
    inv = ROPE_THETA ** (-jnp.arange(0, dh, 2, dtype=jnp.float32) / dh)
    ang = jnp.arange(s, dtype=jnp.float32)[:, None] * inv[None, :]
    cos = jnp.cos(ang)[None, :, None, :]
    sin = jnp.sin(ang)[None, :, None, :]
    x1, x2 = x[..., : dh // 2], x[..., dh // 2:]
    return jnp.concatenate([x1 * cos - x2 * sin, x1 * sin + x2 * cos], axis=-1)


def _rwkv7_step(state, inp):
    r_t, decay_t, k_t, v_t, kk_t, a_t = inp
    sa = jnp.einsum('nhvk,nhk->nhv', state, -kk_t)
    state = (state * decay_t[:, :, None, :]
             + sa[..., None] * (kk_t * a_t)[:, :, None, :]
             + v_t[..., None] * k_t[:, :, None, :])
    y = jnp.einsum('nhvk,nhk->nhv', state, r_t)
    return state, y


def _rwkv7(r, k, v, w_lat, a_lat, w0, w_up, a0, a_up, k_k, k_a, r_k, gn_w, gn_b):
    bsz, s, _ = r.shape
    H, Dh = RWKV_HEADS, RWKV_HEAD_DIM
    f32 = jnp.float32
    r, k, v, w_lat, a_lat = (t.astype(f32) for t in (r, k, v, w_lat, a_lat))
    w_log = -RWKV_DECAY_SCALE * jax.nn.sigmoid(
        w0[:, None, None, :] + jnp.einsum('bsr,zrc->zbsc', jnp.tanh(w_lat), w_up))
    a = jax.nn.sigmoid(a0[:, None, None, :] + jnp.einsum('bsr,zrc->zbsc', a_lat, a_up))
    kk = (k * k_k).reshape(bsz, s, H, Dh)
    kk = kk / jnp.maximum(jnp.sqrt(jnp.sum(kk * kk, axis=-1, keepdims=True)), 1e-12)
    kk = kk.reshape(bsz, s, RWKV_WIDTH)
    k_dir = k[None] * (1.0 + (a - 1.0) * k_a)
    rk = jnp.sum((r[None] * k_dir).reshape(2, bsz, s, H, Dh) * r_k.reshape(H, Dh), axis=(0, -1))
    bonus = rk[..., None] * v.reshape(bsz, s, H, Dh)

    both = lambda t: jnp.stack([t, jnp.flip(t, 1)])
    rev = lambda t: jnp.stack([t[0], jnp.flip(t[1], 1)])
    to_scan = lambda t: t.reshape(2 * bsz, s, H, Dh).transpose(1, 0, 2, 3)
    xs = tuple(to_scan(t) for t in (both(r), rev(jnp.exp(w_log)), rev(k_dir),
                                     both(v), both(kk), rev(a)))
    state0 = jnp.zeros((2 * bsz, H, Dh, Dh), f32)
    _, ys = lax.scan(_rwkv7_step, state0, xs)
    ys = ys.transpose(1, 0, 2, 3).reshape(2, bsz, s, H, Dh)
    y = ys[0] + jnp.flip(ys[1], 1)
    mean = jnp.mean(y, axis=-1, keepdims=True)
    var = jnp.mean(jnp.square(y - mean), axis=-1, keepdims=True)
    y = (y - mean) * lax.rsqrt(var + RWKV_GN_EPS) * gn_w.reshape(H, Dh) + gn_b.reshape(H, Dh)
    return (y + bonus).reshape(bsz, s, RWKV_WIDTH)


def _banded_attention(q, k, v, n_side):
    n, length, h, dh = q.shape
    blk = n_side
    nb = -(-length // blk)
    lp = nb * blk
    qb = jnp.pad(q, ((0, 0), (0, lp - length), (0, 0), (0, 0))).reshape(n, nb, blk, h, dh)
    kv_pad = ((0, 0), (blk, lp - length + blk), (0, 0), (0, 0))
    kp, vp = jnp.pad(k, kv_pad), jnp.pad(v, kv_pad)

    def neighbourhood(t):
        return jnp.concatenate(
            [t[:, o * blk:o * blk + lp].reshape(n, nb, blk, h, dh) for o in range(3)], axis=2)

    kb, vb = neighbourhood(kp), neighbourhood(vp)
    qi = jnp.arange(blk)[:, None]
    kj = jnp.arange(3 * blk)[None, :]
    kpos = jnp.arange(nb)[:, None, None] * blk - blk + kj[None]
    mask = (jnp.abs(kj - blk - qi)[None] <= n_side) & (kpos >= 0) & (kpos < length)
    scores = jnp.einsum('nbqhd,nbkhd->nbhqk', qb, kb) * (dh ** -0.5)
    scores = jnp.where(mask[None, :, None], scores, NEG_INF)
    lse = jax.nn.logsumexp(scores, axis=-1)
    probs = jnp.exp(scores - lse[..., None])
    out = jnp.einsum('nbhqk,nbkhd->nbqhd', probs, vb).reshape(n, lp, h, dh)[:, :length]
    lse = lse.transpose(0, 1, 3, 2).reshape(n, lp, h)[:, :length]
    return out, lse


def _dilated_attention(q, k, v):
    b, s, h, dh = q.shape
    outs, lses = [], []
    for window, dil in DILATED_PATTERNS:
        n_side = window // (2 * dil)
        sub = s // dil
        to_sub = lambda t: t.reshape(b, sub, dil, h, dh).transpose(0, 2, 1, 3, 4).reshape(b * dil, sub, h, dh)
        o, lse = _banded_attention(to_sub(q), to_sub(k), to_sub(v), n_side)
        outs.append(o.reshape(b, dil, sub, h, dh).transpose(0, 2, 1, 3, 4).reshape(b, s, h, dh))
        lses.append(lse.reshape(b, dil, sub, h).transpose(0, 2, 1, 3).reshape(b, s, h))
    weights = jax.nn.softmax(jnp.stack(lses), axis=0)
    return jnp.sum(weights[..., None] * jnp.stack(outs), axis=0)


def _gla_chunked(q, k, v, g):
    n, s, h, dk = q.shape
    dv = v.shape[-1]
    c = GLA_CHUNK
    nc = s // c
    rs = lambda t: t.reshape(n, nc, c, h, t.shape[-1]).transpose(1, 0, 3, 2, 4)
    qc, kc, vc, gc = rs(q), rs(k), rs(v), rs(g)
    bcum = jnp.cumsum(gc, axis=3)
    q_dec = qc * jnp.exp(bcum)
    k_dec = kc * jnp.exp(-bcum)
    causal = jnp.tril(jnp.ones((c, c), dtype=bool))
    att = jnp.where(causal, jnp.einsum('znhtk,znhsk->znhts', q_dec, k_dec), 0.0)
    o_intra = jnp.einsum('znhts,znhsv->znhtv', att, vc)
    b_last = bcum[..., -1:, :]
    k_end = kc * jnp.exp(b_last - bcum)
    chunk_decay = jnp.exp(b_last[..., 0, :])

    def step(state, inp):
        q_c, k_c, v_c, d_c = inp
        o_inter = jnp.einsum('nhtk,nhkv->nhtv', q_c, state)
        state = state * d_c[..., None] + jnp.einsum('nhsk,nhsv->nhkv', k_c, v_c)
        return state, o_inter

    state0 = jnp.zeros((n, h, dk, dv), jnp.float32)
    _, o_inter = lax.scan(step, state0, (q_dec, k_end, vc, chunk_decay))
    o = o_intra + o_inter
    return o.transpose(1, 0, 3, 2, 4).reshape(n, s, h, dv)


def _gla(q, k, v, gate_lat, gate_up, gate_bias, norm_w):
    bsz, s, _ = q.shape
    H = GLA_HEADS
    f32 = jnp.float32
    q = q.astype(f32).reshape(bsz, s, H, GLA_HK) * (GLA_HK ** -0.5)
    k = k.astype(f32).reshape(bsz, s, H, GLA_HK)
    v = v.astype(f32).reshape(bsz, s, H, GLA_HV)
    g = jax.nn.log_sigmoid(jnp.einsum('bsr,zrc->zbsc', gate_lat.astype(f32), gate_up)
                           + gate_bias[:, None, None, :]) / GLA_GATE_NORM
    g = g.reshape(2, bsz, s, H, GLA_HK)
    both = lambda t: jnp.stack([t, jnp.flip(t, 1)]).reshape(2 * bsz, s, H, t.shape[-1])
    g_dir = jnp.stack([g[0], jnp.flip(g[1], 1)]).reshape(2 * bsz, s, H, GLA_HK)
    o = _gla_chunked(both(q), both(k), both(v), g_dir).reshape(2, bsz, s, H, GLA_HV)
    o = o[0] + jnp.flip(o[1], 1)
    o = o * lax.rsqrt(jnp.mean(o * o, axis=-1, keepdims=True) + RMS_EPS) * norm_w
    return o.reshape(bsz, s, GLA_VAL_DIM)


def _even_layer(x, norm_w, w_in, mu_prev, mu_next, w0, w_up, a0, a_up, k_k, k_a, r_k, gn_w, gn_b, w_out):
    bsz, s, _ = x.shape
    f32 = jnp.float32
    p = _rmsnorm(x, norm_w) @ w_in
    shifted = _centred_shift(p[..., :EVEN_SHIFT], mu_prev, mu_next)
    r, k, v, w_lat, a_lat = _split(shifted, (RWKV_WIDTH, RWKV_WIDTH, RWKV_WIDTH, RWKV_LORA_W, RWKV_LORA_A))
    gate_a, q_b, k_b, v_b, gate_b = _split(p[..., EVEN_SHIFT:], (RWKV_WIDTH,) + (ATTN_WIDTH,) * 4)
    y_a = _rwkv7(r, k, v, w_lat, a_lat, w0, w_up, a0, a_up, k_k, k_a, r_k, gn_w, gn_b)
    heads = lambda t: t.astype(f32).reshape(bsz, s, ATTN_HEADS, ATTN_HEAD_DIM)
    y_b = _dilated_attention(_rope(heads(q_b)), _rope(heads(k_b)), heads(v_b)).reshape(bsz, s, ATTN_WIDTH)
    y = jnp.concatenate([y_a * jax.nn.silu(gate_a.astype(f32)),
                         y_b * jax.nn.silu(gate_b.astype(f32))], axis=-1)
    return x + y.astype(x.dtype) @ w_out


def _odd_layer(x, norm_w, w_in, gate_up, gate_bias, gnorm_w, w_out):
    p = _rmsnorm(x, norm_w) @ w_in
    q, k, v, gate_lat, gate = _split(p, (GLA_KEY_DIM, GLA_KEY_DIM, GLA_VAL_DIM, GLA_GATE_RANK, GLA_VAL_DIM))
    y = _gla(q, k, v, gate_lat, gate_up, gate_bias, gnorm_w) * jax.nn.silu(gate.astype(jnp.float32))
    return x + y.astype(x.dtype) @ w_out


def _trunk(x, even_norm, even_w_in, even_mu_prev, even_mu_next, rwkv_w0, rwkv_w_up, rwkv_a0, rwkv_a_up,
           rwkv_k_k, rwkv_k_a, rwkv_r_k, rwkv_gn_w, rwkv_gn_b, even_w_out,
           odd_norm, odd_w_in, gla_gate_up, gla_gate_bias, gla_norm, odd_w_out, final_norm):
    for layer in range(DEPTH):
        i = layer // 2
        if layer % 2 == 0:
            x = _even_layer(x, even_norm[i], even_w_in[i], even_mu_prev[i], even_mu_next[i],
                            rwkv_w0[i], rwkv_w_up[i], rwkv_a0[i], rwkv_a_up[i], rwkv_k_k[i],
                            rwkv_k_a[i], rwkv_r_k[i], rwkv_gn_w[i], rwkv_gn_b[i], even_w_out[i])
        else:
            x = _odd_layer(x, odd_norm[i], odd_w_in[i], gla_gate_up[i], gla_gate_bias[i],
                           gla_norm[i], odd_w_out[i])
    return _rmsnorm(x, final_norm)


def setup_inputs(seed: int = 0) -> dict:
    key = jax.random.key(seed)
    ks = iter(jax.random.split(key, 32))
    nrm = lambda shape, scale: scale * jax.random.normal(next(ks), shape, jnp.float32)
    unif = lambda shape, lo, hi: jax.random.uniform(next(ks), shape, jnp.float32, lo, hi)
    E, O = N_EVEN, N_ODD
    return {
        "x_prompt": nrm((BATCH, SEQ, D_MODEL), 1.0),
        "x_sample": nrm((DEC_BATCH, DEC_SEQ, D_MODEL), 1.0),
        "even_norm": 1.0 + nrm((E, D_MODEL), 0.02),
        "even_w_in": nrm((E, D_MODEL, EVEN_COLS), D_MODEL ** -0.5),
        "even_mu_prev": unif((E, EVEN_SHIFT), 0.0, 0.5),
        "even_mu_next": unif((E, EVEN_SHIFT), 0.0, 0.5),
        "rwkv_w0": nrm((E, 2, RWKV_WIDTH), 0.5),
        "rwkv_w_up": nrm((E, 2, RWKV_LORA_W, RWKV_WIDTH), 0.5 * RWKV_LORA_W ** -0.5),
        "rwkv_a0": nrm((E, 2, RWKV_WIDTH), 0.5),
        "rwkv_a_up": nrm((E, 2, RWKV_LORA_A, RWKV_WIDTH), 0.5 * RWKV_LORA_A ** -0.5),
        "rwkv_k_k": 0.85 + nrm((E, RWKV_WIDTH), 0.05),
        "rwkv_k_a": 1.0 + nrm((E, RWKV_WIDTH), 0.05),
        "rwkv_r_k": nrm((E, RWKV_WIDTH), 0.1),
        "rwkv_gn_w": 1.0 + nrm((E, RWKV_WIDTH), 0.02),
        "rwkv_gn_b": nrm((E, RWKV_WIDTH), 0.02),
        "even_w_out": nrm((E, EVEN_MIX, D_MODEL), EVEN_MIX ** -0.5),
        "odd_norm": 1.0 + nrm((O, D_MODEL), 0.02),
        "odd_w_in": nrm((O, D_MODEL, ODD_COLS), D_MODEL ** -0.5),
        "gla_gate_up": nrm((O, 2, GLA_GATE_RANK, GLA_KEY_DIM), GLA_GATE_RANK ** -0.5),
        "gla_gate_bias": nrm((O, 2, GLA_KEY_DIM), 0.5),
        "gla_norm": 1.0 + nrm((O, GLA_HV), 0.02),
        "odd_w_out": nrm((O, GLA_VAL_DIM, D_MODEL), GLA_VAL_DIM ** -0.5),
        "final_norm": 1.0 + nrm((D_MODEL,), 0.02),
    }


def reference(x_prompt, x_sample, even_norm, even_w_in, even_mu_prev, even_mu_next, rwkv_w0, rwkv_w_up,
              rwkv_a0, rwkv_a_up, rwkv_k_k, rwkv_k_a, rwkv_r_k, rwkv_gn_w, rwkv_gn_b, even_w_out,
              odd_norm, odd_w_in, gla_gate_up, gla_gate_bias, gla_norm, odd_w_out, final_norm):
    y_prompt = _trunk(x_prompt, even_norm, even_w_in, even_mu_prev, even_mu_next, rwkv_w0, rwkv_w_up,
                      rwkv_a0, rwkv_a_up, rwkv_k_k, rwkv_k_a, rwkv_r_k, rwkv_gn_w, rwkv_gn_b, even_w_out,
                      odd_norm, odd_w_in, gla_gate_up, gla_gate_bias, gla_norm, odd_w_out, final_norm)
    y_sample = _trunk(x_sample, even_norm, even_w_in, even_mu_prev, even_mu_next, rwkv_w0, rwkv_w_up,
                      rwkv_a0, rwkv_a_up, rwkv_k_k, rwkv_k_a, rwkv_r_k, rwkv_gn_w, rwkv_gn_b, even_w_out,
                      odd_norm, odd_w_in, gla_gate_up, gla_gate_bias, gla_norm, odd_w_out, final_norm)
    return (y_prompt, y_sample)
```

```python
import functools
import math

import jax
import jax.numpy as jnp
from jax import lax
from jax.experimental import pallas as pl
from jax.experimental.pallas import tpu as pltpu

f32 = jnp.float32
bf16 = jnp.bfloat16

D_MODEL = 1024
RMS_EPS = 1e-6

RW_HEADS = 8
RW_DH = 64
RW_WIDTH = RW_HEADS * RW_DH
RW_LORA = 64
RW_DECAY_SCALE = 0.6065306597126334
RW_GN_EPS = 64e-5
AT_HEADS = 8
AT_DH = 64
AT_WIDTH = AT_HEADS * AT_DH
AT_SIDE = 64
AT_DILATIONS = (1, 4, 16)
ROPE_THETA = 10000.0
EVEN_SHIFT = 3 * RW_WIDTH + 2 * RW_LORA
EVEN_COLS = EVEN_SHIFT + RW_WIDTH + 4 * AT_WIDTH

GLA_HEADS = 4
GLA_KEY = 512
GLA_VAL = 1024
GLA_HK = GLA_KEY // GLA_HEADS
GLA_HV = GLA_VAL // GLA_HEADS
GLA_RANK = 16
GLA_GATE_NORM = 16.0
ODD_COLS = 2 * GLA_KEY + GLA_VAL + GLA_RANK + GLA_VAL

CHUNK = 64
LANES = 128
VMEM_LIMIT = 56 * 1024 * 1024

_NT = (((1,), (1,)), ((), ()))
_TN = (((0,), (0,)), ((), ()))


def _dot(a, b):
    return jnp.dot(a, b, preferred_element_type=f32)


def _dot_nt(a, b):
    return lax.dot_general(a, b, _NT, preferred_element_type=f32)


def _dot_tn(a, b):
    return lax.dot_general(a, b, _TN, preferred_element_type=f32)


def _split2(x):
    hi = x.astype(bf16)
    lo = (x - hi.astype(f32)).astype(bf16)
    return hi, lo


def _split3(x):
    hi = x.astype(bf16)
    r = x - hi.astype(f32)
    mid = r.astype(bf16)
    lo = (r - mid.astype(f32)).astype(bf16)
    return hi, mid, lo


def _dot_exact_rhs(x, e):
    hi, mid, lo = _split3(x)
    return _dot(hi, e) + _dot(mid, e) + _dot(lo, e)


def _dot_exact_lhs(e, x):
    hi, mid, lo = _split3(x)
    return _dot(e, hi) + _dot(e, mid) + _dot(e, lo)


def _dot3(a, b):
    ah, al = _split2(a)
    bh, bl = _split2(b)
    return _dot(ah, bh) + (_dot(ah, bl) + _dot(al, bh))


def _sigmoid(x):
    return 1.0 / (1.0 + jnp.exp(-x))


def _silu(x):
    return x * _sigmoid(x)


def _rms_rows(x, w):
    return x * lax.rsqrt(jnp.mean(x * x, axis=-1, keepdims=True) + RMS_EPS) * w


def _cparams(*sem):
    return pltpu.CompilerParams(dimension_semantics=sem, vmem_limit_bytes=VMEM_LIMIT)


def _rope_partner(x):
    lane = lax.broadcasted_iota(jnp.int32, x.shape, 1)
    return jnp.where((lane & 32) == 0, pltpu.roll(x, LANES - 32, 1), pltpu.roll(x, 32, 1))


def _even_in_kernel(x_ref, nw_ref, w_ref, cos_ref, sin_ref, ps_ref, ga_ref, qkv_ref, gb_ref):
    xn = _rms_rows(x_ref[...], nw_ref[...]).astype(bf16)

    def cols(c0, c1):
        return _dot(xn, w_ref[:, c0:c1])

    for c0 in range(0, EVEN_SHIFT, 512):
        c1 = min(c0 + 512, EVEN_SHIFT)
        ps_ref[:, c0:c1] = cols(c0, c1)
    base = EVEN_SHIFT
    ga_ref[...] = cols(base, base + RW_WIDTH)
    base += RW_WIDTH
    cos = cos_ref[...]
    sin = sin_ref[...]
    for j in range(2 * AT_WIDTH // LANES):
        t = cols(base + j * LANES, base + (j + 1) * LANES)
        qkv_ref[:, j * LANES:(j + 1) * LANES] = t * cos + _rope_partner(t) * sin
    base += 2 * AT_WIDTH
    qkv_ref[:, 2 * AT_WIDTH:3 * AT_WIDTH] = cols(base, base + AT_WIDTH)
    base += AT_WIDTH
    gb_ref[...] = cols(base, base + AT_WIDTH)


def _rope_tables(seq):
    inv = ROPE_THETA ** (-jnp.arange(0, AT_DH, 2, dtype=f32) / AT_DH)
    ang = jnp.arange(seq, dtype=f32)[:, None] * inv[None, :]
    cos, sin = jnp.cos(ang), jnp.sin(ang)
    return (jnp.concatenate([cos, cos, cos, cos], axis=-1),
            jnp.concatenate([-sin, sin, -sin, sin], axis=-1))


def _even_in(x2d, seq, norm_w, w_in_bf16, block_rows=512):
    m = x2d.shape[0]
    tm = block_rows
    per_seq = seq // tm
    cos, sin = _rope_tables(seq)
    row = lambda i: (i, 0)
    full = lambda i: (0, 0)
    tab = lambda i: (i % per_seq, 0)
    return pl.pallas_call(
        _even_in_kernel,
        grid=(m // tm,),
        in_specs=[
            pl.BlockSpec((tm, D_MODEL), row),
            pl.BlockSpec((1, D_MODEL), full),
            pl.BlockSpec((D_MODEL, EVEN_COLS), full),
            pl.BlockSpec((tm, LANES), tab),
            pl.BlockSpec((tm, LANES), tab),
        ],
        out_specs=[
            pl.BlockSpec((tm, EVEN_SHIFT), row),
            pl.BlockSpec((tm, RW_WIDTH), row),
            pl.BlockSpec((tm, 3 * AT_WIDTH), row),
            pl.BlockSpec((tm, AT_WIDTH), row),
        ],
        out_shape=[
            jax.ShapeDtypeStruct((m, EVEN_SHIFT), f32),
            jax.ShapeDtypeStruct((m, RW_WIDTH), f32),
            jax.ShapeDtypeStruct((m, 3 * AT_WIDTH), f32),
            jax.ShapeDtypeStruct((m, AT_WIDTH), f32),
        ],
        compiler_params=_cparams("parallel"),
        name="even_in",
    )(x2d, norm_w.reshape(1, D_MODEL), w_in_bf16, cos, sin)


ODD_PAD_COLS = 2 * GLA_KEY + 2 * GLA_VAL + LANES


def _mid_kernel(x_ref, ya_ref, yb_ref, wo_ref, nw_ref, wi_ref,
                x1_ref, q_ref, k_ref, v_ref, g_ref, gl_ref):
    x1 = (x_ref[...] + _dot(ya_ref[...], wo_ref[0:RW_WIDTH, :])
          + _dot(yb_ref[...], wo_ref[RW_WIDTH:RW_WIDTH + AT_WIDTH, :]))
    x1_ref[...] = x1
    xn = _rms_rows(x1, nw_ref[...]).astype(bf16)
    c = 0
    for ref, width in ((q_ref, GLA_KEY), (k_ref, GLA_KEY), (v_ref, GLA_VAL), (g_ref, GLA_VAL),
                       (gl_ref, LANES)):
        for c0 in range(0, width, 512):
            c1 = min(c0 + 512, width)
            ref[:, c0:c1] = _dot(xn, wi_ref[:, c + c0:c + c1])
        c += width


def _mid(x2d, ya, yb, w_out_bf16, norm_w, w_in_pad_bf16, block_rows=512):
    m = x2d.shape[0]
    tm = block_rows
    row = lambda i: (i, 0)
    full = lambda i: (0, 0)
    widths = (D_MODEL, GLA_KEY, GLA_KEY, GLA_VAL, GLA_VAL, LANES)
    return pl.pallas_call(
        _mid_kernel,
        grid=(m // tm,),
        in_specs=[
            pl.BlockSpec((tm, D_MODEL), row),
            pl.BlockSpec((tm, RW_WIDTH), row),
            pl.BlockSpec((tm, AT_WIDTH), row),
            pl.BlockSpec((RW_WIDTH + AT_WIDTH, D_MODEL), full),
            pl.BlockSpec((1, D_MODEL), full),
            pl.BlockSpec((D_MODEL, ODD_PAD_COLS), full),
        ],
        out_specs=[pl.BlockSpec((tm, w), row) for w in widths],
        out_shape=[jax.ShapeDtypeStruct((m, w), f32) for w in widths],
        compiler_params=_cparams("parallel"),
        name="mid",
    )(x2d, ya, yb, w_out_bf16, norm_w.reshape(1, D_MODEL), w_in_pad_bf16)


def _final_kernel(x_ref, y_ref, wo_ref, nw_ref, o_ref):
    x2 = x_ref[...] + _dot(y_ref[...], wo_ref[...])
    o_ref[...] = _rms_rows(x2, nw_ref[...])


def _final(x2d, yc, w_out_bf16, norm_w, block_rows=512):
    m = x2d.shape[0]
    tm = block_rows
    row = lambda i: (i, 0)
    full = lambda i: (0, 0)
    return pl.pallas_call(
        _final_kernel,
        grid=(m // tm,),
        in_specs=[
            pl.BlockSpec((tm, D_MODEL), row),
            pl.BlockSpec((tm, GLA_VAL), row),
            pl.BlockSpec((GLA_VAL, D_MODEL), full),
            pl.BlockSpec((1, D_MODEL), full),
        ],
        out_specs=pl.BlockSpec((tm, D_MODEL), row),
        out_shape=jax.ShapeDtypeStruct((m, D_MODEL), f32),
        compiler_params=_cparams("parallel"),
        name="final",
    )(x2d, yc, w_out_bf16, norm_w.reshape(1, D_MODEL))


def _scan_masks(n, reverse):
    t = lax.broadcasted_iota(jnp.int32, (n, n), 0) % CHUNK
    s = lax.broadcasted_iota(jnp.int32, (n, n), 1) % CHUNK
    if reverse:
        return s >= t, s > t
    return s <= t, s < t


def _row_to_col(row):
    n = row.shape[1]
    eye = (lax.broadcasted_iota(jnp.int32, (n, n), 0) == lax.broadcasted_iota(jnp.int32, (n, n), 1))
    return jnp.sum(jnp.where(eye, jnp.broadcast_to(row, (n, n)), 0.0), axis=1, keepdims=True)


def _gla_kernel(*refs, reverse, final, block_rows):
    if final:
        (q_ref, k_ref, v_ref, gl_ref, gup_ref, gb_ref, gate_ref, of_ref, nw_ref, o_ref, st_ref) = refs
    else:
        (q_ref, k_ref, v_ref, gl_ref, gup_ref, gb_ref, o_ref, st_ref) = refs
    d = 1 if reverse else 0

    @pl.when(pl.program_id(1) == 0)
    def _():
        st_ref[...] = jnp.zeros_like(st_ref)

    incl, _ = _scan_masks(CHUNK, reverse)
    tri = incl.astype(bf16)
    gup = gup_ref[d]
    gbias = gb_ref[d:d + 1, :]
    nchunks = block_rows // CHUNK
    last = 0 if reverse else CHUNK - 1

    def chunk(ci, carry):
        c = (nchunks - 1 - ci) if reverse else ci
        rows = pl.ds(pl.multiple_of(c * CHUNK, CHUNK), CHUNK)
        x = _dot3(gl_ref[rows, :], gup) + gbias
        g = (jnp.minimum(x, 0.0) - jnp.log(1.0 + jnp.exp(-jnp.abs(x)))) / GLA_GATE_NORM
        bcum = _dot_exact_lhs(tri, g)
        b_last = bcum[last:last + 1, :]
        q = q_ref[rows, :] * (GLA_HK ** -0.5)
        k = k_ref[rows, :]
        q_dec = (q * jnp.exp(bcum)).astype(bf16)
        k_dec = (k * jnp.exp(-bcum)).astype(bf16)
        k_end = (k * jnp.exp(b_last - bcum)).astype(bf16)
        decay = jnp.exp(b_last)
        for h in range(GLA_HEADS):
            kl = slice(h * GLA_HK, (h + 1) * GLA_HK)
            vl = slice(h * GLA_HV, (h + 1) * GLA_HV)
            v = v_ref[rows, vl].astype(bf16)
            att = jnp.where(incl, _dot_nt(q_dec[:, kl], k_dec[:, kl]), 0.0)
            st = st_ref[h]
            o = _dot(att.astype(bf16), v) + _dot(q_dec[:, kl], st.astype(bf16))
            st_ref[h] = st * _row_to_col(decay[:, kl]) + _dot_tn(k_end[:, kl], v)
            if final:
                o = o + of_ref[rows, vl]
                o = o * lax.rsqrt(jnp.mean(o * o, axis=-1, keepdims=True) + RMS_EPS) * nw_ref[...]
                o_ref[rows, vl] = (o * _silu(gate_ref[rows, vl])).astype(o_ref.dtype)
            else:
                o_ref[rows, vl] = o
        return carry

    lax.fori_loop(0, nchunks, chunk, 0)


def _gla_scan(q, k, v, gl, gate_up_pad, gate_bias, *, reverse, gate=None, o_fwd=None, norm_w=None,
              block_rows=256):
    bsz, seq, _ = q.shape
    tb = block_rows
    nb = seq // tb
    final = reverse
    blk = (lambda b, j: (b, nb - 1 - j, 0)) if reverse else (lambda b, j: (b, j, 0))
    full2 = lambda b, j: (0, 0)
    full3 = lambda b, j: (0, 0, 0)
    in_specs = [
        pl.BlockSpec((None, tb, GLA_KEY), blk),
        pl.BlockSpec((None, tb, GLA_KEY), blk),
        pl.BlockSpec((None, tb, GLA_VAL), blk),
        pl.BlockSpec((None, tb, LANES), blk),
        pl.BlockSpec((2, LANES, GLA_KEY), full3),
        pl.BlockSpec((2, GLA_KEY), full2),
    ]
    args = [q, k, v, gl, gate_up_pad, gate_bias]
    if final:
        in_specs += [
            pl.BlockSpec((None, tb, GLA_VAL), blk),
            pl.BlockSpec((None, tb, GLA_VAL), blk),
            pl.BlockSpec((1, GLA_HV), full2),
        ]
        args += [gate, o_fwd, norm_w.reshape(1, GLA_HV)]
    return pl.pallas_call(
        functools.partial(_gla_kernel, reverse=reverse, final=final, block_rows=tb),
        grid=(bsz, nb),
        in_specs=in_specs,
        out_specs=pl.BlockSpec((None, tb, GLA_VAL), blk),
        out_shape=jax.ShapeDtypeStruct((bsz, seq, GLA_VAL), bf16 if final else f32),
        scratch_shapes=[pltpu.VMEM((GLA_HEADS, GLA_HK, GLA_HV), f32)],
        compiler_params=_cparams("parallel", "arbitrary"),
        name="gla_bwd" if reverse else "gla_fwd",
    )(*args)


AT_NEG = -1e30
AT_QBLK = 128
AT_KBLK = 256


def _attn_kernel(q_ref, k_ref, v_ref, g_ref, o_ref, qp, kp, vp, m_s, l_s, acc_s, *, seq):
    lane = lax.broadcasted_iota(jnp.int32, (1, LANES), 1)
    left = lane < AT_DH
    scale = AT_DH ** -0.5

    for dil in AT_DILATIONS:
        sub = seq // dil
        tq = min(AT_QBLK, sub)
        nk = min(AT_KBLK, sub)
        piece = min(sub, 512)
        pieces = sub // piece

        def gather(i, carry, dil=dil, sub=sub, piece=piece, pieces=pieces):
            r = i // pieces
            part = i % pieces
            src = r + dil * part * piece
            dst = pl.multiple_of(r * sub + part * piece, piece)
            idx = pl.ds(src, piece, stride=dil) if dil > 1 else pl.ds(src, piece)
            qp[pl.ds(dst, piece), :] = (q_ref[idx, :] * scale).astype(bf16)
            kp[pl.ds(dst, piece), :] = k_ref[idx, :].astype(bf16)
            vp[pl.ds(dst, piece), :] = v_ref[idx, :].astype(bf16)
            return carry

        lax.fori_loop(0, dil * pieces, gather, 0)

        nblk = sub // tq
        offs = (lax.broadcasted_iota(jnp.int32, (tq, nk), 0)
                - lax.broadcasted_iota(jnp.int32, (tq, nk), 1))

        def block(i, carry, dil=dil, sub=sub, tq=tq, nk=nk, nblk=nblk, offs=offs):
            r = i // nblk
            m0 = (i % nblk) * tq
            ks = jnp.clip(m0 - AT_SIDE, 0, sub - nk)
            keep = jnp.abs(offs + (m0 - ks)) <= AT_SIDE
            qb = qp[pl.ds(pl.multiple_of(r * sub + m0, AT_SIDE), tq), :]
            kb = kp[pl.ds(pl.multiple_of(r * sub + ks, AT_SIDE), nk), :]
            vb = vp[pl.ds(pl.multiple_of(r * sub + ks, AT_SIDE), nk), :]
            parts = []
            for first in (True, False):
                qh = jnp.where(left if first else ~left, qb, jnp.zeros_like(qb))
                s = jnp.where(keep, _dot_nt(qh, kb), AT_NEG)
                mh = jnp.max(s, axis=1, keepdims=True)
                p = jnp.exp(s - mh)
                lh = jnp.sum(p, axis=1, keepdims=True)
                parts.append((mh, lh, _dot(p.astype(bf16), vb)))
            m_new = jnp.where(left, parts[0][0], parts[1][0])
            l_new = jnp.where(left, parts[0][1], parts[1][1])
            a_new = jnp.where(left, parts[0][2], parts[1][2])
            pos = r + dil * m0
            if dil == 1:
                idx = pl.ds(pl.multiple_of(pos, tq), tq)
                m_s[idx, :] = m_new
                l_s[idx, :] = l_new
                acc_s[idx, :] = a_new
            else:
                idx = pl.ds(pos, tq, stride=dil)
                m_old = m_s[idx, :]
                m = jnp.maximum(m_old, m_new)
                w_old = jnp.exp(m_old - m)
                w_new = jnp.exp(m_new - m)
                m_s[idx, :] = m
                l_s[idx, :] = l_s[idx, :] * w_old + l_new * w_new
                acc_s[idx, :] = acc_s[idx, :] * w_old + a_new * w_new
            return carry

        lax.fori_loop(0, dil * nblk, block, 0)

    rows = min(seq, 512)

    def finish(i, carry):
        idx = pl.ds(pl.multiple_of(i * rows, rows), rows)
        y = acc_s[idx, :] / l_s[idx, :]
        o_ref[idx, :] = (y * _silu(g_ref[idx, :])).astype(o_ref.dtype)
        return carry

    lax.fori_loop(0, seq // rows, finish, 0)


def _attention(qkv, gate):
    bsz, seq, _ = qkv.shape
    pairs = AT_WIDTH // LANES
    once = pl.Buffered(1)
    col = lambda off: (lambda b, p: (b, 0, off + p))
    return pl.pallas_call(
        functools.partial(_attn_kernel, seq=seq),
        grid=(bsz, pairs),
        in_specs=[
            pl.BlockSpec((None, seq, LANES), col(0), pipeline_mode=once),
            pl.BlockSpec((None, seq, LANES), col(pairs), pipeline_mode=once),
            pl.BlockSpec((None, seq, LANES), col(2 * pairs), pipeline_mode=once),
            pl.BlockSpec((None, seq, LANES), col(0), pipeline_mode=once),
        ],
        out_specs=pl.BlockSpec((None, seq, LANES), col(0)),
        out_shape=jax.ShapeDtypeStruct((bsz, seq, AT_WIDTH), bf16),
        scratch_shapes=[pltpu.VMEM((seq, LANES), bf16)] * 3 + [pltpu.VMEM((seq, LANES), f32)] * 3,
        compiler_params=_cparams("parallel", "parallel"),
        name="dilated_attn",
    )(qkv, qkv, qkv, gate)


RW_GROUP = 2
RW_GW = RW_GROUP * RW_DH
RW_NGROUPS = RW_HEADS // RW_GROUP


def _head_stack(x, masks):
    return jnp.concatenate([jnp.where(mk, x, jnp.zeros_like(x)) for mk in masks], axis=0)


def _unit_lower_inverse(a):
    n = a.shape[0]
    eye = (lax.broadcasted_iota(jnp.int32, (n, n), 0)
           == lax.broadcasted_iota(jnp.int32, (n, n), 1)).astype(f32)
    t = eye + a
    p = a
    for _ in range(int(math.log2(CHUNK)) - 1):
        p = _dot3(p, p)
        t = t + _dot3(t, p)
    return t


def _rwkv_kernel(*refs, reverse, final, block_rows, nblocks):
    (ps_ref, prev_ref, next_ref, mup_ref, mun_ref, w0_ref, wup_ref, a0_ref, aup_ref,
     kk_ref, ka_ref, seg_ref) = refs[:12]
    if final:
        rk_ref, gnw_ref, gnb_ref, ga_ref, yf_ref, o_ref, sh_ref, st_ref = refs[12:]
    else:
        o_ref, sh_ref, st_ref = refs[12:]
    d = 1 if reverse else 0
    tb = block_rows
    j = pl.program_id(1)
    blk = (nblocks - 1 - j) if reverse else j

    @pl.when(j == 0)
    def _():
        st_ref[...] = jnp.zeros_like(st_ref)

    rid = lax.broadcasted_iota(jnp.int32, (tb, 1), 0)
    has_prev = (blk > 0).astype(f32)
    has_next = (blk < nblocks - 1).astype(f32)
    for c0 in range(0, EVEN_SHIFT, 256):
        c1 = min(c0 + 256, EVEN_SHIFT)
        x = ps_ref[:, c0:c1]
        before = prev_ref[7:8, c0:c1] * has_prev
        after = next_ref[0:1, c0:c1] * has_next
        prv = jnp.where(rid == 0, before, pltpu.roll(x, 1, 0))
        nxt = jnp.where(rid == tb - 1, after, pltpu.roll(x, tb - 1, 0))
        sh_ref[:, c0:c1] = x + mup_ref[:, c0:c1] * (prv - x) + mun_ref[:, c0:c1] * (nxt - x)

    incl1, _ = _scan_masks(CHUNK, reverse)
    tri = incl1.astype(bf16)
    gi = lax.broadcasted_iota(jnp.int32, (2 * RW_GW, 2 * RW_GW), 0)
    ahead = (lax.broadcasted_iota(jnp.int32, (2 * RW_GW, 2 * RW_GW), 1) % CHUNK) - (gi % CHUNK)
    ahead = -ahead if reverse else ahead
    keep = ahead < jnp.where(gi < RW_GW, 0, 1)
    lane = lax.broadcasted_iota(jnp.int32, (1, RW_GW), 1)
    masks = [(lane // RW_DH) == h for h in range(RW_GROUP)]
    seg = seg_ref[...]
    last = 0 if reverse else CHUNK - 1
    nchunks = tb // CHUNK
    w_lo, a_lo = 3 * RW_WIDTH, 3 * RW_WIDTH + RW_LORA

    def lr_gate(a_lat, dd):
        return _sigmoid(a0_ref[dd:dd + 1, :] + _dot3(a_lat, aup_ref[dd]))

    def chunk(ci, carry):
        c = (nchunks - 1 - ci) if reverse else ci
        rows = pl.ds(pl.multiple_of(c * CHUNK, CHUNK), CHUNK)
        r = sh_ref[rows, 0:RW_WIDTH]
        k = sh_ref[rows, RW_WIDTH:2 * RW_WIDTH]
        v = sh_ref[rows, 2 * RW_WIDTH:3 * RW_WIDTH]
        w_lat = sh_ref[rows, w_lo:w_lo + RW_LORA]
        a_lat = sh_ref[rows, a_lo:a_lo + RW_LORA]
        w_log = -RW_DECAY_SCALE * _sigmoid(w0_ref[d:d + 1, :] + _dot3(jnp.tanh(w_lat), wup_ref[d]))
        lr = lr_gate(a_lat, d)
        lr_other = lr_gate(a_lat, 1 - d) if final else None
        kk = k * kk_ref[...]
        kk = kk / jnp.maximum(jnp.sqrt(_dot_exact_rhs(kk * kk, seg)), 1e-12)
        k_dir = k * (1.0 + (lr - 1.0) * ka_ref[...])
        cum = _dot_exact_lhs(tri, w_log)
        grow = jnp.exp(-cum)
        a_t = -kk * jnp.exp(cum - w_log)
        b_t = kk * lr * grow
        k_t = k_dir * grow
        r_t = r * jnp.exp(cum)
        p_end = jnp.exp(cum[last:last + 1, :])

        for g in range(RW_NGROUPS):
            gl = slice(g * RW_GW, (g + 1) * RW_GW)
            a_s = _head_stack(a_t[:, gl], masks).astype(bf16)
            r_s = _head_stack(r_t[:, gl], masks).astype(bf16)
            b_s = _head_stack(b_t[:, gl], masks).astype(bf16)
            k_s = _head_stack(k_t[:, gl], masks).astype(bf16)
            v_s = _head_stack(v[:, gl], masks)
            v_st = v_s.T.astype(bf16)
            v_s = v_s.astype(bf16)
            gram = _dot_nt(jnp.concatenate([a_s, r_s], axis=0), jnp.concatenate([b_s, k_s], axis=0))
            gram = jnp.where(keep, gram, 0.0)
            a_ab = gram[0:RW_GW, 0:RW_GW]
            a_ak = gram[0:RW_GW, RW_GW:].astype(bf16)
            a_rb = gram[RW_GW:, 0:RW_GW].astype(bf16)
            a_rk = gram[RW_GW:, RW_GW:].astype(bf16)
            t_inv = _unit_lower_inverse(a_ab).astype(bf16)
            w_m = _dot(t_inv, a_s).astype(bf16)
            z_t = _dot_nt(_dot_nt(v_st, a_ak).astype(bf16), t_inv)
            vk = _dot(v_st, k_s)
            s0 = st_ref[g]
            s0b = s0.astype(bf16)
            u_t = (_dot_nt(s0b, w_m) + z_t).astype(bf16)
            y = _dot_nt(r_s, s0b) + _dot_nt(a_rb, u_t) + _dot(a_rk, v_s)
            st_ref[g] = (s0 + _dot(u_t, b_s) + vk) * p_end[:, gl]
            yg = y[0:CHUNK]
            for h in range(1, RW_GROUP):
                yg = yg + y[h * CHUNK:(h + 1) * CHUNK]
            if not final:
                o_ref[rows, gl] = yg
            else:
                segg = seg[gl, gl]
                yg = yg + yf_ref[rows, gl]
                mean = _dot_exact_rhs(yg, segg) * (1.0 / RW_DH)
                cen = yg - mean
                var = _dot_exact_rhs(cen * cen, segg) * (1.0 / RW_DH)
                yn = cen * lax.rsqrt(var + RW_GN_EPS) * gnw_ref[:, gl] + gnb_ref[:, gl]
                k_other = k[:, gl] * (1.0 + (lr_other[:, gl] - 1.0) * ka_ref[:, gl])
                rk = _dot_exact_rhs(r[:, gl] * (k_dir[:, gl] + k_other) * rk_ref[:, gl], segg)
                out = (yn + rk * v[:, gl]) * _silu(ga_ref[rows, gl])
                o_ref[rows, gl] = out.astype(o_ref.dtype)
        return carry

    lax.fori_loop(0, nchunks, chunk, 0)


def _segment_ones():
    h = jnp.arange(RW_WIDTH) // RW_DH
    return (h[:, None] == h[None, :]).astype(bf16)


def _rwkv_scan(ps, mu_prev, mu_next, w0, w_up, a0, a_up, k_k, k_a, *, reverse,
               r_k=None, gn_w=None, gn_b=None, gate=None, y_fwd=None, block_rows=256):
    bsz, seq, _ = ps.shape
    tb = block_rows
    nb = seq // tb
    final = reverse
    halo = tb // 8
    pos = (lambda j: nb - 1 - j) if reverse else (lambda j: j)
    blk = lambda b, j: (b, pos(j), 0)
    prev = lambda b, j: (b, jnp.maximum(pos(j) * halo - 1, 0), 0)
    nxt = lambda b, j: (b, jnp.minimum((pos(j) + 1) * halo, seq // 8 - 1), 0)
    full2 = lambda b, j: (0, 0)
    full3 = lambda b, j: (0, 0, 0)
    vec = lambda n: pl.BlockSpec((1, n), full2)
    in_specs = [
        pl.BlockSpec((None, tb, EVEN_SHIFT), blk),
        pl.BlockSpec((None, 8, EVEN_SHIFT), prev),
        pl.BlockSpec((None, 8, EVEN_SHIFT), nxt),
        vec(EVEN_SHIFT), vec(EVEN_SHIFT),
        pl.BlockSpec((2, RW_WIDTH), full2),
        pl.BlockSpec((2, RW_LORA, RW_WIDTH), full3),
        pl.BlockSpec((2, RW_WIDTH), full2),
        pl.BlockSpec((2, RW_LORA, RW_WIDTH), full3),
        vec(RW_WIDTH), vec(RW_WIDTH),
        pl.BlockSpec((RW_WIDTH, RW_WIDTH), full2),
    ]
    row = lambda t: t.reshape(1, -1)
    args = [ps, ps, ps, row(mu_prev), row(mu_next), w0, w_up, a0, a_up, row(k_k), row(k_a),
            _segment_ones()]
    if final:
        in_specs += [vec(RW_WIDTH), vec(RW_WIDTH), vec(RW_WIDTH),
                     pl.BlockSpec((None, tb, RW_WIDTH), blk),
                     pl.BlockSpec((None, tb, RW_WIDTH), blk)]
        args += [row(r_k), row(gn_w), row(gn_b), gate, y_fwd]
    return pl.pallas_call(
        functools.partial(_rwkv_kernel, reverse=reverse, final=final, block_rows=tb, nblocks=nb),
        grid=(bsz, nb),
        in_specs=in_specs,
        out_specs=pl.BlockSpec((None, tb, RW_WIDTH), blk),
        out_shape=jax.ShapeDtypeStruct((bsz, seq, RW_WIDTH), bf16 if final else f32),
        scratch_shapes=[pltpu.VMEM((tb, EVEN_SHIFT), f32),
                        pltpu.VMEM((RW_NGROUPS, RW_GW, RW_GW), f32)],
        compiler_params=_cparams("parallel", "arbitrary"),
        name="rwkv_bwd" if reverse else "rwkv_fwd",
    )(*args)


def _trunk(x, p):
    bsz, seq, _ = x.shape
    tokens = bsz * seq
    x2d = x.reshape(tokens, D_MODEL)
    seq3 = lambda t: t.reshape(bsz, seq, t.shape[-1])
    flat = lambda t: t.reshape(tokens, t.shape[-1])

    ps, ga, qkv, gb = _even_in(x2d, seq, p["even_norm"], p["even_w_in"])
    rw = (seq3(ps), p["mu_prev"], p["mu_next"], p["w0"], p["w_up"], p["a0"], p["a_up"],
          p["k_k"], p["k_a"])
    y_fwd = _rwkv_scan(*rw, reverse=False)
    ya = _rwkv_scan(*rw, reverse=True, r_k=p["r_k"], gn_w=p["gn_w"], gn_b=p["gn_b"],
                    gate=seq3(ga), y_fwd=y_fwd)
    yb = _attention(seq3(qkv), seq3(gb))

    x1, q, k, v, gate, gate_lat = _mid(x2d, flat(ya), flat(yb), p["even_w_out"], p["odd_norm"],
                                       p["odd_w_in"])
    gla = (seq3(q), seq3(k), seq3(v), seq3(gate_lat), p["gate_up"], p["gate_bias"])
    o_fwd = _gla_scan(*gla, reverse=False)
    yc = _gla_scan(*gla, reverse=True, gate=seq3(gate), o_fwd=o_fwd, norm_w=p["gla_norm"])
    return _final(x1, flat(yc), p["odd_w_out"], p["final_norm"]).reshape(bsz, seq, D_MODEL)


def _prepare(even_norm, even_w_in, even_mu_prev, even_mu_next, rwkv_w0, rwkv_w_up, rwkv_a0,
             rwkv_a_up, rwkv_k_k, rwkv_k_a, rwkv_r_k, rwkv_gn_w, rwkv_gn_b, even_w_out, odd_norm,
             odd_w_in, gla_gate_up, gla_gate_bias, gla_norm, odd_w_out, final_norm):
    wi = odd_w_in[0]
    lat0 = 2 * GLA_KEY + GLA_VAL
    lat = jnp.pad(wi[:, lat0:lat0 + GLA_RANK], ((0, 0), (0, LANES - GLA_RANK)))
    odd_in = jnp.concatenate([wi[:, :lat0], wi[:, lat0 + GLA_RANK:], lat], axis=1)
    return {
        "even_norm": even_norm[0], "even_w_in": even_w_in[0].astype(bf16),
        "mu_prev": even_mu_prev[0], "mu_next": even_mu_next[0],
        "w0": rwkv_w0[0], "w_up": rwkv_w_up[0], "a0": rwkv_a0[0], "a_up": rwkv_a_up[0],
        "k_k": rwkv_k_k[0], "k_a": rwkv_k_a[0], "r_k": rwkv_r_k[0],
        "gn_w": rwkv_gn_w[0], "gn_b": rwkv_gn_b[0],
        "even_w_out": even_w_out[0].astype(bf16),
        "odd_norm": odd_norm[0], "odd_w_in": odd_in.astype(bf16),
        "gate_up": jnp.pad(gla_gate_up[0], ((0, 0), (0, LANES - GLA_RANK), (0, 0))),
        "gate_bias": gla_gate_bias[0], "gla_norm": gla_norm[0],
        "odd_w_out": odd_w_out[0].astype(bf16), "final_norm": final_norm,
    }


def kernel(x_prompt, x_sample, even_norm, even_w_in, even_mu_prev, even_mu_next, rwkv_w0, rwkv_w_up,
           rwkv_a0, rwkv_a_up, rwkv_k_k, rwkv_k_a, rwkv_r_k, rwkv_gn_w, rwkv_gn_b, even_w_out,
           odd_norm, odd_w_in, gla_gate_up, gla_gate_bias, gla_norm, odd_w_out, final_norm):
    p = _prepare(even_norm, even_w_in, even_mu_prev, even_mu_next, rwkv_w0, rwkv_w_up, rwkv_a0,
                 rwkv_a_up, rwkv_k_k, rwkv_k_a, rwkv_r_k, rwkv_gn_w, rwkv_gn_b, even_w_out,
                 odd_norm, odd_w_in, gla_gate_up, gla_gate_bias, gla_norm, odd_w_out, final_norm)
    return (_trunk(x_prompt, p), _trunk(x_sample, p))
```

```python
import functools
import math

import jax
import jax.numpy as jnp
from jax import lax
from jax.experimental import pallas as pl
from jax.experimental.pallas import tpu as pltpu

f32 = jnp.float32
bf16 = jnp.bfloat16

D_MODEL = 1024
RMS_EPS = 1e-6

RW_HEADS = 8
RW_DH = 64
RW_WIDTH = RW_HEADS * RW_DH
RW_LORA = 64
RW_DECAY_SCALE = 0.6065306597126334
RW_GN_EPS = 64e-5
AT_HEADS = 8
AT_DH = 64
AT_WIDTH = AT_HEADS * AT_DH
AT_SIDE = 64
AT_DILATIONS = (1, 4, 16)
ROPE_THETA = 10000.0
EVEN_SHIFT = 3 * RW_WIDTH + 2 * RW_LORA
EVEN_COLS = EVEN_SHIFT + RW_WIDTH + 4 * AT_WIDTH

GLA_HEADS = 4
GLA_KEY = 512
GLA_VAL = 1024
GLA_HK = GLA_KEY // GLA_HEADS
GLA_HV = GLA_VAL // GLA_HEADS
GLA_RANK = 16
GLA_GATE_NORM = 16.0
ODD_COLS = 2 * GLA_KEY + GLA_VAL + GLA_RANK + GLA_VAL

CHUNK = 64
LANES = 128
VMEM_LIMIT = 56 * 1024 * 1024

_NT = (((1,), (1,)), ((), ()))
_TN = (((0,), (0,)), ((), ()))


def _dot(a, b):
    return jnp.dot(a, b, preferred_element_type=f32)


def _dot_nt(a, b):
    return lax.dot_general(a, b, _NT, preferred_element_type=f32)


def _dot_tn(a, b):
    return lax.dot_general(a, b, _TN, preferred_element_type=f32)


def _split2(x):
    hi = x.astype(bf16)
    lo = (x - hi.astype(f32)).astype(bf16)
    return hi, lo


def _dot_seg(x, e):
    hi, lo = _split2(x)
    return _dot(hi, e) + _dot(lo, e)


def _cumsum_rows(tri, x):
    hi, lo = _split2(x)
    return _dot(tri, hi) + _dot(tri, lo)


def _dot3(a, b):
    ah, al = _split2(a)
    bh, bl = _split2(b)
    return _dot(ah, bh) + (_dot(ah, bl) + _dot(al, bh))


def _sigmoid(x):
    return 1.0 / (1.0 + jnp.exp(-x))


def _silu(x):
    return x * _sigmoid(x)


def _rms_rows(x, w):
    return x * lax.rsqrt(jnp.mean(x * x, axis=-1, keepdims=True) + RMS_EPS) * w


def _cparams(*sem):
    return pltpu.CompilerParams(dimension_semantics=sem, vmem_limit_bytes=VMEM_LIMIT)


def _rope_partner(x):
    lane = lax.broadcasted_iota(jnp.int32, x.shape, 1)
    return jnp.where((lane & 32) == 0, pltpu.roll(x, LANES - 32, 1), pltpu.roll(x, 32, 1))


def _even_in_kernel(x_ref, nw_ref, w_ref, cos_ref, sin_ref, ps_ref, ga_ref, qkv_ref, gb_ref):
    xn = _rms_rows(x_ref[...], nw_ref[...]).astype(bf16)

    def cols(c0, c1):
        return _dot(xn, w_ref[:, c0:c1])

    for c0 in range(0, EVEN_SHIFT, 512):
        c1 = min(c0 + 512, EVEN_SHIFT)
        ps_ref[:, c0:c1] = cols(c0, c1)
    base = EVEN_SHIFT
    ga_ref[...] = cols(base, base + RW_WIDTH)
    base += RW_WIDTH
    cos = cos_ref[...]
    sin = sin_ref[...]
    for j in range(2 * AT_WIDTH // LANES):
        t = cols(base + j * LANES, base + (j + 1) * LANES)
        qkv_ref[:, j * LANES:(j + 1) * LANES] = t * cos + _rope_partner(t) * sin
    base += 2 * AT_WIDTH
    qkv_ref[:, 2 * AT_WIDTH:3 * AT_WIDTH] = cols(base, base + AT_WIDTH)
    base += AT_WIDTH
    gb_ref[...] = cols(base, base + AT_WIDTH)


def _rope_tables(seq):
    inv = ROPE_THETA ** (-jnp.arange(0, AT_DH, 2, dtype=f32) / AT_DH)
    ang = jnp.arange(seq, dtype=f32)[:, None] * inv[None, :]
    cos, sin = jnp.cos(ang), jnp.sin(ang)
    return (jnp.concatenate([cos, cos, cos, cos], axis=-1),
            jnp.concatenate([-sin, sin, -sin, sin], axis=-1))


def _even_in(x2d, seq, norm_w, w_in_bf16, block_rows=512):
    m = x2d.shape[0]
    tm = block_rows
    per_seq = seq // tm
    cos, sin = _rope_tables(seq)
    row = lambda i: (i, 0)
    full = lambda i: (0, 0)
    tab = lambda i: (i % per_seq, 0)
    return pl.pallas_call(
        _even_in_kernel,
        grid=(m // tm,),
        in_specs=[
            pl.BlockSpec((tm, D_MODEL), row),
            pl.BlockSpec((1, D_MODEL), full),
            pl.BlockSpec((D_MODEL, EVEN_COLS), full),
            pl.BlockSpec((tm, LANES), tab),
            pl.BlockSpec((tm, LANES), tab),
        ],
        out_specs=[
            pl.BlockSpec((tm, EVEN_SHIFT), row),
            pl.BlockSpec((tm, RW_WIDTH), row),
            pl.BlockSpec((tm, 3 * AT_WIDTH), row),
            pl.BlockSpec((tm, AT_WIDTH), row),
        ],
        out_shape=[
            jax.ShapeDtypeStruct((m, EVEN_SHIFT), f32),
            jax.ShapeDtypeStruct((m, RW_WIDTH), f32),
            jax.ShapeDtypeStruct((m, 3 * AT_WIDTH), f32),
            jax.ShapeDtypeStruct((m, AT_WIDTH), f32),
        ],
        compiler_params=_cparams("parallel"),
        name="even_in",
    )(x2d, norm_w.reshape(1, D_MODEL), w_in_bf16, cos, sin)


ODD_PAD_COLS = 2 * GLA_KEY + 2 * GLA_VAL + LANES


def _mid_kernel(x_ref, ya_ref, yb_ref, wo_ref, nw_ref, wi_ref,
                x1_ref, q_ref, k_ref, v_ref, g_ref, gl_ref):
    x1 = (x_ref[...] + _dot(ya_ref[...], wo_ref[0:RW_WIDTH, :])
          + _dot(yb_ref[...], wo_ref[RW_WIDTH:RW_WIDTH + AT_WIDTH, :]))
    x1_ref[...] = x1
    xn = _rms_rows(x1, nw_ref[...]).astype(bf16)
    c = 0
    for ref, width in ((q_ref, GLA_KEY), (k_ref, GLA_KEY), (v_ref, GLA_VAL), (g_ref, GLA_VAL),
                       (gl_ref, LANES)):
        for c0 in range(0, width, 512):
            c1 = min(c0 + 512, width)
            ref[:, c0:c1] = _dot(xn, wi_ref[:, c + c0:c + c1])
        c += width


def _mid(x2d, ya, yb, w_out_bf16, norm_w, w_in_pad_bf16, block_rows=512):
    m = x2d.shape[0]
    tm = block_rows
    row = lambda i: (i, 0)
    full = lambda i: (0, 0)
    widths = (D_MODEL, GLA_KEY, GLA_KEY, GLA_VAL, GLA_VAL, LANES)
    return pl.pallas_call(
        _mid_kernel,
        grid=(m // tm,),
        in_specs=[
            pl.BlockSpec((tm, D_MODEL), row),
            pl.BlockSpec((tm, RW_WIDTH), row),
            pl.BlockSpec((tm, AT_WIDTH), row),
            pl.BlockSpec((RW_WIDTH + AT_WIDTH, D_MODEL), full),
            pl.BlockSpec((1, D_MODEL), full),
            pl.BlockSpec((D_MODEL, ODD_PAD_COLS), full),
        ],
        out_specs=[pl.BlockSpec((tm, w), row) for w in widths],
        out_shape=[jax.ShapeDtypeStruct((m, w), f32) for w in widths],
        compiler_params=_cparams("parallel"),
        name="mid",
    )(x2d, ya, yb, w_out_bf16, norm_w.reshape(1, D_MODEL), w_in_pad_bf16)


def _final_kernel(x_ref, y_ref, wo_ref, nw_ref, o_ref):
    x2 = x_ref[...] + _dot(y_ref[...], wo_ref[...])
    o_ref[...] = _rms_rows(x2, nw_ref[...])


def _final(x2d, yc, w_out_bf16, norm_w, block_rows=512):
    m = x2d.shape[0]
    tm = block_rows
    row = lambda i: (i, 0)
    full = lambda i: (0, 0)
    return pl.pallas_call(
        _final_kernel,
        grid=(m // tm,),
        in_specs=[
            pl.BlockSpec((tm, D_MODEL), row),
            pl.BlockSpec((tm, GLA_VAL), row),
            pl.BlockSpec((GLA_VAL, D_MODEL), full),
            pl.BlockSpec((1, D_MODEL), full),
        ],
        out_specs=pl.BlockSpec((tm, D_MODEL), row),
        out_shape=jax.ShapeDtypeStruct((m, D_MODEL), f32),
        compiler_params=_cparams("parallel"),
        name="final",
    )(x2d, yc, w_out_bf16, norm_w.reshape(1, D_MODEL))


def _scan_masks(n, reverse):
    t = lax.broadcasted_iota(jnp.int32, (n, n), 0) % CHUNK
    s = lax.broadcasted_iota(jnp.int32, (n, n), 1) % CHUNK
    if reverse:
        return s >= t, s > t
    return s <= t, s < t


def _row_to_col(row):
    n = row.shape[1]
    eye = (lax.broadcasted_iota(jnp.int32, (n, n), 0) == lax.broadcasted_iota(jnp.int32, (n, n), 1))
    return jnp.sum(jnp.where(eye, jnp.broadcast_to(row, (n, n)), 0.0), axis=1, keepdims=True)


def _gla_kernel(*refs, reverse, final, block_rows):
    if final:
        (q_ref, k_ref, v_ref, gl_ref, gup_ref, gb_ref, gate_ref, of_ref, nw_ref, o_ref, st_ref) = refs
    else:
        (q_ref, k_ref, v_ref, gl_ref, gup_ref, gb_ref, o_ref, st_ref) = refs
    d = 1 if reverse else 0

    @pl.when(pl.program_id(1) == 0)
    def _():
        st_ref[...] = jnp.zeros_like(st_ref)

    incl, _ = _scan_masks(CHUNK, reverse)
    tri = incl.astype(bf16)
    gup = gup_ref[d]
    gbias = gb_ref[d:d + 1, :]
    nchunks = block_rows // CHUNK
    last = 0 if reverse else CHUNK - 1

    def chunk(ci, carry):
        c = (nchunks - 1 - ci) if reverse else ci
        rows = pl.ds(pl.multiple_of(c * CHUNK, CHUNK), CHUNK)
        x = _dot3(gl_ref[rows, :], gup) + gbias
        g = (jnp.minimum(x, 0.0) - jnp.log(1.0 + jnp.exp(-jnp.abs(x)))) / GLA_GATE_NORM
        bcum = _cumsum_rows(tri, g)
        b_last = bcum[last:last + 1, :]
        q = q_ref[rows, :] * (GLA_HK ** -0.5)
        k = k_ref[rows, :]
        q_dec = (q * jnp.exp(bcum)).astype(bf16)
        k_dec = (k * jnp.exp(-bcum)).astype(bf16)
        k_end = (k * jnp.exp(b_last - bcum)).astype(bf16)
        decay = jnp.exp(b_last)
        for h in range(GLA_HEADS):
            kl = slice(h * GLA_HK, (h + 1) * GLA_HK)
            vl = slice(h * GLA_HV, (h + 1) * GLA_HV)
            v = v_ref[rows, vl].astype(bf16)
            att = jnp.where(incl, _dot_nt(q_dec[:, kl], k_dec[:, kl]), 0.0)
            st = st_ref[h]
            o = _dot(att.astype(bf16), v) + _dot(q_dec[:, kl], st.astype(bf16))
            st_ref[h] = st * _row_to_col(decay[:, kl]) + _dot_tn(k_end[:, kl], v)
            if final:
                o = o + of_ref[rows, vl]
                o = o * lax.rsqrt(jnp.mean(o * o, axis=-1, keepdims=True) + RMS_EPS) * nw_ref[...]
                o_ref[rows, vl] = (o * _silu(gate_ref[rows, vl])).astype(o_ref.dtype)
            else:
                o_ref[rows, vl] = o
        return carry

    lax.fori_loop(0, nchunks, chunk, 0)


def _gla_scan(q, k, v, gl, gate_up_pad, gate_bias, *, reverse, gate=None, o_fwd=None, norm_w=None,
              block_rows=256):
    bsz, seq, _ = q.shape
    tb = block_rows
    nb = seq // tb
    final = reverse
    blk = (lambda b, j: (b, nb - 1 - j, 0)) if reverse else (lambda b, j: (b, j, 0))
    full2 = lambda b, j: (0, 0)
    full3 = lambda b, j: (0, 0, 0)
    in_specs = [
        pl.BlockSpec((None, tb, GLA_KEY), blk),
        pl.BlockSpec((None, tb, GLA_KEY), blk),
        pl.BlockSpec((None, tb, GLA_VAL), blk),
        pl.BlockSpec((None, tb, LANES), blk),
        pl.BlockSpec((2, LANES, GLA_KEY), full3),
        pl.BlockSpec((2, GLA_KEY), full2),
    ]
    args = [q, k, v, gl, gate_up_pad, gate_bias]
    if final:
        in_specs += [
            pl.BlockSpec((None, tb, GLA_VAL), blk),
            pl.BlockSpec((None, tb, GLA_VAL), blk),
            pl.BlockSpec((1, GLA_HV), full2),
        ]
        args += [gate, o_fwd, norm_w.reshape(1, GLA_HV)]
    return pl.pallas_call(
        functools.partial(_gla_kernel, reverse=reverse, final=final, block_rows=tb),
        grid=(bsz, nb),
        in_specs=in_specs,
        out_specs=pl.BlockSpec((None, tb, GLA_VAL), blk),
        out_shape=jax.ShapeDtypeStruct((bsz, seq, GLA_VAL), bf16 if final else f32),
        scratch_shapes=[pltpu.VMEM((GLA_HEADS, GLA_HK, GLA_HV), f32)],
        compiler_params=_cparams("parallel", "arbitrary"),
        name="gla_bwd" if reverse else "gla_fwd",
    )(*args)


AT_NEG = -1e30
AT_QBLK = 128
AT_KBLK = 256


def _attn_kernel(q_ref, k_ref, v_ref, g_ref, o_ref, qp, kp, vp, m_s, l_s, acc_s, *, seq):
    lane = lax.broadcasted_iota(jnp.int32, (1, LANES), 1)
    left = lane < AT_DH
    scale = AT_DH ** -0.5

    for dil in AT_DILATIONS:
        sub = seq // dil
        tq = min(AT_QBLK, sub)
        nk = min(AT_KBLK, sub)
        piece = min(sub, 512)
        pieces = sub // piece

        def gather(i, carry, dil=dil, sub=sub, piece=piece, pieces=pieces):
            r = i // pieces
            part = i % pieces
            src = r + dil * part * piece
            dst = pl.multiple_of(r * sub + part * piece, piece)
            idx = pl.ds(src, piece, stride=dil) if dil > 1 else pl.ds(src, piece)
            qp[pl.ds(dst, piece), :] = (q_ref[idx, :] * scale).astype(bf16)
            kp[pl.ds(dst, piece), :] = k_ref[idx, :].astype(bf16)
            vp[pl.ds(dst, piece), :] = v_ref[idx, :].astype(bf16)
            return carry

        lax.fori_loop(0, dil * pieces, gather, 0)

        nblk = sub // tq
        offs = (lax.broadcasted_iota(jnp.int32, (tq, nk), 0)
                - lax.broadcasted_iota(jnp.int32, (tq, nk), 1))

        def block(i, carry, dil=dil, sub=sub, tq=tq, nk=nk, nblk=nblk, offs=offs):
            r = i // nblk
            m0 = (i % nblk) * tq
            ks = jnp.clip(m0 - AT_SIDE, 0, sub - nk)
            keep = jnp.abs(offs + (m0 - ks)) <= AT_SIDE
            qb = qp[pl.ds(pl.multiple_of(r * sub + m0, AT_SIDE), tq), :]
            kb = kp[pl.ds(pl.multiple_of(r * sub + ks, AT_SIDE), nk), :]
            vb = vp[pl.ds(pl.multiple_of(r * sub + ks, AT_SIDE), nk), :]
            parts = []
            for first in (True, False):
                qh = jnp.where(left if first else ~left, qb, jnp.zeros_like(qb))
                s = jnp.where(keep, _dot_nt(qh, kb), AT_NEG)
                mh = jnp.max(s, axis=1, keepdims=True)
                p = jnp.exp(s - mh)
                lh = jnp.sum(p, axis=1, keepdims=True)
                parts.append((mh, lh, _dot(p.astype(bf16), vb)))
            m_new = jnp.where(left, parts[0][0], parts[1][0])
            l_new = jnp.where(left, parts[0][1], parts[1][1])
            a_new = jnp.where(left, parts[0][2], parts[1][2])
            pos = r + dil * m0
            if dil == 1:
                idx = pl.ds(pl.multiple_of(pos, tq), tq)
                m_s[idx, :] = m_new
                l_s[idx, :] = l_new
                acc_s[idx, :] = a_new
            else:
                idx = pl.ds(pos, tq, stride=dil)
                m_old = m_s[idx, :]
                m = jnp.maximum(m_old, m_new)
                w_old = jnp.exp(m_old - m)
                w_new = jnp.exp(m_new - m)
                m_s[idx, :] = m
                l_s[idx, :] = l_s[idx, :] * w_old + l_new * w_new
                acc_s[idx, :] = acc_s[idx, :] * w_old + a_new * w_new
            return carry

        lax.fori_loop(0, dil * nblk, block, 0)

    rows = min(seq, 512)

    def finish(i, carry):
        idx = pl.ds(pl.multiple_of(i * rows, rows), rows)
        y = acc_s[idx, :] / l_s[idx, :]
        o_ref[idx, :] = (y * _silu(g_ref[idx, :])).astype(o_ref.dtype)
        return carry

    lax.fori_loop(0, seq // rows, finish, 0)


def _attention(qkv, gate):
    bsz, seq, _ = qkv.shape
    pairs = AT_WIDTH // LANES
    once = pl.Buffered(1)
    col = lambda off: (lambda b, p: (b, 0, off + p))
    return pl.pallas_call(
        functools.partial(_attn_kernel, seq=seq),
        grid=(bsz, pairs),
        in_specs=[
            pl.BlockSpec((None, seq, LANES), col(0), pipeline_mode=once),
            pl.BlockSpec((None, seq, LANES), col(pairs), pipeline_mode=once),
            pl.BlockSpec((None, seq, LANES), col(2 * pairs), pipeline_mode=once),
            pl.BlockSpec((None, seq, LANES), col(0), pipeline_mode=once),
        ],
        out_specs=pl.BlockSpec((None, seq, LANES), col(0)),
        out_shape=jax.ShapeDtypeStruct((bsz, seq, AT_WIDTH), bf16),
        scratch_shapes=[pltpu.VMEM((seq, LANES), bf16)] * 3 + [pltpu.VMEM((seq, LANES), f32)] * 3,
        compiler_params=_cparams("parallel", "parallel"),
        name="dilated_attn",
    )(qkv, qkv, qkv, gate)


RW_GROUP = 2
RW_GW = RW_GROUP * RW_DH
RW_NGROUPS = RW_HEADS // RW_GROUP


def _head_stack(x, masks):
    return jnp.concatenate([jnp.where(mk, x, jnp.zeros_like(x)) for mk in masks], axis=0)


def _unit_lower_inverses(mats):
    n = mats[0].shape[0]
    eye = (lax.broadcasted_iota(jnp.int32, (n, n), 0)
           == lax.broadcasted_iota(jnp.int32, (n, n), 1)).astype(f32)
    ts = [eye + a for a in mats]
    qs = [a.astype(bf16) for a in mats]
    qs = [_dot(q, q).astype(bf16) for q in qs]
    steps = int(math.log2(CHUNK)) - 1
    for step in range(steps):
        if step < steps - 1:
            both = [_dot(q, jnp.concatenate([q, t.astype(bf16)], axis=1)) for q, t in zip(qs, ts)]
            qs = [x[:, :n].astype(bf16) for x in both]
            ts = [t + x[:, n:] for t, x in zip(ts, both)]
        else:
            ts = [t + _dot(q, t.astype(bf16)) for q, t in zip(qs, ts)]
    return [t.astype(bf16) for t in ts]


def _rwkv_kernel(*refs, reverse, final, block_rows, nblocks):
    (ps_ref, prev_ref, next_ref, mup_ref, mun_ref, w0_ref, wup_ref, a0_ref, aup_ref,
     kk_ref, ka_ref, seg_ref) = refs[:12]
    if final:
        rk_ref, gnw_ref, gnb_ref, ga_ref, yf_ref, o_ref, sh_ref, st_ref = refs[12:]
    else:
        o_ref, sh_ref, st_ref = refs[12:]
    d = 1 if reverse else 0
    tb = block_rows
    j = pl.program_id(1)
    blk = (nblocks - 1 - j) if reverse else j

    @pl.when(j == 0)
    def _():
        st_ref[...] = jnp.zeros_like(st_ref)

    rid = lax.broadcasted_iota(jnp.int32, (tb, 1), 0)
    has_prev = (blk > 0).astype(f32)
    has_next = (blk < nblocks - 1).astype(f32)
    for c0 in range(0, EVEN_SHIFT, 256):
        c1 = min(c0 + 256, EVEN_SHIFT)
        x = ps_ref[:, c0:c1]
        before = prev_ref[7:8, c0:c1] * has_prev
        after = next_ref[0:1, c0:c1] * has_next
        prv = jnp.where(rid == 0, before, pltpu.roll(x, 1, 0))
        nxt = jnp.where(rid == tb - 1, after, pltpu.roll(x, tb - 1, 0))
        sh_ref[:, c0:c1] = x + mup_ref[:, c0:c1] * (prv - x) + mun_ref[:, c0:c1] * (nxt - x)

    incl1, _ = _scan_masks(CHUNK, reverse)
    tri = incl1.astype(bf16)
    gi = lax.broadcasted_iota(jnp.int32, (2 * RW_GW, 2 * RW_GW), 0)
    ahead = (lax.broadcasted_iota(jnp.int32, (2 * RW_GW, 2 * RW_GW), 1) % CHUNK) - (gi % CHUNK)
    ahead = -ahead if reverse else ahead
    keep = ahead < jnp.where(gi < RW_GW, 0, 1)
    lane = lax.broadcasted_iota(jnp.int32, (1, RW_GW), 1)
    masks = [(lane // RW_DH) == h for h in range(RW_GROUP)]
    seg = seg_ref[...]
    last = 0 if reverse else CHUNK - 1
    nchunks = tb // CHUNK
    w_lo, a_lo = 3 * RW_WIDTH, 3 * RW_WIDTH + RW_LORA

    def lr_gate(a_lat, dd):
        return _sigmoid(a0_ref[dd:dd + 1, :] + _dot3(a_lat, aup_ref[dd]))

    r = sh_ref[:, 0:RW_WIDTH]
    k = sh_ref[:, RW_WIDTH:2 * RW_WIDTH]
    v = sh_ref[:, 2 * RW_WIDTH:3 * RW_WIDTH]
    w_lat = sh_ref[:, w_lo:w_lo + RW_LORA]
    a_lat = sh_ref[:, a_lo:a_lo + RW_LORA]
    w_log = -RW_DECAY_SCALE * _sigmoid(w0_ref[d:d + 1, :] + _dot3(jnp.tanh(w_lat), wup_ref[d]))
    lr = lr_gate(a_lat, d)
    kk = k * kk_ref[...]
    kk = kk / jnp.maximum(jnp.sqrt(_dot_seg(kk * kk, seg)), 1e-12)
    k_dir = k * (1.0 + (lr - 1.0) * ka_ref[...])
    chunk_rows = [slice(c * CHUNK, (c + 1) * CHUNK) for c in range(nchunks)]
    cum = jnp.concatenate([_cumsum_rows(tri, w_log[rs]) for rs in chunk_rows], axis=0)
    grow = jnp.exp(-cum)
    a_t = -kk * jnp.exp(cum - w_log)
    b_t = kk * lr * grow
    k_t = k_dir * grow
    r_t = r * jnp.exp(cum)

    order = list(reversed(range(nchunks))) if reverse else list(range(nchunks))
    lanes = [slice(g * RW_GW, (g + 1) * RW_GW) for g in range(RW_NGROUPS)]
    units = [(c, g) for c in order for g in range(RW_NGROUPS)]
    stack = lambda x: [_head_stack(x[chunk_rows[c], lanes[g]], masks) for c, g in units]
    a_s = [t.astype(bf16) for t in stack(a_t)]
    r_s = [t.astype(bf16) for t in stack(r_t)]
    b_s = [t.astype(bf16) for t in stack(b_t)]
    k_s = [t.astype(bf16) for t in stack(k_t)]
    v_f = stack(v)
    v_st = [t.T.astype(bf16) for t in v_f]
    v_s = [t.astype(bf16) for t in v_f]
    n = range(len(units))
    gram = [jnp.where(keep, _dot_nt(jnp.concatenate([a_s[i], r_s[i]], axis=0),
                                     jnp.concatenate([b_s[i], k_s[i]], axis=0)), 0.0) for i in n]
    a_ak = [gram[i][0:RW_GW, RW_GW:].astype(bf16) for i in n]
    a_rb = [gram[i][RW_GW:, 0:RW_GW].astype(bf16) for i in n]
    a_rk = [gram[i][RW_GW:, RW_GW:].astype(bf16) for i in n]
    t_inv = _unit_lower_inverses([gram[i][0:RW_GW, 0:RW_GW] for i in n])
    w_m = [_dot(t_inv[i], a_s[i]).astype(bf16) for i in n]
    av_t = [_dot_nt(v_st[i], a_ak[i]).astype(bf16) for i in n]
    z_t = [_dot_nt(av_t[i], t_inv[i]) for i in n]
    vk = [_dot(v_st[i], k_s[i]) for i in n]
    y_local = [_dot(a_rk[i], v_s[i]) for i in n]

    state = [st_ref[g] for g in range(RW_NGROUPS)]
    y_rows = {}
    for pos_c, c in enumerate(order):
        p_end = jnp.exp(cum[c * CHUNK + last:c * CHUNK + last + 1, :])
        ys = []
        for g in range(RW_NGROUPS):
            i = pos_c * RW_NGROUPS + g
            s0 = state[g]
            s0b = s0.astype(bf16)
            u_t = (_dot_nt(s0b, w_m[i]) + z_t[i]).astype(bf16)
            state[g] = (s0 + _dot(u_t, b_s[i]) + vk[i]) * p_end[:, lanes[g]]
            y = _dot_nt(r_s[i], s0b) + _dot_nt(a_rb[i], u_t) + y_local[i]
            yg = y[0:CHUNK]
            for h in range(1, RW_GROUP):
                yg = yg + y[h * CHUNK:(h + 1) * CHUNK]
            ys.append(yg)
        y_rows[c] = jnp.concatenate(ys, axis=1)
    for g in range(RW_NGROUPS):
        st_ref[g] = state[g]
    y_all = jnp.concatenate([y_rows[c] for c in range(nchunks)], axis=0)

    if not final:
        o_ref[...] = y_all
    else:
        y_all = y_all + yf_ref[...]
        mean = _dot_seg(y_all, seg) * (1.0 / RW_DH)
        cen = y_all - mean
        var = _dot_seg(cen * cen, seg) * (1.0 / RW_DH)
        yn = cen * lax.rsqrt(var + RW_GN_EPS) * gnw_ref[...] + gnb_ref[...]
        k_other = k * (1.0 + (lr_gate(a_lat, 1 - d) - 1.0) * ka_ref[...])
        rk = _dot_seg(r * (k_dir + k_other) * rk_ref[...], seg)
        o_ref[...] = ((yn + rk * v) * _silu(ga_ref[...])).astype(o_ref.dtype)


def _segment_ones():
    h = jnp.arange(RW_WIDTH) // RW_DH
    return (h[:, None] == h[None, :]).astype(bf16)


def _rwkv_scan(ps, mu_prev, mu_next, w0, w_up, a0, a_up, k_k, k_a, *, reverse,
               r_k=None, gn_w=None, gn_b=None, gate=None, y_fwd=None, block_rows=256):
    bsz, seq, _ = ps.shape
    tb = block_rows
    nb = seq // tb
    final = reverse
    halo = tb // 8
    pos = (lambda j: nb - 1 - j) if reverse else (lambda j: j)
    blk = lambda b, j: (b, pos(j), 0)
    prev = lambda b, j: (b, jnp.maximum(pos(j) * halo - 1, 0), 0)
    nxt = lambda b, j: (b, jnp.minimum((pos(j) + 1) * halo, seq // 8 - 1), 0)
    full2 = lambda b, j: (0, 0)
    full3 = lambda b, j: (0, 0, 0)
    vec = lambda n: pl.BlockSpec((1, n), full2)
    in_specs = [
        pl.BlockSpec((None, tb, EVEN_SHIFT), blk),
        pl.BlockSpec((None, 8, EVEN_SHIFT), prev),
        pl.BlockSpec((None, 8, EVEN_SHIFT), nxt),
        vec(EVEN_SHIFT), vec(EVEN_SHIFT),
        pl.BlockSpec((2, RW_WIDTH), full2),
        pl.BlockSpec((2, RW_LORA, RW_WIDTH), full3),
        pl.BlockSpec((2, RW_WIDTH), full2),
        pl.BlockSpec((2, RW_LORA, RW_WIDTH), full3),
        vec(RW_WIDTH), vec(RW_WIDTH),
        pl.BlockSpec((RW_WIDTH, RW_WIDTH), full2),
    ]
    row = lambda t: t.reshape(1, -1)
    args = [ps, ps, ps, row(mu_prev), row(mu_next), w0, w_up, a0, a_up, row(k_k), row(k_a),
            _segment_ones()]
    if final:
        in_specs += [vec(RW_WIDTH), vec(RW_WIDTH), vec(RW_WIDTH),
                     pl.BlockSpec((None, tb, RW_WIDTH), blk),
                     pl.BlockSpec((None, tb, RW_WIDTH), blk)]
        args += [row(r_k), row(gn_w), row(gn_b), gate, y_fwd]
    return pl.pallas_call(
        functools.partial(_rwkv_kernel, reverse=reverse, final=final, block_rows=tb, nblocks=nb),
        grid=(bsz, nb),
        in_specs=in_specs,
        out_specs=pl.BlockSpec((None, tb, RW_WIDTH), blk),
        out_shape=jax.ShapeDtypeStruct((bsz, seq, RW_WIDTH), bf16 if final else f32),
        scratch_shapes=[pltpu.VMEM((tb, EVEN_SHIFT), f32),
                        pltpu.VMEM((RW_NGROUPS, RW_GW, RW_GW), f32)],
        compiler_params=_cparams("parallel", "arbitrary"),
        name="rwkv_bwd" if reverse else "rwkv_fwd",
    )(*args)


def _trunk(x, p):
    bsz, seq, _ = x.shape
    tokens = bsz * seq
    x2d = x.reshape(tokens, D_MODEL)
    seq3 = lambda t: t.reshape(bsz, seq, t.shape[-1])
    flat = lambda t: t.reshape(tokens, t.shape[-1])

    ps, ga, qkv, gb = _even_in(x2d, seq, p["even_norm"], p["even_w_in"])
    rw = (seq3(ps), p["mu_prev"], p["mu_next"], p["w0"], p["w_up"], p["a0"], p["a_up"],
          p["k_k"], p["k_a"])
    y_fwd = _rwkv_scan(*rw, reverse=False)
    ya = _rwkv_scan(*rw, reverse=True, r_k=p["r_k"], gn_w=p["gn_w"], gn_b=p["gn_b"],
                    gate=seq3(ga), y_fwd=y_fwd)
    yb = _attention(seq3(qkv), seq3(gb))

    x1, q, k, v, gate, gate_lat = _mid(x2d, flat(ya), flat(yb), p["even_w_out"], p["odd_norm"],
                                       p["odd_w_in"])
    gla = (seq3(q), seq3(k), seq3(v), seq3(gate_lat), p["gate_up"], p["gate_bias"])
    o_fwd = _gla_scan(*gla, reverse=False)
    yc = _gla_scan(*gla, reverse=True, gate=seq3(gate), o_fwd=o_fwd, norm_w=p["gla_norm"])
    return _final(x1, flat(yc), p["odd_w_out"], p["final_norm"]).reshape(bsz, seq, D_MODEL)


def _prepare(even_norm, even_w_in, even_mu_prev, even_mu_next, rwkv_w0, rwkv_w_up, rwkv_a0,
             rwkv_a_up, rwkv_k_k, rwkv_k_a, rwkv_r_k, rwkv_gn_w, rwkv_gn_b, even_w_out, odd_norm,
             odd_w_in, gla_gate_up, gla_gate_bias, gla_norm, odd_w_out, final_norm):
    wi = odd_w_in[0]
    lat0 = 2 * GLA_KEY + GLA_VAL
    lat = jnp.pad(wi[:, lat0:lat0 + GLA_RANK], ((0, 0), (0, LANES - GLA_RANK)))
    odd_in = jnp.concatenate([wi[:, :lat0], wi[:, lat0 + GLA_RANK:], lat], axis=1)
    return {
        "even_norm": even_norm[0], "even_w_in": even_w_in[0].astype(bf16),
        "mu_prev": even_mu_prev[0], "mu_next": even_mu_next[0],
        "w0": rwkv_w0[0], "w_up": rwkv_w_up[0], "a0": rwkv_a0[0], "a_up": rwkv_a_up[0],
        "k_k": rwkv_k_k[0], "k_a": rwkv_k_a[0], "r_k": rwkv_r_k[0],
        "gn_w": rwkv_gn_w[0], "gn_b": rwkv_gn_b[0],
        "even_w_out": even_w_out[0].astype(bf16),
        "odd_norm": odd_norm[0], "odd_w_in": odd_in.astype(bf16),
        "gate_up": jnp.pad(gla_gate_up[0], ((0, 0), (0, LANES - GLA_RANK), (0, 0))),
        "gate_bias": gla_gate_bias[0], "gla_norm": gla_norm[0],
        "odd_w_out": odd_w_out[0].astype(bf16), "final_norm": final_norm,
    }


def kernel(x_prompt, x_sample, even_norm, even_w_in, even_mu_prev, even_mu_next, rwkv_w0, rwkv_w_up,
           rwkv_a0, rwkv_a_up, rwkv_k_k, rwkv_k_a, rwkv_r_k, rwkv_gn_w, rwkv_gn_b, even_w_out,
           odd_norm, odd_w_in, gla_gate_up, gla_gate_bias, gla_norm, odd_w_out, final_norm):
    p = _prepare(even_norm, even_w_in, even_mu_prev, even_mu_next, rwkv_w0, rwkv_w_up, rwkv_a0,
                 rwkv_a_up, rwkv_k_k, rwkv_k_a, rwkv_r_k, rwkv_gn_w, rwkv_gn_b, even_w_out,
                 odd_norm, odd_w_in, gla_gate_up, gla_gate_bias, gla_norm, odd_w_out, final_norm)
    return (_trunk(x_prompt, p), _trunk(x_sample, p))
```

```python
import functools
import math

import jax
import jax.numpy as jnp
from jax import lax
from jax.experimental import pallas as pl
from jax.experimental.pallas import tpu as pltpu

f32 = jnp.float32
bf16 = jnp.bfloat16

D_MODEL = 1024
RMS_EPS = 1e-6

RW_HEADS = 8
RW_DH = 64
RW_WIDTH = RW_HEADS * RW_DH
RW_LORA = 64
RW_DECAY_SCALE = 0.6065306597126334
RW_GN_EPS = 64e-5
AT_HEADS = 8
AT_DH = 64
AT_WIDTH = AT_HEADS * AT_DH
AT_SIDE = 64
AT_DILATIONS = (1, 4, 16)
ROPE_THETA = 10000.0
EVEN_SHIFT = 3 * RW_WIDTH + 2 * RW_LORA
EVEN_COLS = EVEN_SHIFT + RW_WIDTH + 4 * AT_WIDTH

GLA_HEADS = 4
GLA_KEY = 512
GLA_VAL = 1024
GLA_HK = GLA_KEY // GLA_HEADS
GLA_HV = GLA_VAL // GLA_HEADS
GLA_RANK = 16
GLA_GATE_NORM = 16.0
ODD_COLS = 2 * GLA_KEY + GLA_VAL + GLA_RANK + GLA_VAL

CHUNK = 64
LANES = 128
VMEM_LIMIT = 56 * 1024 * 1024

_NT = (((1,), (1,)), ((), ()))
_TN = (((0,), (0,)), ((), ()))


def _dot(a, b):
    return jnp.dot(a, b, preferred_element_type=f32)


def _dot_nt(a, b):
    return lax.dot_general(a, b, _NT, preferred_element_type=f32)


def _dot_tn(a, b):
    return lax.dot_general(a, b, _TN, preferred_element_type=f32)


def _split2(x):
    hi = x.astype(bf16)
    lo = (x - hi.astype(f32)).astype(bf16)
    return hi, lo


def _dot_seg(x, e):
    hi, lo = _split2(x)
    return _dot(hi, e) + _dot(lo, e)


def _cumsum_rows(tri, x):
    hi, lo = _split2(x)
    return _dot(tri, hi) + _dot(tri, lo)


def _dot3(a, b):
    ah, al = _split2(a)
    bh, bl = _split2(b)
    return _dot(ah, bh) + (_dot(ah, bl) + _dot(al, bh))


def _sigmoid(x):
    return 0.5 * (jnp.tanh(0.5 * x) + 1.0)


def _silu(x):
    return x * _sigmoid(x)


def _rms_rows(x, w):
    return x * lax.rsqrt(jnp.mean(x * x, axis=-1, keepdims=True) + RMS_EPS) * w


def _cparams(*sem):
    return pltpu.CompilerParams(dimension_semantics=sem, vmem_limit_bytes=VMEM_LIMIT)


def _rope_partner(x):
    lane = lax.broadcasted_iota(jnp.int32, x.shape, 1)
    return jnp.where((lane & 32) == 0, pltpu.roll(x, LANES - 32, 1), pltpu.roll(x, 32, 1))


def _even_in_kernel(x_ref, nw_ref, w_ref, cos_ref, sin_ref, ps_ref, ga_ref, qkv_ref, gb_ref):
    xn = _rms_rows(x_ref[...], nw_ref[...]).astype(bf16)

    def cols(c0, c1):
        return _dot(xn, w_ref[:, c0:c1])

    for c0 in range(0, EVEN_SHIFT, 512):
        c1 = min(c0 + 512, EVEN_SHIFT)
        ps_ref[:, c0:c1] = cols(c0, c1)
    base = EVEN_SHIFT
    ga_ref[...] = cols(base, base + RW_WIDTH)
    base += RW_WIDTH
    cos = cos_ref[...]
    sin = sin_ref[...]
    for j in range(2 * AT_WIDTH // LANES):
        t = cols(base + j * LANES, base + (j + 1) * LANES)
        qkv_ref[:, j * LANES:(j + 1) * LANES] = t * cos + _rope_partner(t) * sin
    base += 2 * AT_WIDTH
    qkv_ref[:, 2 * AT_WIDTH:3 * AT_WIDTH] = cols(base, base + AT_WIDTH)
    base += AT_WIDTH
    gb_ref[...] = cols(base, base + AT_WIDTH)


def _rope_tables(seq):
    inv = ROPE_THETA ** (-jnp.arange(0, AT_DH, 2, dtype=f32) / AT_DH)
    ang = jnp.arange(seq, dtype=f32)[:, None] * inv[None, :]
    cos, sin = jnp.cos(ang), jnp.sin(ang)
    return (jnp.concatenate([cos, cos, cos, cos], axis=-1),
            jnp.concatenate([-sin, sin, -sin, sin], axis=-1))


def _even_in(x2d, seq, norm_w, w_in_bf16, block_rows=512):
    m = x2d.shape[0]
    tm = block_rows
    per_seq = seq // tm
    cos, sin = _rope_tables(seq)
    row = lambda i: (i, 0)
    full = lambda i: (0, 0)
    tab = lambda i: (i % per_seq, 0)
    return pl.pallas_call(
        _even_in_kernel,
        grid=(m // tm,),
        in_specs=[
            pl.BlockSpec((tm, D_MODEL), row),
            pl.BlockSpec((1, D_MODEL), full),
            pl.BlockSpec((D_MODEL, EVEN_COLS), full),
            pl.BlockSpec((tm, LANES), tab),
            pl.BlockSpec((tm, LANES), tab),
        ],
        out_specs=[
            pl.BlockSpec((tm, EVEN_SHIFT), row),
            pl.BlockSpec((tm, RW_WIDTH), row),
            pl.BlockSpec((tm, 3 * AT_WIDTH), row),
            pl.BlockSpec((tm, AT_WIDTH), row),
        ],
        out_shape=[
            jax.ShapeDtypeStruct((m, EVEN_SHIFT), f32),
            jax.ShapeDtypeStruct((m, RW_WIDTH), f32),
            jax.ShapeDtypeStruct((m, 3 * AT_WIDTH), f32),
            jax.ShapeDtypeStruct((m, AT_WIDTH), f32),
        ],
        compiler_params=_cparams("parallel"),
        name="even_in",
    )(x2d, norm_w.reshape(1, D_MODEL), w_in_bf16, cos, sin)


ODD_PAD_COLS = 2 * GLA_KEY + 2 * GLA_VAL + LANES


def _mid_kernel(x_ref, ya_ref, yb_ref, wo_ref, nw_ref, wi_ref,
                x1_ref, q_ref, k_ref, v_ref, g_ref, gl_ref):
    x1 = (x_ref[...] + _dot(ya_ref[...], wo_ref[0:RW_WIDTH, :])
          + _dot(yb_ref[...], wo_ref[RW_WIDTH:RW_WIDTH + AT_WIDTH, :]))
    x1_ref[...] = x1
    xn = _rms_rows(x1, nw_ref[...]).astype(bf16)
    c = 0
    for ref, width in ((q_ref, GLA_KEY), (k_ref, GLA_KEY), (v_ref, GLA_VAL), (g_ref, GLA_VAL),
                       (gl_ref, LANES)):
        for c0 in range(0, width, 512):
            c1 = min(c0 + 512, width)
            ref[:, c0:c1] = _dot(xn, wi_ref[:, c + c0:c + c1])
        c += width


def _mid(x2d, ya, yb, w_out_bf16, norm_w, w_in_pad_bf16, block_rows=512):
    m = x2d.shape[0]
    tm = block_rows
    row = lambda i: (i, 0)
    full = lambda i: (0, 0)
    widths = (D_MODEL, GLA_KEY, GLA_KEY, GLA_VAL, GLA_VAL, LANES)
    return pl.pallas_call(
        _mid_kernel,
        grid=(m // tm,),
        in_specs=[
            pl.BlockSpec((tm, D_MODEL), row),
            pl.BlockSpec((tm, RW_WIDTH), row),
            pl.BlockSpec((tm, AT_WIDTH), row),
            pl.BlockSpec((RW_WIDTH + AT_WIDTH, D_MODEL), full),
            pl.BlockSpec((1, D_MODEL), full),
            pl.BlockSpec((D_MODEL, ODD_PAD_COLS), full),
        ],
        out_specs=[pl.BlockSpec((tm, w), row) for w in widths],
        out_shape=[jax.ShapeDtypeStruct((m, w), f32) for w in widths],
        compiler_params=_cparams("parallel"),
        name="mid",
    )(x2d, ya, yb, w_out_bf16, norm_w.reshape(1, D_MODEL), w_in_pad_bf16)


def _final_kernel(x_ref, y_ref, wo_ref, nw_ref, o_ref):
    x2 = x_ref[...] + _dot(y_ref[...], wo_ref[...])
    o_ref[...] = _rms_rows(x2, nw_ref[...])


def _final(x2d, yc, w_out_bf16, norm_w, block_rows=512):
    m = x2d.shape[0]
    tm = block_rows
    row = lambda i: (i, 0)
    full = lambda i: (0, 0)
    return pl.pallas_call(
        _final_kernel,
        grid=(m // tm,),
        in_specs=[
            pl.BlockSpec((tm, D_MODEL), row),
            pl.BlockSpec((tm, GLA_VAL), row),
            pl.BlockSpec((GLA_VAL, D_MODEL), full),
            pl.BlockSpec((1, D_MODEL), full),
        ],
        out_specs=pl.BlockSpec((tm, D_MODEL), row),
        out_shape=jax.ShapeDtypeStruct((m, D_MODEL), f32),
        compiler_params=_cparams("parallel"),
        name="final",
    )(x2d, yc, w_out_bf16, norm_w.reshape(1, D_MODEL))


def _scan_masks(n, reverse):
    t = lax.broadcasted_iota(jnp.int32, (n, n), 0) % CHUNK
    s = lax.broadcasted_iota(jnp.int32, (n, n), 1) % CHUNK
    if reverse:
        return s >= t, s > t
    return s <= t, s < t


def _row_to_col(row):
    n = row.shape[1]
    eye = (lax.broadcasted_iota(jnp.int32, (n, n), 0) == lax.broadcasted_iota(jnp.int32, (n, n), 1))
    return jnp.sum(jnp.where(eye, jnp.broadcast_to(row, (n, n)), 0.0), axis=1, keepdims=True)


def _gla_kernel(*refs, reverse, final, block_rows):
    if final:
        (q_ref, k_ref, v_ref, gl_ref, gup_ref, gb_ref, gate_ref, of_ref, nw_ref, o_ref, st_ref) = refs
    else:
        (q_ref, k_ref, v_ref, gl_ref, gup_ref, gb_ref, o_ref, st_ref) = refs
    d = 1 if reverse else 0

    @pl.when(pl.program_id(1) == 0)
    def _():
        st_ref[...] = jnp.zeros_like(st_ref)

    incl, _ = _scan_masks(CHUNK, reverse)
    tri = incl.astype(bf16)
    gup = gup_ref[d]
    gbias = gb_ref[d:d + 1, :]
    nchunks = block_rows // CHUNK
    last = 0 if reverse else CHUNK - 1

    x = _dot3(gl_ref[...], gup) + gbias
    g = (jnp.minimum(x, 0.0) - jnp.log(1.0 + jnp.exp(-jnp.abs(x)))) / GLA_GATE_NORM
    chunk_rows = [slice(c * CHUNK, (c + 1) * CHUNK) for c in range(nchunks)]
    bcum = jnp.concatenate([_cumsum_rows(tri, g[rs]) for rs in chunk_rows], axis=0)
    b_last = [bcum[c * CHUNK + last:c * CHUNK + last + 1, :] for c in range(nchunks)]
    b_end = jnp.concatenate([jnp.broadcast_to(b, (CHUNK, GLA_KEY)) for b in b_last], axis=0)
    q = q_ref[...] * (GLA_HK ** -0.5)
    k = k_ref[...]
    q_dec = (q * jnp.exp(bcum)).astype(bf16)
    k_dec = (k * jnp.exp(-bcum)).astype(bf16)
    k_end = (k * jnp.exp(b_end - bcum)).astype(bf16)

    order = list(reversed(range(nchunks))) if reverse else list(range(nchunks))
    klanes = [slice(h * GLA_HK, (h + 1) * GLA_HK) for h in range(GLA_HEADS)]
    vlanes = [slice(h * GLA_HV, (h + 1) * GLA_HV) for h in range(GLA_HEADS)]
    units = [(c, h) for c in order for h in range(GLA_HEADS)]
    v = [v_ref[chunk_rows[c], vlanes[h]].astype(bf16) for c, h in units]
    att = [jnp.where(incl, _dot_nt(q_dec[chunk_rows[c], klanes[h]], k_dec[chunk_rows[c], klanes[h]]),
                     0.0).astype(bf16) for c, h in units]
    o_intra = [_dot(att[i], v[i]) for i in range(len(units))]
    kv = [_dot_tn(k_end[chunk_rows[c], klanes[h]], v[i]) for i, (c, h) in enumerate(units)]
    dcol = [_row_to_col(jnp.exp(b_last[c][:, klanes[h]])) for c, h in units]

    state = [st_ref[h] for h in range(GLA_HEADS)]
    for i, (c, h) in enumerate(units):
        rows = chunk_rows[c]
        o = o_intra[i] + _dot(q_dec[rows, klanes[h]], state[h].astype(bf16))
        state[h] = state[h] * dcol[i] + kv[i]
        if final:
            o = o + of_ref[rows, vlanes[h]]
            o = o * lax.rsqrt(jnp.mean(o * o, axis=-1, keepdims=True) + RMS_EPS) * nw_ref[...]
            o_ref[rows, vlanes[h]] = (o * _silu(gate_ref[rows, vlanes[h]])).astype(o_ref.dtype)
        else:
            o_ref[rows, vlanes[h]] = o
    for h in range(GLA_HEADS):
        st_ref[h] = state[h]


def _gla_scan(q, k, v, gl, gate_up_pad, gate_bias, *, reverse, gate=None, o_fwd=None, norm_w=None,
              block_rows=256):
    bsz, seq, _ = q.shape
    tb = block_rows
    nb = seq // tb
    final = reverse
    blk = (lambda b, j: (b, nb - 1 - j, 0)) if reverse else (lambda b, j: (b, j, 0))
    full2 = lambda b, j: (0, 0)
    full3 = lambda b, j: (0, 0, 0)
    in_specs = [
        pl.BlockSpec((None, tb, GLA_KEY), blk),
        pl.BlockSpec((None, tb, GLA_KEY), blk),
        pl.BlockSpec((None, tb, GLA_VAL), blk),
        pl.BlockSpec((None, tb, LANES), blk),
        pl.BlockSpec((2, LANES, GLA_KEY), full3),
        pl.BlockSpec((2, GLA_KEY), full2),
    ]
    args = [q, k, v, gl, gate_up_pad, gate_bias]
    if final:
        in_specs += [
            pl.BlockSpec((None, tb, GLA_VAL), blk),
            pl.BlockSpec((None, tb, GLA_VAL), blk),
            pl.BlockSpec((1, GLA_HV), full2),
        ]
        args += [gate, o_fwd, norm_w.reshape(1, GLA_HV)]
    return pl.pallas_call(
        functools.partial(_gla_kernel, reverse=reverse, final=final, block_rows=tb),
        grid=(bsz, nb),
        in_specs=in_specs,
        out_specs=pl.BlockSpec((None, tb, GLA_VAL), blk),
        out_shape=jax.ShapeDtypeStruct((bsz, seq, GLA_VAL), bf16 if final else f32),
        scratch_shapes=[pltpu.VMEM((GLA_HEADS, GLA_HK, GLA_HV), f32)],
        compiler_params=_cparams("parallel", "arbitrary"),
        name="gla_bwd" if reverse else "gla_fwd",
    )(*args)


AT_NEG = -1e30
AT_QBLK = 128
AT_KBLK = 256
AT_UNROLL = 4


def _attn_kernel(q_ref, k_ref, v_ref, g_ref, o_ref, qp, kp, vp, m_s, l_s, acc_s, *, seq):
    lane = lax.broadcasted_iota(jnp.int32, (1, LANES), 1)
    left = lane < AT_DH
    scale = AT_DH ** -0.5

    for dil in AT_DILATIONS:
        sub = seq // dil
        tq = min(AT_QBLK, sub)
        nk = min(AT_KBLK, sub)
        piece = min(sub, 512)
        pieces = sub // piece

        def gather(i, carry, dil=dil, sub=sub, piece=piece, pieces=pieces):
            r = i // pieces
            part = i % pieces
            src = r + dil * part * piece
            dst = pl.multiple_of(r * sub + part * piece, piece)
            idx = pl.ds(src, piece, stride=dil) if dil > 1 else pl.ds(src, piece)
            qp[pl.ds(dst, piece), :] = (q_ref[idx, :] * scale).astype(bf16)
            kp[pl.ds(dst, piece), :] = k_ref[idx, :].astype(bf16)
            vp[pl.ds(dst, piece), :] = v_ref[idx, :].astype(bf16)
            return carry

        lax.fori_loop(0, dil * pieces, gather, 0)

        nblk = sub // tq
        offs = (lax.broadcasted_iota(jnp.int32, (tq, nk), 0)
                - lax.broadcasted_iota(jnp.int32, (tq, nk), 1))

        def blocks(it, carry, dil=dil, sub=sub, tq=tq, nk=nk, nblk=nblk, offs=offs):
            units = []
            for u in range(AT_UNROLL):
                i = it * AT_UNROLL + u
                r = i // nblk
                m0 = (i % nblk) * tq
                ks = jnp.clip(m0 - AT_SIDE, 0, sub - nk)
                keep = jnp.abs(offs + (m0 - ks)) <= AT_SIDE
                qb = qp[pl.ds(pl.multiple_of(r * sub + m0, AT_SIDE), tq), :]
                kb = kp[pl.ds(pl.multiple_of(r * sub + ks, AT_SIDE), nk), :]
                vb = vp[pl.ds(pl.multiple_of(r * sub + ks, AT_SIDE), nk), :]
                pos = r + dil * m0
                if dil == 1:
                    idx = pl.ds(pl.multiple_of(pos, tq), tq)
                else:
                    idx = pl.ds(pos, tq, stride=dil)
                units.append((keep, qb, kb, vb, idx))
            heads = [(u, first) for u in range(AT_UNROLL) for first in (True, False)]
            scores = []
            for u, first in heads:
                keep, qb, kb, _, _ = units[u]
                qh = jnp.where(left if first else ~left, qb, jnp.zeros_like(qb))
                scores.append(jnp.where(keep, _dot_nt(qh, kb), AT_NEG))
            maxes = [jnp.max(s, axis=1, keepdims=True) for s in scores]
            probs = [jnp.exp(s - mh) for s, mh in zip(scores, maxes)]
            sums = [jnp.sum(p, axis=1, keepdims=True) for p in probs]
            outs = [_dot(p.astype(bf16), units[u][3]) for p, (u, _) in zip(probs, heads)]
            old = None
            if dil > 1:
                old = [(m_s[units[u][4], :], l_s[units[u][4], :], acc_s[units[u][4], :])
                       for u in range(AT_UNROLL)]
            for u in range(AT_UNROLL):
                idx = units[u][4]
                m_new = jnp.where(left, maxes[2 * u], maxes[2 * u + 1])
                l_new = jnp.where(left, sums[2 * u], sums[2 * u + 1])
                a_new = jnp.where(left, outs[2 * u], outs[2 * u + 1])
                if dil == 1:
                    m_s[idx, :] = m_new
                    l_s[idx, :] = l_new
                    acc_s[idx, :] = a_new
                else:
                    m_old, l_old, a_old = old[u]
                    m = jnp.maximum(m_old, m_new)
                    w_old = jnp.exp(m_old - m)
                    w_new = jnp.exp(m_new - m)
                    m_s[idx, :] = m
                    l_s[idx, :] = l_old * w_old + l_new * w_new
                    acc_s[idx, :] = a_old * w_old + a_new * w_new
            return carry

        lax.fori_loop(0, dil * nblk // AT_UNROLL, blocks, 0)

    rows = min(seq, 512)

    def finish(i, carry):
        idx = pl.ds(pl.multiple_of(i * rows, rows), rows)
        y = acc_s[idx, :] / l_s[idx, :]
        o_ref[idx, :] = (y * _silu(g_ref[idx, :])).astype(o_ref.dtype)
        return carry

    lax.fori_loop(0, seq // rows, finish, 0)


def _attention(qkv, gate):
    bsz, seq, _ = qkv.shape
    pairs = AT_WIDTH // LANES
    once = pl.Buffered(1)
    col = lambda off: (lambda b, p: (b, 0, off + p))
    return pl.pallas_call(
        functools.partial(_attn_kernel, seq=seq),
        grid=(bsz, pairs),
        in_specs=[
            pl.BlockSpec((None, seq, LANES), col(0), pipeline_mode=once),
            pl.BlockSpec((None, seq, LANES), col(pairs), pipeline_mode=once),
            pl.BlockSpec((None, seq, LANES), col(2 * pairs), pipeline_mode=once),
            pl.BlockSpec((None, seq, LANES), col(0), pipeline_mode=once),
        ],
        out_specs=pl.BlockSpec((None, seq, LANES), col(0)),
        out_shape=jax.ShapeDtypeStruct((bsz, seq, AT_WIDTH), bf16),
        scratch_shapes=[pltpu.VMEM((seq, LANES), bf16)] * 3 + [pltpu.VMEM((seq, LANES), f32)] * 3,
        compiler_params=_cparams("parallel", "parallel"),
        name="dilated_attn",
    )(qkv, qkv, qkv, gate)


RW_GROUP = 2
RW_GW = RW_GROUP * RW_DH
RW_NGROUPS = RW_HEADS // RW_GROUP


def _head_stack(x, masks):
    return jnp.concatenate([jnp.where(mk, x, jnp.zeros_like(x)) for mk in masks], axis=0)


def _unit_lower_inverses(mats):
    n = mats[0].shape[0]
    eye = (lax.broadcasted_iota(jnp.int32, (n, n), 0)
           == lax.broadcasted_iota(jnp.int32, (n, n), 1)).astype(f32)
    ts = [eye + a for a in mats]
    qs = [a.astype(bf16) for a in mats]
    qs = [_dot(q, q).astype(bf16) for q in qs]
    steps = int(math.log2(CHUNK)) - 1
    for step in range(steps):
        if step < steps - 1:
            both = [_dot(q, jnp.concatenate([q, t.astype(bf16)], axis=1)) for q, t in zip(qs, ts)]
            qs = [x[:, :n].astype(bf16) for x in both]
            ts = [t + x[:, n:] for t, x in zip(ts, both)]
        else:
            ts = [t + _dot(q, t.astype(bf16)) for q, t in zip(qs, ts)]
    return [t.astype(bf16) for t in ts]


def _rwkv_kernel(*refs, reverse, final, block_rows, nblocks):
    (ps_ref, prev_ref, next_ref, mup_ref, mun_ref, w0_ref, wup_ref, a0_ref, aup_ref,
     kk_ref, ka_ref, seg_ref) = refs[:12]
    if final:
        rk_ref, gnw_ref, gnb_ref, ga_ref, yf_ref, o_ref, sh_ref, st_ref = refs[12:]
    else:
        o_ref, sh_ref, st_ref = refs[12:]
    d = 1 if reverse else 0
    tb = block_rows
    j = pl.program_id(1)
    blk = (nblocks - 1 - j) if reverse else j

    @pl.when(j == 0)
    def _():
        st_ref[...] = jnp.zeros_like(st_ref)

    rid = lax.broadcasted_iota(jnp.int32, (tb, 1), 0)
    has_prev = (blk > 0).astype(f32)
    has_next = (blk < nblocks - 1).astype(f32)
    for c0 in range(0, EVEN_SHIFT, 256):
        c1 = min(c0 + 256, EVEN_SHIFT)
        x = ps_ref[:, c0:c1]
        before = prev_ref[7:8, c0:c1] * has_prev
        after = next_ref[0:1, c0:c1] * has_next
        prv = jnp.where(rid == 0, before, pltpu.roll(x, 1, 0))
        nxt = jnp.where(rid == tb - 1, after, pltpu.roll(x, tb - 1, 0))
        sh_ref[:, c0:c1] = x + mup_ref[:, c0:c1] * (prv - x) + mun_ref[:, c0:c1] * (nxt - x)

    incl1, _ = _scan_masks(CHUNK, reverse)
    tri = incl1.astype(bf16)
    gi = lax.broadcasted_iota(jnp.int32, (2 * RW_GW, 2 * RW_GW), 0)
    ahead = (lax.broadcasted_iota(jnp.int32, (2 * RW_GW, 2 * RW_GW), 1) % CHUNK) - (gi % CHUNK)
    ahead = -ahead if reverse else ahead
    keep = ahead < jnp.where(gi < RW_GW, 0, 1)
    lane = lax.broadcasted_iota(jnp.int32, (1, RW_GW), 1)
    masks = [(lane // RW_DH) == h for h in range(RW_GROUP)]
    seg = seg_ref[...]
    last = 0 if reverse else CHUNK - 1
    nchunks = tb // CHUNK
    w_lo, a_lo = 3 * RW_WIDTH, 3 * RW_WIDTH + RW_LORA

    def lr_gate(a_lat, dd):
        return _sigmoid(a0_ref[dd:dd + 1, :] + _dot3(a_lat, aup_ref[dd]))

    r = sh_ref[:, 0:RW_WIDTH]
    k = sh_ref[:, RW_WIDTH:2 * RW_WIDTH]
    v = sh_ref[:, 2 * RW_WIDTH:3 * RW_WIDTH]
    w_lat = sh_ref[:, w_lo:w_lo + RW_LORA]
    a_lat = sh_ref[:, a_lo:a_lo + RW_LORA]
    w_log = -RW_DECAY_SCALE * _sigmoid(w0_ref[d:d + 1, :] + _dot3(jnp.tanh(w_lat), wup_ref[d]))
    lr = lr_gate(a_lat, d)
    kk = k * kk_ref[...]
    kk = kk / jnp.maximum(jnp.sqrt(_dot_seg(kk * kk, seg)), 1e-12)
    k_dir = k * (1.0 + (lr - 1.0) * ka_ref[...])
    chunk_rows = [slice(c * CHUNK, (c + 1) * CHUNK) for c in range(nchunks)]
    cum = jnp.concatenate([_cumsum_rows(tri, w_log[rs]) for rs in chunk_rows], axis=0)
    grow = jnp.exp(-cum)
    a_t = -kk * jnp.exp(cum - w_log)
    b_t = kk * lr * grow
    k_t = k_dir * grow
    r_t = r * jnp.exp(cum)

    order = list(reversed(range(nchunks))) if reverse else list(range(nchunks))
    lanes = [slice(g * RW_GW, (g + 1) * RW_GW) for g in range(RW_NGROUPS)]
    units = [(c, g) for c in order for g in range(RW_NGROUPS)]
    stack = lambda x: [_head_stack(x[chunk_rows[c], lanes[g]], masks) for c, g in units]
    a_s = [t.astype(bf16) for t in stack(a_t)]
    r_s = [t.astype(bf16) for t in stack(r_t)]
    b_s = [t.astype(bf16) for t in stack(b_t)]
    k_s = [t.astype(bf16) for t in stack(k_t)]
    v_f = stack(v)
    v_st = [t.T.astype(bf16) for t in v_f]
    v_s = [t.astype(bf16) for t in v_f]
    n = range(len(units))
    gram = [jnp.where(keep, _dot_nt(jnp.concatenate([a_s[i], r_s[i]], axis=0),
                                     jnp.concatenate([b_s[i], k_s[i]], axis=0)), 0.0) for i in n]
    a_ak = [gram[i][0:RW_GW, RW_GW:].astype(bf16) for i in n]
    a_rb = [gram[i][RW_GW:, 0:RW_GW].astype(bf16) for i in n]
    a_rk = [gram[i][RW_GW:, RW_GW:].astype(bf16) for i in n]
    t_inv = _unit_lower_inverses([gram[i][0:RW_GW, 0:RW_GW] for i in n])
    w_m = [_dot(t_inv[i], a_s[i]).astype(bf16) for i in n]
    av_t = [_dot_nt(v_st[i], a_ak[i]).astype(bf16) for i in n]
    z_t = [_dot_nt(av_t[i], t_inv[i]) for i in n]
    vk = [_dot(v_st[i], k_s[i]) for i in n]
    y_local = [_dot(a_rk[i], v_s[i]) for i in n]

    state = [st_ref[g] for g in range(RW_NGROUPS)]
    y_rows = {}
    for pos_c, c in enumerate(order):
        p_end = jnp.exp(cum[c * CHUNK + last:c * CHUNK + last + 1, :])
        ys = []
        for g in range(RW_NGROUPS):
            i = pos_c * RW_NGROUPS + g
            s0 = state[g]
            s0b = s0.astype(bf16)
            u_t = (_dot_nt(s0b, w_m[i]) + z_t[i]).astype(bf16)
            state[g] = (s0 + _dot(u_t, b_s[i]) + vk[i]) * p_end[:, lanes[g]]
            y = _dot_nt(r_s[i], s0b) + _dot_nt(a_rb[i], u_t) + y_local[i]
            yg = y[0:CHUNK]
            for h in range(1, RW_GROUP):
                yg = yg + y[h * CHUNK:(h + 1) * CHUNK]
            ys.append(yg)
        y_rows[c] = jnp.concatenate(ys, axis=1)
    for g in range(RW_NGROUPS):
        st_ref[g] = state[g]
    y_all = jnp.concatenate([y_rows[c] for c in range(nchunks)], axis=0)

    if not final:
        o_ref[...] = y_all
    else:
        y_all = y_all + yf_ref[...]
        mean = _dot_seg(y_all, seg) * (1.0 / RW_DH)
        cen = y_all - mean
        var = _dot_seg(cen * cen, seg) * (1.0 / RW_DH)
        yn = cen * lax.rsqrt(var + RW_GN_EPS) * gnw_ref[...] + gnb_ref[...]
        k_other = k * (1.0 + (lr_gate(a_lat, 1 - d) - 1.0) * ka_ref[...])
        rk = _dot_seg(r * (k_dir + k_other) * rk_ref[...], seg)
        o_ref[...] = ((yn + rk * v) * _silu(ga_ref[...])).astype(o_ref.dtype)


def _segment_ones():
    h = jnp.arange(RW_WIDTH) // RW_DH
    return (h[:, None] == h[None, :]).astype(bf16)


def _rwkv_scan(ps, mu_prev, mu_next, w0, w_up, a0, a_up, k_k, k_a, *, reverse,
               r_k=None, gn_w=None, gn_b=None, gate=None, y_fwd=None, block_rows=256):
    bsz, seq, _ = ps.shape
    tb = block_rows
    nb = seq // tb
    final = reverse
    halo = tb // 8
    pos = (lambda j: nb - 1 - j) if reverse else (lambda j: j)
    blk = lambda b, j: (b, pos(j), 0)
    prev = lambda b, j: (b, jnp.maximum(pos(j) * halo - 1, 0), 0)
    nxt = lambda b, j: (b, jnp.minimum((pos(j) + 1) * halo, seq // 8 - 1), 0)
    full2 = lambda b, j: (0, 0)
    full3 = lambda b, j: (0, 0, 0)
    vec = lambda n: pl.BlockSpec((1, n), full2)
    in_specs = [
        pl.BlockSpec((None, tb, EVEN_SHIFT), blk),
        pl.BlockSpec((None, 8, EVEN_SHIFT), prev),
        pl.BlockSpec((None, 8, EVEN_SHIFT), nxt),
        vec(EVEN_SHIFT), vec(EVEN_SHIFT),
        pl.BlockSpec((2, RW_WIDTH), full2),
        pl.BlockSpec((2, RW_LORA, RW_WIDTH), full3),
        pl.BlockSpec((2, RW_WIDTH), full2),
        pl.BlockSpec((2, RW_LORA, RW_WIDTH), full3),
        vec(RW_WIDTH), vec(RW_WIDTH),
        pl.BlockSpec((RW_WIDTH, RW_WIDTH), full2),
    ]
    row = lambda t: t.reshape(1, -1)
    args = [ps, ps, ps, row(mu_prev), row(mu_next), w0, w_up, a0, a_up, row(k_k), row(k_a),
            _segment_ones()]
    if final:
        in_specs += [vec(RW_WIDTH), vec(RW_WIDTH), vec(RW_WIDTH),
                     pl.BlockSpec((None, tb, RW_WIDTH), blk),
                     pl.BlockSpec((None, tb, RW_WIDTH), blk)]
        args += [row(r_k), row(gn_w), row(gn_b), gate, y_fwd]
    return pl.pallas_call(
        functools.partial(_rwkv_kernel, reverse=reverse, final=final, block_rows=tb, nblocks=nb),
        grid=(bsz, nb),
        in_specs=in_specs,
        out_specs=pl.BlockSpec((None, tb, RW_WIDTH), blk),
        out_shape=jax.ShapeDtypeStruct((bsz, seq, RW_WIDTH), bf16 if final else f32),
        scratch_shapes=[pltpu.VMEM((tb, EVEN_SHIFT), f32),
                        pltpu.VMEM((RW_NGROUPS, RW_GW, RW_GW), f32)],
        compiler_params=_cparams("parallel", "arbitrary"),
        name="rwkv_bwd" if reverse else "rwkv_fwd",
    )(*args)


def _trunk(x, p):
    bsz, seq, _ = x.shape
    tokens = bsz * seq
    x2d = x.reshape(tokens, D_MODEL)
    seq3 = lambda t: t.reshape(bsz, seq, t.shape[-1])
    flat = lambda t: t.reshape(tokens, t.shape[-1])

    ps, ga, qkv, gb = _even_in(x2d, seq, p["even_norm"], p["even_w_in"])
    rw = (seq3(ps), p["mu_prev"], p["mu_next"], p["w0"], p["w_up"], p["a0"], p["a_up"],
          p["k_k"], p["k_a"])
    y_fwd = _rwkv_scan(*rw, reverse=False)
    ya = _rwkv_scan(*rw, reverse=True, r_k=p["r_k"], gn_w=p["gn_w"], gn_b=p["gn_b"],
                    gate=seq3(ga), y_fwd=y_fwd)
    yb = _attention(seq3(qkv), seq3(gb))

    x1, q, k, v, gate, gate_lat = _mid(x2d, flat(ya), flat(yb), p["even_w_out"], p["odd_norm"],
                                       p["odd_w_in"])
    gla = (seq3(q), seq3(k), seq3(v), seq3(gate_lat), p["gate_up"], p["gate_bias"])
    o_fwd = _gla_scan(*gla, reverse=False)
    yc = _gla_scan(*gla, reverse=True, gate=seq3(gate), o_fwd=o_fwd, norm_w=p["gla_norm"])
    return _final(x1, flat(yc), p["odd_w_out"], p["final_norm"]).reshape(bsz, seq, D_MODEL)


def _prepare(even_norm, even_w_in, even_mu_prev, even_mu_next, rwkv_w0, rwkv_w_up, rwkv_a0,
             rwkv_a_up, rwkv_k_k, rwkv_k_a, rwkv_r_k, rwkv_gn_w, rwkv_gn_b, even_w_out, odd_norm,
             odd_w_in, gla_gate_up, gla_gate_bias, gla_norm, odd_w_out, final_norm):
    wi = odd_w_in[0]
    lat0 = 2 * GLA_KEY + GLA_VAL
    lat = jnp.pad(wi[:, lat0:lat0 + GLA_RANK], ((0, 0), (0, LANES - GLA_RANK)))
    odd_in = jnp.concatenate([wi[:, :lat0], wi[:, lat0 + GLA_RANK:], lat], axis=1)
    return {
        "even_norm": even_norm[0], "even_w_in": even_w_in[0].astype(bf16),
        "mu_prev": even_mu_prev[0], "mu_next": even_mu_next[0],
        "w0": rwkv_w0[0], "w_up": rwkv_w_up[0], "a0": rwkv_a0[0], "a_up": rwkv_a_up[0],
        "k_k": rwkv_k_k[0], "k_a": rwkv_k_a[0], "r_k": rwkv_r_k[0],
        "gn_w": rwkv_gn_w[0], "gn_b": rwkv_gn_b[0],
        "even_w_out": even_w_out[0].astype(bf16),
        "odd_norm": odd_norm[0], "odd_w_in": odd_in.astype(bf16),
        "gate_up": jnp.pad(gla_gate_up[0], ((0, 0), (0, LANES - GLA_RANK), (0, 0))),
        "gate_bias": gla_gate_bias[0], "gla_norm": gla_norm[0],
        "odd_w_out": odd_w_out[0].astype(bf16), "final_norm": final_norm,
    }


def kernel(x_prompt, x_sample, even_norm, even_w_in, even_mu_prev, even_mu_next, rwkv_w0, rwkv_w_up,
           rwkv_a0, rwkv_a_up, rwkv_k_k, rwkv_k_a, rwkv_r_k, rwkv_gn_w, rwkv_gn_b, even_w_out,
           odd_norm, odd_w_in, gla_gate_up, gla_gate_bias, gla_norm, odd_w_out, final_norm):
    p = _prepare(even_norm, even_w_in, even_mu_prev, even_mu_next, rwkv_w0, rwkv_w_up, rwkv_a0,
                 rwkv_a_up, rwkv_k_k, rwkv_k_a, rwkv_r_k, rwkv_gn_w, rwkv_gn_b, even_w_out,
                 odd_norm, odd_w_in, gla_gate_up, gla_gate_bias, gla_norm, odd_w_out, final_norm)
    return (_trunk(x_prompt, p), _trunk(x_sample, p))
```

```python
import functools
import math

import jax
import jax.numpy as jnp
from jax import lax
from jax.experimental import pallas as pl
from jax.experimental.pallas import tpu as pltpu

f32 = jnp.float32
bf16 = jnp.bfloat16

D_MODEL = 1024
RMS_EPS = 1e-6

RW_HEADS = 8
RW_DH = 64
RW_WIDTH = RW_HEADS * RW_DH
RW_LORA = 64
RW_DECAY_SCALE = 0.6065306597126334
RW_GN_EPS = 64e-5
AT_HEADS = 8
AT_DH = 64
AT_WIDTH = AT_HEADS * AT_DH
AT_SIDE = 64
AT_DILATIONS = (1, 4, 16)
ROPE_THETA = 10000.0
EVEN_SHIFT = 3 * RW_WIDTH + 2 * RW_LORA
EVEN_COLS = EVEN_SHIFT + RW_WIDTH + 4 * AT_WIDTH

GLA_HEADS = 4
GLA_KEY = 512
GLA_VAL = 1024
GLA_HK = GLA_KEY // GLA_HEADS
GLA_HV = GLA_VAL // GLA_HEADS
GLA_RANK = 16
GLA_GATE_NORM = 16.0
ODD_COLS = 2 * GLA_KEY + GLA_VAL + GLA_RANK + GLA_VAL

CHUNK = 64
LANES = 128
VMEM_LIMIT = 56 * 1024 * 1024

_NT = (((1,), (1,)), ((), ()))
_TN = (((0,), (0,)), ((), ()))


def _dot(a, b):
    return jnp.dot(a, b, preferred_element_type=f32)


def _dot_nt(a, b):
    return lax.dot_general(a, b, _NT, preferred_element_type=f32)


def _dot_tn(a, b):
    return lax.dot_general(a, b, _TN, preferred_element_type=f32)


def _split2(x):
    hi = x.astype(bf16)
    lo = (x - hi.astype(f32)).astype(bf16)
    return hi, lo


def _dot_seg(x, e):
    hi, lo = _split2(x)
    return _dot(hi, e) + _dot(lo, e)


def _cumsum_rows(tri, x):
    hi, lo = _split2(x)
    return _dot(tri, hi) + _dot(tri, lo)


def _dot3(a, b):
    ah, al = _split2(a)
    bh, bl = _split2(b)
    return _dot(ah, bh) + (_dot(ah, bl) + _dot(al, bh))


def _sigmoid(x):
    return 0.5 * (jnp.tanh(0.5 * x) + 1.0)


def _silu(x):
    return x * _sigmoid(x)


def _rms_rows(x, w):
    return x * lax.rsqrt(jnp.mean(x * x, axis=-1, keepdims=True) + RMS_EPS) * w


def _cparams(*sem):
    return pltpu.CompilerParams(dimension_semantics=sem, vmem_limit_bytes=VMEM_LIMIT)


def _rope_partner(x):
    lane = lax.broadcasted_iota(jnp.int32, x.shape, 1)
    return jnp.where((lane & 32) == 0, pltpu.roll(x, LANES - 32, 1), pltpu.roll(x, 32, 1))


def _even_in_kernel(x_ref, nw_ref, w_ref, cos_ref, sin_ref, ps_ref, ga_ref, qkv_ref, gb_ref):
    xn = _rms_rows(x_ref[...], nw_ref[...]).astype(bf16)

    def cols(c0, c1):
        return _dot(xn, w_ref[:, c0:c1])

    for c0 in range(0, EVEN_SHIFT, 512):
        c1 = min(c0 + 512, EVEN_SHIFT)
        ps_ref[:, c0:c1] = cols(c0, c1)
    base = EVEN_SHIFT
    ga_ref[...] = cols(base, base + RW_WIDTH)
    base += RW_WIDTH
    cos = cos_ref[...]
    sin = sin_ref[...]
    for j in range(2 * AT_WIDTH // LANES):
        t = cols(base + j * LANES, base + (j + 1) * LANES)
        qkv_ref[:, j * LANES:(j + 1) * LANES] = t * cos + _rope_partner(t) * sin
    base += 2 * AT_WIDTH
    qkv_ref[:, 2 * AT_WIDTH:3 * AT_WIDTH] = cols(base, base + AT_WIDTH)
    base += AT_WIDTH
    gb_ref[...] = cols(base, base + AT_WIDTH)


def _rope_tables(seq):
    inv = ROPE_THETA ** (-jnp.arange(0, AT_DH, 2, dtype=f32) / AT_DH)
    ang = jnp.arange(seq, dtype=f32)[:, None] * inv[None, :]
    cos, sin = jnp.cos(ang), jnp.sin(ang)
    return (jnp.concatenate([cos, cos, cos, cos], axis=-1),
            jnp.concatenate([-sin, sin, -sin, sin], axis=-1))


def _even_in(x2d, seq, norm_w, w_in_bf16, block_rows=512):
    m = x2d.shape[0]
    tm = block_rows
    per_seq = seq // tm
    cos, sin = _rope_tables(seq)
    row = lambda i: (i, 0)
    full = lambda i: (0, 0)
    tab = lambda i: (i % per_seq, 0)
    return pl.pallas_call(
        _even_in_kernel,
        grid=(m // tm,),
        in_specs=[
            pl.BlockSpec((tm, D_MODEL), row),
            pl.BlockSpec((1, D_MODEL), full),
            pl.BlockSpec((D_MODEL, EVEN_COLS), full),
            pl.BlockSpec((tm, LANES), tab),
            pl.BlockSpec((tm, LANES), tab),
        ],
        out_specs=[
            pl.BlockSpec((tm, EVEN_SHIFT), row),
            pl.BlockSpec((tm, RW_WIDTH), row),
            pl.BlockSpec((tm, 3 * AT_WIDTH), row),
            pl.BlockSpec((tm, AT_WIDTH), row),
        ],
        out_shape=[
            jax.ShapeDtypeStruct((m, EVEN_SHIFT), f32),
            jax.ShapeDtypeStruct((m, RW_WIDTH), f32),
            jax.ShapeDtypeStruct((m, 3 * AT_WIDTH), f32),
            jax.ShapeDtypeStruct((m, AT_WIDTH), f32),
        ],
        compiler_params=_cparams("parallel"),
        name="even_in",
    )(x2d, norm_w.reshape(1, D_MODEL), w_in_bf16, cos, sin)


ODD_PAD_COLS = 2 * GLA_KEY + 2 * GLA_VAL + LANES


def _mid_kernel(x_ref, ya_ref, yb_ref, wo_ref, nw_ref, wi_ref,
                x1_ref, q_ref, k_ref, v_ref, g_ref, gl_ref):
    x1 = (x_ref[...] + _dot(ya_ref[...], wo_ref[0:RW_WIDTH, :])
          + _dot(yb_ref[...], wo_ref[RW_WIDTH:RW_WIDTH + AT_WIDTH, :]))
    x1_ref[...] = x1
    xn = _rms_rows(x1, nw_ref[...]).astype(bf16)
    c = 0
    for ref, width in ((q_ref, GLA_KEY), (k_ref, GLA_KEY), (v_ref, GLA_VAL), (g_ref, GLA_VAL),
                       (gl_ref, LANES)):
        for c0 in range(0, width, 512):
            c1 = min(c0 + 512, width)
            ref[:, c0:c1] = _dot(xn, wi_ref[:, c + c0:c + c1])
        c += width


def _mid(x2d, ya, yb, w_out_bf16, norm_w, w_in_pad_bf16, block_rows=512):
    m = x2d.shape[0]
    tm = block_rows
    row = lambda i: (i, 0)
    full = lambda i: (0, 0)
    widths = (D_MODEL, GLA_KEY, GLA_KEY, GLA_VAL, GLA_VAL, LANES)
    return pl.pallas_call(
        _mid_kernel,
        grid=(m // tm,),
        in_specs=[
            pl.BlockSpec((tm, D_MODEL), row),
            pl.BlockSpec((tm, RW_WIDTH), row),
            pl.BlockSpec((tm, AT_WIDTH), row),
            pl.BlockSpec((RW_WIDTH + AT_WIDTH, D_MODEL), full),
            pl.BlockSpec((1, D_MODEL), full),
            pl.BlockSpec((D_MODEL, ODD_PAD_COLS), full),
        ],
        out_specs=[pl.BlockSpec((tm, w), row) for w in widths],
        out_shape=[jax.ShapeDtypeStruct((m, w), f32) for w in widths],
        compiler_params=_cparams("parallel"),
        name="mid",
    )(x2d, ya, yb, w_out_bf16, norm_w.reshape(1, D_MODEL), w_in_pad_bf16)


def _final_kernel(x_ref, y_ref, wo_ref, nw_ref, o_ref):
    x2 = x_ref[...] + _dot(y_ref[...], wo_ref[...])
    o_ref[...] = _rms_rows(x2, nw_ref[...])


def _final(x2d, yc, w_out_bf16, norm_w, block_rows=512):
    m = x2d.shape[0]
    tm = block_rows
    row = lambda i: (i, 0)
    full = lambda i: (0, 0)
    return pl.pallas_call(
        _final_kernel,
        grid=(m // tm,),
        in_specs=[
            pl.BlockSpec((tm, D_MODEL), row),
            pl.BlockSpec((tm, GLA_VAL), row),
            pl.BlockSpec((GLA_VAL, D_MODEL), full),
            pl.BlockSpec((1, D_MODEL), full),
        ],
        out_specs=pl.BlockSpec((tm, D_MODEL), row),
        out_shape=jax.ShapeDtypeStruct((m, D_MODEL), f32),
        compiler_params=_cparams("parallel"),
        name="final",
    )(x2d, yc, w_out_bf16, norm_w.reshape(1, D_MODEL))


def _scan_masks(n, reverse):
    t = lax.broadcasted_iota(jnp.int32, (n, n), 0) % CHUNK
    s = lax.broadcasted_iota(jnp.int32, (n, n), 1) % CHUNK
    if reverse:
        return s >= t, s > t
    return s <= t, s < t


def _row_to_col(row):
    n = row.shape[1]
    eye = (lax.broadcasted_iota(jnp.int32, (n, n), 0) == lax.broadcasted_iota(jnp.int32, (n, n), 1))
    return jnp.sum(jnp.where(eye, jnp.broadcast_to(row, (n, n)), 0.0), axis=1, keepdims=True)


def _gla_kernel(*refs, reverse, final, block_rows):
    if final:
        (q_ref, k_ref, v_ref, gl_ref, gup_ref, gb_ref, gate_ref, of_ref, nw_ref, o_ref, st_ref) = refs
    else:
        (q_ref, k_ref, v_ref, gl_ref, gup_ref, gb_ref, o_ref, st_ref) = refs
    d = 1 if reverse else 0

    @pl.when(pl.program_id(1) == 0)
    def _():
        st_ref[...] = jnp.zeros_like(st_ref)

    incl, _ = _scan_masks(CHUNK, reverse)
    tri = incl.astype(bf16)
    gup = gup_ref[d]
    gbias = gb_ref[d:d + 1, :]
    nchunks = block_rows // CHUNK
    last = 0 if reverse else CHUNK - 1

    x = _dot3(gl_ref[...], gup) + gbias
    g = (jnp.minimum(x, 0.0) - jnp.log(1.0 + jnp.exp(-jnp.abs(x)))) / GLA_GATE_NORM
    chunk_rows = [slice(c * CHUNK, (c + 1) * CHUNK) for c in range(nchunks)]
    bcum = jnp.concatenate([_cumsum_rows(tri, g[rs]) for rs in chunk_rows], axis=0)
    b_last = [bcum[c * CHUNK + last:c * CHUNK + last + 1, :] for c in range(nchunks)]
    b_end = jnp.concatenate([jnp.broadcast_to(b, (CHUNK, GLA_KEY)) for b in b_last], axis=0)
    q = q_ref[...] * (GLA_HK ** -0.5)
    k = k_ref[...]
    q_dec = (q * jnp.exp(bcum)).astype(bf16)
    k_dec = (k * jnp.exp(-bcum)).astype(bf16)
    k_end = (k * jnp.exp(b_end - bcum)).astype(bf16)

    order = list(reversed(range(nchunks))) if reverse else list(range(nchunks))
    klanes = [slice(h * GLA_HK, (h + 1) * GLA_HK) for h in range(GLA_HEADS)]
    vlanes = [slice(h * GLA_HV, (h + 1) * GLA_HV) for h in range(GLA_HEADS)]
    units = [(c, h) for c in order for h in range(GLA_HEADS)]
    v = [v_ref[chunk_rows[c], vlanes[h]].astype(bf16) for c, h in units]
    att = [jnp.where(incl, _dot_nt(q_dec[chunk_rows[c], klanes[h]], k_dec[chunk_rows[c], klanes[h]]),
                     0.0).astype(bf16) for c, h in units]
    o_intra = [_dot(att[i], v[i]) for i in range(len(units))]
    kv = [_dot_tn(k_end[chunk_rows[c], klanes[h]], v[i]) for i, (c, h) in enumerate(units)]
    dcol = [_row_to_col(jnp.exp(b_last[c][:, klanes[h]])) for c, h in units]

    state = [st_ref[h] for h in range(GLA_HEADS)]
    for i, (c, h) in enumerate(units):
        rows = chunk_rows[c]
        o = o_intra[i] + _dot(q_dec[rows, klanes[h]], state[h].astype(bf16))
        state[h] = state[h] * dcol[i] + kv[i]
        if final:
            o = o + of_ref[rows, vlanes[h]]
            o = o * lax.rsqrt(jnp.mean(o * o, axis=-1, keepdims=True) + RMS_EPS) * nw_ref[...]
            o_ref[rows, vlanes[h]] = (o * _silu(gate_ref[rows, vlanes[h]])).astype(o_ref.dtype)
        else:
            o_ref[rows, vlanes[h]] = o
    for h in range(GLA_HEADS):
        st_ref[h] = state[h]


def _gla_scan(q, k, v, gl, gate_up_pad, gate_bias, *, reverse, gate=None, o_fwd=None, norm_w=None,
              block_rows=256):
    bsz, seq, _ = q.shape
    tb = block_rows
    nb = seq // tb
    final = reverse
    blk = (lambda b, j: (b, nb - 1 - j, 0)) if reverse else (lambda b, j: (b, j, 0))
    full2 = lambda b, j: (0, 0)
    full3 = lambda b, j: (0, 0, 0)
    in_specs = [
        pl.BlockSpec((None, tb, GLA_KEY), blk),
        pl.BlockSpec((None, tb, GLA_KEY), blk),
        pl.BlockSpec((None, tb, GLA_VAL), blk),
        pl.BlockSpec((None, tb, LANES), blk),
        pl.BlockSpec((2, LANES, GLA_KEY), full3),
        pl.BlockSpec((2, GLA_KEY), full2),
    ]
    args = [q, k, v, gl, gate_up_pad, gate_bias]
    if final:
        in_specs += [
            pl.BlockSpec((None, tb, GLA_VAL), blk),
            pl.BlockSpec((None, tb, GLA_VAL), blk),
            pl.BlockSpec((1, GLA_HV), full2),
        ]
        args += [gate, o_fwd, norm_w.reshape(1, GLA_HV)]
    return pl.pallas_call(
        functools.partial(_gla_kernel, reverse=reverse, final=final, block_rows=tb),
        grid=(bsz, nb),
        in_specs=in_specs,
        out_specs=pl.BlockSpec((None, tb, GLA_VAL), blk),
        out_shape=jax.ShapeDtypeStruct((bsz, seq, GLA_VAL), bf16 if final else f32),
        scratch_shapes=[pltpu.VMEM((GLA_HEADS, GLA_HK, GLA_HV), f32)],
        compiler_params=_cparams("parallel", "arbitrary"),
        name="gla_bwd" if reverse else "gla_fwd",
    )(*args)


AT_NEG = -1e30
AT_QBLK = 128
AT_KBLK = 256
AT_UNROLL = 8


AT_FOLD = 4


def _attn_kernel(q_ref, k_ref, v_ref, g_ref, o_ref, qp, kp, vp, m_s, l_s, acc_s, bias_s, tmp_s,
                 *, seq):
    lane = lax.broadcasted_iota(jnp.int32, (1, LANES), 1)
    left = lane < AT_DH
    scale = AT_DH ** -0.5
    fold_len = seq // AT_FOLD

    def gather(src_ref, dst_ref, dil, mul=None):
        sub = seq // dil
        piece = min(sub, 512)
        pieces = sub // piece

        def body(i, carry):
            r = i // pieces
            part = i % pieces
            src = r + dil * part * piece
            dst = pl.multiple_of(r * sub + part * piece, piece)
            idx = pl.ds(src, piece, stride=dil) if dil > 1 else pl.ds(src, piece)
            x = src_ref[idx, :]
            dst_ref[pl.ds(dst, piece), :] = (x if mul is None else x * mul).astype(bf16)
            return carry

        lax.fori_loop(0, dil * pieces, body, 0)

    def fill_bias(offs, tq, nk, deltas):
        for d, delta in enumerate(deltas):
            bias_s[d, 0:tq, 0:nk] = jnp.where(jnp.abs(offs + delta) <= AT_SIDE, 0.0, AT_NEG)

    def run_blocks(nblocks, load_unit):
        def body(it, carry):
            units = [load_unit(it * AT_UNROLL + u) for u in range(AT_UNROLL)]
            heads = [(u, first) for u in range(AT_UNROLL) for first in (True, False)]
            scores = []
            for u, first in heads:
                qb, kb, _, bias, _, _ = units[u]
                qh = jnp.where(left if first else ~left, qb, jnp.zeros_like(qb))
                scores.append(_dot_nt(qh, kb) + bias)
            maxes = [jnp.max(s, axis=1, keepdims=True) for s in scores]
            probs = [jnp.exp(s - mh) for s, mh in zip(scores, maxes)]
            sums = [jnp.sum(p, axis=1, keepdims=True) for p in probs]
            outs = [_dot(p.astype(bf16), units[u][2]) for p, (u, _) in zip(probs, heads)]
            olds = [None if units[u][5] else
                    [(m_s[idx, :], l_s[idx, :], acc_s[idx, :]) for idx, _ in units[u][4]]
                    for u in range(AT_UNROLL)]
            for u in range(AT_UNROLL):
                m_new = jnp.where(left, maxes[2 * u], maxes[2 * u + 1])
                l_new = jnp.where(left, sums[2 * u], sums[2 * u + 1])
                a_new = jnp.where(left, outs[2 * u], outs[2 * u + 1])
                for pi, (idx, rows) in enumerate(units[u][4]):
                    mp, lp, ap = m_new[rows], l_new[rows], a_new[rows]
                    if units[u][5]:
                        m_s[idx, :] = mp
                        l_s[idx, :] = lp
                        acc_s[idx, :] = ap
                    else:
                        m_old, l_old, a_old = olds[u][pi]
                        m = jnp.maximum(m_old, mp)
                        w_old = jnp.exp(m_old - m)
                        w_new = jnp.exp(mp - m)
                        m_s[idx, :] = m
                        l_s[idx, :] = l_old * w_old + lp * w_new
                        acc_s[idx, :] = a_old * w_old + ap * w_new
            return carry

        lax.fori_loop(0, nblocks // AT_UNROLL, body, 0)

    def window(m0, sub, tq, nk):
        ks = jnp.clip(m0 - AT_SIDE, 0, sub - nk)
        return ks, (m0 - ks) // AT_SIDE

    tq, nk = AT_QBLK, AT_KBLK
    per = tq // AT_FOLD
    gather(q_ref, qp, AT_FOLD, scale)
    gather(k_ref, kp, 1)
    gather(v_ref, vp, 1)
    qi = lax.broadcasted_iota(jnp.int32, (tq, nk), 0)
    kj = lax.broadcasted_iota(jnp.int32, (tq, nk), 1)
    fill_bias((qi // per) + AT_FOLD * (qi % per) - kj, tq, nk, (0, AT_SIDE, 2 * AT_SIDE))

    def unit_d1(i):
        t0 = i * tq
        ks, bi = window(t0, seq, tq, nk)
        starts = [pl.multiple_of(rho * fold_len + i * per, per) for rho in range(AT_FOLD)]
        qb = jnp.concatenate([qp[pl.ds(s, per), :] for s in starts], axis=0)
        kb = kp[pl.ds(pl.multiple_of(ks, AT_SIDE), nk), :]
        vb = vp[pl.ds(pl.multiple_of(ks, AT_SIDE), nk), :]
        state = [(pl.ds(s, per), slice(rho * per, (rho + 1) * per)) for rho, s in enumerate(starts)]
        return qb, kb, vb, bias_s[bi], state, True

    run_blocks(seq // tq, unit_d1)

    sub = fold_len
    gather(k_ref, kp, AT_FOLD)
    gather(v_ref, vp, AT_FOLD)
    fill_bias(qi - kj, tq, nk, (0, AT_SIDE, 2 * AT_SIDE))
    nblk = sub // tq

    def unit_fold(i):
        r = i // nblk
        m0 = (i % nblk) * tq
        ks, bi = window(m0, sub, tq, nk)
        qrow = pl.ds(pl.multiple_of(r * sub + m0, tq), tq)
        krow = pl.ds(pl.multiple_of(r * sub + ks, AT_SIDE), nk)
        return qp[qrow, :], kp[krow, :], vp[krow, :], bias_s[bi], [(qrow, slice(0, tq))], False

    run_blocks(AT_FOLD * nblk, unit_fold)

    dil = AT_DILATIONS[-1]
    sub = seq // dil
    tq, nk = min(AT_QBLK, sub), min(AT_KBLK, sub)
    gather(q_ref, qp, dil, scale)
    gather(k_ref, kp, dil)
    gather(v_ref, vp, dil)
    fill_bias((qi - kj)[0:tq, 0:nk], tq, nk, (0, AT_SIDE, nk - tq))
    nblk = sub // tq

    def unit_wide(i):
        r = i // nblk
        m0 = (i % nblk) * tq
        ks, bi = window(m0, sub, tq, nk)
        qrow = pl.ds(pl.multiple_of(r * sub + m0, AT_SIDE), tq)
        krow = pl.ds(pl.multiple_of(r * sub + ks, AT_SIDE), nk)
        srow = (r % AT_FOLD) * fold_len + r // AT_FOLD + (dil // AT_FOLD) * m0
        state = [(pl.ds(srow, tq, stride=dil // AT_FOLD), slice(0, tq))]
        return qp[qrow, :], kp[krow, :], vp[krow, :], bias_s[bi, 0:tq, 0:nk], state, False

    run_blocks(dil * nblk, unit_wide)

    rows = tmp_s.shape[0]
    per_fold = rows // AT_FOLD

    def finish(i, carry):
        for rho in range(AT_FOLD):
            idx = pl.ds(pl.multiple_of(rho * fold_len + i * per_fold, per_fold), per_fold)
            tmp_s[pl.ds(rho, per_fold, stride=AT_FOLD), :] = acc_s[idx, :] / l_s[idx, :]
        out = pl.ds(pl.multiple_of(i * rows, rows), rows)
        o_ref[out, :] = (tmp_s[...] * _silu(g_ref[out, :])).astype(o_ref.dtype)
        return carry

    lax.fori_loop(0, seq // rows, finish, 0)


def _attention(qkv, gate):
    bsz, seq, _ = qkv.shape
    pairs = AT_WIDTH // LANES
    once = pl.Buffered(1)
    col = lambda off: (lambda b, p: (b, 0, off + p))
    return pl.pallas_call(
        functools.partial(_attn_kernel, seq=seq),
        grid=(bsz, pairs),
        in_specs=[
            pl.BlockSpec((None, seq, LANES), col(0), pipeline_mode=once),
            pl.BlockSpec((None, seq, LANES), col(pairs), pipeline_mode=once),
            pl.BlockSpec((None, seq, LANES), col(2 * pairs), pipeline_mode=once),
            pl.BlockSpec((None, seq, LANES), col(0), pipeline_mode=once),
        ],
        out_specs=pl.BlockSpec((None, seq, LANES), col(0)),
        out_shape=jax.ShapeDtypeStruct((bsz, seq, AT_WIDTH), bf16),
        scratch_shapes=([pltpu.VMEM((seq, LANES), bf16)] * 3 + [pltpu.VMEM((seq, LANES), f32)] * 3
                        + [pltpu.VMEM((3, AT_QBLK, AT_KBLK), f32),
                           pltpu.VMEM((min(seq, 512), LANES), f32)]),
        compiler_params=_cparams("parallel", "parallel"),
        name="dilated_attn",
    )(qkv, qkv, qkv, gate)


RW_GROUP = 2
RW_GW = RW_GROUP * RW_DH
RW_NGROUPS = RW_HEADS // RW_GROUP
RW_WAVE = 4


def _head_stack(x, masks):
    return jnp.concatenate([jnp.where(mk, x, jnp.zeros_like(x)) for mk in masks], axis=0)


def _unit_lower_inverse_stages(w, src, dst):
    steps = int(math.log2(CHUNK)) - 1

    def start():
        n = w[src][0].shape[0]
        eye = (lax.broadcasted_iota(jnp.int32, (n, n), 0)
               == lax.broadcasted_iota(jnp.int32, (n, n), 1)).astype(f32)
        w["_t"] = [eye + a for a in w[src]]
        qs = [a.astype(bf16) for a in w[src]]
        w["_q"] = [_dot(q, q).astype(bf16) for q in qs]

    def double():
        n = w[src][0].shape[0]
        both = [_dot(jnp.concatenate([q, t.astype(bf16)], axis=0), q)
                for q, t in zip(w["_q"], w["_t"])]
        w["_q"] = [x[:n].astype(bf16) for x in both]
        w["_t"] = [t + x[n:] for t, x in zip(w["_t"], both)]

    def finish():
        w[dst] = [(t + _dot(t.astype(bf16), q)).astype(bf16) for q, t in zip(w["_q"], w["_t"])]

    return [start] + [double] * (steps - 1) + [finish]


def _rwkv_kernel(*refs, reverse, final, block_rows, nblocks):
    (ps_ref, prev_ref, next_ref, mup_ref, mun_ref, w0_ref, wup_ref, a0_ref, aup_ref,
     kk_ref, ka_ref, seg_ref) = refs[:12]
    if final:
        rk_ref, gnw_ref, gnb_ref, ga_ref, yf_ref, o_ref, sh_ref, st_ref = refs[12:]
    else:
        o_ref, sh_ref, st_ref = refs[12:]
    d = 1 if reverse else 0
    tb = block_rows
    j = pl.program_id(1)
    blk = (nblocks - 1 - j) if reverse else j

    @pl.when(j == 0)
    def _():
        st_ref[...] = jnp.zeros_like(st_ref)

    rid = lax.broadcasted_iota(jnp.int32, (tb, 1), 0)
    has_prev = (blk > 0).astype(f32)
    has_next = (blk < nblocks - 1).astype(f32)
    for c0 in range(0, EVEN_SHIFT, 256):
        c1 = min(c0 + 256, EVEN_SHIFT)
        x = ps_ref[:, c0:c1]
        before = prev_ref[7:8, c0:c1] * has_prev
        after = next_ref[0:1, c0:c1] * has_next
        prv = jnp.where(rid == 0, before, pltpu.roll(x, 1, 0))
        nxt = jnp.where(rid == tb - 1, after, pltpu.roll(x, tb - 1, 0))
        sh_ref[:, c0:c1] = x + mup_ref[:, c0:c1] * (prv - x) + mun_ref[:, c0:c1] * (nxt - x)

    incl1, _ = _scan_masks(CHUNK, reverse)
    tri = incl1.astype(bf16)
    gi = lax.broadcasted_iota(jnp.int32, (2 * RW_GW, 2 * RW_GW), 0)
    ahead = (lax.broadcasted_iota(jnp.int32, (2 * RW_GW, 2 * RW_GW), 1) % CHUNK) - (gi % CHUNK)
    ahead = -ahead if reverse else ahead
    keep = ahead < jnp.where(gi < RW_GW, 0, 1)
    lane = lax.broadcasted_iota(jnp.int32, (1, RW_GW), 1)
    masks = [(lane // RW_DH) == h for h in range(RW_GROUP)]
    seg = seg_ref[...]
    last = 0 if reverse else CHUNK - 1
    nchunks = tb // CHUNK
    w_lo, a_lo = 3 * RW_WIDTH, 3 * RW_WIDTH + RW_LORA

    def lr_gate(a_lat, dd):
        return _sigmoid(a0_ref[dd:dd + 1, :] + _dot3(a_lat, aup_ref[dd]))

    r = sh_ref[:, 0:RW_WIDTH]
    k = sh_ref[:, RW_WIDTH:2 * RW_WIDTH]
    v = sh_ref[:, 2 * RW_WIDTH:3 * RW_WIDTH]
    w_lat = sh_ref[:, w_lo:w_lo + RW_LORA]
    a_lat = sh_ref[:, a_lo:a_lo + RW_LORA]
    w_log = -RW_DECAY_SCALE * _sigmoid(w0_ref[d:d + 1, :] + _dot3(jnp.tanh(w_lat), wup_ref[d]))
    lr = lr_gate(a_lat, d)
    kk = k * kk_ref[...]
    kk = kk / jnp.maximum(jnp.sqrt(_dot_seg(kk * kk, seg)), 1e-12)
    k_dir = k * (1.0 + (lr - 1.0) * ka_ref[...])
    chunk_rows = [slice(c * CHUNK, (c + 1) * CHUNK) for c in range(nchunks)]
    cum = jnp.concatenate([_cumsum_rows(tri, w_log[rs]) for rs in chunk_rows], axis=0)
    grow = jnp.exp(-cum)
    a_t = -kk * jnp.exp(cum - w_log)
    b_t = kk * lr * grow
    k_t = k_dir * grow
    r_t = r * jnp.exp(cum)

    order = list(reversed(range(nchunks))) if reverse else list(range(nchunks))
    lanes = [slice(g * RW_GW, (g + 1) * RW_GW) for g in range(RW_NGROUPS)]
    state = [st_ref[g] for g in range(RW_NGROUPS)]
    y_rows = {}

    def wave_stages(chunks):
        units = [(c, g) for c in chunks for g in range(RW_NGROUPS)]
        n = range(len(units))
        w = {}

        def stacked():
            stack = lambda x: [_head_stack(x[chunk_rows[c], lanes[g]], masks) for c, g in units]
            w["a_s"] = [t.astype(bf16) for t in stack(a_t)]
            w["r_s"] = [t.astype(bf16) for t in stack(r_t)]
            w["b_s"] = [t.astype(bf16) for t in stack(b_t)]
            w["k_s"] = [t.astype(bf16) for t in stack(k_t)]
            v_f = stack(v)
            w["v_st"] = [t.T.astype(bf16) for t in v_f]
            w["v_s"] = [t.astype(bf16) for t in v_f]

        def grams():
            gram = [jnp.where(keep, _dot_nt(jnp.concatenate([w["a_s"][i], w["r_s"][i]], axis=0),
                                             jnp.concatenate([w["b_s"][i], w["k_s"][i]], axis=0)),
                              0.0) for i in n]
            w["a_ab"] = [gram[i][0:RW_GW, 0:RW_GW] for i in n]
            w["a_ak"] = [gram[i][0:RW_GW, RW_GW:].astype(bf16) for i in n]
            w["a_rb"] = [gram[i][RW_GW:, 0:RW_GW].astype(bf16) for i in n]
            w["a_rk"] = [gram[i][RW_GW:, RW_GW:].astype(bf16) for i in n]

        inverse = _unit_lower_inverse_stages(w, "a_ab", "t_inv")

        def apply_inverse():
            w["w_m"] = [_dot(w["t_inv"][i], w["a_s"][i]).astype(bf16) for i in n]
            w["av_t"] = [_dot_nt(w["v_st"][i], w["a_ak"][i]).astype(bf16) for i in n]

        def locals_():
            w["z_t"] = [_dot_nt(w["av_t"][i], w["t_inv"][i]) for i in n]
            w["vk"] = [_dot(w["v_st"][i], w["k_s"][i]) for i in n]
            w["y_local"] = [_dot(w["a_rk"][i], w["v_s"][i]) for i in n]

        def recur(pos_c, c):
            def new_state():
                p_end = jnp.exp(cum[c * CHUNK + last:c * CHUNK + last + 1, :])
                w["s0b"], w["u_t"] = [], []
                for g in range(RW_NGROUPS):
                    i = pos_c * RW_NGROUPS + g
                    s0 = state[g]
                    s0b = s0.astype(bf16)
                    u_t = (_dot_nt(s0b, w["w_m"][i]) + w["z_t"][i]).astype(bf16)
                    state[g] = (s0 + _dot(u_t, w["b_s"][i]) + w["vk"][i]) * p_end[:, lanes[g]]
                    w["s0b"].append(s0b)
                    w["u_t"].append(u_t)

            def outputs():
                ys = []
                for g in range(RW_NGROUPS):
                    i = pos_c * RW_NGROUPS + g
                    y = (_dot_nt(w["r_s"][i], w["s0b"][g]) + _dot_nt(w["a_rb"][i], w["u_t"][g])
                         + w["y_local"][i])
                    yg = y[0:CHUNK]
                    for h in range(1, RW_GROUP):
                        yg = yg + y[h * CHUNK:(h + 1) * CHUNK]
                    ys.append(yg)
                y_rows[c] = jnp.concatenate(ys, axis=1)

            return [new_state, outputs]

        independent = [stacked, grams] + inverse + [apply_inverse, locals_]
        recurrence = [f for pos_c, c in enumerate(chunks) for f in recur(pos_c, c)]
        return independent, recurrence

    waves = [order[i:i + RW_WAVE] for i in range(0, nchunks, RW_WAVE)]
    pending = []
    for chunks in waves:
        independent, recurrence = wave_stages(chunks)
        every = max(1, len(independent) // (len(pending) + 1)) if pending else 0
        for si, stage in enumerate(independent):
            stage()
            if pending and (si + 1) % every == 0:
                pending.pop(0)()
        while pending:
            pending.pop(0)()
        pending = recurrence
    while pending:
        pending.pop(0)()
    for g in range(RW_NGROUPS):
        st_ref[g] = state[g]
    y_all = jnp.concatenate([y_rows[c] for c in range(nchunks)], axis=0)

    if not final:
        o_ref[...] = y_all
    else:
        y_all = y_all + yf_ref[...]
        mean = _dot_seg(y_all, seg) * (1.0 / RW_DH)
        cen = y_all - mean
        var = _dot_seg(cen * cen, seg) * (1.0 / RW_DH)
        yn = cen * lax.rsqrt(var + RW_GN_EPS) * gnw_ref[...] + gnb_ref[...]
        k_other = k * (1.0 + (lr_gate(a_lat, 1 - d) - 1.0) * ka_ref[...])
        rk = _dot_seg(r * (k_dir + k_other) * rk_ref[...], seg)
        o_ref[...] = ((yn + rk * v) * _silu(ga_ref[...])).astype(o_ref.dtype)


def _segment_ones():
    h = jnp.arange(RW_WIDTH) // RW_DH
    return (h[:, None] == h[None, :]).astype(bf16)


def _rwkv_scan(ps, mu_prev, mu_next, w0, w_up, a0, a_up, k_k, k_a, *, reverse,
               r_k=None, gn_w=None, gn_b=None, gate=None, y_fwd=None, block_rows=512):
    bsz, seq, _ = ps.shape
    tb = block_rows
    nb = seq // tb
    final = reverse
    halo = tb // 8
    pos = (lambda j: nb - 1 - j) if reverse else (lambda j: j)
    blk = lambda b, j: (b, pos(j), 0)
    prev = lambda b, j: (b, jnp.maximum(pos(j) * halo - 1, 0), 0)
    nxt = lambda b, j: (b, jnp.minimum((pos(j) + 1) * halo, seq // 8 - 1), 0)
    full2 = lambda b, j: (0, 0)
    full3 = lambda b, j: (0, 0, 0)
    vec = lambda n: pl.BlockSpec((1, n), full2)
    in_specs = [
        pl.BlockSpec((None, tb, EVEN_SHIFT), blk),
        pl.BlockSpec((None, 8, EVEN_SHIFT), prev),
        pl.BlockSpec((None, 8, EVEN_SHIFT), nxt),
        vec(EVEN_SHIFT), vec(EVEN_SHIFT),
        pl.BlockSpec((2, RW_WIDTH), full2),
        pl.BlockSpec((2, RW_LORA, RW_WIDTH), full3),
        pl.BlockSpec((2, RW_WIDTH), full2),
        pl.BlockSpec((2, RW_LORA, RW_WIDTH), full3),
        vec(RW_WIDTH), vec(RW_WIDTH),
        pl.BlockSpec((RW_WIDTH, RW_WIDTH), full2),
    ]
    row = lambda t: t.reshape(1, -1)
    args = [ps, ps, ps, row(mu_prev), row(mu_next), w0, w_up, a0, a_up, row(k_k), row(k_a),
            _segment_ones()]
    if final:
        in_specs += [vec(RW_WIDTH), vec(RW_WIDTH), vec(RW_WIDTH),
                     pl.BlockSpec((None, tb, RW_WIDTH), blk),
                     pl.BlockSpec((None, tb, RW_WIDTH), blk)]
        args += [row(r_k), row(gn_w), row(gn_b), gate, y_fwd]
    return pl.pallas_call(
        functools.partial(_rwkv_kernel, reverse=reverse, final=final, block_rows=tb, nblocks=nb),
        grid=(bsz, nb),
        in_specs=in_specs,
        out_specs=pl.BlockSpec((None, tb, RW_WIDTH), blk),
        out_shape=jax.ShapeDtypeStruct((bsz, seq, RW_WIDTH), bf16 if final else f32),
        scratch_shapes=[pltpu.VMEM((tb, EVEN_SHIFT), f32),
                        pltpu.VMEM((RW_NGROUPS, RW_GW, RW_GW), f32)],
        compiler_params=_cparams("parallel", "arbitrary"),
        name="rwkv_bwd" if reverse else "rwkv_fwd",
    )(*args)


def _trunk(x, p):
    bsz, seq, _ = x.shape
    tokens = bsz * seq
    x2d = x.reshape(tokens, D_MODEL)
    seq3 = lambda t: t.reshape(bsz, seq, t.shape[-1])
    flat = lambda t: t.reshape(tokens, t.shape[-1])

    ps, ga, qkv, gb = _even_in(x2d, seq, p["even_norm"], p["even_w_in"])
    rw = (seq3(ps), p["mu_prev"], p["mu_next"], p["w0"], p["w_up"], p["a0"], p["a_up"],
          p["k_k"], p["k_a"])
    y_fwd = _rwkv_scan(*rw, reverse=False)
    ya = _rwkv_scan(*rw, reverse=True, r_k=p["r_k"], gn_w=p["gn_w"], gn_b=p["gn_b"],
                    gate=seq3(ga), y_fwd=y_fwd)
    yb = _attention(seq3(qkv), seq3(gb))

    x1, q, k, v, gate, gate_lat = _mid(x2d, flat(ya), flat(yb), p["even_w_out"], p["odd_norm"],
                                       p["odd_w_in"])
    gla = (seq3(q), seq3(k), seq3(v), seq3(gate_lat), p["gate_up"], p["gate_bias"])
    o_fwd = _gla_scan(*gla, reverse=False)
    yc = _gla_scan(*gla, reverse=True, gate=seq3(gate), o_fwd=o_fwd, norm_w=p["gla_norm"])
    return _final(x1, flat(yc), p["odd_w_out"], p["final_norm"]).reshape(bsz, seq, D_MODEL)


def _prepare(even_norm, even_w_in, even_mu_prev, even_mu_next, rwkv_w0, rwkv_w_up, rwkv_a0,
             rwkv_a_up, rwkv_k_k, rwkv_k_a, rwkv_r_k, rwkv_gn_w, rwkv_gn_b, even_w_out, odd_norm,
             odd_w_in, gla_gate_up, gla_gate_bias, gla_norm, odd_w_out, final_norm):
    wi = odd_w_in[0]
    lat0 = 2 * GLA_KEY + GLA_VAL
    lat = jnp.pad(wi[:, lat0:lat0 + GLA_RANK], ((0, 0), (0, LANES - GLA_RANK)))
    odd_in = jnp.concatenate([wi[:, :lat0], wi[:, lat0 + GLA_RANK:], lat], axis=1)
    return {
        "even_norm": even_norm[0], "even_w_in": even_w_in[0].astype(bf16),
        "mu_prev": even_mu_prev[0], "mu_next": even_mu_next[0],
        "w0": rwkv_w0[0], "w_up": rwkv_w_up[0], "a0": rwkv_a0[0], "a_up": rwkv_a_up[0],
        "k_k": rwkv_k_k[0], "k_a": rwkv_k_a[0], "r_k": rwkv_r_k[0],
        "gn_w": rwkv_gn_w[0], "gn_b": rwkv_gn_b[0],
        "even_w_out": even_w_out[0].astype(bf16),
        "odd_norm": odd_norm[0], "odd_w_in": odd_in.astype(bf16),
        "gate_up": jnp.pad(gla_gate_up[0], ((0, 0), (0, LANES - GLA_RANK), (0, 0))),
        "gate_bias": gla_gate_bias[0], "gla_norm": gla_norm[0],
        "odd_w_out": odd_w_out[0].astype(bf16), "final_norm": final_norm,
    }


def kernel(x_prompt, x_sample, even_norm, even_w_in, even_mu_prev, even_mu_next, rwkv_w0, rwkv_w_up,
           rwkv_a0, rwkv_a_up, rwkv_k_k, rwkv_k_a, rwkv_r_k, rwkv_gn_w, rwkv_gn_b, even_w_out,
           odd_norm, odd_w_in, gla_gate_up, gla_gate_bias, gla_norm, odd_w_out, final_norm):
    p = _prepare(even_norm, even_w_in, even_mu_prev, even_mu_next, rwkv_w0, rwkv_w_up, rwkv_a0,
                 rwkv_a_up, rwkv_k_k, rwkv_k_a, rwkv_r_k, rwkv_gn_w, rwkv_gn_b, even_w_out,
                 odd_norm, odd_w_in, gla_gate_up, gla_gate_bias, gla_norm, odd_w_out, final_norm)
    return (_trunk(x_prompt, p), _trunk(x_sample, p))
```

```python
import functools
import math

import jax
import jax.numpy as jnp
from jax import lax
from jax.experimental import pallas as pl
from jax.experimental.pallas import tpu as pltpu

f32 = jnp.float32
bf16 = jnp.bfloat16

D_MODEL = 1024
RMS_EPS = 1e-6

RW_HEADS = 8
RW_DH = 64
RW_WIDTH = RW_HEADS * RW_DH
RW_LORA = 64
RW_DECAY_SCALE = 0.6065306597126334
RW_GN_EPS = 64e-5
AT_HEADS = 8
AT_DH = 64
AT_WIDTH = AT_HEADS * AT_DH
AT_SIDE = 64
AT_DILATIONS = (1, 4, 16)
ROPE_THETA = 10000.0
EVEN_SHIFT = 3 * RW_WIDTH + 2 * RW_LORA
EVEN_COLS = EVEN_SHIFT + RW_WIDTH + 4 * AT_WIDTH

GLA_HEADS = 4
GLA_KEY = 512
GLA_VAL = 1024
GLA_HK = GLA_KEY // GLA_HEADS
GLA_HV = GLA_VAL // GLA_HEADS
GLA_RANK = 16
GLA_GATE_NORM = 16.0
ODD_COLS = 2 * GLA_KEY + GLA_VAL + GLA_RANK + GLA_VAL

CHUNK = 64
LANES = 128
VMEM_LIMIT = 56 * 1024 * 1024

_NT = (((1,), (1,)), ((), ()))
_TN = (((0,), (0,)), ((), ()))


def _dot(a, b):
    return jnp.dot(a, b, preferred_element_type=f32)


def _dot_nt(a, b):
    return lax.dot_general(a, b, _NT, preferred_element_type=f32)


def _dot_tn(a, b):
    return lax.dot_general(a, b, _TN, preferred_element_type=f32)


def _split2(x):
    hi = x.astype(bf16)
    lo = (x - hi.astype(f32)).astype(bf16)
    return hi, lo


def _dot_seg(x, e):
    xb = x.astype(bf16)
    return jnp.concatenate([_dot(xb[:, c:c + LANES], e) for c in range(0, x.shape[1], LANES)], axis=1)


def _cumsum_rows(tri, x):
    hi, lo = _split2(x)
    return _dot(tri, hi) + _dot(tri, lo)


def _sigmoid(x):
    return 0.5 * (jnp.tanh(0.5 * x) + 1.0)


def _silu(x):
    return x * _sigmoid(x)


def _rms_rows(x, w):
    return x * lax.rsqrt(jnp.mean(x * x, axis=-1, keepdims=True) + RMS_EPS) * w


def _cparams(*sem):
    return pltpu.CompilerParams(dimension_semantics=sem, vmem_limit_bytes=VMEM_LIMIT)


def _rope_partner(x):
    lane = lax.broadcasted_iota(jnp.int32, x.shape, 1)
    return jnp.where((lane & 32) == 0, pltpu.roll(x, LANES - 32, 1), pltpu.roll(x, 32, 1))


def _even_in_kernel(x_ref, nw_ref, w_ref, cos_ref, sin_ref, ps_ref, ga_ref, qkv_ref, gb_ref):
    xn = _rms_rows(x_ref[...], nw_ref[...]).astype(bf16)

    def cols(c0, c1):
        return _dot(xn, w_ref[:, c0:c1])

    for c0 in range(0, EVEN_SHIFT, 512):
        c1 = min(c0 + 512, EVEN_SHIFT)
        ps_ref[:, c0:c1] = cols(c0, c1)
    base = EVEN_SHIFT
    ga_ref[...] = cols(base, base + RW_WIDTH)
    base += RW_WIDTH
    cos = cos_ref[...]
    sin = sin_ref[...]
    for j in range(2 * AT_WIDTH // LANES):
        t = cols(base + j * LANES, base + (j + 1) * LANES)
        qkv_ref[:, j * LANES:(j + 1) * LANES] = t * cos + _rope_partner(t) * sin
    base += 2 * AT_WIDTH
    qkv_ref[:, 2 * AT_WIDTH:3 * AT_WIDTH] = cols(base, base + AT_WIDTH)
    base += AT_WIDTH
    gb_ref[...] = cols(base, base + AT_WIDTH)


def _rope_tables(seq):
    inv = ROPE_THETA ** (-jnp.arange(0, AT_DH, 2, dtype=f32) / AT_DH)
    ang = jnp.arange(seq, dtype=f32)[:, None] * inv[None, :]
    cos, sin = jnp.cos(ang), jnp.sin(ang)
    return (jnp.concatenate([cos, cos, cos, cos], axis=-1),
            jnp.concatenate([-sin, sin, -sin, sin], axis=-1))


def _even_in(x2d, seq, norm_w, w_in_bf16, block_rows=512):
    m = x2d.shape[0]
    tm = block_rows
    per_seq = seq // tm
    cos, sin = _rope_tables(seq)
    row = lambda i: (i, 0)
    full = lambda i: (0, 0)
    tab = lambda i: (i % per_seq, 0)
    return pl.pallas_call(
        _even_in_kernel,
        grid=(m // tm,),
        in_specs=[
            pl.BlockSpec((tm, D_MODEL), row),
            pl.BlockSpec((1, D_MODEL), full),
            pl.BlockSpec((D_MODEL, EVEN_COLS), full),
            pl.BlockSpec((tm, LANES), tab),
            pl.BlockSpec((tm, LANES), tab),
        ],
        out_specs=[
            pl.BlockSpec((tm, EVEN_SHIFT), row),
            pl.BlockSpec((tm, RW_WIDTH), row),
            pl.BlockSpec((tm, 3 * AT_WIDTH), row),
            pl.BlockSpec((tm, AT_WIDTH), row),
        ],
        out_shape=[
            jax.ShapeDtypeStruct((m, EVEN_SHIFT), f32),
            jax.ShapeDtypeStruct((m, RW_WIDTH), f32),
            jax.ShapeDtypeStruct((m, 3 * AT_WIDTH), f32),
            jax.ShapeDtypeStruct((m, AT_WIDTH), f32),
        ],
        compiler_params=_cparams("parallel"),
        name="even_in",
    )(x2d, norm_w.reshape(1, D_MODEL), w_in_bf16, cos, sin)


ODD_PAD_COLS = 2 * GLA_KEY + 2 * GLA_VAL + LANES


def _mid_kernel(x_ref, ya_ref, yb_ref, wo_ref, nw_ref, wi_ref,
                x1_ref, q_ref, k_ref, v_ref, g_ref, gl_ref):
    x1 = (x_ref[...] + _dot(ya_ref[...], wo_ref[0:RW_WIDTH, :])
          + _dot(yb_ref[...], wo_ref[RW_WIDTH:RW_WIDTH + AT_WIDTH, :]))
    x1_ref[...] = x1
    xn = _rms_rows(x1, nw_ref[...]).astype(bf16)
    c = 0
    for ref, width in ((q_ref, GLA_KEY), (k_ref, GLA_KEY), (v_ref, GLA_VAL), (g_ref, GLA_VAL),
                       (gl_ref, LANES)):
        for c0 in range(0, width, 512):
            c1 = min(c0 + 512, width)
            ref[:, c0:c1] = _dot(xn, wi_ref[:, c + c0:c + c1])
        c += width


def _mid(x2d, ya, yb, w_out_bf16, norm_w, w_in_pad_bf16, block_rows=512):
    m = x2d.shape[0]
    tm = block_rows
    row = lambda i: (i, 0)
    full = lambda i: (0, 0)
    widths = (D_MODEL, GLA_KEY, GLA_KEY, GLA_VAL, GLA_VAL, LANES)
    return pl.pallas_call(
        _mid_kernel,
        grid=(m // tm,),
        in_specs=[
            pl.BlockSpec((tm, D_MODEL), row),
            pl.BlockSpec((tm, RW_WIDTH), row),
            pl.BlockSpec((tm, AT_WIDTH), row),
            pl.BlockSpec((RW_WIDTH + AT_WIDTH, D_MODEL), full),
            pl.BlockSpec((1, D_MODEL), full),
            pl.BlockSpec((D_MODEL, ODD_PAD_COLS), full),
        ],
        out_specs=[pl.BlockSpec((tm, w), row) for w in widths],
        out_shape=[jax.ShapeDtypeStruct((m, w), f32) for w in widths],
        compiler_params=_cparams("parallel"),
        name="mid",
    )(x2d, ya, yb, w_out_bf16, norm_w.reshape(1, D_MODEL), w_in_pad_bf16)


def _final_kernel(x_ref, y_ref, wo_ref, nw_ref, o_ref):
    x2 = x_ref[...] + _dot(y_ref[...], wo_ref[...])
    o_ref[...] = _rms_rows(x2, nw_ref[...])


def _final(x2d, yc, w_out_bf16, norm_w, block_rows=512):
    m = x2d.shape[0]
    tm = block_rows
    row = lambda i: (i, 0)
    full = lambda i: (0, 0)
    return pl.pallas_call(
        _final_kernel,
        grid=(m // tm,),
        in_specs=[
            pl.BlockSpec((tm, D_MODEL), row),
            pl.BlockSpec((tm, GLA_VAL), row),
            pl.BlockSpec((GLA_VAL, D_MODEL), full),
            pl.BlockSpec((1, D_MODEL), full),
        ],
        out_specs=pl.BlockSpec((tm, D_MODEL), row),
        out_shape=jax.ShapeDtypeStruct((m, D_MODEL), f32),
        compiler_params=_cparams("parallel"),
        name="final",
    )(x2d, yc, w_out_bf16, norm_w.reshape(1, D_MODEL))


def _scan_masks(n, reverse):
    t = lax.broadcasted_iota(jnp.int32, (n, n), 0) % CHUNK
    s = lax.broadcasted_iota(jnp.int32, (n, n), 1) % CHUNK
    if reverse:
        return s >= t, s > t
    return s <= t, s < t


def _row_to_col(row):
    n = row.shape[1]
    eye = (lax.broadcasted_iota(jnp.int32, (n, n), 0) == lax.broadcasted_iota(jnp.int32, (n, n), 1))
    return jnp.sum(jnp.where(eye, jnp.broadcast_to(row, (n, n)), 0.0), axis=1, keepdims=True)


def _gla_kernel(*refs, reverse, final, block_rows):
    if final:
        (q_ref, k_ref, v_ref, gl_ref, gup_ref, gb_ref, gate_ref, of_ref, nw_ref, o_ref, st_ref) = refs
    else:
        (q_ref, k_ref, v_ref, gl_ref, gup_ref, gb_ref, o_ref, st_ref) = refs
    d = 1 if reverse else 0

    @pl.when(pl.program_id(1) == 0)
    def _():
        st_ref[...] = jnp.zeros_like(st_ref)

    incl, _ = _scan_masks(CHUNK, reverse)
    tri = incl.astype(bf16)
    gup = gup_ref[d]
    gbias = gb_ref[d:d + 1, :]
    nchunks = block_rows // CHUNK
    last = 0 if reverse else CHUNK - 1

    x = _dot(gl_ref[...].astype(bf16), gup) + gbias
    g = (jnp.minimum(x, 0.0) - jnp.log(1.0 + jnp.exp(-jnp.abs(x)))) / GLA_GATE_NORM
    chunk_rows = [slice(c * CHUNK, (c + 1) * CHUNK) for c in range(nchunks)]
    bcum = jnp.concatenate([_cumsum_rows(tri, g[rs]) for rs in chunk_rows], axis=0)
    b_last = [bcum[c * CHUNK + last:c * CHUNK + last + 1, :] for c in range(nchunks)]
    b_end = jnp.concatenate([jnp.broadcast_to(b, (CHUNK, GLA_KEY)) for b in b_last], axis=0)
    q = q_ref[...] * (GLA_HK ** -0.5)
    k = k_ref[...]
    q_dec = (q * jnp.exp(bcum)).astype(bf16)
    k_dec = (k * jnp.exp(-bcum)).astype(bf16)
    k_end = (k * jnp.exp(b_end - bcum)).astype(bf16)

    order = list(reversed(range(nchunks))) if reverse else list(range(nchunks))
    klanes = [slice(h * GLA_HK, (h + 1) * GLA_HK) for h in range(GLA_HEADS)]
    vlanes = [slice(h * GLA_HV, (h + 1) * GLA_HV) for h in range(GLA_HEADS)]
    units = [(c, h) for c in order for h in range(GLA_HEADS)]
    v = [v_ref[chunk_rows[c], vlanes[h]].astype(bf16) for c, h in units]
    att = [jnp.where(incl, _dot_nt(q_dec[chunk_rows[c], klanes[h]], k_dec[chunk_rows[c], klanes[h]]),
                     0.0).astype(bf16) for c, h in units]
    o_intra = [_dot(att[i], v[i]) for i in range(len(units))]
    kv = [_dot_tn(k_end[chunk_rows[c], klanes[h]], v[i]) for i, (c, h) in enumerate(units)]
    dcol = [_row_to_col(jnp.exp(b_last[c][:, klanes[h]])) for c, h in units]

    state = [st_ref[h] for h in range(GLA_HEADS)]
    for i, (c, h) in enumerate(units):
        rows = chunk_rows[c]
        o = o_intra[i] + _dot(q_dec[rows, klanes[h]], state[h].astype(bf16))
        state[h] = state[h] * dcol[i] + kv[i]
        if final:
            o = o + of_ref[rows, vlanes[h]]
            o = o * lax.rsqrt(jnp.mean(o * o, axis=-1, keepdims=True) + RMS_EPS) * nw_ref[...]
            o_ref[rows, vlanes[h]] = (o * _silu(gate_ref[rows, vlanes[h]])).astype(o_ref.dtype)
        else:
            o_ref[rows, vlanes[h]] = o
    for h in range(GLA_HEADS):
        st_ref[h] = state[h]


def _gla_scan(q, k, v, gl, gate_up_pad, gate_bias, *, reverse, gate=None, o_fwd=None, norm_w=None,
              block_rows=256):
    bsz, seq, _ = q.shape
    tb = block_rows
    nb = seq // tb
    final = reverse
    blk = (lambda b, j: (b, nb - 1 - j, 0)) if reverse else (lambda b, j: (b, j, 0))
    full2 = lambda b, j: (0, 0)
    full3 = lambda b, j: (0, 0, 0)
    in_specs = [
        pl.BlockSpec((None, tb, GLA_KEY), blk),
        pl.BlockSpec((None, tb, GLA_KEY), blk),
        pl.BlockSpec((None, tb, GLA_VAL), blk),
        pl.BlockSpec((None, tb, LANES), blk),
        pl.BlockSpec((2, LANES, GLA_KEY), full3),
        pl.BlockSpec((2, GLA_KEY), full2),
    ]
    args = [q, k, v, gl, gate_up_pad.astype(bf16), gate_bias]
    if final:
        in_specs += [
            pl.BlockSpec((None, tb, GLA_VAL), blk),
            pl.BlockSpec((None, tb, GLA_VAL), blk),
            pl.BlockSpec((1, GLA_HV), full2),
        ]
        args += [gate, o_fwd, norm_w.reshape(1, GLA_HV)]
    return pl.pallas_call(
        functools.partial(_gla_kernel, reverse=reverse, final=final, block_rows=tb),
        grid=(bsz, nb),
        in_specs=in_specs,
        out_specs=pl.BlockSpec((None, tb, GLA_VAL), blk),
        out_shape=jax.ShapeDtypeStruct((bsz, seq, GLA_VAL), bf16 if final else f32),
        scratch_shapes=[pltpu.VMEM((GLA_HEADS, GLA_HK, GLA_HV), f32)],
        compiler_params=_cparams("parallel", "arbitrary"),
        name="gla_bwd" if reverse else "gla_fwd",
    )(*args)


AT_NEG = -1e30
AT_QBLK = 128
AT_KBLK = 256
AT_UNROLL = 8


AT_FOLD = 4


def _attn_kernel(q_ref, k_ref, v_ref, g_ref, o_ref, qp, kp, vp, m_s, l_s, acc_s, bias_s, tmp_s,
                 *, seq):
    lane = lax.broadcasted_iota(jnp.int32, (1, LANES), 1)
    left = lane < AT_DH
    scale = AT_DH ** -0.5
    fold_len = seq // AT_FOLD

    def gather(src_ref, dst_ref, dil, mul=None):
        sub = seq // dil
        piece = min(sub, 512)
        pieces = sub // piece

        def body(i, carry):
            r = i // pieces
            part = i % pieces
            src = r + dil * part * piece
            dst = pl.multiple_of(r * sub + part * piece, piece)
            idx = pl.ds(src, piece, stride=dil) if dil > 1 else pl.ds(src, piece)
            x = src_ref[idx, :]
            dst_ref[pl.ds(dst, piece), :] = (x if mul is None else x * mul).astype(bf16)
            return carry

        lax.fori_loop(0, dil * pieces, body, 0)

    def fill_bias(offs, tq, nk, deltas):
        for d, delta in enumerate(deltas):
            bias_s[d, 0:tq, 0:nk] = jnp.where(jnp.abs(offs + delta) <= AT_SIDE, 0.0, AT_NEG)

    def run_blocks(nblocks, load_unit):
        def body(it, carry):
            units = [load_unit(it * AT_UNROLL + u) for u in range(AT_UNROLL)]
            heads = [(u, first) for u in range(AT_UNROLL) for first in (True, False)]
            scores = []
            for u, first in heads:
                qb, kb, _, bias, _, _ = units[u]
                qh = jnp.where(left if first else ~left, qb, jnp.zeros_like(qb))
                scores.append(_dot_nt(qh, kb) + bias)
            maxes = [jnp.max(s, axis=1, keepdims=True) for s in scores]
            probs = [jnp.exp(s - mh) for s, mh in zip(scores, maxes)]
            sums = [jnp.sum(p, axis=1, keepdims=True) for p in probs]
            outs = [_dot(p.astype(bf16), units[u][2]) for p, (u, _) in zip(probs, heads)]
            olds = [None if units[u][5] else
                    [(m_s[idx, :], l_s[idx, :], acc_s[idx, :]) for idx, _ in units[u][4]]
                    for u in range(AT_UNROLL)]
            for u in range(AT_UNROLL):
                m_new = jnp.where(left, maxes[2 * u], maxes[2 * u + 1])
                l_new = jnp.where(left, sums[2 * u], sums[2 * u + 1])
                a_new = jnp.where(left, outs[2 * u], outs[2 * u + 1])
                for pi, (idx, rows) in enumerate(units[u][4]):
                    mp, lp, ap = m_new[rows], l_new[rows], a_new[rows]
                    if units[u][5]:
                        m_s[idx, :] = mp
                        l_s[idx, :] = lp
                        acc_s[idx, :] = ap
                    else:
                        m_old, l_old, a_old = olds[u][pi]
                        m = jnp.maximum(m_old, mp)
                        w_old = jnp.exp(m_old - m)
                        w_new = jnp.exp(mp - m)
                        m_s[idx, :] = m
                        l_s[idx, :] = l_old * w_old + lp * w_new
                        acc_s[idx, :] = a_old * w_old + ap * w_new
            return carry

        lax.fori_loop(0, nblocks // AT_UNROLL, body, 0)

    def window(m0, sub, tq, nk):
        ks = jnp.clip(m0 - AT_SIDE, 0, sub - nk)
        return ks, (m0 - ks) // AT_SIDE

    tq, nk = AT_QBLK, AT_KBLK
    per = tq // AT_FOLD
    gather(q_ref, qp, AT_FOLD, scale)
    gather(k_ref, kp, 1)
    gather(v_ref, vp, 1)
    qi = lax.broadcasted_iota(jnp.int32, (tq, nk), 0)
    kj = lax.broadcasted_iota(jnp.int32, (tq, nk), 1)
    fill_bias((qi // per) + AT_FOLD * (qi % per) - kj, tq, nk, (0, AT_SIDE, 2 * AT_SIDE))

    def unit_d1(i):
        t0 = i * tq
        ks, bi = window(t0, seq, tq, nk)
        starts = [pl.multiple_of(rho * fold_len + i * per, per) for rho in range(AT_FOLD)]
        qb = jnp.concatenate([qp[pl.ds(s, per), :] for s in starts], axis=0)
        kb = kp[pl.ds(pl.multiple_of(ks, AT_SIDE), nk), :]
        vb = vp[pl.ds(pl.multiple_of(ks, AT_SIDE), nk), :]
        state = [(pl.ds(s, per), slice(rho * per, (rho + 1) * per)) for rho, s in enumerate(starts)]
        return qb, kb, vb, bias_s[bi], state, True

    run_blocks(seq // tq, unit_d1)

    sub = fold_len
    gather(k_ref, kp, AT_FOLD)
    gather(v_ref, vp, AT_FOLD)
    fill_bias(qi - kj, tq, nk, (0, AT_SIDE, 2 * AT_SIDE))
    nblk = sub // tq

    def unit_fold(i):
        r = i // nblk
        m0 = (i % nblk) * tq
        ks, bi = window(m0, sub, tq, nk)
        qrow = pl.ds(pl.multiple_of(r * sub + m0, tq), tq)
        krow = pl.ds(pl.multiple_of(r * sub + ks, AT_SIDE), nk)
        return qp[qrow, :], kp[krow, :], vp[krow, :], bias_s[bi], [(qrow, slice(0, tq))], False

    run_blocks(AT_FOLD * nblk, unit_fold)

    dil = AT_DILATIONS[-1]
    sub = seq // dil
    tq, nk = min(AT_QBLK, sub), min(AT_KBLK, sub)
    gather(q_ref, qp, dil, scale)
    gather(k_ref, kp, dil)
    gather(v_ref, vp, dil)
    fill_bias((qi - kj)[0:tq, 0:nk], tq, nk, (0, AT_SIDE, nk - tq))
    nblk = sub // tq

    def unit_wide(i):
        r = i // nblk
        m0 = (i % nblk) * tq
        ks, bi = window(m0, sub, tq, nk)
        qrow = pl.ds(pl.multiple_of(r * sub + m0, AT_SIDE), tq)
        krow = pl.ds(pl.multiple_of(r * sub + ks, AT_SIDE), nk)
        srow = (r % AT_FOLD) * fold_len + r // AT_FOLD + (dil // AT_FOLD) * m0
        state = [(pl.ds(srow, tq, stride=dil // AT_FOLD), slice(0, tq))]
        return qp[qrow, :], kp[krow, :], vp[krow, :], bias_s[bi, 0:tq, 0:nk], state, False

    run_blocks(dil * nblk, unit_wide)

    rows = tmp_s.shape[0]
    per_fold = rows // AT_FOLD

    def finish(i, carry):
        for rho in range(AT_FOLD):
            idx = pl.ds(pl.multiple_of(rho * fold_len + i * per_fold, per_fold), per_fold)
            tmp_s[pl.ds(rho, per_fold, stride=AT_FOLD), :] = acc_s[idx, :] / l_s[idx, :]
        out = pl.ds(pl.multiple_of(i * rows, rows), rows)
        o_ref[out, :] = (tmp_s[...] * _silu(g_ref[out, :])).astype(o_ref.dtype)
        return carry

    lax.fori_loop(0, seq // rows, finish, 0)


def _attention(qkv, gate):
    bsz, seq, _ = qkv.shape
    pairs = AT_WIDTH // LANES
    once = pl.Buffered(1)
    col = lambda off: (lambda b, p: (b, 0, off + p))
    return pl.pallas_call(
        functools.partial(_attn_kernel, seq=seq),
        grid=(bsz, pairs),
        in_specs=[
            pl.BlockSpec((None, seq, LANES), col(0), pipeline_mode=once),
            pl.BlockSpec((None, seq, LANES), col(pairs), pipeline_mode=once),
            pl.BlockSpec((None, seq, LANES), col(2 * pairs), pipeline_mode=once),
            pl.BlockSpec((None, seq, LANES), col(0), pipeline_mode=once),
        ],
        out_specs=pl.BlockSpec((None, seq, LANES), col(0)),
        out_shape=jax.ShapeDtypeStruct((bsz, seq, AT_WIDTH), bf16),
        scratch_shapes=([pltpu.VMEM((seq, LANES), bf16)] * 3 + [pltpu.VMEM((seq, LANES), f32)] * 3
                        + [pltpu.VMEM((3, AT_QBLK, AT_KBLK), f32),
                           pltpu.VMEM((min(seq, 512), LANES), f32)]),
        compiler_params=_cparams("parallel", "parallel"),
        name="dilated_attn",
    )(qkv, qkv, qkv, gate)


RW_GROUP = 2
RW_GW = RW_GROUP * RW_DH
RW_NGROUPS = RW_HEADS // RW_GROUP
RW_WAVE = 4


def _head_stack(x, masks):
    return jnp.concatenate([jnp.where(mk, x, jnp.zeros_like(x)) for mk in masks], axis=0)


def _head_stack_t(x, row_masks):
    xt = x.T
    return jnp.concatenate([jnp.where(mk, xt, jnp.zeros_like(xt)) for mk in row_masks], axis=1)


def _unit_lower_inverse_stages(w, src, dst):
    steps = int(math.log2(CHUNK)) - 1

    def start():
        n = w[src][0].shape[0]
        eye = (lax.broadcasted_iota(jnp.int32, (n, n), 0)
               == lax.broadcasted_iota(jnp.int32, (n, n), 1)).astype(f32)
        w["_t"] = [eye + a for a in w[src]]
        qs = [a.astype(bf16) for a in w[src]]
        w["_q"] = [_dot(q, q).astype(bf16) for q in qs]

    def double():
        n = w[src][0].shape[0]
        both = [_dot(jnp.concatenate([q, t.astype(bf16)], axis=0), q)
                for q, t in zip(w["_q"], w["_t"])]
        w["_q"] = [x[:n].astype(bf16) for x in both]
        w["_t"] = [t + x[n:] for t, x in zip(w["_t"], both)]

    def finish():
        w[dst] = [(t + _dot(t.astype(bf16), q)).astype(bf16) for q, t in zip(w["_q"], w["_t"])]

    return [start] + [double] * (steps - 1) + [finish]


def _rwkv_kernel(*refs, reverse, final, block_rows, nblocks):
    (ps_ref, prev_ref, next_ref, mup_ref, mun_ref, w0_ref, wup_ref, a0_ref, aup_ref,
     kk_ref, ka_ref, seg_ref) = refs[:12]
    if final:
        rk_ref, gnw_ref, gnb_ref, ga_ref, yf_ref, o_ref, sh_ref, st_ref = refs[12:]
    else:
        o_ref, sh_ref, st_ref = refs[12:]
    d = 1 if reverse else 0
    tb = block_rows
    j = pl.program_id(1)
    blk = (nblocks - 1 - j) if reverse else j

    @pl.when(j == 0)
    def _():
        st_ref[...] = jnp.zeros_like(st_ref)

    rid = lax.broadcasted_iota(jnp.int32, (tb, 1), 0)
    has_prev = (blk > 0).astype(f32)
    has_next = (blk < nblocks - 1).astype(f32)
    for c0 in range(0, EVEN_SHIFT, 256):
        c1 = min(c0 + 256, EVEN_SHIFT)
        x = ps_ref[:, c0:c1]
        before = prev_ref[7:8, c0:c1] * has_prev
        after = next_ref[0:1, c0:c1] * has_next
        prv = jnp.where(rid == 0, before, pltpu.roll(x, 1, 0))
        nxt = jnp.where(rid == tb - 1, after, pltpu.roll(x, tb - 1, 0))
        sh_ref[:, c0:c1] = x + mup_ref[:, c0:c1] * (prv - x) + mun_ref[:, c0:c1] * (nxt - x)

    incl1, _ = _scan_masks(CHUNK, reverse)
    tri = incl1.astype(bf16)
    gi = lax.broadcasted_iota(jnp.int32, (2 * RW_GW, 2 * RW_GW), 0)
    gj = lax.broadcasted_iota(jnp.int32, (2 * RW_GW, 2 * RW_GW), 1)
    ahead = (gi % CHUNK) - (gj % CHUNK)
    ahead = -ahead if reverse else ahead
    keep = ahead < jnp.where(gj < RW_GW, 0, 1)
    lane = lax.broadcasted_iota(jnp.int32, (1, RW_GW), 1)
    masks = [(lane // RW_DH) == h for h in range(RW_GROUP)]
    rowi = lax.broadcasted_iota(jnp.int32, (RW_GW, 1), 0)
    row_masks = [(rowi // RW_DH) == h for h in range(RW_GROUP)]
    seg = seg_ref[...]
    last = 0 if reverse else CHUNK - 1
    nchunks = tb // CHUNK
    w_lo, a_lo = 3 * RW_WIDTH, 3 * RW_WIDTH + RW_LORA

    def lr_gate(a_lat, dd):
        return _sigmoid(a0_ref[dd:dd + 1, :] + _dot(a_lat.astype(bf16), aup_ref[dd]))

    r = sh_ref[:, 0:RW_WIDTH]
    k = sh_ref[:, RW_WIDTH:2 * RW_WIDTH]
    v = sh_ref[:, 2 * RW_WIDTH:3 * RW_WIDTH]
    w_lat = sh_ref[:, w_lo:w_lo + RW_LORA]
    a_lat = sh_ref[:, a_lo:a_lo + RW_LORA]
    w_log = -RW_DECAY_SCALE * _sigmoid(
        w0_ref[d:d + 1, :] + _dot(jnp.tanh(w_lat).astype(bf16), wup_ref[d]))
    lr = lr_gate(a_lat, d)
    kk = k * kk_ref[...]
    kk = kk / jnp.maximum(jnp.sqrt(_dot_seg(kk * kk, seg)), 1e-12)
    k_dir = k * (1.0 + (lr - 1.0) * ka_ref[...])
    chunk_rows = [slice(c * CHUNK, (c + 1) * CHUNK) for c in range(nchunks)]
    cum = jnp.concatenate([_cumsum_rows(tri, w_log[rs]) for rs in chunk_rows], axis=0)
    grow = jnp.exp(-cum)
    a_t = -kk * jnp.exp(cum - w_log)
    b_t = kk * lr * grow
    k_t = k_dir * grow
    r_t = r * jnp.exp(cum)

    order = list(reversed(range(nchunks))) if reverse else list(range(nchunks))
    lanes = [slice(g * RW_GW, (g + 1) * RW_GW) for g in range(RW_NGROUPS)]
    state = [st_ref[g] for g in range(RW_NGROUPS)]
    y_rows = {}

    def wave_stages(chunks):
        units = [(c, g) for c in chunks for g in range(RW_NGROUPS)]
        n = range(len(units))
        w = {}

        def stacked():
            stack = lambda x: [_head_stack(x[chunk_rows[c], lanes[g]], masks) for c, g in units]
            stack_t = lambda x: [_head_stack_t(x[chunk_rows[c], lanes[g]], row_masks) for c, g in units]
            w["a_s"] = [t.astype(bf16) for t in stack(a_t)]
            w["r_s"] = [t.astype(bf16) for t in stack(r_t)]
            w["b_s"] = [t.astype(bf16) for t in stack(b_t)]
            w["k_s"] = [t.astype(bf16) for t in stack(k_t)]
            w["a_st"] = [t.astype(bf16) for t in stack_t(a_t)]
            w["r_st"] = [t.astype(bf16) for t in stack_t(r_t)]
            w["v_st"] = [t.astype(bf16) for t in stack_t(v)]

        def grams():
            gram = [jnp.where(keep, _dot_nt(jnp.concatenate([w["b_s"][i], w["k_s"][i]], axis=0),
                                             jnp.concatenate([w["a_s"][i], w["r_s"][i]], axis=0)),
                              0.0) for i in n]
            w["ab_t"] = [gram[i][0:RW_GW, 0:RW_GW] for i in n]
            w["ak_rk_t"] = [gram[i][RW_GW:, :].astype(bf16) for i in n]
            w["rb_t"] = [gram[i][0:RW_GW, RW_GW:].astype(bf16) for i in n]

        inverse = _unit_lower_inverse_stages(w, "ab_t", "t_inv_t")

        def value_products():
            x = [_dot(w["v_st"][i], w["ak_rk_t"][i]) for i in n]
            w["av_t"] = [t[:, 0:RW_GW].astype(bf16) for t in x]
            w["y_local_t"] = [t[:, RW_GW:] for t in x]
            w["vk"] = [_dot(w["v_st"][i], w["k_s"][i]) for i in n]

        def apply_inverse():
            x = [_dot(jnp.concatenate([w["av_t"][i], w["a_st"][i]], axis=0), w["t_inv_t"][i])
                 for i in n]
            w["z_t"] = [t[0:RW_GW] for t in x]
            w["wm_r_t"] = [jnp.concatenate([x[i][RW_GW:].astype(bf16), w["r_st"][i]], axis=1)
                           for i in n]
            w["rb_b"] = [jnp.concatenate([w["rb_t"][i], w["b_s"][i]], axis=1) for i in n]

        def recur(pos_c, c):
            def new_state():
                p_end = jnp.exp(cum[c * CHUNK + last:c * CHUNK + last + 1, :])
                w["y_t"] = []
                for g in range(RW_NGROUPS):
                    i = pos_c * RW_NGROUPS + g
                    s0 = state[g]
                    x = _dot(s0.astype(bf16), w["wm_r_t"][i])
                    u_t = (x[:, 0:RW_GW] + w["z_t"][i]).astype(bf16)
                    x2 = _dot(u_t, w["rb_b"][i])
                    state[g] = (s0 + x2[:, RW_GW:] + w["vk"][i]) * p_end[:, lanes[g]]
                    w["y_t"].append(x[:, RW_GW:] + x2[:, 0:RW_GW] + w["y_local_t"][i])

            def outputs():
                ys = []
                for g in range(RW_NGROUPS):
                    y = w["y_t"][g].T
                    yg = y[0:CHUNK]
                    for h in range(1, RW_GROUP):
                        yg = yg + y[h * CHUNK:(h + 1) * CHUNK]
                    ys.append(yg)
                y_rows[c] = jnp.concatenate(ys, axis=1)

            return [new_state, outputs]

        independent = [stacked, grams] + inverse + [value_products, apply_inverse]
        recurrence = [f for pos_c, c in enumerate(chunks) for f in recur(pos_c, c)]
        return independent, recurrence

    waves = [order[i:i + RW_WAVE] for i in range(0, nchunks, RW_WAVE)]
    pending = []
    for chunks in waves:
        independent, recurrence = wave_stages(chunks)
        every = max(1, len(independent) // (len(pending) + 1)) if pending else 0
        for si, stage in enumerate(independent):
            stage()
            if pending and (si + 1) % every == 0:
                pending.pop(0)()
        while pending:
            pending.pop(0)()
        pending = recurrence
    while pending:
        pending.pop(0)()
    for g in range(RW_NGROUPS):
        st_ref[g] = state[g]
    y_all = jnp.concatenate([y_rows[c] for c in range(nchunks)], axis=0)

    if not final:
        o_ref[...] = y_all
    else:
        y_all = y_all + yf_ref[...]
        mean = _dot_seg(y_all, seg) * (1.0 / RW_DH)
        cen = y_all - mean
        var = _dot_seg(cen * cen, seg) * (1.0 / RW_DH)
        yn = cen * lax.rsqrt(var + RW_GN_EPS) * gnw_ref[...] + gnb_ref[...]
        k_other = k * (1.0 + (lr_gate(a_lat, 1 - d) - 1.0) * ka_ref[...])
        rk = _dot_seg(r * (k_dir + k_other) * rk_ref[...], seg)
        o_ref[...] = ((yn + rk * v) * _silu(ga_ref[...])).astype(o_ref.dtype)


def _segment_ones():
    h = jnp.arange(LANES) // RW_DH
    return (h[:, None] == h[None, :]).astype(bf16)


def _rwkv_scan(ps, mu_prev, mu_next, w0, w_up, a0, a_up, k_k, k_a, *, reverse,
               r_k=None, gn_w=None, gn_b=None, gate=None, y_fwd=None, block_rows=512):
    bsz, seq, _ = ps.shape
    tb = block_rows
    nb = seq // tb
    final = reverse
    halo = tb // 8
    pos = (lambda j: nb - 1 - j) if reverse else (lambda j: j)
    blk = lambda b, j: (b, pos(j), 0)
    prev = lambda b, j: (b, jnp.maximum(pos(j) * halo - 1, 0), 0)
    nxt = lambda b, j: (b, jnp.minimum((pos(j) + 1) * halo, seq // 8 - 1), 0)
    full2 = lambda b, j: (0, 0)
    full3 = lambda b, j: (0, 0, 0)
    vec = lambda n: pl.BlockSpec((1, n), full2)
    in_specs = [
        pl.BlockSpec((None, tb, EVEN_SHIFT), blk),
        pl.BlockSpec((None, 8, EVEN_SHIFT), prev),
        pl.BlockSpec((None, 8, EVEN_SHIFT), nxt),
        vec(EVEN_SHIFT), vec(EVEN_SHIFT),
        pl.BlockSpec((2, RW_WIDTH), full2),
        pl.BlockSpec((2, RW_LORA, RW_WIDTH), full3),
        pl.BlockSpec((2, RW_WIDTH), full2),
        pl.BlockSpec((2, RW_LORA, RW_WIDTH), full3),
        vec(RW_WIDTH), vec(RW_WIDTH),
        pl.BlockSpec((LANES, LANES), full2),
    ]
    row = lambda t: t.reshape(1, -1)
    args = [ps, ps, ps, row(mu_prev), row(mu_next), w0, w_up.astype(bf16), a0, a_up.astype(bf16),
            row(k_k), row(k_a),
            _segment_ones()]
    if final:
        in_specs += [vec(RW_WIDTH), vec(RW_WIDTH), vec(RW_WIDTH),
                     pl.BlockSpec((None, tb, RW_WIDTH), blk),
                     pl.BlockSpec((None, tb, RW_WIDTH), blk)]
        args += [row(r_k), row(gn_w), row(gn_b), gate, y_fwd]
    return pl.pallas_call(
        functools.partial(_rwkv_kernel, reverse=reverse, final=final, block_rows=tb, nblocks=nb),
        grid=(bsz, nb),
        in_specs=in_specs,
        out_specs=pl.BlockSpec((None, tb, RW_WIDTH), blk),
        out_shape=jax.ShapeDtypeStruct((bsz, seq, RW_WIDTH), bf16 if final else f32),
        scratch_shapes=[pltpu.VMEM((tb, EVEN_SHIFT), f32),
                        pltpu.VMEM((RW_NGROUPS, RW_GW, RW_GW), f32)],
        compiler_params=_cparams("parallel", "arbitrary"),
        name="rwkv_bwd" if reverse else "rwkv_fwd",
    )(*args)


def _trunk(x, p):
    bsz, seq, _ = x.shape
    tokens = bsz * seq
    x2d = x.reshape(tokens, D_MODEL)
    seq3 = lambda t: t.reshape(bsz, seq, t.shape[-1])
    flat = lambda t: t.reshape(tokens, t.shape[-1])

    ps, ga, qkv, gb = _even_in(x2d, seq, p["even_norm"], p["even_w_in"])
    rw = (seq3(ps), p["mu_prev"], p["mu_next"], p["w0"], p["w_up"], p["a0"], p["a_up"],
          p["k_k"], p["k_a"])
    y_fwd = _rwkv_scan(*rw, reverse=False)
    ya = _rwkv_scan(*rw, reverse=True, r_k=p["r_k"], gn_w=p["gn_w"], gn_b=p["gn_b"],
                    gate=seq3(ga), y_fwd=y_fwd)
    yb = _attention(seq3(qkv), seq3(gb))

    x1, q, k, v, gate, gate_lat = _mid(x2d, flat(ya), flat(yb), p["even_w_out"], p["odd_norm"],
                                       p["odd_w_in"])
    gla = (seq3(q), seq3(k), seq3(v), seq3(gate_lat), p["gate_up"], p["gate_bias"])
    o_fwd = _gla_scan(*gla, reverse=False)
    yc = _gla_scan(*gla, reverse=True, gate=seq3(gate), o_fwd=o_fwd, norm_w=p["gla_norm"])
    return _final(x1, flat(yc), p["odd_w_out"], p["final_norm"]).reshape(bsz, seq, D_MODEL)


def _prepare(even_norm, even_w_in, even_mu_prev, even_mu_next, rwkv_w0, rwkv_w_up, rwkv_a0,
             rwkv_a_up, rwkv_k_k, rwkv_k_a, rwkv_r_k, rwkv_gn_w, rwkv_gn_b, even_w_out, odd_norm,
             odd_w_in, gla_gate_up, gla_gate_bias, gla_norm, odd_w_out, final_norm):
    wi = odd_w_in[0]
    lat0 = 2 * GLA_KEY + GLA_VAL
    lat = jnp.pad(wi[:, lat0:lat0 + GLA_RANK], ((0, 0), (0, LANES - GLA_RANK)))
    odd_in = jnp.concatenate([wi[:, :lat0], wi[:, lat0 + GLA_RANK:], lat], axis=1)
    return {
        "even_norm": even_norm[0], "even_w_in": even_w_in[0].astype(bf16),
        "mu_prev": even_mu_prev[0], "mu_next": even_mu_next[0],
        "w0": rwkv_w0[0], "w_up": rwkv_w_up[0], "a0": rwkv_a0[0], "a_up": rwkv_a_up[0],
        "k_k": rwkv_k_k[0], "k_a": rwkv_k_a[0], "r_k": rwkv_r_k[0],
        "gn_w": rwkv_gn_w[0], "gn_b": rwkv_gn_b[0],
        "even_w_out": even_w_out[0].astype(bf16),
        "odd_norm": odd_norm[0], "odd_w_in": odd_in.astype(bf16),
        "gate_up": jnp.pad(gla_gate_up[0], ((0, 0), (0, LANES - GLA_RANK), (0, 0))),
        "gate_bias": gla_gate_bias[0], "gla_norm": gla_norm[0],
        "odd_w_out": odd_w_out[0].astype(bf16), "final_norm": final_norm,
    }


def kernel(x_prompt, x_sample, even_norm, even_w_in, even_mu_prev, even_mu_next, rwkv_w0, rwkv_w_up,
           rwkv_a0, rwkv_a_up, rwkv_k_k, rwkv_k_a, rwkv_r_k, rwkv_gn_w, rwkv_gn_b, even_w_out,
           odd_norm, odd_w_in, gla_gate_up, gla_gate_bias, gla_norm, odd_w_out, final_norm):
    p = _prepare(even_norm, even_w_in, even_mu_prev, even_mu_next, rwkv_w0, rwkv_w_up, rwkv_a0,
                 rwkv_a_up, rwkv_k_k, rwkv_k_a, rwkv_r_k, rwkv_gn_w, rwkv_gn_b, even_w_out,
                 odd_norm, odd_w_in, gla_gate_up, gla_gate_bias, gla_norm, odd_w_out, final_norm)
    return (_trunk(x_prompt, p), _trunk(x_sample, p))
```

```python
import functools
import math

import jax
import jax.numpy as jnp
from jax import lax
from jax.experimental import pallas as pl
from jax.experimental.pallas import tpu as pltpu

f32 = jnp.float32
bf16 = jnp.bfloat16

D_MODEL = 1024
RMS_EPS = 1e-6

RW_HEADS = 8
RW_DH = 64
RW_WIDTH = RW_HEADS * RW_DH
RW_LORA = 64
RW_DECAY_SCALE = 0.6065306597126334
RW_GN_EPS = 64e-5
AT_HEADS = 8
AT_DH = 64
AT_WIDTH = AT_HEADS * AT_DH
AT_SIDE = 64
AT_DILATIONS = (1, 4, 16)
ROPE_THETA = 10000.0
EVEN_SHIFT = 3 * RW_WIDTH + 2 * RW_LORA
EVEN_COLS = EVEN_SHIFT + RW_WIDTH + 4 * AT_WIDTH

GLA_HEADS = 4
GLA_KEY = 512
GLA_VAL = 1024
GLA_HK = GLA_KEY // GLA_HEADS
GLA_HV = GLA_VAL // GLA_HEADS
GLA_RANK = 16
GLA_GATE_NORM = 16.0
ODD_COLS = 2 * GLA_KEY + GLA_VAL + GLA_RANK + GLA_VAL

CHUNK = 64
LANES = 128
VMEM_LIMIT = 56 * 1024 * 1024

_NT = (((1,), (1,)), ((), ()))
_TN = (((0,), (0,)), ((), ()))


def _dot(a, b):
    return jnp.dot(a, b, preferred_element_type=f32)


def _dot_nt(a, b):
    return lax.dot_general(a, b, _NT, preferred_element_type=f32)


def _dot_tn(a, b):
    return lax.dot_general(a, b, _TN, preferred_element_type=f32)


def _split2(x):
    hi = x.astype(bf16)
    lo = (x - hi.astype(f32)).astype(bf16)
    return hi, lo


def _dot_seg(x, e):
    xb = x.astype(bf16)
    return jnp.concatenate([_dot(xb[:, c:c + LANES], e) for c in range(0, x.shape[1], LANES)], axis=1)


def _cumsum_rows(tri, x):
    hi, lo = _split2(x)
    return _dot(tri, hi) + _dot(tri, lo)


def _sigmoid(x):
    return 0.5 * (jnp.tanh(0.5 * x) + 1.0)


def _silu(x):
    return x * _sigmoid(x)


def _rms_rows(x, w):
    return x * lax.rsqrt(jnp.mean(x * x, axis=-1, keepdims=True) + RMS_EPS) * w


def _cparams(*sem):
    return pltpu.CompilerParams(dimension_semantics=sem, vmem_limit_bytes=VMEM_LIMIT)


def _rope_partner(x):
    lane = lax.broadcasted_iota(jnp.int32, x.shape, 1)
    return jnp.where((lane & 32) == 0, pltpu.roll(x, LANES - 32, 1), pltpu.roll(x, 32, 1))


def _even_in_kernel(x_ref, nw_ref, w_ref, cos_ref, sin_ref, ps_ref, ga_ref, qkv_ref, gb_ref):
    xn = _rms_rows(x_ref[...], nw_ref[...]).astype(bf16)
    cos = cos_ref[...]
    sin = sin_ref[...]
    dests = ([(ps_ref, c, False) for c in range(0, EVEN_SHIFT, LANES)]
             + [(ga_ref, c, False) for c in range(0, RW_WIDTH, LANES)]
             + [(qkv_ref, c, True) for c in range(0, 2 * AT_WIDTH, LANES)]
             + [(qkv_ref, c, False) for c in range(2 * AT_WIDTH, 3 * AT_WIDTH, LANES)]
             + [(gb_ref, c, False) for c in range(0, AT_WIDTH, LANES)])
    tile = 512
    for c0 in range(0, EVEN_COLS, tile):
        c1 = min(c0 + tile, EVEN_COLS)
        acc = _dot(xn, w_ref[:, c0:c1])
        for j in range((c1 - c0) // LANES):
            ref, off, rotary = dests[c0 // LANES + j]
            t = acc[:, j * LANES:(j + 1) * LANES]
            ref[:, off:off + LANES] = t * cos + _rope_partner(t) * sin if rotary else t


def _rope_tables(seq):
    inv = ROPE_THETA ** (-jnp.arange(0, AT_DH, 2, dtype=f32) / AT_DH)
    ang = jnp.arange(seq, dtype=f32)[:, None] * inv[None, :]
    cos, sin = jnp.cos(ang), jnp.sin(ang)
    return (jnp.concatenate([cos, cos, cos, cos], axis=-1),
            jnp.concatenate([-sin, sin, -sin, sin], axis=-1))


def _even_in(x2d, seq, norm_w, w_in_bf16, block_rows=512):
    m = x2d.shape[0]
    tm = block_rows
    per_seq = seq // tm
    cos, sin = _rope_tables(seq)
    row = lambda i: (i, 0)
    full = lambda i: (0, 0)
    tab = lambda i: (i % per_seq, 0)
    return pl.pallas_call(
        _even_in_kernel,
        grid=(m // tm,),
        in_specs=[
            pl.BlockSpec((tm, D_MODEL), row),
            pl.BlockSpec((1, D_MODEL), full),
            pl.BlockSpec((D_MODEL, EVEN_COLS), full),
            pl.BlockSpec((tm, LANES), tab),
            pl.BlockSpec((tm, LANES), tab),
        ],
        out_specs=[
            pl.BlockSpec((tm, EVEN_SHIFT), row),
            pl.BlockSpec((tm, RW_WIDTH), row),
            pl.BlockSpec((tm, 3 * AT_WIDTH), row),
            pl.BlockSpec((tm, AT_WIDTH), row),
        ],
        out_shape=[
            jax.ShapeDtypeStruct((m, EVEN_SHIFT), f32),
            jax.ShapeDtypeStruct((m, RW_WIDTH), f32),
            jax.ShapeDtypeStruct((m, 3 * AT_WIDTH), f32),
            jax.ShapeDtypeStruct((m, AT_WIDTH), f32),
        ],
        compiler_params=_cparams("parallel"),
        name="even_in",
    )(x2d, norm_w.reshape(1, D_MODEL), w_in_bf16, cos, sin)


ODD_PAD_COLS = 2 * GLA_KEY + 2 * GLA_VAL + LANES


def _mid_kernel(x_ref, ya_ref, yb_ref, wo_ref, nw_ref, wi_ref,
                x1_ref, q_ref, k_ref, v_ref, g_ref, gl_ref):
    x1 = (x_ref[...] + _dot(ya_ref[...], wo_ref[0:RW_WIDTH, :])
          + _dot(yb_ref[...], wo_ref[RW_WIDTH:RW_WIDTH + AT_WIDTH, :]))
    x1_ref[...] = x1
    xn = _rms_rows(x1, nw_ref[...]).astype(bf16)
    c = 0
    for ref, width in ((q_ref, GLA_KEY), (k_ref, GLA_KEY), (v_ref, GLA_VAL), (g_ref, GLA_VAL),
                       (gl_ref, LANES)):
        for c0 in range(0, width, 512):
            c1 = min(c0 + 512, width)
            ref[:, c0:c1] = _dot(xn, wi_ref[:, c + c0:c + c1]).astype(ref.dtype)
        c += width


def _mid(x2d, ya, yb, w_out_bf16, norm_w, w_in_pad_bf16, block_rows=512):
    m = x2d.shape[0]
    tm = block_rows
    row = lambda i: (i, 0)
    full = lambda i: (0, 0)
    widths = (D_MODEL, GLA_KEY, GLA_KEY, GLA_VAL, GLA_VAL, LANES)
    return pl.pallas_call(
        _mid_kernel,
        grid=(m // tm,),
        in_specs=[
            pl.BlockSpec((tm, D_MODEL), row),
            pl.BlockSpec((tm, RW_WIDTH), row),
            pl.BlockSpec((tm, AT_WIDTH), row),
            pl.BlockSpec((RW_WIDTH + AT_WIDTH, D_MODEL), full),
            pl.BlockSpec((1, D_MODEL), full),
            pl.BlockSpec((D_MODEL, ODD_PAD_COLS), full),
        ],
        out_specs=[pl.BlockSpec((tm, w), row) for w in widths],
        out_shape=[jax.ShapeDtypeStruct((m, w), bf16 if i == 3 else f32)
                   for i, w in enumerate(widths)],
        compiler_params=_cparams("parallel"),
        name="mid",
    )(x2d, ya, yb, w_out_bf16, norm_w.reshape(1, D_MODEL), w_in_pad_bf16)


def _final_kernel(x_ref, y_ref, wo_ref, nw_ref, o_ref):
    x2 = x_ref[...] + _dot(y_ref[...], wo_ref[...])
    o_ref[...] = _rms_rows(x2, nw_ref[...])


def _final(x2d, yc, w_out_bf16, norm_w, block_rows=512):
    m = x2d.shape[0]
    tm = block_rows
    row = lambda i: (i, 0)
    full = lambda i: (0, 0)
    return pl.pallas_call(
        _final_kernel,
        grid=(m // tm,),
        in_specs=[
            pl.BlockSpec((tm, D_MODEL), row),
            pl.BlockSpec((tm, GLA_VAL), row),
            pl.BlockSpec((GLA_VAL, D_MODEL), full),
            pl.BlockSpec((1, D_MODEL), full),
        ],
        out_specs=pl.BlockSpec((tm, D_MODEL), row),
        out_shape=jax.ShapeDtypeStruct((m, D_MODEL), f32),
        compiler_params=_cparams("parallel"),
        name="final",
    )(x2d, yc, w_out_bf16, norm_w.reshape(1, D_MODEL))


def _scan_masks(n, reverse):
    t = lax.broadcasted_iota(jnp.int32, (n, n), 0) % CHUNK
    s = lax.broadcasted_iota(jnp.int32, (n, n), 1) % CHUNK
    if reverse:
        return s >= t, s > t
    return s <= t, s < t


def _row_to_col(row):
    n = row.shape[1]
    eye = (lax.broadcasted_iota(jnp.int32, (n, n), 0) == lax.broadcasted_iota(jnp.int32, (n, n), 1))
    return jnp.sum(jnp.where(eye, jnp.broadcast_to(row, (n, n)), 0.0), axis=1, keepdims=True)


def _gla_kernel(*refs, reverse, final, block_rows):
    if final:
        (q_ref, k_ref, v_ref, gl_ref, gup_ref, gb_ref, gate_ref, of_ref, nw_ref, o_ref, st_ref) = refs
    else:
        (q_ref, k_ref, v_ref, gl_ref, gup_ref, gb_ref, o_ref, st_ref) = refs
    d = 1 if reverse else 0

    @pl.when(pl.program_id(1) == 0)
    def _():
        st_ref[...] = jnp.zeros_like(st_ref)

    incl, _ = _scan_masks(CHUNK, reverse)
    tri = incl.astype(bf16)
    gup = gup_ref[d]
    gbias = gb_ref[d:d + 1, :]
    nchunks = block_rows // CHUNK
    last = 0 if reverse else CHUNK - 1

    x = _dot(gl_ref[...].astype(bf16), gup) + gbias
    g = (jnp.minimum(x, 0.0) - jnp.log(1.0 + jnp.exp(-jnp.abs(x)))) / GLA_GATE_NORM
    chunk_rows = [slice(c * CHUNK, (c + 1) * CHUNK) for c in range(nchunks)]
    bcum = jnp.concatenate([_cumsum_rows(tri, g[rs]) for rs in chunk_rows], axis=0)
    b_last = [bcum[c * CHUNK + last:c * CHUNK + last + 1, :] for c in range(nchunks)]
    b_end = jnp.concatenate([jnp.broadcast_to(b, (CHUNK, GLA_KEY)) for b in b_last], axis=0)
    q = q_ref[...] * (GLA_HK ** -0.5)
    k = k_ref[...]
    q_dec = (q * jnp.exp(bcum)).astype(bf16)
    k_dec = (k * jnp.exp(-bcum)).astype(bf16)
    k_end = (k * jnp.exp(b_end - bcum)).astype(bf16)

    order = list(reversed(range(nchunks))) if reverse else list(range(nchunks))
    klanes = [slice(h * GLA_HK, (h + 1) * GLA_HK) for h in range(GLA_HEADS)]
    vlanes = [slice(h * GLA_HV, (h + 1) * GLA_HV) for h in range(GLA_HEADS)]
    units = [(c, h) for c in order for h in range(GLA_HEADS)]
    v = [v_ref[chunk_rows[c], vlanes[h]].astype(bf16) for c, h in units]
    att = [jnp.where(incl, _dot_nt(q_dec[chunk_rows[c], klanes[h]], k_dec[chunk_rows[c], klanes[h]]),
                     0.0).astype(bf16) for c, h in units]
    o_intra = [_dot(att[i], v[i]) for i in range(len(units))]
    kv = [_dot_tn(k_end[chunk_rows[c], klanes[h]], v[i]) for i, (c, h) in enumerate(units)]
    dcol = [_row_to_col(jnp.exp(b_last[c][:, klanes[h]])) for c, h in units]

    state = [st_ref[h] for h in range(GLA_HEADS)]
    for i, (c, h) in enumerate(units):
        rows = chunk_rows[c]
        o = o_intra[i] + _dot(q_dec[rows, klanes[h]], state[h].astype(bf16))
        state[h] = state[h] * dcol[i] + kv[i]
        if final:
            o = o + of_ref[rows, vlanes[h]]
            o = o * lax.rsqrt(jnp.mean(o * o, axis=-1, keepdims=True) + RMS_EPS) * nw_ref[...]
            o_ref[rows, vlanes[h]] = (o * _silu(gate_ref[rows, vlanes[h]])).astype(o_ref.dtype)
        else:
            o_ref[rows, vlanes[h]] = o
    for h in range(GLA_HEADS):
        st_ref[h] = state[h]


def _gla_scan(q, k, v, gl, gate_up_pad, gate_bias, *, reverse, gate=None, o_fwd=None, norm_w=None,
              block_rows=512):
    bsz, seq, _ = q.shape
    tb = block_rows
    nb = seq // tb
    final = reverse
    blk = (lambda b, j: (b, nb - 1 - j, 0)) if reverse else (lambda b, j: (b, j, 0))
    full2 = lambda b, j: (0, 0)
    full3 = lambda b, j: (0, 0, 0)
    in_specs = [
        pl.BlockSpec((None, tb, GLA_KEY), blk),
        pl.BlockSpec((None, tb, GLA_KEY), blk),
        pl.BlockSpec((None, tb, GLA_VAL), blk),
        pl.BlockSpec((None, tb, LANES), blk),
        pl.BlockSpec((2, LANES, GLA_KEY), full3),
        pl.BlockSpec((2, GLA_KEY), full2),
    ]
    args = [q, k, v, gl, gate_up_pad.astype(bf16), gate_bias]
    if final:
        in_specs += [
            pl.BlockSpec((None, tb, GLA_VAL), blk),
            pl.BlockSpec((None, tb, GLA_VAL), blk),
            pl.BlockSpec((1, GLA_HV), full2),
        ]
        args += [gate, o_fwd, norm_w.reshape(1, GLA_HV)]
    return pl.pallas_call(
        functools.partial(_gla_kernel, reverse=reverse, final=final, block_rows=tb),
        grid=(bsz, nb),
        in_specs=in_specs,
        out_specs=pl.BlockSpec((None, tb, GLA_VAL), blk),
        out_shape=jax.ShapeDtypeStruct((bsz, seq, GLA_VAL), bf16 if final else f32),
        scratch_shapes=[pltpu.VMEM((GLA_HEADS, GLA_HK, GLA_HV), f32)],
        compiler_params=_cparams("parallel", "arbitrary"),
        name="gla_bwd" if reverse else "gla_fwd",
    )(*args)


AT_NEG = -1e30
AT_QBLK = 128
AT_KBLK = 256
AT_UNROLL = 8


AT_FOLD = 4


def _attn_kernel(q_ref, k_ref, v_ref, g_ref, o_ref, qp, kp, vp, m_s, l_s, acc_s, bias_s, tmp_s,
                 *, seq):
    lane = lax.broadcasted_iota(jnp.int32, (1, LANES), 1)
    left = lane < AT_DH
    scale = AT_DH ** -0.5
    fold_len = seq // AT_FOLD

    def gather(src_ref, dst_ref, dil, mul=None):
        sub = seq // dil
        piece = min(sub, 512)
        pieces = sub // piece

        def body(i, carry):
            r = i // pieces
            part = i % pieces
            src = r + dil * part * piece
            dst = pl.multiple_of(r * sub + part * piece, piece)
            idx = pl.ds(src, piece, stride=dil) if dil > 1 else pl.ds(src, piece)
            x = src_ref[idx, :]
            dst_ref[pl.ds(dst, piece), :] = (x if mul is None else x * mul).astype(bf16)
            return carry

        lax.fori_loop(0, dil * pieces, body, 0)

    def fill_bias(offs, tq, nk, deltas):
        for d, delta in enumerate(deltas):
            bias_s[d, 0:tq, 0:nk] = jnp.where(jnp.abs(offs + delta) <= AT_SIDE, 0.0, AT_NEG)

    def run_blocks(nblocks, load_unit):
        def body(it, carry):
            units = [load_unit(it * AT_UNROLL + u) for u in range(AT_UNROLL)]
            heads = [(u, first) for u in range(AT_UNROLL) for first in (True, False)]
            scores = []
            for u, first in heads:
                qb, kb, _, bias, _, _ = units[u]
                qh = jnp.where(left if first else ~left, qb, jnp.zeros_like(qb))
                scores.append(_dot_nt(qh, kb) + bias)
            maxes = [jnp.max(s, axis=1, keepdims=True) for s in scores]
            probs = [jnp.exp(s - mh) for s, mh in zip(scores, maxes)]
            sums = [jnp.sum(p, axis=1, keepdims=True) for p in probs]
            outs = [_dot(p.astype(bf16), units[u][2]) for p, (u, _) in zip(probs, heads)]
            olds = [None if units[u][5] else
                    [(m_s[idx, :], l_s[idx, :], acc_s[idx, :]) for idx, _ in units[u][4]]
                    for u in range(AT_UNROLL)]
            for u in range(AT_UNROLL):
                m_new = jnp.where(left, maxes[2 * u], maxes[2 * u + 1])
                l_new = jnp.where(left, sums[2 * u], sums[2 * u + 1])
                a_new = jnp.where(left, outs[2 * u], outs[2 * u + 1])
                for pi, (idx, rows) in enumerate(units[u][4]):
                    mp, lp, ap = m_new[rows], l_new[rows], a_new[rows]
                    if units[u][5]:
                        m_s[idx, :] = mp
                        l_s[idx, :] = lp
                        acc_s[idx, :] = ap
                    else:
                        m_old, l_old, a_old = olds[u][pi]
                        m = jnp.maximum(m_old, mp)
                        w_old = jnp.exp(m_old - m)
                        w_new = jnp.exp(mp - m)
                        m_s[idx, :] = m
                        l_s[idx, :] = l_old * w_old + lp * w_new
                        acc_s[idx, :] = a_old * w_old + ap * w_new
            return carry

        lax.fori_loop(0, nblocks // AT_UNROLL, body, 0)

    def window(m0, sub, tq, nk):
        ks = jnp.clip(m0 - AT_SIDE, 0, sub - nk)
        return ks, (m0 - ks) // AT_SIDE

    tq, nk = AT_QBLK, AT_KBLK
    per = tq // AT_FOLD
    gather(q_ref, qp, AT_FOLD, scale)
    gather(k_ref, kp, 1)
    gather(v_ref, vp, 1)
    qi = lax.broadcasted_iota(jnp.int32, (tq, nk), 0)
    kj = lax.broadcasted_iota(jnp.int32, (tq, nk), 1)
    fill_bias((qi // per) + AT_FOLD * (qi % per) - kj, tq, nk, (0, AT_SIDE, 2 * AT_SIDE))

    def unit_d1(i):
        t0 = i * tq
        ks, bi = window(t0, seq, tq, nk)
        starts = [pl.multiple_of(rho * fold_len + i * per, per) for rho in range(AT_FOLD)]
        qb = jnp.concatenate([qp[pl.ds(s, per), :] for s in starts], axis=0)
        kb = kp[pl.ds(pl.multiple_of(ks, AT_SIDE), nk), :]
        vb = vp[pl.ds(pl.multiple_of(ks, AT_SIDE), nk), :]
        state = [(pl.ds(s, per), slice(rho * per, (rho + 1) * per)) for rho, s in enumerate(starts)]
        return qb, kb, vb, bias_s[bi], state, True

    run_blocks(seq // tq, unit_d1)

    sub = fold_len
    gather(k_ref, kp, AT_FOLD)
    gather(v_ref, vp, AT_FOLD)
    fill_bias(qi - kj, tq, nk, (0, AT_SIDE, 2 * AT_SIDE))
    nblk = sub // tq

    def unit_fold(i):
        r = i // nblk
        m0 = (i % nblk) * tq
        ks, bi = window(m0, sub, tq, nk)
        qrow = pl.ds(pl.multiple_of(r * sub + m0, tq), tq)
        krow = pl.ds(pl.multiple_of(r * sub + ks, AT_SIDE), nk)
        return qp[qrow, :], kp[krow, :], vp[krow, :], bias_s[bi], [(qrow, slice(0, tq))], False

    run_blocks(AT_FOLD * nblk, unit_fold)

    dil = AT_DILATIONS[-1]
    sub = seq // dil
    tq, nk = min(AT_QBLK, sub), min(AT_KBLK, sub)
    gather(q_ref, qp, dil, scale)
    gather(k_ref, kp, dil)
    gather(v_ref, vp, dil)
    fill_bias((qi - kj)[0:tq, 0:nk], tq, nk, (0, AT_SIDE, nk - tq))
    nblk = sub // tq

    def unit_wide(i):
        r = i // nblk
        m0 = (i % nblk) * tq
        ks, bi = window(m0, sub, tq, nk)
        qrow = pl.ds(pl.multiple_of(r * sub + m0, AT_SIDE), tq)
        krow = pl.ds(pl.multiple_of(r * sub + ks, AT_SIDE), nk)
        srow = (r % AT_FOLD) * fold_len + r // AT_FOLD + (dil // AT_FOLD) * m0
        state = [(pl.ds(srow, tq, stride=dil // AT_FOLD), slice(0, tq))]
        return qp[qrow, :], kp[krow, :], vp[krow, :], bias_s[bi, 0:tq, 0:nk], state, False

    run_blocks(dil * nblk, unit_wide)

    rows = tmp_s.shape[0]
    per_fold = rows // AT_FOLD

    def finish(i, carry):
        for rho in range(AT_FOLD):
            idx = pl.ds(pl.multiple_of(rho * fold_len + i * per_fold, per_fold), per_fold)
            tmp_s[pl.ds(rho, per_fold, stride=AT_FOLD), :] = acc_s[idx, :] / l_s[idx, :]
        out = pl.ds(pl.multiple_of(i * rows, rows), rows)
        o_ref[out, :] = (tmp_s[...] * _silu(g_ref[out, :])).astype(o_ref.dtype)
        return carry

    lax.fori_loop(0, seq // rows, finish, 0)


def _attention(qkv, gate):
    bsz, seq, _ = qkv.shape
    pairs = AT_WIDTH // LANES
    once = pl.Buffered(1)
    col = lambda off: (lambda b, p: (b, 0, off + p))
    return pl.pallas_call(
        functools.partial(_attn_kernel, seq=seq),
        grid=(bsz, pairs),
        in_specs=[
            pl.BlockSpec((None, seq, LANES), col(0), pipeline_mode=once),
            pl.BlockSpec((None, seq, LANES), col(pairs), pipeline_mode=once),
            pl.BlockSpec((None, seq, LANES), col(2 * pairs), pipeline_mode=once),
            pl.BlockSpec((None, seq, LANES), col(0), pipeline_mode=once),
        ],
        out_specs=pl.BlockSpec((None, seq, LANES), col(0)),
        out_shape=jax.ShapeDtypeStruct((bsz, seq, AT_WIDTH), bf16),
        scratch_shapes=([pltpu.VMEM((seq, LANES), bf16)] * 3 + [pltpu.VMEM((seq, LANES), f32)] * 3
                        + [pltpu.VMEM((3, AT_QBLK, AT_KBLK), f32),
                           pltpu.VMEM((min(seq, 512), LANES), f32)]),
        compiler_params=_cparams("parallel", "parallel"),
        name="dilated_attn",
    )(qkv, qkv, qkv, gate)


RW_GROUP = 2
RW_GW = RW_GROUP * RW_DH
RW_NGROUPS = RW_HEADS // RW_GROUP
RW_WAVE = 4


def _head_stack(x, masks):
    return jnp.concatenate([jnp.where(mk, x, jnp.zeros_like(x)) for mk in masks], axis=0)


def _head_stack_t(x, row_masks):
    xt = x.T
    return jnp.concatenate([jnp.where(mk, xt, jnp.zeros_like(xt)) for mk in row_masks], axis=1)


def _unit_lower_inverse_stages(w, src, dst):
    steps = int(math.log2(CHUNK)) - 1

    def start():
        n = w[src][0].shape[0]
        eye = (lax.broadcasted_iota(jnp.int32, (n, n), 0)
               == lax.broadcasted_iota(jnp.int32, (n, n), 1)).astype(f32)
        w["_t"] = [eye + a for a in w[src]]
        qs = [a.astype(bf16) for a in w[src]]
        w["_q"] = [_dot(q, q).astype(bf16) for q in qs]

    def double():
        n = w[src][0].shape[0]
        both = [_dot(jnp.concatenate([q, t.astype(bf16)], axis=0), q)
                for q, t in zip(w["_q"], w["_t"])]
        w["_q"] = [x[:n].astype(bf16) for x in both]
        w["_t"] = [t + x[n:] for t, x in zip(w["_t"], both)]

    def finish():
        w[dst] = [(t + _dot(t.astype(bf16), q)).astype(bf16) for q, t in zip(w["_q"], w["_t"])]

    return [start] + [double] * (steps - 1) + [finish]


def _rwkv_kernel(*refs, reverse, final, block_rows, nblocks):
    (ps_ref, prev_ref, next_ref, mup_ref, mun_ref, w0_ref, wup_ref, a0_ref, aup_ref,
     kk_ref, ka_ref, seg_ref) = refs[:12]
    if final:
        rk_ref, gnw_ref, gnb_ref, ga_ref, yf_ref, o_ref, sh_ref, st_ref = refs[12:]
    else:
        o_ref, sh_ref, st_ref = refs[12:]
    d = 1 if reverse else 0
    tb = block_rows
    j = pl.program_id(1)
    blk = (nblocks - 1 - j) if reverse else j

    @pl.when(j == 0)
    def _():
        st_ref[...] = jnp.zeros_like(st_ref)

    rid = lax.broadcasted_iota(jnp.int32, (tb, 1), 0)
    has_prev = (blk > 0).astype(f32)
    has_next = (blk < nblocks - 1).astype(f32)
    for c0 in range(0, EVEN_SHIFT, 256):
        c1 = min(c0 + 256, EVEN_SHIFT)
        x = ps_ref[:, c0:c1]
        before = prev_ref[7:8, c0:c1] * has_prev
        after = next_ref[0:1, c0:c1] * has_next
        prv = jnp.where(rid == 0, before, pltpu.roll(x, 1, 0))
        nxt = jnp.where(rid == tb - 1, after, pltpu.roll(x, tb - 1, 0))
        sh_ref[:, c0:c1] = x + mup_ref[:, c0:c1] * (prv - x) + mun_ref[:, c0:c1] * (nxt - x)

    incl1, _ = _scan_masks(CHUNK, reverse)
    tri = incl1.astype(bf16)
    gi = lax.broadcasted_iota(jnp.int32, (2 * RW_GW, 2 * RW_GW), 0)
    gj = lax.broadcasted_iota(jnp.int32, (2 * RW_GW, 2 * RW_GW), 1)
    ahead = (gi % CHUNK) - (gj % CHUNK)
    ahead = -ahead if reverse else ahead
    keep = ahead < jnp.where(gj < RW_GW, 0, 1)
    lane = lax.broadcasted_iota(jnp.int32, (1, RW_GW), 1)
    masks = [(lane // RW_DH) == h for h in range(RW_GROUP)]
    rowi = lax.broadcasted_iota(jnp.int32, (RW_GW, 1), 0)
    row_masks = [(rowi // RW_DH) == h for h in range(RW_GROUP)]
    seg = seg_ref[...]
    last = 0 if reverse else CHUNK - 1
    nchunks = tb // CHUNK
    w_lo, a_lo = 3 * RW_WIDTH, 3 * RW_WIDTH + RW_LORA

    def lr_gate(a_lat, dd):
        return _sigmoid(a0_ref[dd:dd + 1, :] + _dot(a_lat.astype(bf16), aup_ref[dd]))

    r = sh_ref[:, 0:RW_WIDTH]
    k = sh_ref[:, RW_WIDTH:2 * RW_WIDTH]
    v = sh_ref[:, 2 * RW_WIDTH:3 * RW_WIDTH]
    w_lat = sh_ref[:, w_lo:w_lo + RW_LORA]
    a_lat = sh_ref[:, a_lo:a_lo + RW_LORA]
    w_log = -RW_DECAY_SCALE * _sigmoid(
        w0_ref[d:d + 1, :] + _dot(jnp.tanh(w_lat).astype(bf16), wup_ref[d]))
    lr = lr_gate(a_lat, d)
    kk = k * kk_ref[...]
    kk = kk / jnp.maximum(jnp.sqrt(_dot_seg(kk * kk, seg)), 1e-12)
    k_dir = k * (1.0 + (lr - 1.0) * ka_ref[...])
    chunk_rows = [slice(c * CHUNK, (c + 1) * CHUNK) for c in range(nchunks)]
    cum = jnp.concatenate([_cumsum_rows(tri, w_log[rs]) for rs in chunk_rows], axis=0)
    grow = jnp.exp(-cum)
    a_t = -kk * jnp.exp(cum - w_log)
    b_t = kk * lr * grow
    k_t = k_dir * grow
    r_t = r * jnp.exp(cum)

    order = list(reversed(range(nchunks))) if reverse else list(range(nchunks))
    lanes = [slice(g * RW_GW, (g + 1) * RW_GW) for g in range(RW_NGROUPS)]
    state = [st_ref[g] for g in range(RW_NGROUPS)]
    y_rows = {}

    def wave_stages(chunks):
        units = [(c, g) for c in chunks for g in range(RW_NGROUPS)]
        n = range(len(units))
        w = {}

        def stacked():
            stack = lambda x: [_head_stack(x[chunk_rows[c], lanes[g]], masks) for c, g in units]
            stack_t = lambda x: [_head_stack_t(x[chunk_rows[c], lanes[g]], row_masks) for c, g in units]
            w["a_s"] = [t.astype(bf16) for t in stack(a_t)]
            w["r_s"] = [t.astype(bf16) for t in stack(r_t)]
            w["b_s"] = [t.astype(bf16) for t in stack(b_t)]
            w["k_s"] = [t.astype(bf16) for t in stack(k_t)]
            w["a_st"] = [t.astype(bf16) for t in stack_t(a_t)]
            w["r_st"] = [t.astype(bf16) for t in stack_t(r_t)]
            w["v_st"] = [t.astype(bf16) for t in stack_t(v)]

        def grams():
            gram = [jnp.where(keep, _dot_nt(jnp.concatenate([w["b_s"][i], w["k_s"][i]], axis=0),
                                             jnp.concatenate([w["a_s"][i], w["r_s"][i]], axis=0)),
                              0.0) for i in n]
            w["ab_t"] = [gram[i][0:RW_GW, 0:RW_GW] for i in n]
            w["ak_rk_t"] = [gram[i][RW_GW:, :].astype(bf16) for i in n]
            w["rb_t"] = [gram[i][0:RW_GW, RW_GW:].astype(bf16) for i in n]

        inverse = _unit_lower_inverse_stages(w, "ab_t", "t_inv_t")

        def value_products():
            x = [_dot(w["v_st"][i], w["ak_rk_t"][i]) for i in n]
            w["av_t"] = [t[:, 0:RW_GW].astype(bf16) for t in x]
            w["y_local_t"] = [t[:, RW_GW:] for t in x]
            w["vk"] = [_dot(w["v_st"][i], w["k_s"][i]) for i in n]

        def apply_inverse():
            x = [_dot(jnp.concatenate([w["av_t"][i], w["a_st"][i]], axis=0), w["t_inv_t"][i])
                 for i in n]
            w["z_t"] = [t[0:RW_GW] for t in x]
            w["wm_r_t"] = [jnp.concatenate([x[i][RW_GW:].astype(bf16), w["r_st"][i]], axis=1)
                           for i in n]
            w["rb_b"] = [jnp.concatenate([w["rb_t"][i], w["b_s"][i]], axis=1) for i in n]

        def recur(pos_c, c):
            def new_state():
                p_end = jnp.exp(cum[c * CHUNK + last:c * CHUNK + last + 1, :])
                w["y_t"] = []
                for g in range(RW_NGROUPS):
                    i = pos_c * RW_NGROUPS + g
                    s0 = state[g]
                    x = _dot(s0.astype(bf16), w["wm_r_t"][i])
                    u_t = (x[:, 0:RW_GW] + w["z_t"][i]).astype(bf16)
                    x2 = _dot(u_t, w["rb_b"][i])
                    state[g] = (s0 + x2[:, RW_GW:] + w["vk"][i]) * p_end[:, lanes[g]]
                    w["y_t"].append(x[:, RW_GW:] + x2[:, 0:RW_GW] + w["y_local_t"][i])

            def outputs():
                ys = []
                for g in range(RW_NGROUPS):
                    y = w["y_t"][g].T
                    yg = y[0:CHUNK]
                    for h in range(1, RW_GROUP):
                        yg = yg + y[h * CHUNK:(h + 1) * CHUNK]
                    ys.append(yg)
                y_rows[c] = jnp.concatenate(ys, axis=1)

            return [new_state, outputs]

        independent = [stacked, grams] + inverse + [value_products, apply_inverse]
        recurrence = [f for pos_c, c in enumerate(chunks) for f in recur(pos_c, c)]
        return independent, recurrence

    waves = [order[i:i + RW_WAVE] for i in range(0, nchunks, RW_WAVE)]
    pending = []
    for chunks in waves:
        independent, recurrence = wave_stages(chunks)
        every = max(1, len(independent) // (len(pending) + 1)) if pending else 0
        for si, stage in enumerate(independent):
            stage()
            if pending and (si + 1) % every == 0:
                pending.pop(0)()
        while pending:
            pending.pop(0)()
        pending = recurrence
    while pending:
        pending.pop(0)()
    for g in range(RW_NGROUPS):
        st_ref[g] = state[g]
    y_all = jnp.concatenate([y_rows[c] for c in range(nchunks)], axis=0)

    if not final:
        o_ref[...] = y_all
    else:
        y_all = y_all + yf_ref[...]
        mean = _dot_seg(y_all, seg) * (1.0 / RW_DH)
        cen = y_all - mean
        var = _dot_seg(cen * cen, seg) * (1.0 / RW_DH)
        yn = cen * lax.rsqrt(var + RW_GN_EPS) * gnw_ref[...] + gnb_ref[...]
        k_other = k * (1.0 + (lr_gate(a_lat, 1 - d) - 1.0) * ka_ref[...])
        rk = _dot_seg(r * (k_dir + k_other) * rk_ref[...], seg)
        o_ref[...] = ((yn + rk * v) * _silu(ga_ref[...])).astype(o_ref.dtype)


def _segment_ones():
    h = jnp.arange(LANES) // RW_DH
    return (h[:, None] == h[None, :]).astype(bf16)


def _rwkv_scan(ps, mu_prev, mu_next, w0, w_up, a0, a_up, k_k, k_a, *, reverse,
               r_k=None, gn_w=None, gn_b=None, gate=None, y_fwd=None, block_rows=512):
    bsz, seq, _ = ps.shape
    tb = block_rows
    nb = seq // tb
    final = reverse
    halo = tb // 8
    pos = (lambda j: nb - 1 - j) if reverse else (lambda j: j)
    blk = lambda b, j: (b, pos(j), 0)
    prev = lambda b, j: (b, jnp.maximum(pos(j) * halo - 1, 0), 0)
    nxt = lambda b, j: (b, jnp.minimum((pos(j) + 1) * halo, seq // 8 - 1), 0)
    full2 = lambda b, j: (0, 0)
    full3 = lambda b, j: (0, 0, 0)
    vec = lambda n: pl.BlockSpec((1, n), full2)
    in_specs = [
        pl.BlockSpec((None, tb, EVEN_SHIFT), blk),
        pl.BlockSpec((None, 8, EVEN_SHIFT), prev),
        pl.BlockSpec((None, 8, EVEN_SHIFT), nxt),
        vec(EVEN_SHIFT), vec(EVEN_SHIFT),
        pl.BlockSpec((2, RW_WIDTH), full2),
        pl.BlockSpec((2, RW_LORA, RW_WIDTH), full3),
        pl.BlockSpec((2, RW_WIDTH), full2),
        pl.BlockSpec((2, RW_LORA, RW_WIDTH), full3),
        vec(RW_WIDTH), vec(RW_WIDTH),
        pl.BlockSpec((LANES, LANES), full2),
    ]
    row = lambda t: t.reshape(1, -1)
    args = [ps, ps, ps, row(mu_prev), row(mu_next), w0, w_up.astype(bf16), a0, a_up.astype(bf16),
            row(k_k), row(k_a),
            _segment_ones()]
    if final:
        in_specs += [vec(RW_WIDTH), vec(RW_WIDTH), vec(RW_WIDTH),
                     pl.BlockSpec((None, tb, RW_WIDTH), blk),
                     pl.BlockSpec((None, tb, RW_WIDTH), blk)]
        args += [row(r_k), row(gn_w), row(gn_b), gate, y_fwd]
    return pl.pallas_call(
        functools.partial(_rwkv_kernel, reverse=reverse, final=final, block_rows=tb, nblocks=nb),
        grid=(bsz, nb),
        in_specs=in_specs,
        out_specs=pl.BlockSpec((None, tb, RW_WIDTH), blk),
        out_shape=jax.ShapeDtypeStruct((bsz, seq, RW_WIDTH), bf16 if final else f32),
        scratch_shapes=[pltpu.VMEM((tb, EVEN_SHIFT), f32),
                        pltpu.VMEM((RW_NGROUPS, RW_GW, RW_GW), f32)],
        compiler_params=_cparams("parallel", "arbitrary"),
        name="rwkv_bwd" if reverse else "rwkv_fwd",
    )(*args)


def _trunk(x, p):
    bsz, seq, _ = x.shape
    tokens = bsz * seq
    x2d = x.reshape(tokens, D_MODEL)
    seq3 = lambda t: t.reshape(bsz, seq, t.shape[-1])
    flat = lambda t: t.reshape(tokens, t.shape[-1])

    ps, ga, qkv, gb = _even_in(x2d, seq, p["even_norm"], p["even_w_in"])
    rw = (seq3(ps), p["mu_prev"], p["mu_next"], p["w0"], p["w_up"], p["a0"], p["a_up"],
          p["k_k"], p["k_a"])
    y_fwd = _rwkv_scan(*rw, reverse=False)
    ya = _rwkv_scan(*rw, reverse=True, r_k=p["r_k"], gn_w=p["gn_w"], gn_b=p["gn_b"],
                    gate=seq3(ga), y_fwd=y_fwd)
    yb = _attention(seq3(qkv), seq3(gb))

    x1, q, k, v, gate, gate_lat = _mid(x2d, flat(ya), flat(yb), p["even_w_out"], p["odd_norm"],
                                       p["odd_w_in"])
    gla = (seq3(q), seq3(k), seq3(v), seq3(gate_lat), p["gate_up"], p["gate_bias"])
    o_fwd = _gla_scan(*gla, reverse=False)
    yc = _gla_scan(*gla, reverse=True, gate=seq3(gate), o_fwd=o_fwd, norm_w=p["gla_norm"])
    return _final(x1, flat(yc), p["odd_w_out"], p["final_norm"]).reshape(bsz, seq, D_MODEL)


def _prepare(even_norm, even_w_in, even_mu_prev, even_mu_next, rwkv_w0, rwkv_w_up, rwkv_a0,
             rwkv_a_up, rwkv_k_k, rwkv_k_a, rwkv_r_k, rwkv_gn_w, rwkv_gn_b, even_w_out, odd_norm,
             odd_w_in, gla_gate_up, gla_gate_bias, gla_norm, odd_w_out, final_norm):
    wi = odd_w_in[0]
    lat0 = 2 * GLA_KEY + GLA_VAL
    lat = jnp.pad(wi[:, lat0:lat0 + GLA_RANK], ((0, 0), (0, LANES - GLA_RANK)))
    odd_in = jnp.concatenate([wi[:, :lat0], wi[:, lat0 + GLA_RANK:], lat], axis=1)
    return {
        "even_norm": even_norm[0], "even_w_in": even_w_in[0].astype(bf16),
        "mu_prev": even_mu_prev[0], "mu_next": even_mu_next[0],
        "w0": rwkv_w0[0], "w_up": rwkv_w_up[0], "a0": rwkv_a0[0], "a_up": rwkv_a_up[0],
        "k_k": rwkv_k_k[0], "k_a": rwkv_k_a[0], "r_k": rwkv_r_k[0],
        "gn_w": rwkv_gn_w[0], "gn_b": rwkv_gn_b[0],
        "even_w_out": even_w_out[0].astype(bf16),
        "odd_norm": odd_norm[0], "odd_w_in": odd_in.astype(bf16),
        "gate_up": jnp.pad(gla_gate_up[0], ((0, 0), (0, LANES - GLA_RANK), (0, 0))),
        "gate_bias": gla_gate_bias[0], "gla_norm": gla_norm[0],
        "odd_w_out": odd_w_out[0].astype(bf16), "final_norm": final_norm,
    }


def kernel(x_prompt, x_sample, even_norm, even_w_in, even_mu_prev, even_mu_next, rwkv_w0, rwkv_w_up,
           rwkv_a0, rwkv_a_up, rwkv_k_k, rwkv_k_a, rwkv_r_k, rwkv_gn_w, rwkv_gn_b, even_w_out,
           odd_norm, odd_w_in, gla_gate_up, gla_gate_bias, gla_norm, odd_w_out, final_norm):
    p = _prepare(even_norm, even_w_in, even_mu_prev, even_mu_next, rwkv_w0, rwkv_w_up, rwkv_a0,
                 rwkv_a_up, rwkv_k_k, rwkv_k_a, rwkv_r_k, rwkv_gn_w, rwkv_gn_b, even_w_out,
                 odd_norm, odd_w_in, gla_gate_up, gla_gate_bias, gla_norm, odd_w_out, final_norm)
    return (_trunk(x_prompt, p), _trunk(x_sample, p))
```

```python
import functools
import math

import jax
import jax.numpy as jnp
from jax import lax
from jax.experimental import pallas as pl
from jax.experimental.pallas import tpu as pltpu

f32 = jnp.float32
bf16 = jnp.bfloat16

D_MODEL = 1024
RMS_EPS = 1e-6

RW_HEADS = 8
RW_DH = 64
RW_WIDTH = RW_HEADS * RW_DH
RW_LORA = 64
RW_DECAY_SCALE = 0.6065306597126334
RW_GN_EPS = 64e-5
AT_HEADS = 8
AT_DH = 64
AT_WIDTH = AT_HEADS * AT_DH
AT_SIDE = 64
AT_DILATIONS = (1, 4, 16)
ROPE_THETA = 10000.0
EVEN_SHIFT = 3 * RW_WIDTH + 2 * RW_LORA
EVEN_COLS = EVEN_SHIFT + RW_WIDTH + 4 * AT_WIDTH

GLA_HEADS = 4
GLA_KEY = 512
GLA_VAL = 1024
GLA_HK = GLA_KEY // GLA_HEADS
GLA_HV = GLA_VAL // GLA_HEADS
GLA_RANK = 16
GLA_GATE_NORM = 16.0

CHUNK = 64
LANES = 128
SUBLANES = 8
MXU_COLS = 256
ROW_BLOCK = 512
VMEM_LIMIT = 56 * 1024 * 1024

_NT = (((1,), (1,)), ((), ()))
_TN = (((0,), (0,)), ((), ()))


def _dot(a, b):
    return jnp.dot(a, b, preferred_element_type=f32)


def _dot_nt(a, b):
    return lax.dot_general(a, b, _NT, preferred_element_type=f32)


def _dot_tn(a, b):
    return lax.dot_general(a, b, _TN, preferred_element_type=f32)


def _split2(x):
    hi = x.astype(bf16)
    lo = (x - hi.astype(f32)).astype(bf16)
    return hi, lo


def _dot_seg(x, e):
    xb = x.astype(bf16)
    return jnp.concatenate([_dot(xb[:, c:c + LANES], e) for c in range(0, x.shape[1], LANES)], axis=1)


def _cumsum_rows(tri, x):
    hi, lo = _split2(x)
    return _dot(tri, hi) + _dot(tri, lo)


def _sigmoid(x):
    return 0.5 * (jnp.tanh(0.5 * x) + 1.0)


def _silu(x):
    return x * _sigmoid(x)


def _rms_rows(x, w):
    return x * lax.rsqrt(jnp.mean(x * x, axis=-1, keepdims=True) + RMS_EPS) * w


def _cparams(*sem):
    return pltpu.CompilerParams(dimension_semantics=sem, vmem_limit_bytes=VMEM_LIMIT)


def _rope_partner(x):
    half = AT_DH // 2
    lane = lax.broadcasted_iota(jnp.int32, x.shape, 1)
    return jnp.where((lane & half) == 0, pltpu.roll(x, LANES - half, 1), pltpu.roll(x, half, 1))


def _even_in_kernel(x_ref, nw_ref, w_ref, cos_ref, sin_ref, ps_ref, ga_ref, qkv_ref, gb_ref):
    xn = _rms_rows(x_ref[...], nw_ref[...]).astype(bf16)
    cos = cos_ref[...]
    sin = sin_ref[...]
    dests = ([(ps_ref, c, False) for c in range(0, EVEN_SHIFT, LANES)]
             + [(ga_ref, c, False) for c in range(0, RW_WIDTH, LANES)]
             + [(qkv_ref, c, True) for c in range(0, 2 * AT_WIDTH, LANES)]
             + [(qkv_ref, c, False) for c in range(2 * AT_WIDTH, 3 * AT_WIDTH, LANES)]
             + [(gb_ref, c, False) for c in range(0, AT_WIDTH, LANES)])
    tile = 2 * MXU_COLS
    for c0 in range(0, EVEN_COLS, tile):
        c1 = min(c0 + tile, EVEN_COLS)
        acc = _dot(xn, w_ref[:, c0:c1])
        for j in range((c1 - c0) // LANES):
            ref, off, rotary = dests[c0 // LANES + j]
            t = acc[:, j * LANES:(j + 1) * LANES]
            ref[:, off:off + LANES] = t * cos + _rope_partner(t) * sin if rotary else t


def _rope_tables(seq):
    inv = ROPE_THETA ** (-jnp.arange(0, AT_DH, 2, dtype=f32) / AT_DH)
    ang = jnp.arange(seq, dtype=f32)[:, None] * inv[None, :]
    cos, sin = jnp.cos(ang), jnp.sin(ang)
    return (jnp.concatenate([cos, cos, cos, cos], axis=-1),
            jnp.concatenate([-sin, sin, -sin, sin], axis=-1))


def _even_in(x2d, seq, norm_w, w_in_bf16, block_rows=ROW_BLOCK):
    m = x2d.shape[0]
    tm = block_rows
    per_seq = seq // tm
    cos, sin = _rope_tables(seq)
    row = lambda i: (i, 0)
    full = lambda i: (0, 0)
    tab = lambda i: (i % per_seq, 0)
    return pl.pallas_call(
        _even_in_kernel,
        grid=(m // tm,),
        in_specs=[
            pl.BlockSpec((tm, D_MODEL), row),
            pl.BlockSpec((1, D_MODEL), full),
            pl.BlockSpec((D_MODEL, EVEN_COLS), full),
            pl.BlockSpec((tm, LANES), tab),
            pl.BlockSpec((tm, LANES), tab),
        ],
        out_specs=[
            pl.BlockSpec((tm, EVEN_SHIFT), row),
            pl.BlockSpec((tm, RW_WIDTH), row),
            pl.BlockSpec((tm, 3 * AT_WIDTH), row),
            pl.BlockSpec((tm, AT_WIDTH), row),
        ],
        out_shape=[
            jax.ShapeDtypeStruct((m, EVEN_SHIFT), f32),
            jax.ShapeDtypeStruct((m, RW_WIDTH), f32),
            jax.ShapeDtypeStruct((m, 3 * AT_WIDTH), f32),
            jax.ShapeDtypeStruct((m, AT_WIDTH), f32),
        ],
        compiler_params=_cparams("parallel"),
        name="even_in",
    )(x2d, norm_w.reshape(1, D_MODEL), w_in_bf16, cos, sin)


ODD_PAD_COLS = 2 * GLA_KEY + 2 * GLA_VAL + LANES


def _mid_kernel(x_ref, ya_ref, yb_ref, wo_ref, nw_ref, wi_ref,
                x1_ref, q_ref, k_ref, v_ref, g_ref, gl_ref):
    x1 = (x_ref[...] + _dot(ya_ref[...], wo_ref[0:RW_WIDTH, :])
          + _dot(yb_ref[...], wo_ref[RW_WIDTH:RW_WIDTH + AT_WIDTH, :]))
    x1_ref[...] = x1
    xn = _rms_rows(x1, nw_ref[...]).astype(bf16)
    c = 0
    for ref, width in ((q_ref, GLA_KEY), (k_ref, GLA_KEY), (v_ref, GLA_VAL), (g_ref, GLA_VAL),
                       (gl_ref, LANES)):
        for c0 in range(0, width, 512):
            c1 = min(c0 + 512, width)
            ref[:, c0:c1] = _dot(xn, wi_ref[:, c + c0:c + c1]).astype(ref.dtype)
        c += width


def _mid(x2d, ya, yb, w_out_bf16, norm_w, w_in_pad_bf16, block_rows=ROW_BLOCK):
    m = x2d.shape[0]
    tm = block_rows
    row = lambda i: (i, 0)
    full = lambda i: (0, 0)
    widths = (D_MODEL, GLA_KEY, GLA_KEY, GLA_VAL, GLA_VAL, LANES)
    return pl.pallas_call(
        _mid_kernel,
        grid=(m // tm,),
        in_specs=[
            pl.BlockSpec((tm, D_MODEL), row),
            pl.BlockSpec((tm, RW_WIDTH), row),
            pl.BlockSpec((tm, AT_WIDTH), row),
            pl.BlockSpec((RW_WIDTH + AT_WIDTH, D_MODEL), full),
            pl.BlockSpec((1, D_MODEL), full),
            pl.BlockSpec((D_MODEL, ODD_PAD_COLS), full),
        ],
        out_specs=[pl.BlockSpec((tm, w), row) for w in widths],
        out_shape=[jax.ShapeDtypeStruct((m, w), bf16 if i == 3 else f32)
                   for i, w in enumerate(widths)],
        compiler_params=_cparams("parallel"),
        name="mid",
    )(x2d, ya, yb, w_out_bf16, norm_w.reshape(1, D_MODEL), w_in_pad_bf16)


def _final_kernel(x_ref, y_ref, wo_ref, nw_ref, o_ref):
    x2 = x_ref[...] + _dot(y_ref[...], wo_ref[...])
    o_ref[...] = _rms_rows(x2, nw_ref[...])


def _final(x2d, yc, w_out_bf16, norm_w, block_rows=ROW_BLOCK):
    m = x2d.shape[0]
    tm = block_rows
    row = lambda i: (i, 0)
    full = lambda i: (0, 0)
    return pl.pallas_call(
        _final_kernel,
        grid=(m // tm,),
        in_specs=[
            pl.BlockSpec((tm, D_MODEL), row),
            pl.BlockSpec((tm, GLA_VAL), row),
            pl.BlockSpec((GLA_VAL, D_MODEL), full),
            pl.BlockSpec((1, D_MODEL), full),
        ],
        out_specs=pl.BlockSpec((tm, D_MODEL), row),
        out_shape=jax.ShapeDtypeStruct((m, D_MODEL), f32),
        compiler_params=_cparams("parallel"),
        name="final",
    )(x2d, yc, w_out_bf16, norm_w.reshape(1, D_MODEL))


def _causal_mask(reverse):
    t = lax.broadcasted_iota(jnp.int32, (CHUNK, CHUNK), 0)
    s = lax.broadcasted_iota(jnp.int32, (CHUNK, CHUNK), 1)
    return s >= t if reverse else s <= t


def _row_to_col(row):
    n = row.shape[1]
    eye = (lax.broadcasted_iota(jnp.int32, (n, n), 0) == lax.broadcasted_iota(jnp.int32, (n, n), 1))
    return jnp.sum(jnp.where(eye, jnp.broadcast_to(row, (n, n)), 0.0), axis=1, keepdims=True)


def _gla_kernel(*refs, reverse, final, block_rows):
    if final:
        (q_ref, k_ref, v_ref, gl_ref, gup_ref, gb_ref, gate_ref, of_ref, nw_ref, o_ref, st_ref) = refs
    else:
        (q_ref, k_ref, v_ref, gl_ref, gup_ref, gb_ref, o_ref, st_ref) = refs
    d = 1 if reverse else 0

    @pl.when(pl.program_id(1) == 0)
    def _():
        st_ref[...] = jnp.zeros_like(st_ref)

    incl = _causal_mask(reverse)
    tri = incl.astype(bf16)
    gup = gup_ref[d]
    gbias = gb_ref[d:d + 1, :]
    nchunks = block_rows // CHUNK
    last = 0 if reverse else CHUNK - 1

    x = _dot(gl_ref[...].astype(bf16), gup) + gbias
    g = (jnp.minimum(x, 0.0) - jnp.log(1.0 + jnp.exp(-jnp.abs(x)))) / GLA_GATE_NORM
    chunk_rows = [slice(c * CHUNK, (c + 1) * CHUNK) for c in range(nchunks)]
    bcum = jnp.concatenate([_cumsum_rows(tri, g[rs]) for rs in chunk_rows], axis=0)
    b_last = [bcum[c * CHUNK + last:c * CHUNK + last + 1, :] for c in range(nchunks)]
    b_end = jnp.concatenate([jnp.broadcast_to(b, (CHUNK, GLA_KEY)) for b in b_last], axis=0)
    q = q_ref[...] * (GLA_HK ** -0.5)
    k = k_ref[...]
    q_dec = (q * jnp.exp(bcum)).astype(bf16)
    k_dec = (k * jnp.exp(-bcum)).astype(bf16)
    k_end = (k * jnp.exp(b_end - bcum)).astype(bf16)

    order = list(reversed(range(nchunks))) if reverse else list(range(nchunks))
    klanes = [slice(h * GLA_HK, (h + 1) * GLA_HK) for h in range(GLA_HEADS)]
    vlanes = [slice(h * GLA_HV, (h + 1) * GLA_HV) for h in range(GLA_HEADS)]
    units = [(c, h) for c in order for h in range(GLA_HEADS)]
    v = [v_ref[chunk_rows[c], vlanes[h]].astype(bf16) for c, h in units]
    att = [jnp.where(incl, _dot_nt(q_dec[chunk_rows[c], klanes[h]], k_dec[chunk_rows[c], klanes[h]]),
                     0.0).astype(bf16) for c, h in units]
    o_intra = [_dot(att[i], v[i]) for i in range(len(units))]
    kv = [_dot_tn(k_end[chunk_rows[c], klanes[h]], v[i]) for i, (c, h) in enumerate(units)]
    dcol = [_row_to_col(jnp.exp(b_last[c][:, klanes[h]])) for c, h in units]

    state = [st_ref[h] for h in range(GLA_HEADS)]
    for i, (c, h) in enumerate(units):
        rows = chunk_rows[c]
        o = o_intra[i] + _dot(q_dec[rows, klanes[h]], state[h].astype(bf16))
        state[h] = state[h] * dcol[i] + kv[i]
        if final:
            o = o + of_ref[rows, vlanes[h]]
            o = o * lax.rsqrt(jnp.mean(o * o, axis=-1, keepdims=True) + RMS_EPS) * nw_ref[...]
            o_ref[rows, vlanes[h]] = (o * _silu(gate_ref[rows, vlanes[h]])).astype(o_ref.dtype)
        else:
            o_ref[rows, vlanes[h]] = o
    for h in range(GLA_HEADS):
        st_ref[h] = state[h]


def _gla_scan(q, k, v, gl, gate_up_pad, gate_bias, *, reverse, gate=None, o_fwd=None, norm_w=None,
              block_rows=ROW_BLOCK):
    bsz, seq, _ = q.shape
    tb = block_rows
    nb = seq // tb
    final = reverse
    blk = (lambda b, j: (b, nb - 1 - j, 0)) if reverse else (lambda b, j: (b, j, 0))
    full2 = lambda b, j: (0, 0)
    full3 = lambda b, j: (0, 0, 0)
    in_specs = [
        pl.BlockSpec((None, tb, GLA_KEY), blk),
        pl.BlockSpec((None, tb, GLA_KEY), blk),
        pl.BlockSpec((None, tb, GLA_VAL), blk),
        pl.BlockSpec((None, tb, LANES), blk),
        pl.BlockSpec((2, LANES, GLA_KEY), full3),
        pl.BlockSpec((2, GLA_KEY), full2),
    ]
    args = [q, k, v, gl, gate_up_pad.astype(bf16), gate_bias]
    if final:
        in_specs += [
            pl.BlockSpec((None, tb, GLA_VAL), blk),
            pl.BlockSpec((None, tb, GLA_VAL), blk),
            pl.BlockSpec((1, GLA_HV), full2),
        ]
        args += [gate, o_fwd, norm_w.reshape(1, GLA_HV)]
    return pl.pallas_call(
        functools.partial(_gla_kernel, reverse=reverse, final=final, block_rows=tb),
        grid=(bsz, nb),
        in_specs=in_specs,
        out_specs=pl.BlockSpec((None, tb, GLA_VAL), blk),
        out_shape=jax.ShapeDtypeStruct((bsz, seq, GLA_VAL), bf16 if final else f32),
        scratch_shapes=[pltpu.VMEM((GLA_HEADS, GLA_HK, GLA_HV), f32)],
        compiler_params=_cparams("parallel", "arbitrary"),
        name="gla_bwd" if reverse else "gla_fwd",
    )(*args)


AT_NEG = -1e30
AT_QBLK = 128
AT_KBLK = 256
AT_UNROLL = 8


AT_FOLD = 4


def _attn_kernel(q_ref, k_ref, v_ref, g_ref, o_ref, qp, kp, vp, m_s, l_s, acc_s, bias_s, tmp_s,
                 *, seq):
    lane = lax.broadcasted_iota(jnp.int32, (1, LANES), 1)
    left = lane < AT_DH
    scale = AT_DH ** -0.5
    fold_len = seq // AT_FOLD

    def gather(src_ref, dst_ref, dil, mul=None):
        sub = seq // dil
        piece = min(sub, 512)
        pieces = sub // piece

        def body(i, carry):
            r = i // pieces
            part = i % pieces
            src = r + dil * part * piece
            dst = pl.multiple_of(r * sub + part * piece, piece)
            idx = pl.ds(src, piece, stride=dil) if dil > 1 else pl.ds(src, piece)
            x = src_ref[idx, :]
            dst_ref[pl.ds(dst, piece), :] = (x if mul is None else x * mul).astype(bf16)
            return carry

        lax.fori_loop(0, dil * pieces, body, 0)

    def fill_bias(offs, tq, nk, deltas):
        for d, delta in enumerate(deltas):
            bias_s[d, 0:tq, 0:nk] = jnp.where(jnp.abs(offs + delta) <= AT_SIDE, 0.0, AT_NEG)

    def run_blocks(nblocks, load_unit):
        def body(it, carry):
            units = [load_unit(it * AT_UNROLL + u) for u in range(AT_UNROLL)]
            heads = [(u, first) for u in range(AT_UNROLL) for first in (True, False)]
            scores = []
            for u, first in heads:
                qb, kb, _, bias, _, _ = units[u]
                qh = jnp.where(left if first else ~left, qb, jnp.zeros_like(qb))
                scores.append(_dot_nt(qh, kb) + bias)
            maxes = [jnp.max(s, axis=1, keepdims=True) for s in scores]
            probs = [jnp.exp(s - mh) for s, mh in zip(scores, maxes)]
            sums = [jnp.sum(p, axis=1, keepdims=True) for p in probs]
            outs = [_dot(p.astype(bf16), units[u][2]) for p, (u, _) in zip(probs, heads)]
            olds = [None if units[u][5] else
                    [(m_s[idx, :], l_s[idx, :], acc_s[idx, :]) for idx, _ in units[u][4]]
                    for u in range(AT_UNROLL)]
            for u in range(AT_UNROLL):
                m_new = jnp.where(left, maxes[2 * u], maxes[2 * u + 1])
                l_new = jnp.where(left, sums[2 * u], sums[2 * u + 1])
                a_new = jnp.where(left, outs[2 * u], outs[2 * u + 1])
                for pi, (idx, rows) in enumerate(units[u][4]):
                    mp, lp, ap = m_new[rows], l_new[rows], a_new[rows]
                    if units[u][5]:
                        m_s[idx, :] = mp
                        l_s[idx, :] = lp
                        acc_s[idx, :] = ap
                    else:
                        m_old, l_old, a_old = olds[u][pi]
                        m = jnp.maximum(m_old, mp)
                        w_old = jnp.exp(m_old - m)
                        w_new = jnp.exp(mp - m)
                        m_s[idx, :] = m
                        l_s[idx, :] = l_old * w_old + lp * w_new
                        acc_s[idx, :] = a_old * w_old + ap * w_new
            return carry

        lax.fori_loop(0, nblocks // AT_UNROLL, body, 0)

    def window(m0, sub, tq, nk):
        ks = jnp.clip(m0 - AT_SIDE, 0, sub - nk)
        return ks, (m0 - ks) // AT_SIDE

    tq, nk = AT_QBLK, AT_KBLK
    per = tq // AT_FOLD
    gather(q_ref, qp, AT_FOLD, scale)
    gather(k_ref, kp, 1)
    gather(v_ref, vp, 1)
    qi = lax.broadcasted_iota(jnp.int32, (tq, nk), 0)
    kj = lax.broadcasted_iota(jnp.int32, (tq, nk), 1)
    fill_bias((qi // per) + AT_FOLD * (qi % per) - kj, tq, nk, (0, AT_SIDE, 2 * AT_SIDE))

    def unit_d1(i):
        t0 = i * tq
        ks, bi = window(t0, seq, tq, nk)
        starts = [pl.multiple_of(rho * fold_len + i * per, per) for rho in range(AT_FOLD)]
        qb = jnp.concatenate([qp[pl.ds(s, per), :] for s in starts], axis=0)
        kb = kp[pl.ds(pl.multiple_of(ks, AT_SIDE), nk), :]
        vb = vp[pl.ds(pl.multiple_of(ks, AT_SIDE), nk), :]
        state = [(pl.ds(s, per), slice(rho * per, (rho + 1) * per)) for rho, s in enumerate(starts)]
        return qb, kb, vb, bias_s[bi], state, True

    run_blocks(seq // tq, unit_d1)

    sub = fold_len
    gather(k_ref, kp, AT_FOLD)
    gather(v_ref, vp, AT_FOLD)
    fill_bias(qi - kj, tq, nk, (0, AT_SIDE, 2 * AT_SIDE))
    nblk = sub // tq

    def unit_fold(i):
        r = i // nblk
        m0 = (i % nblk) * tq
        ks, bi = window(m0, sub, tq, nk)
        qrow = pl.ds(pl.multiple_of(r * sub + m0, tq), tq)
        krow = pl.ds(pl.multiple_of(r * sub + ks, AT_SIDE), nk)
        return qp[qrow, :], kp[krow, :], vp[krow, :], bias_s[bi], [(qrow, slice(0, tq))], False

    run_blocks(AT_FOLD * nblk, unit_fold)

    dil = AT_DILATIONS[-1]
    sub = seq // dil
    tq, nk = min(AT_QBLK, sub), min(AT_KBLK, sub)
    gather(q_ref, qp, dil, scale)
    gather(k_ref, kp, dil)
    gather(v_ref, vp, dil)
    fill_bias((qi - kj)[0:tq, 0:nk], tq, nk, (0, AT_SIDE, nk - tq))
    nblk = sub // tq

    def unit_wide(i):
        r = i // nblk
        m0 = (i % nblk) * tq
        ks, bi = window(m0, sub, tq, nk)
        qrow = pl.ds(pl.multiple_of(r * sub + m0, AT_SIDE), tq)
        krow = pl.ds(pl.multiple_of(r * sub + ks, AT_SIDE), nk)
        srow = (r % AT_FOLD) * fold_len + r // AT_FOLD + (dil // AT_FOLD) * m0
        state = [(pl.ds(srow, tq, stride=dil // AT_FOLD), slice(0, tq))]
        return qp[qrow, :], kp[krow, :], vp[krow, :], bias_s[bi, 0:tq, 0:nk], state, False

    run_blocks(dil * nblk, unit_wide)

    rows = tmp_s.shape[0]
    per_fold = rows // AT_FOLD

    def finish(i, carry):
        for rho in range(AT_FOLD):
            idx = pl.ds(pl.multiple_of(rho * fold_len + i * per_fold, per_fold), per_fold)
            tmp_s[pl.ds(rho, per_fold, stride=AT_FOLD), :] = acc_s[idx, :] / l_s[idx, :]
        out = pl.ds(pl.multiple_of(i * rows, rows), rows)
        o_ref[out, :] = (tmp_s[...] * _silu(g_ref[out, :])).astype(o_ref.dtype)
        return carry

    lax.fori_loop(0, seq // rows, finish, 0)


def _attention(qkv, gate):
    bsz, seq, _ = qkv.shape
    pairs = AT_WIDTH // LANES
    once = pl.Buffered(1)
    col = lambda off: (lambda b, p: (b, 0, off + p))
    return pl.pallas_call(
        functools.partial(_attn_kernel, seq=seq),
        grid=(bsz, pairs),
        in_specs=[
            pl.BlockSpec((None, seq, LANES), col(0), pipeline_mode=once),
            pl.BlockSpec((None, seq, LANES), col(pairs), pipeline_mode=once),
            pl.BlockSpec((None, seq, LANES), col(2 * pairs), pipeline_mode=once),
            pl.BlockSpec((None, seq, LANES), col(0), pipeline_mode=once),
        ],
        out_specs=pl.BlockSpec((None, seq, LANES), col(0)),
        out_shape=jax.ShapeDtypeStruct((bsz, seq, AT_WIDTH), bf16),
        scratch_shapes=([pltpu.VMEM((seq, LANES), bf16)] * 3 + [pltpu.VMEM((seq, LANES), f32)] * 3
                        + [pltpu.VMEM((3, AT_QBLK, AT_KBLK), f32),
                           pltpu.VMEM((min(seq, 512), LANES), f32)]),
        compiler_params=_cparams("parallel", "parallel"),
        name="dilated_attn",
    )(qkv, qkv, qkv, gate)


RW_GROUP = 2
RW_GW = RW_GROUP * RW_DH
RW_NGROUPS = RW_HEADS // RW_GROUP
RW_WAVE = 4


def _head_stack(x, masks):
    return jnp.concatenate([jnp.where(mk, x, jnp.zeros_like(x)) for mk in masks], axis=0)


def _head_stack_t(x, row_masks):
    xt = x.T.astype(bf16)
    return jnp.concatenate([jnp.where(mk, xt, jnp.zeros_like(xt)) for mk in row_masks], axis=1)


def _unit_lower_inverse_stages(w, src, dst):
    steps = int(math.log2(CHUNK)) - 1

    def start():
        n = w[src][0].shape[0]
        eye = (lax.broadcasted_iota(jnp.int32, (n, n), 0)
               == lax.broadcasted_iota(jnp.int32, (n, n), 1)).astype(f32)
        w["_t"] = [eye + a for a in w[src]]
        qs = [a.astype(bf16) for a in w[src]]
        w["_q"] = [_dot(q, q).astype(bf16) for q in qs]

    def double():
        n = w[src][0].shape[0]
        both = [_dot(jnp.concatenate([q, t.astype(bf16)], axis=0), q)
                for q, t in zip(w["_q"], w["_t"])]
        w["_q"] = [x[:n].astype(bf16) for x in both]
        w["_t"] = [t + x[n:] for t, x in zip(w["_t"], both)]

    def finish():
        w[dst] = [(t + _dot(t.astype(bf16), q)).astype(bf16) for q, t in zip(w["_q"], w["_t"])]

    return [start] + [double] * (steps - 1) + [finish]


def _rwkv_kernel(*refs, reverse, final, block_rows, nblocks):
    (ps_ref, prev_ref, next_ref, mup_ref, mun_ref, w0_ref, wup_ref, a0_ref, aup_ref,
     kk_ref, ka_ref, seg_ref) = refs[:12]
    if final:
        rk_ref, gnw_ref, gnb_ref, ga_ref, yf_ref, o_ref, sh_ref, st_ref = refs[12:]
    else:
        o_ref, sh_ref, st_ref = refs[12:]
    d = 1 if reverse else 0
    tb = block_rows
    j = pl.program_id(1)
    blk = (nblocks - 1 - j) if reverse else j

    @pl.when(j == 0)
    def _():
        st_ref[...] = jnp.zeros_like(st_ref)

    rid = lax.broadcasted_iota(jnp.int32, (tb, 1), 0)
    has_prev = (blk > 0).astype(f32)
    has_next = (blk < nblocks - 1).astype(f32)
    for c0 in range(0, EVEN_SHIFT, 256):
        c1 = min(c0 + 256, EVEN_SHIFT)
        x = ps_ref[:, c0:c1]
        before = prev_ref[SUBLANES - 1:SUBLANES, c0:c1] * has_prev
        after = next_ref[0:1, c0:c1] * has_next
        prv = jnp.where(rid == 0, before, pltpu.roll(x, 1, 0))
        nxt = jnp.where(rid == tb - 1, after, pltpu.roll(x, tb - 1, 0))
        sh_ref[:, c0:c1] = x + mup_ref[:, c0:c1] * (prv - x) + mun_ref[:, c0:c1] * (nxt - x)

    tri = _causal_mask(reverse).astype(bf16)
    gi = lax.broadcasted_iota(jnp.int32, (2 * RW_GW, 2 * RW_GW), 0)
    gj = lax.broadcasted_iota(jnp.int32, (2 * RW_GW, 2 * RW_GW), 1)
    ahead = (gi % CHUNK) - (gj % CHUNK)
    ahead = -ahead if reverse else ahead
    keep = ahead < jnp.where(gj < RW_GW, 0, 1)
    lane = lax.broadcasted_iota(jnp.int32, (1, RW_GW), 1)
    masks = [(lane // RW_DH) == h for h in range(RW_GROUP)]
    rowi = lax.broadcasted_iota(jnp.int32, (RW_GW, 1), 0)
    row_masks = [(rowi // RW_DH) == h for h in range(RW_GROUP)]
    seg = seg_ref[...]
    last = 0 if reverse else CHUNK - 1
    nchunks = tb // CHUNK
    w_lo, a_lo = 3 * RW_WIDTH, 3 * RW_WIDTH + RW_LORA

    def lr_gate(a_lat, dd):
        return _sigmoid(a0_ref[dd:dd + 1, :] + _dot(a_lat.astype(bf16), aup_ref[dd]))

    r = sh_ref[:, 0:RW_WIDTH]
    k = sh_ref[:, RW_WIDTH:2 * RW_WIDTH]
    v = sh_ref[:, 2 * RW_WIDTH:3 * RW_WIDTH]
    w_lat = sh_ref[:, w_lo:w_lo + RW_LORA]
    a_lat = sh_ref[:, a_lo:a_lo + RW_LORA]
    w_log = -RW_DECAY_SCALE * _sigmoid(
        w0_ref[d:d + 1, :] + _dot(jnp.tanh(w_lat).astype(bf16), wup_ref[d]))
    lr = lr_gate(a_lat, d)
    kk = k * kk_ref[...]
    kk = kk * lax.rsqrt(jnp.maximum(_dot_seg(kk * kk, seg), 1e-24))
    k_dir = k * (1.0 + (lr - 1.0) * ka_ref[...])
    chunk_rows = [slice(c * CHUNK, (c + 1) * CHUNK) for c in range(nchunks)]
    cum = jnp.concatenate([_cumsum_rows(tri, w_log[rs]) for rs in chunk_rows], axis=0)
    grow = jnp.exp(-cum)
    a_t = -kk * jnp.exp(cum - w_log)
    b_t = kk * lr * grow
    k_t = k_dir * grow
    r_t = r * jnp.exp(cum)
    a_b, b_b, k_b, r_b = (t.astype(bf16) for t in (a_t, b_t, k_t, r_t))

    order = list(reversed(range(nchunks))) if reverse else list(range(nchunks))
    lanes = [slice(g * RW_GW, (g + 1) * RW_GW) for g in range(RW_NGROUPS)]
    state = [st_ref[g] for g in range(RW_NGROUPS)]
    y_rows = {}

    def wave_stages(chunks):
        units = [(c, g) for c in chunks for g in range(RW_NGROUPS)]
        n = range(len(units))
        w = {}

        def stacked():
            stack = lambda x: [_head_stack(x[chunk_rows[c], lanes[g]], masks) for c, g in units]
            stack_t = lambda x: [_head_stack_t(x[chunk_rows[c], lanes[g]], row_masks) for c, g in units]
            w["a_s"] = stack(a_b)
            w["r_s"] = stack(r_b)
            w["b_s"] = stack(b_b)
            w["k_s"] = stack(k_b)
            w["a_st"] = stack_t(a_t)
            w["r_st"] = stack_t(r_t)
            w["v_st"] = stack_t(v)

        def grams():
            gram = [jnp.where(keep, _dot_nt(jnp.concatenate([w["b_s"][i], w["k_s"][i]], axis=0),
                                             jnp.concatenate([w["a_s"][i], w["r_s"][i]], axis=0)),
                              0.0) for i in n]
            w["ab_t"] = [gram[i][0:RW_GW, 0:RW_GW] for i in n]
            w["ak_rk_t"] = [gram[i][RW_GW:, :].astype(bf16) for i in n]
            w["rb_t"] = [gram[i][0:RW_GW, RW_GW:].astype(bf16) for i in n]

        inverse = _unit_lower_inverse_stages(w, "ab_t", "t_inv_t")

        def value_products():
            x = [_dot(w["v_st"][i], w["ak_rk_t"][i]) for i in n]
            w["av_t"] = [t[:, 0:RW_GW].astype(bf16) for t in x]
            w["y_local_t"] = [t[:, RW_GW:] for t in x]
            w["vk"] = [_dot(w["v_st"][i], w["k_s"][i]) for i in n]

        def apply_inverse():
            x = [_dot(jnp.concatenate([w["av_t"][i], w["a_st"][i]], axis=0), w["t_inv_t"][i])
                 for i in n]
            w["z_t"] = [t[0:RW_GW] for t in x]
            w["wm_r_t"] = [jnp.concatenate([x[i][RW_GW:].astype(bf16), w["r_st"][i]], axis=1)
                           for i in n]
            w["rb_b"] = [jnp.concatenate([w["rb_t"][i], w["b_s"][i]], axis=1) for i in n]

        def recur(pos_c, c):
            def new_state():
                p_end = jnp.exp(cum[c * CHUNK + last:c * CHUNK + last + 1, :])
                w["y_t"] = []
                for g in range(RW_NGROUPS):
                    i = pos_c * RW_NGROUPS + g
                    s0 = state[g]
                    x = _dot(s0.astype(bf16), w["wm_r_t"][i])
                    u_t = (x[:, 0:RW_GW] + w["z_t"][i]).astype(bf16)
                    x2 = _dot(u_t, w["rb_b"][i])
                    state[g] = (s0 + x2[:, RW_GW:] + w["vk"][i]) * p_end[:, lanes[g]]
                    w["y_t"].append(x[:, RW_GW:] + x2[:, 0:RW_GW] + w["y_local_t"][i])

            def outputs():
                ys = []
                for g in range(RW_NGROUPS):
                    y = w["y_t"][g].T
                    yg = y[0:CHUNK]
                    for h in range(1, RW_GROUP):
                        yg = yg + y[h * CHUNK:(h + 1) * CHUNK]
                    ys.append(yg)
                y_rows[c] = jnp.concatenate(ys, axis=1)

            return [new_state, outputs]

        independent = [stacked, grams] + inverse + [value_products, apply_inverse]
        recurrence = [f for pos_c, c in enumerate(chunks) for f in recur(pos_c, c)]
        return independent, recurrence

    waves = [order[i:i + RW_WAVE] for i in range(0, nchunks, RW_WAVE)]
    pending = []
    for chunks in waves:
        independent, recurrence = wave_stages(chunks)
        every = max(1, len(independent) // (len(pending) + 1)) if pending else 0
        for si, stage in enumerate(independent):
            stage()
            if pending and (si + 1) % every == 0:
                pending.pop(0)()
        while pending:
            pending.pop(0)()
        pending = recurrence
    while pending:
        pending.pop(0)()
    for g in range(RW_NGROUPS):
        st_ref[g] = state[g]
    y_all = jnp.concatenate([y_rows[c] for c in range(nchunks)], axis=0)

    if not final:
        o_ref[...] = y_all
    else:
        y_all = y_all + yf_ref[...]
        mean = _dot_seg(y_all, seg) * (1.0 / RW_DH)
        cen = y_all - mean
        var = _dot_seg(cen * cen, seg) * (1.0 / RW_DH)
        yn = cen * lax.rsqrt(var + RW_GN_EPS) * gnw_ref[...] + gnb_ref[...]
        k_other = k * (1.0 + (lr_gate(a_lat, 1 - d) - 1.0) * ka_ref[...])
        rk = _dot_seg(r * (k_dir + k_other) * rk_ref[...], seg)
        o_ref[...] = ((yn + rk * v) * _silu(ga_ref[...])).astype(o_ref.dtype)


def _segment_ones():
    h = jnp.arange(LANES) // RW_DH
    return (h[:, None] == h[None, :]).astype(bf16)


def _rwkv_scan(ps, mu_prev, mu_next, w0, w_up, a0, a_up, k_k, k_a, *, reverse,
               r_k=None, gn_w=None, gn_b=None, gate=None, y_fwd=None, block_rows=2 * ROW_BLOCK):
    bsz, seq, _ = ps.shape
    tb = block_rows
    nb = seq // tb
    final = reverse
    halo = tb // SUBLANES
    pos = (lambda j: nb - 1 - j) if reverse else (lambda j: j)
    blk = lambda b, j: (b, pos(j), 0)
    prev = lambda b, j: (b, jnp.maximum(pos(j) * halo - 1, 0), 0)
    nxt = lambda b, j: (b, jnp.minimum((pos(j) + 1) * halo, seq // SUBLANES - 1), 0)
    full2 = lambda b, j: (0, 0)
    full3 = lambda b, j: (0, 0, 0)
    vec = lambda n: pl.BlockSpec((1, n), full2)
    in_specs = [
        pl.BlockSpec((None, tb, EVEN_SHIFT), blk),
        pl.BlockSpec((None, SUBLANES, EVEN_SHIFT), prev),
        pl.BlockSpec((None, SUBLANES, EVEN_SHIFT), nxt),
        vec(EVEN_SHIFT), vec(EVEN_SHIFT),
        pl.BlockSpec((2, RW_WIDTH), full2),
        pl.BlockSpec((2, RW_LORA, RW_WIDTH), full3),
        pl.BlockSpec((2, RW_WIDTH), full2),
        pl.BlockSpec((2, RW_LORA, RW_WIDTH), full3),
        vec(RW_WIDTH), vec(RW_WIDTH),
        pl.BlockSpec((LANES, LANES), full2),
    ]
    row = lambda t: t.reshape(1, -1)
    args = [ps, ps, ps, row(mu_prev), row(mu_next), w0, w_up.astype(bf16), a0, a_up.astype(bf16),
            row(k_k), row(k_a),
            _segment_ones()]
    if final:
        in_specs += [vec(RW_WIDTH), vec(RW_WIDTH), vec(RW_WIDTH),
                     pl.BlockSpec((None, tb, RW_WIDTH), blk),
                     pl.BlockSpec((None, tb, RW_WIDTH), blk)]
        args += [row(r_k), row(gn_w), row(gn_b), gate, y_fwd]
    return pl.pallas_call(
        functools.partial(_rwkv_kernel, reverse=reverse, final=final, block_rows=tb, nblocks=nb),
        grid=(bsz, nb),
        in_specs=in_specs,
        out_specs=pl.BlockSpec((None, tb, RW_WIDTH), blk),
        out_shape=jax.ShapeDtypeStruct((bsz, seq, RW_WIDTH), bf16 if final else f32),
        scratch_shapes=[pltpu.VMEM((tb, EVEN_SHIFT), f32),
                        pltpu.VMEM((RW_NGROUPS, RW_GW, RW_GW), f32)],
        compiler_params=_cparams("parallel", "arbitrary"),
        name="rwkv_bwd" if reverse else "rwkv_fwd",
    )(*args)


def _trunk(x, p):
    bsz, seq, _ = x.shape
    tokens = bsz * seq
    x2d = x.reshape(tokens, D_MODEL)
    seq3 = lambda t: t.reshape(bsz, seq, t.shape[-1])
    flat = lambda t: t.reshape(tokens, t.shape[-1])

    ps, ga, qkv, gb = _even_in(x2d, seq, p["even_norm"], p["even_w_in"])
    rw = (seq3(ps), p["mu_prev"], p["mu_next"], p["w0"], p["w_up"], p["a0"], p["a_up"],
          p["k_k"], p["k_a"])
    y_fwd = _rwkv_scan(*rw, reverse=False)
    ya = _rwkv_scan(*rw, reverse=True, r_k=p["r_k"], gn_w=p["gn_w"], gn_b=p["gn_b"],
                    gate=seq3(ga), y_fwd=y_fwd)
    yb = _attention(seq3(qkv), seq3(gb))

    x1, q, k, v, gate, gate_lat = _mid(x2d, flat(ya), flat(yb), p["even_w_out"], p["odd_norm"],
                                       p["odd_w_in"])
    gla = (seq3(q), seq3(k), seq3(v), seq3(gate_lat), p["gate_up"], p["gate_bias"])
    o_fwd = _gla_scan(*gla, reverse=False)
    yc = _gla_scan(*gla, reverse=True, gate=seq3(gate), o_fwd=o_fwd, norm_w=p["gla_norm"])
    return _final(x1, flat(yc), p["odd_w_out"], p["final_norm"]).reshape(bsz, seq, D_MODEL)


def _prepare(even_norm, even_w_in, even_mu_prev, even_mu_next, rwkv_w0, rwkv_w_up, rwkv_a0,
             rwkv_a_up, rwkv_k_k, rwkv_k_a, rwkv_r_k, rwkv_gn_w, rwkv_gn_b, even_w_out, odd_norm,
             odd_w_in, gla_gate_up, gla_gate_bias, gla_norm, odd_w_out, final_norm):
    wi = odd_w_in[0]
    lat0 = 2 * GLA_KEY + GLA_VAL
    lat = jnp.pad(wi[:, lat0:lat0 + GLA_RANK], ((0, 0), (0, LANES - GLA_RANK)))
    odd_in = jnp.concatenate([wi[:, :lat0], wi[:, lat0 + GLA_RANK:], lat], axis=1)
    return {
        "even_norm": even_norm[0], "even_w_in": even_w_in[0].astype(bf16),
        "mu_prev": even_mu_prev[0], "mu_next": even_mu_next[0],
        "w0": rwkv_w0[0], "w_up": rwkv_w_up[0], "a0": rwkv_a0[0], "a_up": rwkv_a_up[0],
        "k_k": rwkv_k_k[0], "k_a": rwkv_k_a[0], "r_k": rwkv_r_k[0],
        "gn_w": rwkv_gn_w[0], "gn_b": rwkv_gn_b[0],
        "even_w_out": even_w_out[0].astype(bf16),
        "odd_norm": odd_norm[0], "odd_w_in": odd_in.astype(bf16),
        "gate_up": jnp.pad(gla_gate_up[0], ((0, 0), (0, LANES - GLA_RANK), (0, 0))),
        "gate_bias": gla_gate_bias[0], "gla_norm": gla_norm[0],
        "odd_w_out": odd_w_out[0].astype(bf16), "final_norm": final_norm,
    }


def kernel(x_prompt, x_sample, even_norm, even_w_in, even_mu_prev, even_mu_next, rwkv_w0, rwkv_w_up,
           rwkv_a0, rwkv_a_up, rwkv_k_k, rwkv_k_a, rwkv_r_k, rwkv_gn_w, rwkv_gn_b, even_w_out,
           odd_norm, odd_w_in, gla_gate_up, gla_gate_bias, gla_norm, odd_w_out, final_norm):
    p = _prepare(even_norm, even_w_in, even_mu_prev, even_mu_next, rwkv_w0, rwkv_w_up, rwkv_a0,
                 rwkv_a_up, rwkv_k_k, rwkv_k_a, rwkv_r_k, rwkv_gn_w, rwkv_gn_b, even_w_out,
                 odd_norm, odd_w_in, gla_gate_up, gla_gate_bias, gla_norm, odd_w_out, final_norm)
    return (_trunk(x_prompt, p), _trunk(x_sample, p))
```

```python
import functools
import math

import jax
import jax.numpy as jnp
from jax import lax
from jax.experimental import pallas as pl
from jax.experimental.pallas import tpu as pltpu

f32 = jnp.float32
bf16 = jnp.bfloat16

D_MODEL = 1024
RMS_EPS = 1e-6

RW_HEADS = 8
RW_DH = 64
RW_WIDTH = RW_HEADS * RW_DH
RW_LORA = 64
RW_DECAY_SCALE = 0.6065306597126334
RW_GN_EPS = 64e-5
AT_HEADS = 8
AT_DH = 64
AT_WIDTH = AT_HEADS * AT_DH
AT_SIDE = 64
AT_DILATIONS = (1, 4, 16)
ROPE_THETA = 10000.0
EVEN_SHIFT = 3 * RW_WIDTH + 2 * RW_LORA
EVEN_COLS = EVEN_SHIFT + RW_WIDTH + 4 * AT_WIDTH

GLA_HEADS = 4
GLA_KEY = 512
GLA_VAL = 1024
GLA_HK = GLA_KEY // GLA_HEADS
GLA_HV = GLA_VAL // GLA_HEADS
GLA_RANK = 16
GLA_GATE_NORM = 16.0

CHUNK = 64
LANES = 128
SUBLANES = 8
MXU_COLS = 256
ROW_BLOCK = 512
VMEM_LIMIT = 56 * 1024 * 1024

_NT = (((1,), (1,)), ((), ()))
_TN = (((0,), (0,)), ((), ()))


def _dot(a, b):
    return jnp.dot(a, b, preferred_element_type=f32)


def _dot_nt(a, b):
    return lax.dot_general(a, b, _NT, preferred_element_type=f32)


def _dot_tn(a, b):
    return lax.dot_general(a, b, _TN, preferred_element_type=f32)


def _split2(x):
    hi = x.astype(bf16)
    lo = (x - hi.astype(f32)).astype(bf16)
    return hi, lo


def _dot_seg(x, e):
    xb = x.astype(bf16)
    return jnp.concatenate([_dot(xb[:, c:c + LANES], e) for c in range(0, x.shape[1], LANES)], axis=1)


def _cumsum_rows(tri, x):
    hi, lo = _split2(x)
    return _dot(tri, hi) + _dot(tri, lo)


def _sigmoid(x):
    return 0.5 * (jnp.tanh(0.5 * x) + 1.0)


def _silu(x):
    h = 0.5 * x
    return h + h * jnp.tanh(h)


def _rms_rows(x, w):
    return x * lax.rsqrt(jnp.mean(x * x, axis=-1, keepdims=True) + RMS_EPS) * w


def _cparams(*sem):
    return pltpu.CompilerParams(dimension_semantics=sem, vmem_limit_bytes=VMEM_LIMIT)


def _rope_partner(x):
    half = AT_DH // 2
    lane = lax.broadcasted_iota(jnp.int32, x.shape, 1)
    return jnp.where((lane & half) == 0, pltpu.roll(x, LANES - half, 1), pltpu.roll(x, half, 1))


AT_FOLD = AT_DILATIONS[1]
assert AT_DILATIONS == (1, AT_FOLD, AT_FOLD * AT_FOLD)
AT_Q_DILS = AT_DILATIONS[1:]
AT_KV_DILS = AT_DILATIONS
AT_GROUPS = 3 * AT_WIDTH // LANES


def _even_in_kernel(x_ref, nw_ref, w_ref, cos_ref, sin_ref, *refs):
    ps_ref, ga_ref, gb_ref = refs[:3]
    nq, nkv = len(AT_Q_DILS), len(AT_KV_DILS)
    q_refs = refs[3:3 + nq]
    k_refs = refs[3 + nq:3 + nq + nkv]
    v_refs = refs[3 + nq + nkv:3 + nq + 2 * nkv]
    stage = refs[3 + nq + 2 * nkv:]
    tmp_refs, fold_refs = stage[:AT_GROUPS], stage[AT_GROUPS:]
    tm = x_ref.shape[0]
    xn = _rms_rows(x_ref[...], nw_ref[...]).astype(bf16)
    cos = cos_ref[...]
    sin = sin_ref[...]

    def plain(ref, off):
        def put(t, _):
            ref[:, off:off + LANES] = t
        return put

    def spread(outs, off, rotary, mul=None):
        def put(t, slot):
            if rotary:
                t = t * cos + _rope_partner(t) * sin
            if mul is not None:
                t = t * mul
            nat, fold = tmp_refs[slot], fold_refs[slot % len(fold_refs)]
            nat[...] = t
            per = tm // AT_FOLD
            by_dil = dict((dil, ref) for ref, dil in outs)
            if 1 in by_dil:
                by_dil[1][0, :, off:off + LANES] = t.astype(bf16)
            for r in range(AT_FOLD):
                x = nat[pl.ds(r, per, stride=AT_FOLD), :]
                by_dil[AT_FOLD][r, :, off:off + LANES] = x.astype(bf16)
                fold[r * per:(r + 1) * per, :] = x
            wide = AT_FOLD * AT_FOLD
            for r in range(wide):
                rows = pl.ds((r % AT_FOLD) * per + r // AT_FOLD, tm // wide, stride=AT_FOLD)
                by_dil[wide][r, :, off:off + LANES] = fold[rows, :].astype(bf16)
        return put

    cols = range(0, AT_WIDTH, LANES)
    dests = ([plain(ps_ref, c) for c in range(0, EVEN_SHIFT, LANES)]
             + [plain(ga_ref, c) for c in range(0, RW_WIDTH, LANES)]
             + [spread(list(zip(q_refs, AT_Q_DILS)), c, True, AT_DH ** -0.5) for c in cols]
             + [spread(list(zip(k_refs, AT_KV_DILS)), c, True) for c in cols]
             + [spread(list(zip(v_refs, AT_KV_DILS)), c, False) for c in cols]
             + [plain(gb_ref, c) for c in cols])
    first_at = (EVEN_SHIFT + RW_WIDTH) // LANES
    tile = 2 * MXU_COLS
    for c0 in range(0, EVEN_COLS, tile):
        c1 = min(c0 + tile, EVEN_COLS)
        acc = _dot(xn, w_ref[:, c0:c1])
        for j in range((c1 - c0) // LANES):
            g = c0 // LANES + j
            dests[g](acc[:, j * LANES:(j + 1) * LANES], (g - first_at) % AT_GROUPS)


def _rope_tables(seq):
    inv = ROPE_THETA ** (-jnp.arange(0, AT_DH, 2, dtype=f32) / AT_DH)
    ang = jnp.arange(seq, dtype=f32)[:, None] * inv[None, :]
    cos, sin = jnp.cos(ang), jnp.sin(ang)
    return (jnp.concatenate([cos, cos, cos, cos], axis=-1),
            jnp.concatenate([-sin, sin, -sin, sin], axis=-1))


def _even_in(x2d, seq, norm_w, w_in_bf16, block_rows=ROW_BLOCK):
    m = x2d.shape[0]
    tm = block_rows
    per_seq = seq // tm
    bsz = m // seq
    cos, sin = _rope_tables(seq)
    row = lambda i: (i, 0)
    full = lambda i: (0, 0)
    tab = lambda i: (i % per_seq, 0)
    res = lambda i: (i // per_seq, 0, i % per_seq, 0)
    dils = AT_Q_DILS + AT_KV_DILS + AT_KV_DILS
    outs = pl.pallas_call(
        _even_in_kernel,
        grid=(m // tm,),
        in_specs=[
            pl.BlockSpec((tm, D_MODEL), row),
            pl.BlockSpec((1, D_MODEL), full),
            pl.BlockSpec((D_MODEL, EVEN_COLS), full),
            pl.BlockSpec((tm, LANES), tab),
            pl.BlockSpec((tm, LANES), tab),
        ],
        out_specs=[
            pl.BlockSpec((tm, EVEN_SHIFT), row),
            pl.BlockSpec((tm, RW_WIDTH), row),
            pl.BlockSpec((tm, AT_WIDTH), row),
        ] + [pl.BlockSpec((None, d, tm // d, AT_WIDTH), res) for d in dils],
        out_shape=[
            jax.ShapeDtypeStruct((m, EVEN_SHIFT), f32),
            jax.ShapeDtypeStruct((m, RW_WIDTH), f32),
            jax.ShapeDtypeStruct((m, AT_WIDTH), f32),
        ] + [jax.ShapeDtypeStruct((bsz, d, seq // d, AT_WIDTH), bf16) for d in dils],
        scratch_shapes=[pltpu.VMEM((tm, LANES), f32)] * (AT_GROUPS + AT_FOLD),
        compiler_params=_cparams("parallel"),
        name="even_in",
    )(x2d, norm_w.reshape(1, D_MODEL), w_in_bf16, cos, sin)
    nq, nkv = len(AT_Q_DILS), len(AT_KV_DILS)
    return (outs[0], outs[1], outs[2], outs[3:3 + nq], outs[3 + nq:3 + nq + nkv],
            outs[3 + nq + nkv:])


ODD_PAD_COLS = 2 * GLA_KEY + 2 * GLA_VAL + LANES


def _mid_kernel(x_ref, ya_ref, yb_ref, wo_ref, nw_ref, wi_ref,
                x1_ref, q_ref, k_ref, v_ref, g_ref, gl_ref):
    x1 = (x_ref[...] + _dot(ya_ref[...], wo_ref[0:RW_WIDTH, :])
          + _dot(yb_ref[...], wo_ref[RW_WIDTH:RW_WIDTH + AT_WIDTH, :]))
    x1_ref[...] = x1
    xn = _rms_rows(x1, nw_ref[...]).astype(bf16)
    c = 0
    for ref, width in ((q_ref, GLA_KEY), (k_ref, GLA_KEY), (v_ref, GLA_VAL), (g_ref, GLA_VAL),
                       (gl_ref, LANES)):
        for c0 in range(0, width, 512):
            c1 = min(c0 + 512, width)
            ref[:, c0:c1] = _dot(xn, wi_ref[:, c + c0:c + c1]).astype(ref.dtype)
        c += width


def _mid(x2d, ya, yb, w_out_bf16, norm_w, w_in_pad_bf16, block_rows=ROW_BLOCK):
    m = x2d.shape[0]
    tm = block_rows
    row = lambda i: (i, 0)
    full = lambda i: (0, 0)
    widths = (D_MODEL, GLA_KEY, GLA_KEY, GLA_VAL, GLA_VAL, LANES)
    return pl.pallas_call(
        _mid_kernel,
        grid=(m // tm,),
        in_specs=[
            pl.BlockSpec((tm, D_MODEL), row),
            pl.BlockSpec((tm, RW_WIDTH), row),
            pl.BlockSpec((tm, AT_WIDTH), row),
            pl.BlockSpec((RW_WIDTH + AT_WIDTH, D_MODEL), full),
            pl.BlockSpec((1, D_MODEL), full),
            pl.BlockSpec((D_MODEL, ODD_PAD_COLS), full),
        ],
        out_specs=[pl.BlockSpec((tm, w), row) for w in widths],
        out_shape=[jax.ShapeDtypeStruct((m, w), bf16 if i == 3 else f32)
                   for i, w in enumerate(widths)],
        compiler_params=_cparams("parallel"),
        name="mid",
    )(x2d, ya, yb, w_out_bf16, norm_w.reshape(1, D_MODEL), w_in_pad_bf16)


def _final_kernel(x_ref, y_ref, wo_ref, nw_ref, o_ref):
    x2 = x_ref[...] + _dot(y_ref[...], wo_ref[...])
    o_ref[...] = _rms_rows(x2, nw_ref[...])


def _final(x2d, yc, w_out_bf16, norm_w, block_rows=ROW_BLOCK):
    m = x2d.shape[0]
    tm = block_rows
    row = lambda i: (i, 0)
    full = lambda i: (0, 0)
    return pl.pallas_call(
        _final_kernel,
        grid=(m // tm,),
        in_specs=[
            pl.BlockSpec((tm, D_MODEL), row),
            pl.BlockSpec((tm, GLA_VAL), row),
            pl.BlockSpec((GLA_VAL, D_MODEL), full),
            pl.BlockSpec((1, D_MODEL), full),
        ],
        out_specs=pl.BlockSpec((tm, D_MODEL), row),
        out_shape=jax.ShapeDtypeStruct((m, D_MODEL), f32),
        compiler_params=_cparams("parallel"),
        name="final",
    )(x2d, yc, w_out_bf16, norm_w.reshape(1, D_MODEL))


def _causal_mask(reverse):
    t = lax.broadcasted_iota(jnp.int32, (CHUNK, CHUNK), 0)
    s = lax.broadcasted_iota(jnp.int32, (CHUNK, CHUNK), 1)
    return s >= t if reverse else s <= t


def _gla_kernel(*refs, reverse, final, block_rows):
    if final:
        (q_ref, k_ref, v_ref, gl_ref, gup_ref, gb_ref, gate_ref, of_ref, nw_ref, o_ref, st_ref) = refs
    else:
        (q_ref, k_ref, v_ref, gl_ref, gup_ref, gb_ref, o_ref, st_ref) = refs
    d = 1 if reverse else 0

    @pl.when(pl.program_id(1) == 0)
    def _():
        st_ref[...] = jnp.zeros_like(st_ref)

    incl = _causal_mask(reverse)
    tri = incl.astype(bf16)
    gup = gup_ref[d]
    gbias = gb_ref[d:d + 1, :]
    nchunks = block_rows // CHUNK
    last = 0 if reverse else CHUNK - 1

    x = _dot(gl_ref[...].astype(bf16), gup) + gbias
    g = (jnp.minimum(x, 0.0) - jnp.log(1.0 + jnp.exp(-jnp.abs(x)))) / GLA_GATE_NORM
    chunk_rows = [slice(c * CHUNK, (c + 1) * CHUNK) for c in range(nchunks)]
    bcum = jnp.concatenate([_cumsum_rows(tri, g[rs]) for rs in chunk_rows], axis=0)
    b_last = [bcum[c * CHUNK + last:c * CHUNK + last + 1, :] for c in range(nchunks)]
    b_end = jnp.concatenate([jnp.broadcast_to(b, (CHUNK, GLA_KEY)) for b in b_last], axis=0)
    q = q_ref[...] * (GLA_HK ** -0.5)
    k = k_ref[...]
    q_dec = (q * jnp.exp(bcum)).astype(bf16)
    k_dec = (k * jnp.exp(-bcum)).astype(bf16)
    k_end = (k * jnp.exp(b_end - bcum)).astype(bf16)

    order = list(reversed(range(nchunks))) if reverse else list(range(nchunks))
    klanes = [slice(h * GLA_HK, (h + 1) * GLA_HK) for h in range(GLA_HEADS)]
    vlanes = [slice(h * GLA_HV, (h + 1) * GLA_HV) for h in range(GLA_HEADS)]
    units = [(c, h) for c in order for h in range(GLA_HEADS)]
    v = [v_ref[chunk_rows[c], vlanes[h]].astype(bf16) for c, h in units]
    att = [jnp.where(incl, _dot_nt(q_dec[chunk_rows[c], klanes[h]], k_dec[chunk_rows[c], klanes[h]]),
                     0.0).astype(bf16) for c, h in units]
    o_intra = [_dot(att[i], v[i]) for i in range(len(units))]
    vk = [_dot_tn(v[i], k_end[chunk_rows[c], klanes[h]]) for i, (c, h) in enumerate(units)]
    decay = [jnp.exp(b) for b in b_last]

    state = [st_ref[h] for h in range(GLA_HEADS)]
    for i, (c, h) in enumerate(units):
        rows = chunk_rows[c]
        o = o_intra[i] + _dot_nt(q_dec[rows, klanes[h]], state[h].astype(bf16))
        state[h] = state[h] * decay[c][:, klanes[h]] + vk[i]
        if final:
            o = o + of_ref[rows, vlanes[h]]
            o = o * lax.rsqrt(jnp.mean(o * o, axis=-1, keepdims=True) + RMS_EPS) * nw_ref[...]
            o_ref[rows, vlanes[h]] = (o * _silu(gate_ref[rows, vlanes[h]])).astype(o_ref.dtype)
        else:
            o_ref[rows, vlanes[h]] = o
    for h in range(GLA_HEADS):
        st_ref[h] = state[h]


def _gla_scan(q, k, v, gl, gate_up_pad, gate_bias, *, reverse, gate=None, o_fwd=None, norm_w=None,
              block_rows=ROW_BLOCK):
    bsz, seq, _ = q.shape
    tb = block_rows
    nb = seq // tb
    final = reverse
    blk = (lambda b, j: (b, nb - 1 - j, 0)) if reverse else (lambda b, j: (b, j, 0))
    full2 = lambda b, j: (0, 0)
    full3 = lambda b, j: (0, 0, 0)
    in_specs = [
        pl.BlockSpec((None, tb, GLA_KEY), blk),
        pl.BlockSpec((None, tb, GLA_KEY), blk),
        pl.BlockSpec((None, tb, GLA_VAL), blk),
        pl.BlockSpec((None, tb, LANES), blk),
        pl.BlockSpec((2, LANES, GLA_KEY), full3),
        pl.BlockSpec((2, GLA_KEY), full2),
    ]
    args = [q, k, v, gl, gate_up_pad.astype(bf16), gate_bias]
    if final:
        in_specs += [
            pl.BlockSpec((None, tb, GLA_VAL), blk),
            pl.BlockSpec((None, tb, GLA_VAL), blk),
            pl.BlockSpec((1, GLA_HV), full2),
        ]
        args += [gate, o_fwd, norm_w.reshape(1, GLA_HV)]
    return pl.pallas_call(
        functools.partial(_gla_kernel, reverse=reverse, final=final, block_rows=tb),
        grid=(bsz, nb),
        in_specs=in_specs,
        out_specs=pl.BlockSpec((None, tb, GLA_VAL), blk),
        out_shape=jax.ShapeDtypeStruct((bsz, seq, GLA_VAL), bf16 if final else f32),
        scratch_shapes=[pltpu.VMEM((GLA_HEADS, GLA_HV, GLA_HK), f32)],
        compiler_params=_cparams("parallel", "arbitrary"),
        name="gla_bwd" if reverse else "gla_fwd",
    )(*args)


AT_NEG = -1e30
AT_QBLK = 128
AT_KBLK = 256
AT_UNROLL = 8


def _attn_kernel(q4, q16, k1, k4, k16, v1, v4, v16, g_ref, o_ref, m_s, l_s, acc_s, bias_s, tmp_s,
                 *, seq):
    lane = lax.broadcasted_iota(jnp.int32, (1, LANES), 1)
    left = lane < AT_DH
    fold_len = seq // AT_FOLD

    def fill_bias(offs, tq, nk, deltas):
        for d, delta in enumerate(deltas):
            bias_s[d, 0:tq, 0:nk] = jnp.where(jnp.abs(offs + delta) <= AT_SIDE, 0.0, AT_NEG)

    def run_blocks(nblocks, load_unit):
        def body(it, carry):
            units = [load_unit(it * AT_UNROLL + u) for u in range(AT_UNROLL)]
            heads = [(u, first) for u in range(AT_UNROLL) for first in (True, False)]
            scores = []
            for u, first in heads:
                qb, kb, _, bias, _, _ = units[u]
                qh = jnp.where(left if first else ~left, qb, jnp.zeros_like(qb))
                scores.append(_dot_nt(qh, kb) + bias)
            maxes = [jnp.max(s, axis=1, keepdims=True) for s in scores]
            probs = [jnp.exp(s - mh) for s, mh in zip(scores, maxes)]
            sums = [jnp.sum(p, axis=1, keepdims=True) for p in probs]
            outs = [_dot(p.astype(bf16), units[u][2]) for p, (u, _) in zip(probs, heads)]
            olds = [None if units[u][5] else
                    [(m_s[idx, :], l_s[idx, :], acc_s[idx, :]) for idx, _ in units[u][4]]
                    for u in range(AT_UNROLL)]
            for u in range(AT_UNROLL):
                m_new = jnp.where(left, maxes[2 * u], maxes[2 * u + 1])
                l_new = jnp.where(left, sums[2 * u], sums[2 * u + 1])
                a_new = jnp.where(left, outs[2 * u], outs[2 * u + 1])
                for pi, (idx, rows) in enumerate(units[u][4]):
                    mp, lp, ap = m_new[rows], l_new[rows], a_new[rows]
                    if units[u][5]:
                        m_s[idx, :] = mp
                        l_s[idx, :] = lp
                        acc_s[idx, :] = ap
                    else:
                        m_old, l_old, a_old = olds[u][pi]
                        m = jnp.maximum(m_old, mp)
                        w_old = jnp.exp(m_old - m)
                        w_new = jnp.exp(mp - m)
                        m_s[idx, :] = m
                        l_s[idx, :] = l_old * w_old + lp * w_new
                        acc_s[idx, :] = a_old * w_old + ap * w_new
            return carry

        lax.fori_loop(0, nblocks // AT_UNROLL, body, 0)

    def window(m0, sub, tq, nk):
        ks = jnp.clip(m0 - AT_SIDE, 0, sub - nk)
        return ks, (m0 - ks) // AT_SIDE

    tq, nk = AT_QBLK, AT_KBLK
    per = tq // AT_FOLD
    qi = lax.broadcasted_iota(jnp.int32, (tq, nk), 0)
    kj = lax.broadcasted_iota(jnp.int32, (tq, nk), 1)
    fill_bias((qi // per) + AT_FOLD * (qi % per) - kj, tq, nk, (0, AT_SIDE, 2 * AT_SIDE))

    def unit_d1(i):
        t0 = i * tq
        ks, bi = window(t0, seq, tq, nk)
        qrow = pl.ds(pl.multiple_of(i * per, per), per)
        qb = jnp.concatenate([q4[rho, qrow, :] for rho in range(AT_FOLD)], axis=0)
        krow = pl.ds(pl.multiple_of(ks, AT_SIDE), nk)
        state = [(pl.ds(pl.multiple_of(rho * fold_len + i * per, per), per),
                  slice(rho * per, (rho + 1) * per)) for rho in range(AT_FOLD)]
        return qb, k1[0, krow, :], v1[0, krow, :], bias_s[bi], state, True

    run_blocks(seq // tq, unit_d1)

    sub = fold_len
    fill_bias(qi - kj, tq, nk, (0, AT_SIDE, 2 * AT_SIDE))
    nblk = sub // tq

    def unit_fold(i):
        r = i // nblk
        m0 = (i % nblk) * tq
        ks, bi = window(m0, sub, tq, nk)
        qrow = pl.ds(pl.multiple_of(m0, tq), tq)
        krow = pl.ds(pl.multiple_of(ks, AT_SIDE), nk)
        state = [(pl.ds(pl.multiple_of(r * sub + m0, tq), tq), slice(0, tq))]
        return q4[r, qrow, :], k4[r, krow, :], v4[r, krow, :], bias_s[bi], state, False

    run_blocks(AT_FOLD * nblk, unit_fold)

    dil = AT_DILATIONS[-1]
    sub = seq // dil
    tq, nk = min(AT_QBLK, sub), min(AT_KBLK, sub)
    fill_bias((qi - kj)[0:tq, 0:nk], tq, nk, (0, AT_SIDE, nk - tq))
    nblk = sub // tq

    def unit_wide(i):
        r = i // nblk
        m0 = (i % nblk) * tq
        ks, bi = window(m0, sub, tq, nk)
        qrow = pl.ds(pl.multiple_of(m0, AT_SIDE), tq)
        krow = pl.ds(pl.multiple_of(ks, AT_SIDE), nk)
        srow = (r % AT_FOLD) * fold_len + r // AT_FOLD + (dil // AT_FOLD) * m0
        state = [(pl.ds(srow, tq, stride=dil // AT_FOLD), slice(0, tq))]
        return (q16[r, qrow, :], k16[r, krow, :], v16[r, krow, :], bias_s[bi, 0:tq, 0:nk], state,
                False)

    run_blocks(dil * nblk, unit_wide)

    rows = tmp_s.shape[0]
    per_fold = rows // AT_FOLD

    def finish(i, carry):
        for rho in range(AT_FOLD):
            idx = pl.ds(pl.multiple_of(rho * fold_len + i * per_fold, per_fold), per_fold)
            tmp_s[pl.ds(rho, per_fold, stride=AT_FOLD), :] = acc_s[idx, :] / l_s[idx, :]
        out = pl.ds(pl.multiple_of(i * rows, rows), rows)
        o_ref[out, :] = (tmp_s[...] * _silu(g_ref[out, :])).astype(o_ref.dtype)
        return carry

    lax.fori_loop(0, seq // rows, finish, 0)


def _attention(qs, ks, vs, gate):
    bsz, seq, _ = gate.shape
    pairs = AT_WIDTH // LANES
    once = pl.Buffered(1)
    col = lambda b, p: (b, 0, p)
    res = lambda b, p: (b, 0, 0, p)
    copies = list(qs) + list(ks) + list(vs)
    return pl.pallas_call(
        functools.partial(_attn_kernel, seq=seq),
        grid=(bsz, pairs),
        in_specs=[pl.BlockSpec((None, t.shape[1], t.shape[2], LANES), res, pipeline_mode=once)
                  for t in copies]
        + [pl.BlockSpec((None, seq, LANES), col, pipeline_mode=once)],
        out_specs=pl.BlockSpec((None, seq, LANES), col),
        out_shape=jax.ShapeDtypeStruct((bsz, seq, AT_WIDTH), bf16),
        scratch_shapes=([pltpu.VMEM((seq, LANES), f32)] * 3
                        + [pltpu.VMEM((3, AT_QBLK, AT_KBLK), f32),
                           pltpu.VMEM((min(seq, 512), LANES), f32)]),
        compiler_params=_cparams("parallel", "parallel"),
        name="dilated_attn",
    )(*copies, gate)


RW_GROUP = 2
RW_GW = RW_GROUP * RW_DH
RW_NGROUPS = RW_HEADS // RW_GROUP
RW_WAVE = 4


def _head_stack(x, masks):
    return jnp.concatenate([jnp.where(mk, x, jnp.zeros_like(x)) for mk in masks], axis=0)


def _head_stack_t(x, row_masks):
    xt = x.T.astype(bf16)
    return jnp.concatenate([jnp.where(mk, xt, jnp.zeros_like(xt)) for mk in row_masks], axis=1)


def _unit_lower_inverse_stages(w, src, dst):
    steps = int(math.log2(CHUNK)) - 1

    def start():
        n = w[src][0].shape[0]
        eye = (lax.broadcasted_iota(jnp.int32, (n, n), 0)
               == lax.broadcasted_iota(jnp.int32, (n, n), 1)).astype(f32)
        w["_t"] = [eye + a for a in w[src]]
        qs = [a.astype(bf16) for a in w[src]]
        w["_q"] = [_dot(q, q).astype(bf16) for q in qs]

    def double():
        n = w[src][0].shape[0]
        both = [_dot(jnp.concatenate([q, t.astype(bf16)], axis=0), q)
                for q, t in zip(w["_q"], w["_t"])]
        w["_q"] = [x[:n].astype(bf16) for x in both]
        w["_t"] = [t + x[n:] for t, x in zip(w["_t"], both)]

    def finish():
        w[dst] = [(t + _dot(t.astype(bf16), q)).astype(bf16) for q, t in zip(w["_q"], w["_t"])]

    return [start] + [double] * (steps - 1) + [finish]


def _rwkv_kernel(*refs, reverse, final, block_rows, nblocks):
    (ps_ref, prev_ref, next_ref, mup_ref, mun_ref, w0_ref, wup_ref, a0_ref, aup_ref,
     kk_ref, ka_ref, seg_ref) = refs[:12]
    if final:
        rk_ref, gnw_ref, gnb_ref, ga_ref, yf_ref, o_ref, sh_ref, st_ref = refs[12:]
    else:
        o_ref, sh_ref, st_ref = refs[12:]
    d = 1 if reverse else 0
    tb = block_rows
    j = pl.program_id(1)
    blk = (nblocks - 1 - j) if reverse else j

    @pl.when(j == 0)
    def _():
        st_ref[...] = jnp.zeros_like(st_ref)

    rid = lax.broadcasted_iota(jnp.int32, (tb, 1), 0)
    has_prev = (blk > 0).astype(f32)
    has_next = (blk < nblocks - 1).astype(f32)
    for c0 in range(0, EVEN_SHIFT, 256):
        c1 = min(c0 + 256, EVEN_SHIFT)
        x = ps_ref[:, c0:c1]
        before = prev_ref[SUBLANES - 1:SUBLANES, c0:c1] * has_prev
        after = next_ref[0:1, c0:c1] * has_next
        prv = jnp.where(rid == 0, before, pltpu.roll(x, 1, 0))
        nxt = jnp.where(rid == tb - 1, after, pltpu.roll(x, tb - 1, 0))
        sh_ref[:, c0:c1] = x + mup_ref[:, c0:c1] * (prv - x) + mun_ref[:, c0:c1] * (nxt - x)

    tri = _causal_mask(reverse).astype(bf16)
    gi = lax.broadcasted_iota(jnp.int32, (2 * RW_GW, 2 * RW_GW), 0)
    gj = lax.broadcasted_iota(jnp.int32, (2 * RW_GW, 2 * RW_GW), 1)
    ahead = (gi % CHUNK) - (gj % CHUNK)
    ahead = -ahead if reverse else ahead
    keep = ahead < jnp.where(gj < RW_GW, 0, 1)
    lane = lax.broadcasted_iota(jnp.int32, (1, RW_GW), 1)
    masks = [(lane // RW_DH) == h for h in range(RW_GROUP)]
    rowi = lax.broadcasted_iota(jnp.int32, (RW_GW, 1), 0)
    row_masks = [(rowi // RW_DH) == h for h in range(RW_GROUP)]
    seg = seg_ref[...]
    last = 0 if reverse else CHUNK - 1
    nchunks = tb // CHUNK
    w_lo, a_lo = 3 * RW_WIDTH, 3 * RW_WIDTH + RW_LORA

    def lr_gate(a_lat, dd):
        return _sigmoid(a0_ref[dd:dd + 1, :] + _dot(a_lat.astype(bf16), aup_ref[dd]))

    r = sh_ref[:, 0:RW_WIDTH]
    k = sh_ref[:, RW_WIDTH:2 * RW_WIDTH]
    v = sh_ref[:, 2 * RW_WIDTH:3 * RW_WIDTH]
    w_lat = sh_ref[:, w_lo:w_lo + RW_LORA]
    a_lat = sh_ref[:, a_lo:a_lo + RW_LORA]
    w_log = -RW_DECAY_SCALE * _sigmoid(
        w0_ref[d:d + 1, :] + _dot(jnp.tanh(w_lat).astype(bf16), wup_ref[d]))
    lr = lr_gate(a_lat, d)
    kk = k * kk_ref[...]
    kk = kk * lax.rsqrt(jnp.maximum(_dot_seg(kk * kk, seg), 1e-24))
    k_dir = k * (1.0 + (lr - 1.0) * ka_ref[...])
    chunk_rows = [slice(c * CHUNK, (c + 1) * CHUNK) for c in range(nchunks)]
    cum = jnp.concatenate([_cumsum_rows(tri, w_log[rs]) for rs in chunk_rows], axis=0)
    grow = jnp.exp(-cum)
    a_t = -kk * jnp.exp(cum - w_log)
    b_t = kk * lr * grow
    k_t = k_dir * grow
    r_t = r * jnp.exp(cum)
    a_b, b_b, k_b, r_b = (t.astype(bf16) for t in (a_t, b_t, k_t, r_t))

    order = list(reversed(range(nchunks))) if reverse else list(range(nchunks))
    lanes = [slice(g * RW_GW, (g + 1) * RW_GW) for g in range(RW_NGROUPS)]
    state = [st_ref[g] for g in range(RW_NGROUPS)]
    y_rows = {}

    def wave_stages(chunks):
        units = [(c, g) for c in chunks for g in range(RW_NGROUPS)]
        n = range(len(units))
        w = {}

        def stacked():
            stack = lambda x: [_head_stack(x[chunk_rows[c], lanes[g]], masks) for c, g in units]
            stack_t = lambda x: [_head_stack_t(x[chunk_rows[c], lanes[g]], row_masks) for c, g in units]
            w["a_s"] = stack(a_b)
            w["r_s"] = stack(r_b)
            w["b_s"] = stack(b_b)
            w["k_s"] = stack(k_b)
            w["a_st"] = stack_t(a_t)
            w["r_st"] = stack_t(r_t)
            w["v_st"] = stack_t(v)

        def grams():
            gram = [jnp.where(keep, _dot_nt(jnp.concatenate([w["b_s"][i], w["k_s"][i]], axis=0),
                                             jnp.concatenate([w["a_s"][i], w["r_s"][i]], axis=0)),
                              0.0) for i in n]
            w["ab_t"] = [gram[i][0:RW_GW, 0:RW_GW] for i in n]
            w["ak_rk_t"] = [gram[i][RW_GW:, :].astype(bf16) for i in n]
            w["rb_t"] = [gram[i][0:RW_GW, RW_GW:].astype(bf16) for i in n]

        inverse = _unit_lower_inverse_stages(w, "ab_t", "t_inv_t")

        def value_products():
            x = [_dot(w["v_st"][i], w["ak_rk_t"][i]) for i in n]
            w["av_t"] = [t[:, 0:RW_GW].astype(bf16) for t in x]
            w["y_local_t"] = [t[:, RW_GW:] for t in x]
            w["vk"] = [_dot(w["v_st"][i], w["k_s"][i]) for i in n]

        def apply_inverse():
            x = [_dot(jnp.concatenate([w["av_t"][i], w["a_st"][i]], axis=0), w["t_inv_t"][i])
                 for i in n]
            w["z_t"] = [t[0:RW_GW] for t in x]
            w["wm_r_t"] = [jnp.concatenate([x[i][RW_GW:].astype(bf16), w["r_st"][i]], axis=1)
                           for i in n]
            w["rb_b"] = [jnp.concatenate([w["rb_t"][i], w["b_s"][i]], axis=1) for i in n]

        def recur(pos_c, c):
            def new_state():
                p_end = jnp.exp(cum[c * CHUNK + last:c * CHUNK + last + 1, :])
                w["y_t"] = []
                for g in range(RW_NGROUPS):
                    i = pos_c * RW_NGROUPS + g
                    s0 = state[g]
                    x = _dot(s0.astype(bf16), w["wm_r_t"][i])
                    u_t = (x[:, 0:RW_GW] + w["z_t"][i]).astype(bf16)
                    x2 = _dot(u_t, w["rb_b"][i])
                    state[g] = (s0 + x2[:, RW_GW:] + w["vk"][i]) * p_end[:, lanes[g]]
                    w["y_t"].append(x[:, RW_GW:] + x2[:, 0:RW_GW] + w["y_local_t"][i])

            def outputs():
                ys = []
                for g in range(RW_NGROUPS):
                    y = w["y_t"][g].T
                    yg = y[0:CHUNK]
                    for h in range(1, RW_GROUP):
                        yg = yg + y[h * CHUNK:(h + 1) * CHUNK]
                    ys.append(yg)
                y_rows[c] = jnp.concatenate(ys, axis=1)

            return [new_state, outputs]

        independent = [stacked, grams] + inverse + [value_products, apply_inverse]
        recurrence = [f for pos_c, c in enumerate(chunks) for f in recur(pos_c, c)]
        return independent, recurrence

    waves = [order[i:i + RW_WAVE] for i in range(0, nchunks, RW_WAVE)]
    pending = []
    for chunks in waves:
        independent, recurrence = wave_stages(chunks)
        every = max(1, len(independent) // (len(pending) + 1)) if pending else 0
        for si, stage in enumerate(independent):
            stage()
            if pending and (si + 1) % every == 0:
                pending.pop(0)()
        while pending:
            pending.pop(0)()
        pending = recurrence
    while pending:
        pending.pop(0)()
    for g in range(RW_NGROUPS):
        st_ref[g] = state[g]
    y_all = jnp.concatenate([y_rows[c] for c in range(nchunks)], axis=0)

    if not final:
        o_ref[...] = y_all
    else:
        y_all = y_all + yf_ref[...]
        mean = _dot_seg(y_all, seg) * (1.0 / RW_DH)
        cen = y_all - mean
        var = _dot_seg(cen * cen, seg) * (1.0 / RW_DH)
        yn = cen * lax.rsqrt(var + RW_GN_EPS) * gnw_ref[...] + gnb_ref[...]
        k_other = k * (1.0 + (lr_gate(a_lat, 1 - d) - 1.0) * ka_ref[...])
        rk = _dot_seg(r * (k_dir + k_other) * rk_ref[...], seg)
        o_ref[...] = ((yn + rk * v) * _silu(ga_ref[...])).astype(o_ref.dtype)


def _segment_ones():
    h = jnp.arange(LANES) // RW_DH
    return (h[:, None] == h[None, :]).astype(bf16)


def _rwkv_scan(ps, mu_prev, mu_next, w0, w_up, a0, a_up, k_k, k_a, *, reverse,
               r_k=None, gn_w=None, gn_b=None, gate=None, y_fwd=None, block_rows=2 * ROW_BLOCK):
    bsz, seq, _ = ps.shape
    tb = block_rows
    nb = seq // tb
    final = reverse
    halo = tb // SUBLANES
    pos = (lambda j: nb - 1 - j) if reverse else (lambda j: j)
    blk = lambda b, j: (b, pos(j), 0)
    prev = lambda b, j: (b, jnp.maximum(pos(j) * halo - 1, 0), 0)
    nxt = lambda b, j: (b, jnp.minimum((pos(j) + 1) * halo, seq // SUBLANES - 1), 0)
    full2 = lambda b, j: (0, 0)
    full3 = lambda b, j: (0, 0, 0)
    vec = lambda n: pl.BlockSpec((1, n), full2)
    in_specs = [
        pl.BlockSpec((None, tb, EVEN_SHIFT), blk),
        pl.BlockSpec((None, SUBLANES, EVEN_SHIFT), prev),
        pl.BlockSpec((None, SUBLANES, EVEN_SHIFT), nxt),
        vec(EVEN_SHIFT), vec(EVEN_SHIFT),
        pl.BlockSpec((2, RW_WIDTH), full2),
        pl.BlockSpec((2, RW_LORA, RW_WIDTH), full3),
        pl.BlockSpec((2, RW_WIDTH), full2),
        pl.BlockSpec((2, RW_LORA, RW_WIDTH), full3),
        vec(RW_WIDTH), vec(RW_WIDTH),
        pl.BlockSpec((LANES, LANES), full2),
    ]
    row = lambda t: t.reshape(1, -1)
    args = [ps, ps, ps, row(mu_prev), row(mu_next), w0, w_up.astype(bf16), a0, a_up.astype(bf16),
            row(k_k), row(k_a),
            _segment_ones()]
    if final:
        in_specs += [vec(RW_WIDTH), vec(RW_WIDTH), vec(RW_WIDTH),
                     pl.BlockSpec((None, tb, RW_WIDTH), blk),
                     pl.BlockSpec((None, tb, RW_WIDTH), blk)]
        args += [row(r_k), row(gn_w), row(gn_b), gate, y_fwd]
    return pl.pallas_call(
        functools.partial(_rwkv_kernel, reverse=reverse, final=final, block_rows=tb, nblocks=nb),
        grid=(bsz, nb),
        in_specs=in_specs,
        out_specs=pl.BlockSpec((None, tb, RW_WIDTH), blk),
        out_shape=jax.ShapeDtypeStruct((bsz, seq, RW_WIDTH), bf16 if final else f32),
        scratch_shapes=[pltpu.VMEM((tb, EVEN_SHIFT), f32),
                        pltpu.VMEM((RW_NGROUPS, RW_GW, RW_GW), f32)],
        compiler_params=_cparams("parallel", "arbitrary"),
        name="rwkv_bwd" if reverse else "rwkv_fwd",
    )(*args)


def _trunk(x, p):
    bsz, seq, _ = x.shape
    tokens = bsz * seq
    x2d = x.reshape(tokens, D_MODEL)
    seq3 = lambda t: t.reshape(bsz, seq, t.shape[-1])
    flat = lambda t: t.reshape(tokens, t.shape[-1])

    ps, ga, gb, qs, ks, vs = _even_in(x2d, seq, p["even_norm"], p["even_w_in"])
    rw = (seq3(ps), p["mu_prev"], p["mu_next"], p["w0"], p["w_up"], p["a0"], p["a_up"],
          p["k_k"], p["k_a"])
    y_fwd = _rwkv_scan(*rw, reverse=False)
    ya = _rwkv_scan(*rw, reverse=True, r_k=p["r_k"], gn_w=p["gn_w"], gn_b=p["gn_b"],
                    gate=seq3(ga), y_fwd=y_fwd)
    yb = _attention(qs, ks, vs, seq3(gb))

    x1, q, k, v, gate, gate_lat = _mid(x2d, flat(ya), flat(yb), p["even_w_out"], p["odd_norm"],
                                       p["odd_w_in"])
    gla = (seq3(q), seq3(k), seq3(v), seq3(gate_lat), p["gate_up"], p["gate_bias"])
    o_fwd = _gla_scan(*gla, reverse=False)
    yc = _gla_scan(*gla, reverse=True, gate=seq3(gate), o_fwd=o_fwd, norm_w=p["gla_norm"])
    return _final(x1, flat(yc), p["odd_w_out"], p["final_norm"]).reshape(bsz, seq, D_MODEL)


def _prepare(even_norm, even_w_in, even_mu_prev, even_mu_next, rwkv_w0, rwkv_w_up, rwkv_a0,
             rwkv_a_up, rwkv_k_k, rwkv_k_a, rwkv_r_k, rwkv_gn_w, rwkv_gn_b, even_w_out, odd_norm,
             odd_w_in, gla_gate_up, gla_gate_bias, gla_norm, odd_w_out, final_norm):
    wi = odd_w_in[0]
    lat0 = 2 * GLA_KEY + GLA_VAL
    lat = jnp.pad(wi[:, lat0:lat0 + GLA_RANK], ((0, 0), (0, LANES - GLA_RANK)))
    odd_in = jnp.concatenate([wi[:, :lat0], wi[:, lat0 + GLA_RANK:], lat], axis=1)
    return {
        "even_norm": even_norm[0], "even_w_in": even_w_in[0].astype(bf16),
        "mu_prev": even_mu_prev[0], "mu_next": even_mu_next[0],
        "w0": rwkv_w0[0], "w_up": rwkv_w_up[0], "a0": rwkv_a0[0], "a_up": rwkv_a_up[0],
        "k_k": rwkv_k_k[0], "k_a": rwkv_k_a[0], "r_k": rwkv_r_k[0],
        "gn_w": rwkv_gn_w[0], "gn_b": rwkv_gn_b[0],
        "even_w_out": even_w_out[0].astype(bf16),
        "odd_norm": odd_norm[0], "odd_w_in": odd_in.astype(bf16),
        "gate_up": jnp.pad(gla_gate_up[0], ((0, 0), (0, LANES - GLA_RANK), (0, 0))),
        "gate_bias": gla_gate_bias[0], "gla_norm": gla_norm[0],
        "odd_w_out": odd_w_out[0].astype(bf16), "final_norm": final_norm,
    }


def kernel(x_prompt, x_sample, even_norm, even_w_in, even_mu_prev, even_mu_next, rwkv_w0, rwkv_w_up,
           rwkv_a0, rwkv_a_up, rwkv_k_k, rwkv_k_a, rwkv_r_k, rwkv_gn_w, rwkv_gn_b, even_w_out,
           odd_norm, odd_w_in, gla_gate_up, gla_gate_bias, gla_norm, odd_w_out, final_norm):
    p = _prepare(even_norm, even_w_in, even_mu_prev, even_mu_next, rwkv_w0, rwkv_w_up, rwkv_a0,
                 rwkv_a_up, rwkv_k_k, rwkv_k_a, rwkv_r_k, rwkv_gn_w, rwkv_gn_b, even_w_out,
                 odd_norm, odd_w_in, gla_gate_up, gla_gate_bias, gla_norm, odd_w_out, final_norm)
    return (_trunk(x_prompt, p), _trunk(x_sample, p))
```

```python
import functools
import math

import jax
import jax.numpy as jnp
from jax import lax
from jax.experimental import pallas as pl
from jax.experimental.pallas import tpu as pltpu

f32 = jnp.float32
bf16 = jnp.bfloat16

D_MODEL = 1024
RMS_EPS = 1e-6

RW_HEADS = 8
RW_DH = 64
RW_WIDTH = RW_HEADS * RW_DH
RW_LORA = 64
RW_DECAY_SCALE = 0.6065306597126334
RW_GN_EPS = 64e-5
AT_HEADS = 8
AT_DH = 64
AT_WIDTH = AT_HEADS * AT_DH
AT_SIDE = 64
AT_DILATIONS = (1, 4, 16)
ROPE_THETA = 10000.0
EVEN_SHIFT = 3 * RW_WIDTH + 2 * RW_LORA
EVEN_COLS = EVEN_SHIFT + RW_WIDTH + 4 * AT_WIDTH

GLA_HEADS = 4
GLA_KEY = 512
GLA_VAL = 1024
GLA_HK = GLA_KEY // GLA_HEADS
GLA_HV = GLA_VAL // GLA_HEADS
GLA_RANK = 16
GLA_GATE_NORM = 16.0

CHUNK = 64
LANES = 128
SUBLANES = 8
MXU_COLS = 256
ROW_BLOCK = 512
VMEM_LIMIT = 56 * 1024 * 1024

_NT = (((1,), (1,)), ((), ()))
_TN = (((0,), (0,)), ((), ()))


def _dot(a, b):
    return jnp.dot(a, b, preferred_element_type=f32)


def _dot_nt(a, b):
    return lax.dot_general(a, b, _NT, preferred_element_type=f32)


def _dot_tn(a, b):
    return lax.dot_general(a, b, _TN, preferred_element_type=f32)


def _split2(x):
    hi = x.astype(bf16)
    lo = (x - hi.astype(f32)).astype(bf16)
    return hi, lo


def _dot_seg(x, e):
    xb = x.astype(bf16)
    return jnp.concatenate([_dot(xb[:, c:c + LANES], e) for c in range(0, x.shape[1], LANES)], axis=1)


def _cumsum_rows(tri, x):
    hi, lo = _split2(x)
    return _dot(tri, hi) + _dot(tri, lo)


def _sigmoid(x):
    return 0.5 * (jnp.tanh(0.5 * x) + 1.0)


def _silu(x):
    h = 0.5 * x
    return h + h * jnp.tanh(h)


def _rms_rows(x, w):
    return x * lax.rsqrt(jnp.mean(x * x, axis=-1, keepdims=True) + RMS_EPS) * w


def _cparams(*sem):
    return pltpu.CompilerParams(dimension_semantics=sem, vmem_limit_bytes=VMEM_LIMIT)


def _rope_partner(x):
    half = AT_DH // 2
    lane = lax.broadcasted_iota(jnp.int32, x.shape, 1)
    return jnp.where((lane & half) == 0, pltpu.roll(x, LANES - half, 1), pltpu.roll(x, half, 1))


AT_FOLD = AT_DILATIONS[1]
assert AT_DILATIONS == (1, AT_FOLD, AT_FOLD * AT_FOLD)
AT_Q_DILS = AT_DILATIONS[1:]
AT_KV_DILS = AT_DILATIONS
AT_GROUPS = 3 * AT_WIDTH // LANES


def _even_in_kernel(x_ref, nw_ref, w_ref, cos_ref, sin_ref, *refs):
    ps_ref, ga_ref, gb_ref = refs[:3]
    nq, nkv = len(AT_Q_DILS), len(AT_KV_DILS)
    q_refs = refs[3:3 + nq]
    k_refs = refs[3 + nq:3 + nq + nkv]
    v_refs = refs[3 + nq + nkv:3 + nq + 2 * nkv]
    stage = refs[3 + nq + 2 * nkv:]
    tmp_refs, fold_refs = stage[:AT_GROUPS], stage[AT_GROUPS:]
    tm = x_ref.shape[0]
    xn = _rms_rows(x_ref[...], nw_ref[...]).astype(bf16)
    cos = cos_ref[...]
    sin = sin_ref[...]

    def plain(ref, off):
        def put(t, _):
            ref[:, off:off + LANES] = t
        return put

    def spread(outs, off, rotary, mul=None):
        def put(t, slot):
            if rotary:
                t = t * cos + _rope_partner(t) * sin
            if mul is not None:
                t = t * mul
            nat, fold = tmp_refs[slot], fold_refs[slot % len(fold_refs)]
            nat[...] = t
            per = tm // AT_FOLD
            by_dil = dict((dil, ref) for ref, dil in outs)
            if 1 in by_dil:
                by_dil[1][0, :, off:off + LANES] = t.astype(bf16)
            for r in range(AT_FOLD):
                x = nat[pl.ds(r, per, stride=AT_FOLD), :]
                by_dil[AT_FOLD][r, :, off:off + LANES] = x.astype(bf16)
                fold[r * per:(r + 1) * per, :] = x
            wide = AT_FOLD * AT_FOLD
            for r in range(wide):
                rows = pl.ds((r % AT_FOLD) * per + r // AT_FOLD, tm // wide, stride=AT_FOLD)
                by_dil[wide][r, :, off:off + LANES] = fold[rows, :].astype(bf16)
        return put

    cols = range(0, AT_WIDTH, LANES)
    dests = ([plain(ps_ref, c) for c in range(0, EVEN_SHIFT, LANES)]
             + [plain(ga_ref, c) for c in range(0, RW_WIDTH, LANES)]
             + [spread(list(zip(q_refs, AT_Q_DILS)), c, True, AT_DH ** -0.5) for c in cols]
             + [spread(list(zip(k_refs, AT_KV_DILS)), c, True) for c in cols]
             + [spread(list(zip(v_refs, AT_KV_DILS)), c, False) for c in cols]
             + [plain(gb_ref, c) for c in cols])
    first_at = (EVEN_SHIFT + RW_WIDTH) // LANES
    tile = 2 * MXU_COLS
    for c0 in range(0, EVEN_COLS, tile):
        c1 = min(c0 + tile, EVEN_COLS)
        acc = _dot(xn, w_ref[:, c0:c1])
        for j in range((c1 - c0) // LANES):
            g = c0 // LANES + j
            dests[g](acc[:, j * LANES:(j + 1) * LANES], (g - first_at) % AT_GROUPS)


def _rope_tables(seq):
    inv = ROPE_THETA ** (-jnp.arange(0, AT_DH, 2, dtype=f32) / AT_DH)
    ang = jnp.arange(seq, dtype=f32)[:, None] * inv[None, :]
    cos, sin = jnp.cos(ang), jnp.sin(ang)
    return (jnp.concatenate([cos, cos, cos, cos], axis=-1),
            jnp.concatenate([-sin, sin, -sin, sin], axis=-1))


def _even_in(x2d, seq, norm_w, w_in_bf16, block_rows=ROW_BLOCK):
    m = x2d.shape[0]
    tm = block_rows
    per_seq = seq // tm
    bsz = m // seq
    cos, sin = _rope_tables(seq)
    row = lambda i: (i, 0)
    full = lambda i: (0, 0)
    tab = lambda i: (i % per_seq, 0)
    res = lambda i: (i // per_seq, 0, i % per_seq, 0)
    dils = AT_Q_DILS + AT_KV_DILS + AT_KV_DILS
    outs = pl.pallas_call(
        _even_in_kernel,
        grid=(m // tm,),
        in_specs=[
            pl.BlockSpec((tm, D_MODEL), row),
            pl.BlockSpec((1, D_MODEL), full),
            pl.BlockSpec((D_MODEL, EVEN_COLS), full),
            pl.BlockSpec((tm, LANES), tab),
            pl.BlockSpec((tm, LANES), tab),
        ],
        out_specs=[
            pl.BlockSpec((tm, EVEN_SHIFT), row),
            pl.BlockSpec((tm, RW_WIDTH), row),
            pl.BlockSpec((tm, AT_WIDTH), row),
        ] + [pl.BlockSpec((None, d, tm // d, AT_WIDTH), res) for d in dils],
        out_shape=[
            jax.ShapeDtypeStruct((m, EVEN_SHIFT), f32),
            jax.ShapeDtypeStruct((m, RW_WIDTH), f32),
            jax.ShapeDtypeStruct((m, AT_WIDTH), f32),
        ] + [jax.ShapeDtypeStruct((bsz, d, seq // d, AT_WIDTH), bf16) for d in dils],
        scratch_shapes=[pltpu.VMEM((tm, LANES), f32)] * (AT_GROUPS + AT_FOLD),
        compiler_params=_cparams("parallel"),
        name="even_in",
    )(x2d, norm_w.reshape(1, D_MODEL), w_in_bf16, cos, sin)
    nq, nkv = len(AT_Q_DILS), len(AT_KV_DILS)
    return (outs[0], outs[1], outs[2], outs[3:3 + nq], outs[3 + nq:3 + nq + nkv],
            outs[3 + nq + nkv:])


ODD_PAD_COLS = 2 * GLA_KEY + 2 * GLA_VAL + LANES


def _mid_kernel(x_ref, ya_ref, yb_ref, wo_ref, nw_ref, wi_ref,
                x1_ref, q_ref, k_ref, v_ref, g_ref, gl_ref):
    x1 = (x_ref[...] + _dot(ya_ref[...], wo_ref[0:RW_WIDTH, :])
          + _dot(yb_ref[...], wo_ref[RW_WIDTH:RW_WIDTH + AT_WIDTH, :]))
    x1_ref[...] = x1
    xn = _rms_rows(x1, nw_ref[...]).astype(bf16)
    c = 0
    for ref, width in ((q_ref, GLA_KEY), (k_ref, GLA_KEY), (v_ref, GLA_VAL), (g_ref, GLA_VAL),
                       (gl_ref, LANES)):
        for c0 in range(0, width, 512):
            c1 = min(c0 + 512, width)
            ref[:, c0:c1] = _dot(xn, wi_ref[:, c + c0:c + c1]).astype(ref.dtype)
        c += width


def _mid(x2d, ya, yb, w_out_bf16, norm_w, w_in_pad_bf16, block_rows=ROW_BLOCK):
    m = x2d.shape[0]
    tm = block_rows
    row = lambda i: (i, 0)
    full = lambda i: (0, 0)
    widths = (D_MODEL, GLA_KEY, GLA_KEY, GLA_VAL, GLA_VAL, LANES)
    return pl.pallas_call(
        _mid_kernel,
        grid=(m // tm,),
        in_specs=[
            pl.BlockSpec((tm, D_MODEL), row),
            pl.BlockSpec((tm, RW_WIDTH), row),
            pl.BlockSpec((tm, AT_WIDTH), row),
            pl.BlockSpec((RW_WIDTH + AT_WIDTH, D_MODEL), full),
            pl.BlockSpec((1, D_MODEL), full),
            pl.BlockSpec((D_MODEL, ODD_PAD_COLS), full),
        ],
        out_specs=[pl.BlockSpec((tm, w), row) for w in widths],
        out_shape=[jax.ShapeDtypeStruct((m, w), bf16 if i == 3 else f32)
                   for i, w in enumerate(widths)],
        compiler_params=_cparams("parallel"),
        name="mid",
    )(x2d, ya, yb, w_out_bf16, norm_w.reshape(1, D_MODEL), w_in_pad_bf16)


def _final_kernel(x_ref, y_ref, wo_ref, nw_ref, o_ref):
    x2 = x_ref[...] + _dot(y_ref[...], wo_ref[...])
    o_ref[...] = _rms_rows(x2, nw_ref[...])


def _final(x2d, yc, w_out_bf16, norm_w, block_rows=ROW_BLOCK):
    m = x2d.shape[0]
    tm = block_rows
    row = lambda i: (i, 0)
    full = lambda i: (0, 0)
    return pl.pallas_call(
        _final_kernel,
        grid=(m // tm,),
        in_specs=[
            pl.BlockSpec((tm, D_MODEL), row),
            pl.BlockSpec((tm, GLA_VAL), row),
            pl.BlockSpec((GLA_VAL, D_MODEL), full),
            pl.BlockSpec((1, D_MODEL), full),
        ],
        out_specs=pl.BlockSpec((tm, D_MODEL), row),
        out_shape=jax.ShapeDtypeStruct((m, D_MODEL), f32),
        compiler_params=_cparams("parallel"),
        name="final",
    )(x2d, yc, w_out_bf16, norm_w.reshape(1, D_MODEL))


def _causal_mask(reverse):
    t = lax.broadcasted_iota(jnp.int32, (CHUNK, CHUNK), 0)
    s = lax.broadcasted_iota(jnp.int32, (CHUNK, CHUNK), 1)
    return s >= t if reverse else s <= t


def _row_to_col(row):
    n = row.shape[1]
    eye = (lax.broadcasted_iota(jnp.int32, (n, n), 0) == lax.broadcasted_iota(jnp.int32, (n, n), 1))
    return jnp.sum(jnp.where(eye, jnp.broadcast_to(row, (n, n)), 0.0), axis=1, keepdims=True)


def _gla_kernel(*refs, reverse, final, block_rows):
    if final:
        (q_ref, k_ref, v_ref, gl_ref, gup_ref, gb_ref, gate_ref, of_ref, nw_ref, o_ref, st_ref) = refs
    else:
        (q_ref, k_ref, v_ref, gl_ref, gup_ref, gb_ref, o_ref, st_ref) = refs
    d = 1 if reverse else 0

    @pl.when(pl.program_id(1) == 0)
    def _():
        st_ref[...] = jnp.zeros_like(st_ref)

    incl = _causal_mask(reverse)
    tri = incl.astype(bf16)
    gup = gup_ref[d]
    gbias = gb_ref[d:d + 1, :]
    nchunks = block_rows // CHUNK
    last = 0 if reverse else CHUNK - 1

    x = _dot(gl_ref[...].astype(bf16), gup) + gbias
    g = (jnp.minimum(x, 0.0) - jnp.log(1.0 + jnp.exp(-jnp.abs(x)))) / GLA_GATE_NORM
    chunk_rows = [slice(c * CHUNK, (c + 1) * CHUNK) for c in range(nchunks)]
    bcum = jnp.concatenate([_cumsum_rows(tri, g[rs]) for rs in chunk_rows], axis=0)
    b_last = [bcum[c * CHUNK + last:c * CHUNK + last + 1, :] for c in range(nchunks)]
    b_end = jnp.concatenate([jnp.broadcast_to(b, (CHUNK, GLA_KEY)) for b in b_last], axis=0)
    q = q_ref[...] * (GLA_HK ** -0.5)
    k = k_ref[...]
    q_dec = (q * jnp.exp(bcum)).astype(bf16)
    k_dec = (k * jnp.exp(-bcum)).astype(bf16)
    k_end = (k * jnp.exp(b_end - bcum)).astype(bf16)

    order = list(reversed(range(nchunks))) if reverse else list(range(nchunks))
    klanes = [slice(h * GLA_HK, (h + 1) * GLA_HK) for h in range(GLA_HEADS)]
    vlanes = [slice(h * GLA_HV, (h + 1) * GLA_HV) for h in range(GLA_HEADS)]
    units = [(c, h) for c in order for h in range(GLA_HEADS)]
    v = [v_ref[chunk_rows[c], vlanes[h]].astype(bf16) for c, h in units]
    att = [jnp.where(incl, _dot_nt(q_dec[chunk_rows[c], klanes[h]], k_dec[chunk_rows[c], klanes[h]]),
                     0.0).astype(bf16) for c, h in units]
    o_intra = [_dot(att[i], v[i]) for i in range(len(units))]
    kv = [_dot_tn(k_end[chunk_rows[c], klanes[h]], v[i]) for i, (c, h) in enumerate(units)]
    dcol = [_row_to_col(jnp.exp(b_last[c][:, klanes[h]])) for c, h in units]

    state = [st_ref[h] for h in range(GLA_HEADS)]
    for i, (c, h) in enumerate(units):
        rows = chunk_rows[c]
        o = o_intra[i] + _dot(q_dec[rows, klanes[h]], state[h].astype(bf16))
        state[h] = state[h] * dcol[i] + kv[i]
        if final:
            o = o + of_ref[rows, vlanes[h]]
            o = o * lax.rsqrt(jnp.mean(o * o, axis=-1, keepdims=True) + RMS_EPS) * nw_ref[...]
            o_ref[rows, vlanes[h]] = (o * _silu(gate_ref[rows, vlanes[h]])).astype(o_ref.dtype)
        else:
            o_ref[rows, vlanes[h]] = o
    for h in range(GLA_HEADS):
        st_ref[h] = state[h]


def _gla_scan(q, k, v, gl, gate_up_pad, gate_bias, *, reverse, gate=None, o_fwd=None, norm_w=None,
              block_rows=ROW_BLOCK):
    bsz, seq, _ = q.shape
    tb = block_rows
    nb = seq // tb
    final = reverse
    blk = (lambda b, j: (b, nb - 1 - j, 0)) if reverse else (lambda b, j: (b, j, 0))
    full2 = lambda b, j: (0, 0)
    full3 = lambda b, j: (0, 0, 0)
    in_specs = [
        pl.BlockSpec((None, tb, GLA_KEY), blk),
        pl.BlockSpec((None, tb, GLA_KEY), blk),
        pl.BlockSpec((None, tb, GLA_VAL), blk),
        pl.BlockSpec((None, tb, LANES), blk),
        pl.BlockSpec((2, LANES, GLA_KEY), full3),
        pl.BlockSpec((2, GLA_KEY), full2),
    ]
    args = [q, k, v, gl, gate_up_pad.astype(bf16), gate_bias]
    if final:
        in_specs += [
            pl.BlockSpec((None, tb, GLA_VAL), blk),
            pl.BlockSpec((None, tb, GLA_VAL), blk),
            pl.BlockSpec((1, GLA_HV), full2),
        ]
        args += [gate, o_fwd, norm_w.reshape(1, GLA_HV)]
    return pl.pallas_call(
        functools.partial(_gla_kernel, reverse=reverse, final=final, block_rows=tb),
        grid=(bsz, nb),
        in_specs=in_specs,
        out_specs=pl.BlockSpec((None, tb, GLA_VAL), blk),
        out_shape=jax.ShapeDtypeStruct((bsz, seq, GLA_VAL), bf16 if final else f32),
        scratch_shapes=[pltpu.VMEM((GLA_HEADS, GLA_HK, GLA_HV), f32)],
        compiler_params=_cparams("parallel", "arbitrary"),
        name="gla_bwd" if reverse else "gla_fwd",
    )(*args)


AT_NEG = -1e30
AT_QBLK = 128
AT_KBLK = 256
AT_UNROLL = 8


def _attn_kernel(q4, q16, k1, k4, k16, v1, v4, v16, g_ref, o_ref, m_s, l_s, acc_s, bias_s, tmp_s,
                 *, seq):
    lane = lax.broadcasted_iota(jnp.int32, (1, LANES), 1)
    left = lane < AT_DH
    fold_len = seq // AT_FOLD

    def fill_bias(offs, tq, nk, deltas):
        for d, delta in enumerate(deltas):
            bias_s[d, 0:tq, 0:nk] = jnp.where(jnp.abs(offs + delta) <= AT_SIDE, 0.0, AT_NEG)

    def run_blocks(nblocks, load_unit):
        def body(it, carry):
            units = [load_unit(it * AT_UNROLL + u) for u in range(AT_UNROLL)]
            heads = [(u, first) for u in range(AT_UNROLL) for first in (True, False)]
            scores = []
            for u, first in heads:
                qb, kb, _, bias, _, _ = units[u]
                qh = jnp.where(left if first else ~left, qb, jnp.zeros_like(qb))
                scores.append(_dot_nt(qh, kb) + bias)
            maxes = [jnp.max(s, axis=1, keepdims=True) for s in scores]
            probs = [jnp.exp(s - mh) for s, mh in zip(scores, maxes)]
            sums = [jnp.sum(p, axis=1, keepdims=True) for p in probs]
            outs = [_dot(p.astype(bf16), units[u][2]) for p, (u, _) in zip(probs, heads)]
            olds = [None if units[u][5] else
                    [(m_s[idx, :], l_s[idx, :], acc_s[idx, :]) for idx, _ in units[u][4]]
                    for u in range(AT_UNROLL)]
            for u in range(AT_UNROLL):
                m_new = jnp.where(left, maxes[2 * u], maxes[2 * u + 1])
                l_new = jnp.where(left, sums[2 * u], sums[2 * u + 1])
                a_new = jnp.where(left, outs[2 * u], outs[2 * u + 1])
                for pi, (idx, rows) in enumerate(units[u][4]):
                    mp, lp, ap = m_new[rows], l_new[rows], a_new[rows]
                    if units[u][5]:
                        m_s[idx, :] = mp
                        l_s[idx, :] = lp
                        acc_s[idx, :] = ap
                    else:
                        m_old, l_old, a_old = olds[u][pi]
                        m = jnp.maximum(m_old, mp)
                        w_old = jnp.exp(m_old - m)
                        w_new = jnp.exp(mp - m)
                        m_s[idx, :] = m
                        l_s[idx, :] = l_old * w_old + lp * w_new
                        acc_s[idx, :] = a_old * w_old + ap * w_new
            return carry

        lax.fori_loop(0, nblocks // AT_UNROLL, body, 0)

    def window(m0, sub, tq, nk):
        ks = jnp.clip(m0 - AT_SIDE, 0, sub - nk)
        return ks, (m0 - ks) // AT_SIDE

    tq, nk = AT_QBLK, AT_KBLK
    per = tq // AT_FOLD
    qi = lax.broadcasted_iota(jnp.int32, (tq, nk), 0)
    kj = lax.broadcasted_iota(jnp.int32, (tq, nk), 1)
    fill_bias((qi // per) + AT_FOLD * (qi % per) - kj, tq, nk, (0, AT_SIDE, 2 * AT_SIDE))

    def unit_d1(i):
        t0 = i * tq
        ks, bi = window(t0, seq, tq, nk)
        qrow = pl.ds(pl.multiple_of(i * per, per), per)
        qb = jnp.concatenate([q4[rho, qrow, :] for rho in range(AT_FOLD)], axis=0)
        krow = pl.ds(pl.multiple_of(ks, AT_SIDE), nk)
        state = [(pl.ds(pl.multiple_of(rho * fold_len + i * per, per), per),
                  slice(rho * per, (rho + 1) * per)) for rho in range(AT_FOLD)]
        return qb, k1[0, krow, :], v1[0, krow, :], bias_s[bi], state, True

    run_blocks(seq // tq, unit_d1)

    sub = fold_len
    fill_bias(qi - kj, tq, nk, (0, AT_SIDE, 2 * AT_SIDE))
    nblk = sub // tq

    def unit_fold(i):
        r = i // nblk
        m0 = (i % nblk) * tq
        ks, bi = window(m0, sub, tq, nk)
        qrow = pl.ds(pl.multiple_of(m0, tq), tq)
        krow = pl.ds(pl.multiple_of(ks, AT_SIDE), nk)
        state = [(pl.ds(pl.multiple_of(r * sub + m0, tq), tq), slice(0, tq))]
        return q4[r, qrow, :], k4[r, krow, :], v4[r, krow, :], bias_s[bi], state, False

    run_blocks(AT_FOLD * nblk, unit_fold)

    dil = AT_DILATIONS[-1]
    sub = seq // dil
    tq, nk = min(AT_QBLK, sub), min(AT_KBLK, sub)
    fill_bias((qi - kj)[0:tq, 0:nk], tq, nk, (0, AT_SIDE, nk - tq))
    nblk = sub // tq

    def unit_wide(i):
        r = i // nblk
        m0 = (i % nblk) * tq
        ks, bi = window(m0, sub, tq, nk)
        qrow = pl.ds(pl.multiple_of(m0, AT_SIDE), tq)
        krow = pl.ds(pl.multiple_of(ks, AT_SIDE), nk)
        srow = (r % AT_FOLD) * fold_len + r // AT_FOLD + (dil // AT_FOLD) * m0
        state = [(pl.ds(srow, tq, stride=dil // AT_FOLD), slice(0, tq))]
        return (q16[r, qrow, :], k16[r, krow, :], v16[r, krow, :], bias_s[bi, 0:tq, 0:nk], state,
                False)

    run_blocks(dil * nblk, unit_wide)

    rows = tmp_s.shape[0]
    per_fold = rows // AT_FOLD

    def finish(i, carry):
        for rho in range(AT_FOLD):
            idx = pl.ds(pl.multiple_of(rho * fold_len + i * per_fold, per_fold), per_fold)
            tmp_s[pl.ds(rho, per_fold, stride=AT_FOLD), :] = acc_s[idx, :] / l_s[idx, :]
        out = pl.ds(pl.multiple_of(i * rows, rows), rows)
        o_ref[out, :] = (tmp_s[...] * _silu(g_ref[out, :])).astype(o_ref.dtype)
        return carry

    lax.fori_loop(0, seq // rows, finish, 0)


def _attention(qs, ks, vs, gate):
    bsz, seq, _ = gate.shape
    pairs = AT_WIDTH // LANES
    once = pl.Buffered(1)
    col = lambda b, p: (b, 0, p)
    res = lambda b, p: (b, 0, 0, p)
    copies = list(qs) + list(ks) + list(vs)
    return pl.pallas_call(
        functools.partial(_attn_kernel, seq=seq),
        grid=(bsz, pairs),
        in_specs=[pl.BlockSpec((None, t.shape[1], t.shape[2], LANES), res) for t in copies]
        + [pl.BlockSpec((None, seq, LANES), col, pipeline_mode=once)],
        out_specs=pl.BlockSpec((None, seq, LANES), col),
        out_shape=jax.ShapeDtypeStruct((bsz, seq, AT_WIDTH), bf16),
        scratch_shapes=([pltpu.VMEM((seq, LANES), f32)] * 3
                        + [pltpu.VMEM((3, AT_QBLK, AT_KBLK), f32),
                           pltpu.VMEM((min(seq, 512), LANES), f32)]),
        compiler_params=_cparams("parallel", "parallel"),
        name="dilated_attn",
    )(*copies, gate)


RW_GROUP = 2
RW_GW = RW_GROUP * RW_DH
RW_NGROUPS = RW_HEADS // RW_GROUP
RW_WAVE = 4


def _head_stack(x, masks):
    return jnp.concatenate([jnp.where(mk, x, jnp.zeros_like(x)) for mk in masks], axis=0)


def _head_stack_t(x, row_masks):
    xt = x.T.astype(bf16)
    return jnp.concatenate([jnp.where(mk, xt, jnp.zeros_like(xt)) for mk in row_masks], axis=1)


def _unit_lower_inverse_stages(w, src, dst):
    steps = int(math.log2(CHUNK)) - 1

    def start():
        n = w[src][0].shape[0]
        eye = (lax.broadcasted_iota(jnp.int32, (n, n), 0)
               == lax.broadcasted_iota(jnp.int32, (n, n), 1)).astype(f32)
        w["_t"] = [eye + a for a in w[src]]
        qs = [a.astype(bf16) for a in w[src]]
        w["_q"] = [_dot(q, q).astype(bf16) for q in qs]

    def double():
        n = w[src][0].shape[0]
        both = [_dot(jnp.concatenate([q, t.astype(bf16)], axis=0), q)
                for q, t in zip(w["_q"], w["_t"])]
        w["_q"] = [x[:n].astype(bf16) for x in both]
        w["_t"] = [t + x[n:] for t, x in zip(w["_t"], both)]

    def finish():
        w[dst] = [(t + _dot(t.astype(bf16), q)).astype(bf16) for q, t in zip(w["_q"], w["_t"])]

    return [start] + [double] * (steps - 1) + [finish]


def _rwkv_kernel(*refs, reverse, final, block_rows, nblocks):
    (ps_ref, prev_ref, next_ref, mup_ref, mun_ref, w0_ref, wup_ref, a0_ref, aup_ref,
     kk_ref, ka_ref, seg_ref) = refs[:12]
    if final:
        rk_ref, gnw_ref, gnb_ref, ga_ref, yf_ref, o_ref, sh_ref, st_ref = refs[12:]
    else:
        o_ref, sh_ref, st_ref = refs[12:]
    d = 1 if reverse else 0
    tb = block_rows
    j = pl.program_id(1)
    blk = (nblocks - 1 - j) if reverse else j

    @pl.when(j == 0)
    def _():
        st_ref[...] = jnp.zeros_like(st_ref)

    rid = lax.broadcasted_iota(jnp.int32, (tb, 1), 0)
    has_prev = (blk > 0).astype(f32)
    has_next = (blk < nblocks - 1).astype(f32)
    for c0 in range(0, EVEN_SHIFT, 256):
        c1 = min(c0 + 256, EVEN_SHIFT)
        x = ps_ref[:, c0:c1]
        before = prev_ref[SUBLANES - 1:SUBLANES, c0:c1] * has_prev
        after = next_ref[0:1, c0:c1] * has_next
        prv = jnp.where(rid == 0, before, pltpu.roll(x, 1, 0))
        nxt = jnp.where(rid == tb - 1, after, pltpu.roll(x, tb - 1, 0))
        sh_ref[:, c0:c1] = x + mup_ref[:, c0:c1] * (prv - x) + mun_ref[:, c0:c1] * (nxt - x)

    tri = _causal_mask(reverse).astype(bf16)
    gi = lax.broadcasted_iota(jnp.int32, (2 * RW_GW, 2 * RW_GW), 0)
    gj = lax.broadcasted_iota(jnp.int32, (2 * RW_GW, 2 * RW_GW), 1)
    ahead = (gi % CHUNK) - (gj % CHUNK)
    ahead = -ahead if reverse else ahead
    keep = ahead < jnp.where(gj < RW_GW, 0, 1)
    lane = lax.broadcasted_iota(jnp.int32, (1, RW_GW), 1)
    masks = [(lane // RW_DH) == h for h in range(RW_GROUP)]
    rowi = lax.broadcasted_iota(jnp.int32, (RW_GW, 1), 0)
    row_masks = [(rowi // RW_DH) == h for h in range(RW_GROUP)]
    seg = seg_ref[...]
    last = 0 if reverse else CHUNK - 1
    nchunks = tb // CHUNK
    w_lo, a_lo = 3 * RW_WIDTH, 3 * RW_WIDTH + RW_LORA

    def lr_gate(a_lat, dd):
        return _sigmoid(a0_ref[dd:dd + 1, :] + _dot(a_lat.astype(bf16), aup_ref[dd]))

    r = sh_ref[:, 0:RW_WIDTH]
    k = sh_ref[:, RW_WIDTH:2 * RW_WIDTH]
    v = sh_ref[:, 2 * RW_WIDTH:3 * RW_WIDTH]
    w_lat = sh_ref[:, w_lo:w_lo + RW_LORA]
    a_lat = sh_ref[:, a_lo:a_lo + RW_LORA]
    w_log = -RW_DECAY_SCALE * _sigmoid(
        w0_ref[d:d + 1, :] + _dot(jnp.tanh(w_lat).astype(bf16), wup_ref[d]))
    lr = lr_gate(a_lat, d)
    kk = k * kk_ref[...]
    kk = kk * lax.rsqrt(jnp.maximum(_dot_seg(kk * kk, seg), 1e-24))
    k_dir = k * (1.0 + (lr - 1.0) * ka_ref[...])
    chunk_rows = [slice(c * CHUNK, (c + 1) * CHUNK) for c in range(nchunks)]
    cum = jnp.concatenate([_cumsum_rows(tri, w_log[rs]) for rs in chunk_rows], axis=0)
    grow = jnp.exp(-cum)
    a_t = -kk * jnp.exp(cum - w_log)
    b_t = kk * lr * grow
    k_t = k_dir * grow
    r_t = r * jnp.exp(cum)
    a_b, b_b, k_b, r_b = (t.astype(bf16) for t in (a_t, b_t, k_t, r_t))

    order = list(reversed(range(nchunks))) if reverse else list(range(nchunks))
    lanes = [slice(g * RW_GW, (g + 1) * RW_GW) for g in range(RW_NGROUPS)]
    state = [st_ref[g] for g in range(RW_NGROUPS)]
    y_rows = {}

    def wave_stages(chunks):
        units = [(c, g) for c in chunks for g in range(RW_NGROUPS)]
        n = range(len(units))
        w = {}

        def stacked():
            stack = lambda x: [_head_stack(x[chunk_rows[c], lanes[g]], masks) for c, g in units]
            stack_t = lambda x: [_head_stack_t(x[chunk_rows[c], lanes[g]], row_masks) for c, g in units]
            w["a_s"] = stack(a_b)
            w["r_s"] = stack(r_b)
            w["b_s"] = stack(b_b)
            w["k_s"] = stack(k_b)
            w["a_st"] = stack_t(a_t)
            w["r_st"] = stack_t(r_t)
            w["v_st"] = stack_t(v)

        def grams():
            gram = [jnp.where(keep, _dot_nt(jnp.concatenate([w["b_s"][i], w["k_s"][i]], axis=0),
                                             jnp.concatenate([w["a_s"][i], w["r_s"][i]], axis=0)),
                              0.0) for i in n]
            w["ab_t"] = [gram[i][0:RW_GW, 0:RW_GW] for i in n]
            w["ak_rk_t"] = [gram[i][RW_GW:, :].astype(bf16) for i in n]
            w["rb_t"] = [gram[i][0:RW_GW, RW_GW:].astype(bf16) for i in n]

        inverse = _unit_lower_inverse_stages(w, "ab_t", "t_inv_t")

        def value_products():
            x = [_dot(w["v_st"][i], w["ak_rk_t"][i]) for i in n]
            w["av_t"] = [t[:, 0:RW_GW].astype(bf16) for t in x]
            w["y_local_t"] = [t[:, RW_GW:] for t in x]
            w["vk"] = [_dot(w["v_st"][i], w["k_s"][i]) for i in n]

        def apply_inverse():
            x = [_dot(jnp.concatenate([w["av_t"][i], w["a_st"][i]], axis=0), w["t_inv_t"][i])
                 for i in n]
            w["z_t"] = [t[0:RW_GW] for t in x]
            w["wm_r_t"] = [jnp.concatenate([x[i][RW_GW:].astype(bf16), w["r_st"][i]], axis=1)
                           for i in n]
            w["rb_b"] = [jnp.concatenate([w["rb_t"][i], w["b_s"][i]], axis=1) for i in n]

        def recur(pos_c, c):
            def new_state():
                p_end = jnp.exp(cum[c * CHUNK + last:c * CHUNK + last + 1, :])
                w["y_t"] = []
                for g in range(RW_NGROUPS):
                    i = pos_c * RW_NGROUPS + g
                    s0 = state[g]
                    x = _dot(s0.astype(bf16), w["wm_r_t"][i])
                    u_t = (x[:, 0:RW_GW] + w["z_t"][i]).astype(bf16)
                    x2 = _dot(u_t, w["rb_b"][i])
                    state[g] = (s0 + x2[:, RW_GW:] + w["vk"][i]) * p_end[:, lanes[g]]
                    w["y_t"].append(x[:, RW_GW:] + x2[:, 0:RW_GW] + w["y_local_t"][i])

            def outputs():
                ys = []
                for g in range(RW_NGROUPS):
                    y = w["y_t"][g].T
                    yg = y[0:CHUNK]
                    for h in range(1, RW_GROUP):
                        yg = yg + y[h * CHUNK:(h + 1) * CHUNK]
                    ys.append(yg)
                y_rows[c] = jnp.concatenate(ys, axis=1)

            return [new_state, outputs]

        independent = [stacked, grams] + inverse + [value_products, apply_inverse]
        recurrence = [f for pos_c, c in enumerate(chunks) for f in recur(pos_c, c)]
        return independent, recurrence

    waves = [order[i:i + RW_WAVE] for i in range(0, nchunks, RW_WAVE)]
    pending = []
    for chunks in waves:
        independent, recurrence = wave_stages(chunks)
        every = max(1, len(independent) // (len(pending) + 1)) if pending else 0
        for si, stage in enumerate(independent):
            stage()
            if pending and (si + 1) % every == 0:
                pending.pop(0)()
        while pending:
            pending.pop(0)()
        pending = recurrence
    while pending:
        pending.pop(0)()
    for g in range(RW_NGROUPS):
        st_ref[g] = state[g]
    y_all = jnp.concatenate([y_rows[c] for c in range(nchunks)], axis=0)

    if not final:
        o_ref[...] = y_all
    else:
        y_all = y_all + yf_ref[...]
        mean = _dot_seg(y_all, seg) * (1.0 / RW_DH)
        cen = y_all - mean
        var = _dot_seg(cen * cen, seg) * (1.0 / RW_DH)
        yn = cen * lax.rsqrt(var + RW_GN_EPS) * gnw_ref[...] + gnb_ref[...]
        k_other = k * (1.0 + (lr_gate(a_lat, 1 - d) - 1.0) * ka_ref[...])
        rk = _dot_seg(r * (k_dir + k_other) * rk_ref[...], seg)
        o_ref[...] = ((yn + rk * v) * _silu(ga_ref[...])).astype(o_ref.dtype)


def _segment_ones():
    h = jnp.arange(LANES) // RW_DH
    return (h[:, None] == h[None, :]).astype(bf16)


def _rwkv_scan(ps, mu_prev, mu_next, w0, w_up, a0, a_up, k_k, k_a, *, reverse,
               r_k=None, gn_w=None, gn_b=None, gate=None, y_fwd=None, block_rows=2 * ROW_BLOCK):
    bsz, seq, _ = ps.shape
    tb = block_rows
    nb = seq // tb
    final = reverse
    halo = tb // SUBLANES
    pos = (lambda j: nb - 1 - j) if reverse else (lambda j: j)
    blk = lambda b, j: (b, pos(j), 0)
    prev = lambda b, j: (b, jnp.maximum(pos(j) * halo - 1, 0), 0)
    nxt = lambda b, j: (b, jnp.minimum((pos(j) + 1) * halo, seq // SUBLANES - 1), 0)
    full2 = lambda b, j: (0, 0)
    full3 = lambda b, j: (0, 0, 0)
    vec = lambda n: pl.BlockSpec((1, n), full2)
    in_specs = [
        pl.BlockSpec((None, tb, EVEN_SHIFT), blk),
        pl.BlockSpec((None, SUBLANES, EVEN_SHIFT), prev),
        pl.BlockSpec((None, SUBLANES, EVEN_SHIFT), nxt),
        vec(EVEN_SHIFT), vec(EVEN_SHIFT),
        pl.BlockSpec((2, RW_WIDTH), full2),
        pl.BlockSpec((2, RW_LORA, RW_WIDTH), full3),
        pl.BlockSpec((2, RW_WIDTH), full2),
        pl.BlockSpec((2, RW_LORA, RW_WIDTH), full3),
        vec(RW_WIDTH), vec(RW_WIDTH),
        pl.BlockSpec((LANES, LANES), full2),
    ]
    row = lambda t: t.reshape(1, -1)
    args = [ps, ps, ps, row(mu_prev), row(mu_next), w0, w_up.astype(bf16), a0, a_up.astype(bf16),
            row(k_k), row(k_a),
            _segment_ones()]
    if final:
        in_specs += [vec(RW_WIDTH), vec(RW_WIDTH), vec(RW_WIDTH),
                     pl.BlockSpec((None, tb, RW_WIDTH), blk),
                     pl.BlockSpec((None, tb, RW_WIDTH), blk)]
        args += [row(r_k), row(gn_w), row(gn_b), gate, y_fwd]
    return pl.pallas_call(
        functools.partial(_rwkv_kernel, reverse=reverse, final=final, block_rows=tb, nblocks=nb),
        grid=(bsz, nb),
        in_specs=in_specs,
        out_specs=pl.BlockSpec((None, tb, RW_WIDTH), blk),
        out_shape=jax.ShapeDtypeStruct((bsz, seq, RW_WIDTH), bf16 if final else f32),
        scratch_shapes=[pltpu.VMEM((tb, EVEN_SHIFT), f32),
                        pltpu.VMEM((RW_NGROUPS, RW_GW, RW_GW), f32)],
        compiler_params=_cparams("parallel", "arbitrary"),
        name="rwkv_bwd" if reverse else "rwkv_fwd",
    )(*args)


def _trunk(x, p):
    bsz, seq, _ = x.shape
    tokens = bsz * seq
    x2d = x.reshape(tokens, D_MODEL)
    seq3 = lambda t: t.reshape(bsz, seq, t.shape[-1])
    flat = lambda t: t.reshape(tokens, t.shape[-1])

    ps, ga, gb, qs, ks, vs = _even_in(x2d, seq, p["even_norm"], p["even_w_in"])
    rw = (seq3(ps), p["mu_prev"], p["mu_next"], p["w0"], p["w_up"], p["a0"], p["a_up"],
          p["k_k"], p["k_a"])
    y_fwd = _rwkv_scan(*rw, reverse=False)
    ya = _rwkv_scan(*rw, reverse=True, r_k=p["r_k"], gn_w=p["gn_w"], gn_b=p["gn_b"],
                    gate=seq3(ga), y_fwd=y_fwd)
    yb = _attention(qs, ks, vs, seq3(gb))

    x1, q, k, v, gate, gate_lat = _mid(x2d, flat(ya), flat(yb), p["even_w_out"], p["odd_norm"],
                                       p["odd_w_in"])
    gla = (seq3(q), seq3(k), seq3(v), seq3(gate_lat), p["gate_up"], p["gate_bias"])
    o_fwd = _gla_scan(*gla, reverse=False)
    yc = _gla_scan(*gla, reverse=True, gate=seq3(gate), o_fwd=o_fwd, norm_w=p["gla_norm"])
    return _final(x1, flat(yc), p["odd_w_out"], p["final_norm"]).reshape(bsz, seq, D_MODEL)


def _prepare(even_norm, even_w_in, even_mu_prev, even_mu_next, rwkv_w0, rwkv_w_up, rwkv_a0,
             rwkv_a_up, rwkv_k_k, rwkv_k_a, rwkv_r_k, rwkv_gn_w, rwkv_gn_b, even_w_out, odd_norm,
             odd_w_in, gla_gate_up, gla_gate_bias, gla_norm, odd_w_out, final_norm):
    wi = odd_w_in[0]
    lat0 = 2 * GLA_KEY + GLA_VAL
    lat = jnp.pad(wi[:, lat0:lat0 + GLA_RANK], ((0, 0), (0, LANES - GLA_RANK)))
    odd_in = jnp.concatenate([wi[:, :lat0], wi[:, lat0 + GLA_RANK:], lat], axis=1)
    return {
        "even_norm": even_norm[0], "even_w_in": even_w_in[0].astype(bf16),
        "mu_prev": even_mu_prev[0], "mu_next": even_mu_next[0],
        "w0": rwkv_w0[0], "w_up": rwkv_w_up[0], "a0": rwkv_a0[0], "a_up": rwkv_a_up[0],
        "k_k": rwkv_k_k[0], "k_a": rwkv_k_a[0], "r_k": rwkv_r_k[0],
        "gn_w": rwkv_gn_w[0], "gn_b": rwkv_gn_b[0],
        "even_w_out": even_w_out[0].astype(bf16),
        "odd_norm": odd_norm[0], "odd_w_in": odd_in.astype(bf16),
        "gate_up": jnp.pad(gla_gate_up[0], ((0, 0), (0, LANES - GLA_RANK), (0, 0))),
        "gate_bias": gla_gate_bias[0], "gla_norm": gla_norm[0],
        "odd_w_out": odd_w_out[0].astype(bf16), "final_norm": final_norm,
    }


def kernel(x_prompt, x_sample, even_norm, even_w_in, even_mu_prev, even_mu_next, rwkv_w0, rwkv_w_up,
           rwkv_a0, rwkv_a_up, rwkv_k_k, rwkv_k_a, rwkv_r_k, rwkv_gn_w, rwkv_gn_b, even_w_out,
           odd_norm, odd_w_in, gla_gate_up, gla_gate_bias, gla_norm, odd_w_out, final_norm):
    p = _prepare(even_norm, even_w_in, even_mu_prev, even_mu_next, rwkv_w0, rwkv_w_up, rwkv_a0,
                 rwkv_a_up, rwkv_k_k, rwkv_k_a, rwkv_r_k, rwkv_gn_w, rwkv_gn_b, even_w_out,
                 odd_norm, odd_w_in, gla_gate_up, gla_gate_bias, gla_norm, odd_w_out, final_norm)
    return (_trunk(x_prompt, p), _trunk(x_sample, p))
```

```python
import functools
import math

import jax
import jax.numpy as jnp
from jax import lax
from jax.experimental import pallas as pl
from jax.experimental.pallas import tpu as pltpu

f32 = jnp.float32
bf16 = jnp.bfloat16

D_MODEL = 1024
RMS_EPS = 1e-6

RW_HEADS = 8
RW_DH = 64
RW_WIDTH = RW_HEADS * RW_DH
RW_LORA = 64
RW_DECAY_SCALE = 0.6065306597126334
RW_GN_EPS = 64e-5
AT_HEADS = 8
AT_DH = 64
AT_WIDTH = AT_HEADS * AT_DH
AT_SIDE = 64
AT_DILATIONS = (1, 4, 16)
ROPE_THETA = 10000.0
EVEN_SHIFT = 3 * RW_WIDTH + 2 * RW_LORA
EVEN_COLS = EVEN_SHIFT + RW_WIDTH + 4 * AT_WIDTH

GLA_HEADS = 4
GLA_KEY = 512
GLA_VAL = 1024
GLA_HK = GLA_KEY // GLA_HEADS
GLA_HV = GLA_VAL // GLA_HEADS
GLA_RANK = 16
GLA_GATE_NORM = 16.0

CHUNK = 64
LANES = 128
SUBLANES = 8
MXU_COLS = 256
ROW_BLOCK = 512
VMEM_LIMIT = 56 * 1024 * 1024

_NT = (((1,), (1,)), ((), ()))
_TN = (((0,), (0,)), ((), ()))


def _dot(a, b):
    return jnp.dot(a, b, preferred_element_type=f32)


def _dot_nt(a, b):
    return lax.dot_general(a, b, _NT, preferred_element_type=f32)


def _dot_tn(a, b):
    return lax.dot_general(a, b, _TN, preferred_element_type=f32)


def _split2(x):
    hi = x.astype(bf16)
    lo = (x - hi.astype(f32)).astype(bf16)
    return hi, lo


def _dot_seg(x, e):
    xb = x.astype(bf16)
    return jnp.concatenate([_dot(xb[:, c:c + LANES], e) for c in range(0, x.shape[1], LANES)], axis=1)


def _cumsum_rows(tri, x):
    hi, lo = _split2(x)
    return _dot(tri, hi) + _dot(tri, lo)


def _sigmoid(x):
    return 0.5 * (jnp.tanh(0.5 * x) + 1.0)


def _silu(x):
    h = 0.5 * x
    return h + h * jnp.tanh(h)


def _rms_rows(x, w):
    return x * lax.rsqrt(jnp.mean(x * x, axis=-1, keepdims=True) + RMS_EPS) * w


def _cparams(*sem):
    return pltpu.CompilerParams(dimension_semantics=sem, vmem_limit_bytes=VMEM_LIMIT)


def _rope_partner(x):
    half = AT_DH // 2
    lane = lax.broadcasted_iota(jnp.int32, x.shape, 1)
    return jnp.where((lane & half) == 0, pltpu.roll(x, LANES - half, 1), pltpu.roll(x, half, 1))


AT_FOLD = AT_DILATIONS[1]
assert AT_DILATIONS == (1, AT_FOLD, AT_FOLD * AT_FOLD)
AT_Q_DILS = AT_DILATIONS[1:]
AT_KV_DILS = AT_DILATIONS
AT_GROUPS = 3 * AT_WIDTH // LANES


def _even_in_kernel(x_ref, nw_ref, w_ref, cos_ref, sin_ref, *refs):
    ps_ref, ga_ref, gb_ref = refs[:3]
    nq, nkv = len(AT_Q_DILS), len(AT_KV_DILS)
    q_refs = refs[3:3 + nq]
    k_refs = refs[3 + nq:3 + nq + nkv]
    v_refs = refs[3 + nq + nkv:3 + nq + 2 * nkv]
    stage = refs[3 + nq + 2 * nkv:]
    tmp_refs, fold_refs = stage[:AT_GROUPS], stage[AT_GROUPS:]
    tm = x_ref.shape[0]
    xn = _rms_rows(x_ref[...], nw_ref[...]).astype(bf16)
    cos = cos_ref[...]
    sin = sin_ref[...]

    def plain(ref, off):
        def put(t, _):
            ref[:, off:off + LANES] = t
        return put

    def spread(outs, off, rotary, mul=None):
        def put(t, slot):
            if rotary:
                t = t * cos + _rope_partner(t) * sin
            if mul is not None:
                t = t * mul
            nat, fold = tmp_refs[slot], fold_refs[slot % len(fold_refs)]
            nat[...] = t
            per = tm // AT_FOLD
            by_dil = dict((dil, ref) for ref, dil in outs)
            if 1 in by_dil:
                by_dil[1][0, :, off:off + LANES] = t.astype(bf16)
            for r in range(AT_FOLD):
                x = nat[pl.ds(r, per, stride=AT_FOLD), :]
                by_dil[AT_FOLD][r, :, off:off + LANES] = x.astype(bf16)
                fold[r * per:(r + 1) * per, :] = x
            wide = AT_FOLD * AT_FOLD
            for r in range(wide):
                rows = pl.ds((r % AT_FOLD) * per + r // AT_FOLD, tm // wide, stride=AT_FOLD)
                by_dil[wide][r, :, off:off + LANES] = fold[rows, :].astype(bf16)
        return put

    cols = range(0, AT_WIDTH, LANES)
    dests = ([plain(ps_ref, c) for c in range(0, EVEN_SHIFT, LANES)]
             + [plain(ga_ref, c) for c in range(0, RW_WIDTH, LANES)]
             + [spread(list(zip(q_refs, AT_Q_DILS)), c, True, AT_DH ** -0.5) for c in cols]
             + [spread(list(zip(k_refs, AT_KV_DILS)), c, True) for c in cols]
             + [spread(list(zip(v_refs, AT_KV_DILS)), c, False) for c in cols]
             + [plain(gb_ref, c) for c in cols])
    first_at = (EVEN_SHIFT + RW_WIDTH) // LANES
    tile = 2 * MXU_COLS
    for c0 in range(0, EVEN_COLS, tile):
        c1 = min(c0 + tile, EVEN_COLS)
        acc = _dot(xn, w_ref[:, c0:c1])
        for j in range((c1 - c0) // LANES):
            g = c0 // LANES + j
            dests[g](acc[:, j * LANES:(j + 1) * LANES], (g - first_at) % AT_GROUPS)


def _rope_tables(seq):
    inv = ROPE_THETA ** (-jnp.arange(0, AT_DH, 2, dtype=f32) / AT_DH)
    ang = jnp.arange(seq, dtype=f32)[:, None] * inv[None, :]
    cos, sin = jnp.cos(ang), jnp.sin(ang)
    return (jnp.concatenate([cos, cos, cos, cos], axis=-1),
            jnp.concatenate([-sin, sin, -sin, sin], axis=-1))


def _even_in(x2d, seq, norm_w, w_in_bf16, block_rows=ROW_BLOCK):
    m = x2d.shape[0]
    tm = block_rows
    per_seq = seq // tm
    bsz = m // seq
    cos, sin = _rope_tables(seq)
    row = lambda i: (i, 0)
    full = lambda i: (0, 0)
    tab = lambda i: (i % per_seq, 0)
    res = lambda i: (i // per_seq, 0, i % per_seq, 0)
    dils = AT_Q_DILS + AT_KV_DILS + AT_KV_DILS
    outs = pl.pallas_call(
        _even_in_kernel,
        grid=(m // tm,),
        in_specs=[
            pl.BlockSpec((tm, D_MODEL), row),
            pl.BlockSpec((1, D_MODEL), full),
            pl.BlockSpec((D_MODEL, EVEN_COLS), full),
            pl.BlockSpec((tm, LANES), tab),
            pl.BlockSpec((tm, LANES), tab),
        ],
        out_specs=[
            pl.BlockSpec((tm, EVEN_SHIFT), row),
            pl.BlockSpec((tm, RW_WIDTH), row),
            pl.BlockSpec((tm, AT_WIDTH), row),
        ] + [pl.BlockSpec((None, d, tm // d, AT_WIDTH), res) for d in dils],
        out_shape=[
            jax.ShapeDtypeStruct((m, EVEN_SHIFT), f32),
            jax.ShapeDtypeStruct((m, RW_WIDTH), f32),
            jax.ShapeDtypeStruct((m, AT_WIDTH), f32),
        ] + [jax.ShapeDtypeStruct((bsz, d, seq // d, AT_WIDTH), bf16) for d in dils],
        scratch_shapes=[pltpu.VMEM((tm, LANES), f32)] * (AT_GROUPS + AT_FOLD),
        compiler_params=_cparams("parallel"),
        name="even_in",
    )(x2d, norm_w.reshape(1, D_MODEL), w_in_bf16, cos, sin)
    nq, nkv = len(AT_Q_DILS), len(AT_KV_DILS)
    return (outs[0], outs[1], outs[2], outs[3:3 + nq], outs[3 + nq:3 + nq + nkv],
            outs[3 + nq + nkv:])


ODD_PAD_COLS = 2 * GLA_KEY + 2 * GLA_VAL + LANES


def _mid_kernel(x_ref, ya_ref, yb_ref, wo_ref, nw_ref, wi_ref,
                x1_ref, q_ref, k_ref, v_ref, g_ref, gl_ref):
    x1 = (x_ref[...] + _dot(ya_ref[...], wo_ref[0:RW_WIDTH, :])
          + _dot(yb_ref[...], wo_ref[RW_WIDTH:RW_WIDTH + AT_WIDTH, :]))
    x1_ref[...] = x1
    xn = _rms_rows(x1, nw_ref[...]).astype(bf16)
    c = 0
    for ref, width in ((q_ref, GLA_KEY), (k_ref, GLA_KEY), (v_ref, GLA_VAL), (g_ref, GLA_VAL),
                       (gl_ref, LANES)):
        for c0 in range(0, width, 512):
            c1 = min(c0 + 512, width)
            ref[:, c0:c1] = _dot(xn, wi_ref[:, c + c0:c + c1]).astype(ref.dtype)
        c += width


def _mid(x2d, ya, yb, w_out_bf16, norm_w, w_in_pad_bf16, block_rows=ROW_BLOCK):
    m = x2d.shape[0]
    tm = block_rows
    row = lambda i: (i, 0)
    full = lambda i: (0, 0)
    widths = (D_MODEL, GLA_KEY, GLA_KEY, GLA_VAL, GLA_VAL, LANES)
    return pl.pallas_call(
        _mid_kernel,
        grid=(m // tm,),
        in_specs=[
            pl.BlockSpec((tm, D_MODEL), row),
            pl.BlockSpec((tm, RW_WIDTH), row),
            pl.BlockSpec((tm, AT_WIDTH), row),
            pl.BlockSpec((RW_WIDTH + AT_WIDTH, D_MODEL), full),
            pl.BlockSpec((1, D_MODEL), full),
            pl.BlockSpec((D_MODEL, ODD_PAD_COLS), full),
        ],
        out_specs=[pl.BlockSpec((tm, w), row) for w in widths],
        out_shape=[jax.ShapeDtypeStruct((m, w), bf16 if i == 3 else f32)
                   for i, w in enumerate(widths)],
        compiler_params=_cparams("parallel"),
        name="mid",
    )(x2d, ya, yb, w_out_bf16, norm_w.reshape(1, D_MODEL), w_in_pad_bf16)


def _causal_mask(reverse):
    t = lax.broadcasted_iota(jnp.int32, (CHUNK, CHUNK), 0)
    s = lax.broadcasted_iota(jnp.int32, (CHUNK, CHUNK), 1)
    return s >= t if reverse else s <= t


def _row_to_col(row):
    n = row.shape[1]
    eye = (lax.broadcasted_iota(jnp.int32, (n, n), 0) == lax.broadcasted_iota(jnp.int32, (n, n), 1))
    return jnp.sum(jnp.where(eye, jnp.broadcast_to(row, (n, n)), 0.0), axis=1, keepdims=True)


def _gla_kernel(*refs, reverse, final, block_rows):
    if final:
        (q_ref, k_ref, v_ref, gl_ref, gup_ref, gb_ref, gate_ref, of_ref, nw_ref, x1_ref, wo_ref,
         fnw_ref, o_ref, st_ref, y_ref) = refs
    else:
        (q_ref, k_ref, v_ref, gl_ref, gup_ref, gb_ref, o_ref, st_ref) = refs
    d = 1 if reverse else 0

    @pl.when(pl.program_id(1) == 0)
    def _():
        st_ref[...] = jnp.zeros_like(st_ref)

    incl = _causal_mask(reverse)
    tri = incl.astype(bf16)
    gup = gup_ref[d]
    gbias = gb_ref[d:d + 1, :]
    nchunks = block_rows // CHUNK
    last = 0 if reverse else CHUNK - 1

    x = _dot(gl_ref[...].astype(bf16), gup) + gbias
    g = (jnp.minimum(x, 0.0) - jnp.log(1.0 + jnp.exp(-jnp.abs(x)))) / GLA_GATE_NORM
    chunk_rows = [slice(c * CHUNK, (c + 1) * CHUNK) for c in range(nchunks)]
    bcum = jnp.concatenate([_cumsum_rows(tri, g[rs]) for rs in chunk_rows], axis=0)
    b_last = [bcum[c * CHUNK + last:c * CHUNK + last + 1, :] for c in range(nchunks)]
    b_end = jnp.concatenate([jnp.broadcast_to(b, (CHUNK, GLA_KEY)) for b in b_last], axis=0)
    q = q_ref[...] * (GLA_HK ** -0.5)
    k = k_ref[...]
    q_dec = (q * jnp.exp(bcum)).astype(bf16)
    k_dec = (k * jnp.exp(-bcum)).astype(bf16)
    k_end = (k * jnp.exp(b_end - bcum)).astype(bf16)

    order = list(reversed(range(nchunks))) if reverse else list(range(nchunks))
    klanes = [slice(h * GLA_HK, (h + 1) * GLA_HK) for h in range(GLA_HEADS)]
    vlanes = [slice(h * GLA_HV, (h + 1) * GLA_HV) for h in range(GLA_HEADS)]
    units = [(c, h) for c in order for h in range(GLA_HEADS)]
    v = [v_ref[chunk_rows[c], vlanes[h]].astype(bf16) for c, h in units]
    att = [jnp.where(incl, _dot_nt(q_dec[chunk_rows[c], klanes[h]], k_dec[chunk_rows[c], klanes[h]]),
                     0.0).astype(bf16) for c, h in units]
    o_intra = [_dot(att[i], v[i]) for i in range(len(units))]
    kv = [_dot_tn(k_end[chunk_rows[c], klanes[h]], v[i]) for i, (c, h) in enumerate(units)]
    dcol = [_row_to_col(jnp.exp(b_last[c][:, klanes[h]])) for c, h in units]

    state = [st_ref[h] for h in range(GLA_HEADS)]
    for i, (c, h) in enumerate(units):
        rows = chunk_rows[c]
        o = o_intra[i] + _dot(q_dec[rows, klanes[h]], state[h].astype(bf16))
        state[h] = state[h] * dcol[i] + kv[i]
        if final:
            o = o + of_ref[rows, vlanes[h]]
            o = o * lax.rsqrt(jnp.mean(o * o, axis=-1, keepdims=True) + RMS_EPS) * nw_ref[...]
            y_ref[rows, vlanes[h]] = (o * _silu(gate_ref[rows, vlanes[h]])).astype(y_ref.dtype)
        else:
            o_ref[rows, vlanes[h]] = o
    for h in range(GLA_HEADS):
        st_ref[h] = state[h]
    if final:
        x2 = x1_ref[...] + _dot(y_ref[...], wo_ref[...])
        o_ref[...] = _rms_rows(x2, fnw_ref[...])


def _gla_scan(q, k, v, gl, gate_up_pad, gate_bias, *, reverse, gate=None, o_fwd=None, norm_w=None,
              x1=None, w_out_bf16=None, final_norm=None, block_rows=ROW_BLOCK):
    bsz, seq, _ = q.shape
    tb = block_rows
    nb = seq // tb
    final = reverse
    blk = (lambda b, j: (b, nb - 1 - j, 0)) if reverse else (lambda b, j: (b, j, 0))
    full2 = lambda b, j: (0, 0)
    full3 = lambda b, j: (0, 0, 0)
    in_specs = [
        pl.BlockSpec((None, tb, GLA_KEY), blk),
        pl.BlockSpec((None, tb, GLA_KEY), blk),
        pl.BlockSpec((None, tb, GLA_VAL), blk),
        pl.BlockSpec((None, tb, LANES), blk),
        pl.BlockSpec((2, LANES, GLA_KEY), full3),
        pl.BlockSpec((2, GLA_KEY), full2),
    ]
    args = [q, k, v, gl, gate_up_pad.astype(bf16), gate_bias]
    if final:
        in_specs += [
            pl.BlockSpec((None, tb, GLA_VAL), blk),
            pl.BlockSpec((None, tb, GLA_VAL), blk),
            pl.BlockSpec((1, GLA_HV), full2),
            pl.BlockSpec((None, tb, D_MODEL), blk),
            pl.BlockSpec((GLA_VAL, D_MODEL), full2),
            pl.BlockSpec((1, D_MODEL), full2),
        ]
        args += [gate, o_fwd, norm_w.reshape(1, GLA_HV), x1, w_out_bf16,
                 final_norm.reshape(1, D_MODEL)]
    scratch = [pltpu.VMEM((GLA_HEADS, GLA_HK, GLA_HV), f32)]
    if final:
        scratch.append(pltpu.VMEM((tb, GLA_VAL), bf16))
    return pl.pallas_call(
        functools.partial(_gla_kernel, reverse=reverse, final=final, block_rows=tb),
        grid=(bsz, nb),
        in_specs=in_specs,
        out_specs=pl.BlockSpec((None, tb, D_MODEL if final else GLA_VAL), blk),
        out_shape=jax.ShapeDtypeStruct((bsz, seq, D_MODEL if final else GLA_VAL), f32),
        scratch_shapes=scratch,
        compiler_params=_cparams("parallel", "arbitrary"),
        name="gla_bwd" if reverse else "gla_fwd",
    )(*args)


AT_NEG = -1e30
AT_QBLK = 128
AT_KBLK = 256
AT_UNROLL = 8


def _attn_kernel(q4, q16, k1, k4, k16, v1, v4, v16, g_ref, o_ref, m_s, l_s, acc_s, bias_s, tmp_s,
                 *, seq):
    lane = lax.broadcasted_iota(jnp.int32, (1, LANES), 1)
    left = lane < AT_DH
    fold_len = seq // AT_FOLD

    def fill_bias(offs, tq, nk, deltas):
        for d, delta in enumerate(deltas):
            bias_s[d, 0:tq, 0:nk] = jnp.where(jnp.abs(offs + delta) <= AT_SIDE, 0.0, AT_NEG)

    def run_blocks(nblocks, load_unit):
        def body(it, carry):
            units = [load_unit(it * AT_UNROLL + u) for u in range(AT_UNROLL)]
            heads = [(u, first) for u in range(AT_UNROLL) for first in (True, False)]
            scores = []
            for u, first in heads:
                qb, kb, _, bias, _, _ = units[u]
                qh = jnp.where(left if first else ~left, qb, jnp.zeros_like(qb))
                scores.append(_dot_nt(qh, kb) + bias)
            maxes = [jnp.max(s, axis=1, keepdims=True) for s in scores]
            probs = [jnp.exp(s - mh) for s, mh in zip(scores, maxes)]
            sums = [jnp.sum(p, axis=1, keepdims=True) for p in probs]
            outs = [_dot(p.astype(bf16), units[u][2]) for p, (u, _) in zip(probs, heads)]
            olds = [None if units[u][5] else
                    [(m_s[idx, :], l_s[idx, :], acc_s[idx, :]) for idx, _ in units[u][4]]
                    for u in range(AT_UNROLL)]
            for u in range(AT_UNROLL):
                m_new = jnp.where(left, maxes[2 * u], maxes[2 * u + 1])
                l_new = jnp.where(left, sums[2 * u], sums[2 * u + 1])
                a_new = jnp.where(left, outs[2 * u], outs[2 * u + 1])
                for pi, (idx, rows) in enumerate(units[u][4]):
                    mp, lp, ap = m_new[rows], l_new[rows], a_new[rows]
                    if units[u][5]:
                        m_s[idx, :] = mp
                        l_s[idx, :] = lp
                        acc_s[idx, :] = ap
                    else:
                        m_old, l_old, a_old = olds[u][pi]
                        m = jnp.maximum(m_old, mp)
                        w_old = jnp.exp(m_old - m)
                        w_new = jnp.exp(mp - m)
                        m_s[idx, :] = m
                        l_s[idx, :] = l_old * w_old + lp * w_new
                        acc_s[idx, :] = a_old * w_old + ap * w_new
            return carry

        lax.fori_loop(0, nblocks // AT_UNROLL, body, 0)

    def window(m0, sub, tq, nk):
        ks = jnp.clip(m0 - AT_SIDE, 0, sub - nk)
        return ks, (m0 - ks) // AT_SIDE

    tq, nk = AT_QBLK, AT_KBLK
    per = tq // AT_FOLD
    qi = lax.broadcasted_iota(jnp.int32, (tq, nk), 0)
    kj = lax.broadcasted_iota(jnp.int32, (tq, nk), 1)
    fill_bias((qi // per) + AT_FOLD * (qi % per) - kj, tq, nk, (0, AT_SIDE, 2 * AT_SIDE))

    def unit_d1(i):
        t0 = i * tq
        ks, bi = window(t0, seq, tq, nk)
        qrow = pl.ds(pl.multiple_of(i * per, per), per)
        qb = jnp.concatenate([q4[rho, qrow, :] for rho in range(AT_FOLD)], axis=0)
        krow = pl.ds(pl.multiple_of(ks, AT_SIDE), nk)
        state = [(pl.ds(pl.multiple_of(rho * fold_len + i * per, per), per),
                  slice(rho * per, (rho + 1) * per)) for rho in range(AT_FOLD)]
        return qb, k1[0, krow, :], v1[0, krow, :], bias_s[bi], state, True

    run_blocks(seq // tq, unit_d1)

    sub = fold_len
    fill_bias(qi - kj, tq, nk, (0, AT_SIDE, 2 * AT_SIDE))
    nblk = sub // tq

    def unit_fold(i):
        r = i // nblk
        m0 = (i % nblk) * tq
        ks, bi = window(m0, sub, tq, nk)
        qrow = pl.ds(pl.multiple_of(m0, tq), tq)
        krow = pl.ds(pl.multiple_of(ks, AT_SIDE), nk)
        state = [(pl.ds(pl.multiple_of(r * sub + m0, tq), tq), slice(0, tq))]
        return q4[r, qrow, :], k4[r, krow, :], v4[r, krow, :], bias_s[bi], state, False

    run_blocks(AT_FOLD * nblk, unit_fold)

    dil = AT_DILATIONS[-1]
    sub = seq // dil
    tq, nk = min(AT_QBLK, sub), min(AT_KBLK, sub)
    fill_bias((qi - kj)[0:tq, 0:nk], tq, nk, (0, AT_SIDE, nk - tq))
    nblk = sub // tq

    def unit_wide(i):
        r = i // nblk
        m0 = (i % nblk) * tq
        ks, bi = window(m0, sub, tq, nk)
        qrow = pl.ds(pl.multiple_of(m0, AT_SIDE), tq)
        krow = pl.ds(pl.multiple_of(ks, AT_SIDE), nk)
        srow = (r % AT_FOLD) * fold_len + r // AT_FOLD + (dil // AT_FOLD) * m0
        state = [(pl.ds(srow, tq, stride=dil // AT_FOLD), slice(0, tq))]
        return (q16[r, qrow, :], k16[r, krow, :], v16[r, krow, :], bias_s[bi, 0:tq, 0:nk], state,
                False)

    run_blocks(dil * nblk, unit_wide)

    rows = tmp_s.shape[0]
    per_fold = rows // AT_FOLD

    def finish(i, carry):
        for rho in range(AT_FOLD):
            idx = pl.ds(pl.multiple_of(rho * fold_len + i * per_fold, per_fold), per_fold)
            tmp_s[pl.ds(rho, per_fold, stride=AT_FOLD), :] = acc_s[idx, :] / l_s[idx, :]
        out = pl.ds(pl.multiple_of(i * rows, rows), rows)
        o_ref[out, :] = (tmp_s[...] * _silu(g_ref[out, :])).astype(o_ref.dtype)
        return carry

    lax.fori_loop(0, seq // rows, finish, 0)


def _attention(qs, ks, vs, gate):
    bsz, seq, _ = gate.shape
    pairs = AT_WIDTH // LANES
    once = pl.Buffered(1)
    col = lambda b, p: (b, 0, p)
    res = lambda b, p: (b, 0, 0, p)
    copies = list(qs) + list(ks) + list(vs)
    return pl.pallas_call(
        functools.partial(_attn_kernel, seq=seq),
        grid=(bsz, pairs),
        in_specs=[pl.BlockSpec((None, t.shape[1], t.shape[2], LANES), res) for t in copies]
        + [pl.BlockSpec((None, seq, LANES), col, pipeline_mode=once)],
        out_specs=pl.BlockSpec((None, seq, LANES), col),
        out_shape=jax.ShapeDtypeStruct((bsz, seq, AT_WIDTH), bf16),
        scratch_shapes=([pltpu.VMEM((seq, LANES), f32)] * 3
                        + [pltpu.VMEM((3, AT_QBLK, AT_KBLK), f32),
                           pltpu.VMEM((min(seq, 512), LANES), f32)]),
        compiler_params=_cparams("parallel", "parallel"),
        name="dilated_attn",
    )(*copies, gate)


RW_GROUP = 2
RW_GW = RW_GROUP * RW_DH
RW_NGROUPS = RW_HEADS // RW_GROUP
RW_WAVE = 4


def _head_stack(x, masks):
    return jnp.concatenate([jnp.where(mk, x, jnp.zeros_like(x)) for mk in masks], axis=0)


def _head_stack_t(x, row_masks):
    xt = x.T.astype(bf16)
    return jnp.concatenate([jnp.where(mk, xt, jnp.zeros_like(xt)) for mk in row_masks], axis=1)


def _unit_lower_inverse_stages(w, src, dst):
    steps = int(math.log2(CHUNK)) - 1

    def start():
        n = w[src][0].shape[0]
        eye = (lax.broadcasted_iota(jnp.int32, (n, n), 0)
               == lax.broadcasted_iota(jnp.int32, (n, n), 1)).astype(f32)
        w["_t"] = [eye + a for a in w[src]]
        qs = [a.astype(bf16) for a in w[src]]
        w["_q"] = [_dot(q, q).astype(bf16) for q in qs]

    def double():
        n = w[src][0].shape[0]
        both = [_dot(jnp.concatenate([q, t.astype(bf16)], axis=0), q)
                for q, t in zip(w["_q"], w["_t"])]
        w["_q"] = [x[:n].astype(bf16) for x in both]
        w["_t"] = [t + x[n:] for t, x in zip(w["_t"], both)]

    def finish():
        w[dst] = [(t + _dot(t.astype(bf16), q)).astype(bf16) for q, t in zip(w["_q"], w["_t"])]

    return [start] + [double] * (steps - 1) + [finish]


def _rwkv_kernel(*refs, reverse, final, block_rows, nblocks):
    (ps_ref, prev_ref, next_ref, mup_ref, mun_ref, w0_ref, wup_ref, a0_ref, aup_ref,
     kk_ref, ka_ref, seg_ref) = refs[:12]
    if final:
        rk_ref, gnw_ref, gnb_ref, ga_ref, yf_ref, o_ref, sh_ref, st_ref = refs[12:]
    else:
        o_ref, sh_ref, st_ref = refs[12:]
    d = 1 if reverse else 0
    tb = block_rows
    j = pl.program_id(1)
    blk = (nblocks - 1 - j) if reverse else j

    @pl.when(j == 0)
    def _():
        st_ref[...] = jnp.zeros_like(st_ref)

    rid = lax.broadcasted_iota(jnp.int32, (tb, 1), 0)
    has_prev = (blk > 0).astype(f32)
    has_next = (blk < nblocks - 1).astype(f32)
    for c0 in range(0, EVEN_SHIFT, 256):
        c1 = min(c0 + 256, EVEN_SHIFT)
        x = ps_ref[:, c0:c1]
        before = prev_ref[SUBLANES - 1:SUBLANES, c0:c1] * has_prev
        after = next_ref[0:1, c0:c1] * has_next
        prv = jnp.where(rid == 0, before, pltpu.roll(x, 1, 0))
        nxt = jnp.where(rid == tb - 1, after, pltpu.roll(x, tb - 1, 0))
        sh_ref[:, c0:c1] = x + mup_ref[:, c0:c1] * (prv - x) + mun_ref[:, c0:c1] * (nxt - x)

    tri = _causal_mask(reverse).astype(bf16)
    gi = lax.broadcasted_iota(jnp.int32, (2 * RW_GW, 2 * RW_GW), 0)
    gj = lax.broadcasted_iota(jnp.int32, (2 * RW_GW, 2 * RW_GW), 1)
    ahead = (gi % CHUNK) - (gj % CHUNK)
    ahead = -ahead if reverse else ahead
    keep = ahead < jnp.where(gj < RW_GW, 0, 1)
    lane = lax.broadcasted_iota(jnp.int32, (1, RW_GW), 1)
    masks = [(lane // RW_DH) == h for h in range(RW_GROUP)]
    rowi = lax.broadcasted_iota(jnp.int32, (RW_GW, 1), 0)
    row_masks = [(rowi // RW_DH) == h for h in range(RW_GROUP)]
    seg = seg_ref[...]
    last = 0 if reverse else CHUNK - 1
    nchunks = tb // CHUNK
    w_lo, a_lo = 3 * RW_WIDTH, 3 * RW_WIDTH + RW_LORA

    def lr_gate(a_lat, dd):
        return _sigmoid(a0_ref[dd:dd + 1, :] + _dot(a_lat.astype(bf16), aup_ref[dd]))

    r = sh_ref[:, 0:RW_WIDTH]
    k = sh_ref[:, RW_WIDTH:2 * RW_WIDTH]
    v = sh_ref[:, 2 * RW_WIDTH:3 * RW_WIDTH]
    w_lat = sh_ref[:, w_lo:w_lo + RW_LORA]
    a_lat = sh_ref[:, a_lo:a_lo + RW_LORA]
    w_log = -RW_DECAY_SCALE * _sigmoid(
        w0_ref[d:d + 1, :] + _dot(jnp.tanh(w_lat).astype(bf16), wup_ref[d]))
    lr = lr_gate(a_lat, d)
    kk = k * kk_ref[...]
    kk = kk * lax.rsqrt(jnp.maximum(_dot_seg(kk * kk, seg), 1e-24))
    k_dir = k * (1.0 + (lr - 1.0) * ka_ref[...])
    chunk_rows = [slice(c * CHUNK, (c + 1) * CHUNK) for c in range(nchunks)]
    cum = jnp.concatenate([_cumsum_rows(tri, w_log[rs]) for rs in chunk_rows], axis=0)
    grow = jnp.exp(-cum)
    a_t = -kk * jnp.exp(cum - w_log)
    b_t = kk * lr * grow
    k_t = k_dir * grow
    r_t = r * jnp.exp(cum)
    a_b, b_b, k_b, r_b = (t.astype(bf16) for t in (a_t, b_t, k_t, r_t))

    order = list(reversed(range(nchunks))) if reverse else list(range(nchunks))
    lanes = [slice(g * RW_GW, (g + 1) * RW_GW) for g in range(RW_NGROUPS)]
    state = [st_ref[g] for g in range(RW_NGROUPS)]
    y_rows = {}

    def wave_stages(chunks):
        units = [(c, g) for c in chunks for g in range(RW_NGROUPS)]
        n = range(len(units))
        w = {}

        def stacked():
            stack = lambda x: [_head_stack(x[chunk_rows[c], lanes[g]], masks) for c, g in units]
            stack_t = lambda x: [_head_stack_t(x[chunk_rows[c], lanes[g]], row_masks) for c, g in units]
            w["a_s"] = stack(a_b)
            w["r_s"] = stack(r_b)
            w["b_s"] = stack(b_b)
            w["k_s"] = stack(k_b)
            w["a_st"] = stack_t(a_t)
            w["r_st"] = stack_t(r_t)
            w["v_st"] = stack_t(v)

        def grams():
            gram = [jnp.where(keep, _dot_nt(jnp.concatenate([w["b_s"][i], w["k_s"][i]], axis=0),
                                             jnp.concatenate([w["a_s"][i], w["r_s"][i]], axis=0)),
                              0.0) for i in n]
            w["ab_t"] = [gram[i][0:RW_GW, 0:RW_GW] for i in n]
            w["ak_rk_t"] = [gram[i][RW_GW:, :].astype(bf16) for i in n]
            w["rb_t"] = [gram[i][0:RW_GW, RW_GW:].astype(bf16) for i in n]

        inverse = _unit_lower_inverse_stages(w, "ab_t", "t_inv_t")

        def value_products():
            x = [_dot(w["v_st"][i], w["ak_rk_t"][i]) for i in n]
            w["av_t"] = [t[:, 0:RW_GW].astype(bf16) for t in x]
            w["y_local_t"] = [t[:, RW_GW:] for t in x]
            w["vk"] = [_dot(w["v_st"][i], w["k_s"][i]) for i in n]

        def apply_inverse():
            x = [_dot(jnp.concatenate([w["av_t"][i], w["a_st"][i]], axis=0), w["t_inv_t"][i])
                 for i in n]
            w["z_t"] = [t[0:RW_GW] for t in x]
            w["wm_r_t"] = [jnp.concatenate([x[i][RW_GW:].astype(bf16), w["r_st"][i]], axis=1)
                           for i in n]
            w["rb_b"] = [jnp.concatenate([w["rb_t"][i], w["b_s"][i]], axis=1) for i in n]

        def recur(pos_c, c):
            def new_state():
                p_end = jnp.exp(cum[c * CHUNK + last:c * CHUNK + last + 1, :])
                w["y_t"] = []
                for g in range(RW_NGROUPS):
                    i = pos_c * RW_NGROUPS + g
                    s0 = state[g]
                    x = _dot(s0.astype(bf16), w["wm_r_t"][i])
                    u_t = (x[:, 0:RW_GW] + w["z_t"][i]).astype(bf16)
                    x2 = _dot(u_t, w["rb_b"][i])
                    state[g] = (s0 + x2[:, RW_GW:] + w["vk"][i]) * p_end[:, lanes[g]]
                    w["y_t"].append(x[:, RW_GW:] + x2[:, 0:RW_GW] + w["y_local_t"][i])

            def outputs():
                ys = []
                for g in range(RW_NGROUPS):
                    y = w["y_t"][g].T
                    yg = y[0:CHUNK]
                    for h in range(1, RW_GROUP):
                        yg = yg + y[h * CHUNK:(h + 1) * CHUNK]
                    ys.append(yg)
                y_rows[c] = jnp.concatenate(ys, axis=1)

            return [new_state, outputs]

        independent = [stacked, grams] + inverse + [value_products, apply_inverse]
        recurrence = [f for pos_c, c in enumerate(chunks) for f in recur(pos_c, c)]
        return independent, recurrence

    waves = [order[i:i + RW_WAVE] for i in range(0, nchunks, RW_WAVE)]
    pending = []
    for chunks in waves:
        independent, recurrence = wave_stages(chunks)
        every = max(1, len(independent) // (len(pending) + 1)) if pending else 0
        for si, stage in enumerate(independent):
            stage()
            if pending and (si + 1) % every == 0:
                pending.pop(0)()
        while pending:
            pending.pop(0)()
        pending = recurrence
    while pending:
        pending.pop(0)()
    for g in range(RW_NGROUPS):
        st_ref[g] = state[g]
    y_all = jnp.concatenate([y_rows[c] for c in range(nchunks)], axis=0)

    if not final:
        o_ref[...] = y_all
    else:
        y_all = y_all + yf_ref[...]
        mean = _dot_seg(y_all, seg) * (1.0 / RW_DH)
        cen = y_all - mean
        var = _dot_seg(cen * cen, seg) * (1.0 / RW_DH)
        yn = cen * lax.rsqrt(var + RW_GN_EPS) * gnw_ref[...] + gnb_ref[...]
        k_other = k * (1.0 + (lr_gate(a_lat, 1 - d) - 1.0) * ka_ref[...])
        rk = _dot_seg(r * (k_dir + k_other) * rk_ref[...], seg)
        o_ref[...] = ((yn + rk * v) * _silu(ga_ref[...])).astype(o_ref.dtype)


def _segment_ones():
    h = jnp.arange(LANES) // RW_DH
    return (h[:, None] == h[None, :]).astype(bf16)


def _rwkv_scan(ps, mu_prev, mu_next, w0, w_up, a0, a_up, k_k, k_a, *, reverse,
               r_k=None, gn_w=None, gn_b=None, gate=None, y_fwd=None, block_rows=2 * ROW_BLOCK):
    bsz, seq, _ = ps.shape
    tb = block_rows
    nb = seq // tb
    final = reverse
    halo = tb // SUBLANES
    pos = (lambda j: nb - 1 - j) if reverse else (lambda j: j)
    blk = lambda b, j: (b, pos(j), 0)
    prev = lambda b, j: (b, jnp.maximum(pos(j) * halo - 1, 0), 0)
    nxt = lambda b, j: (b, jnp.minimum((pos(j) + 1) * halo, seq // SUBLANES - 1), 0)
    full2 = lambda b, j: (0, 0)
    full3 = lambda b, j: (0, 0, 0)
    vec = lambda n: pl.BlockSpec((1, n), full2)
    in_specs = [
        pl.BlockSpec((None, tb, EVEN_SHIFT), blk),
        pl.BlockSpec((None, SUBLANES, EVEN_SHIFT), prev),
        pl.BlockSpec((None, SUBLANES, EVEN_SHIFT), nxt),
        vec(EVEN_SHIFT), vec(EVEN_SHIFT),
        pl.BlockSpec((2, RW_WIDTH), full2),
        pl.BlockSpec((2, RW_LORA, RW_WIDTH), full3),
        pl.BlockSpec((2, RW_WIDTH), full2),
        pl.BlockSpec((2, RW_LORA, RW_WIDTH), full3),
        vec(RW_WIDTH), vec(RW_WIDTH),
        pl.BlockSpec((LANES, LANES), full2),
    ]
    row = lambda t: t.reshape(1, -1)
    args = [ps, ps, ps, row(mu_prev), row(mu_next), w0, w_up.astype(bf16), a0, a_up.astype(bf16),
            row(k_k), row(k_a),
            _segment_ones()]
    if final:
        in_specs += [vec(RW_WIDTH), vec(RW_WIDTH), vec(RW_WIDTH),
                     pl.BlockSpec((None, tb, RW_WIDTH), blk),
                     pl.BlockSpec((None, tb, RW_WIDTH), blk)]
        args += [row(r_k), row(gn_w), row(gn_b), gate, y_fwd]
    return pl.pallas_call(
        functools.partial(_rwkv_kernel, reverse=reverse, final=final, block_rows=tb, nblocks=nb),
        grid=(bsz, nb),
        in_specs=in_specs,
        out_specs=pl.BlockSpec((None, tb, RW_WIDTH), blk),
        out_shape=jax.ShapeDtypeStruct((bsz, seq, RW_WIDTH), bf16 if final else f32),
        scratch_shapes=[pltpu.VMEM((tb, EVEN_SHIFT), f32),
                        pltpu.VMEM((RW_NGROUPS, RW_GW, RW_GW), f32)],
        compiler_params=_cparams("parallel", "arbitrary"),
        name="rwkv_bwd" if reverse else "rwkv_fwd",
    )(*args)


def _trunk(x, p):
    bsz, seq, _ = x.shape
    tokens = bsz * seq
    x2d = x.reshape(tokens, D_MODEL)
    seq3 = lambda t: t.reshape(bsz, seq, t.shape[-1])
    flat = lambda t: t.reshape(tokens, t.shape[-1])

    ps, ga, gb, qs, ks, vs = _even_in(x2d, seq, p["even_norm"], p["even_w_in"])
    rw = (seq3(ps), p["mu_prev"], p["mu_next"], p["w0"], p["w_up"], p["a0"], p["a_up"],
          p["k_k"], p["k_a"])
    y_fwd = _rwkv_scan(*rw, reverse=False)
    ya = _rwkv_scan(*rw, reverse=True, r_k=p["r_k"], gn_w=p["gn_w"], gn_b=p["gn_b"],
                    gate=seq3(ga), y_fwd=y_fwd)
    yb = _attention(qs, ks, vs, seq3(gb))

    x1, q, k, v, gate, gate_lat = _mid(x2d, flat(ya), flat(yb), p["even_w_out"], p["odd_norm"],
                                       p["odd_w_in"])
    gla = (seq3(q), seq3(k), seq3(v), seq3(gate_lat), p["gate_up"], p["gate_bias"])
    o_fwd = _gla_scan(*gla, reverse=False)
    return _gla_scan(*gla, reverse=True, gate=seq3(gate), o_fwd=o_fwd, norm_w=p["gla_norm"],
                     x1=seq3(x1), w_out_bf16=p["odd_w_out"], final_norm=p["final_norm"])


def _prepare(even_norm, even_w_in, even_mu_prev, even_mu_next, rwkv_w0, rwkv_w_up, rwkv_a0,
             rwkv_a_up, rwkv_k_k, rwkv_k_a, rwkv_r_k, rwkv_gn_w, rwkv_gn_b, even_w_out, odd_norm,
             odd_w_in, gla_gate_up, gla_gate_bias, gla_norm, odd_w_out, final_norm):
    wi = odd_w_in[0]
    lat0 = 2 * GLA_KEY + GLA_VAL
    lat = jnp.pad(wi[:, lat0:lat0 + GLA_RANK], ((0, 0), (0, LANES - GLA_RANK)))
    odd_in = jnp.concatenate([wi[:, :lat0], wi[:, lat0 + GLA_RANK:], lat], axis=1)
    return {
        "even_norm": even_norm[0], "even_w_in": even_w_in[0].astype(bf16),
        "mu_prev": even_mu_prev[0], "mu_next": even_mu_next[0],
        "w0": rwkv_w0[0], "w_up": rwkv_w_up[0], "a0": rwkv_a0[0], "a_up": rwkv_a_up[0],
        "k_k": rwkv_k_k[0], "k_a": rwkv_k_a[0], "r_k": rwkv_r_k[0],
        "gn_w": rwkv_gn_w[0], "gn_b": rwkv_gn_b[0],
        "even_w_out": even_w_out[0].astype(bf16),
        "odd_norm": odd_norm[0], "odd_w_in": odd_in.astype(bf16),
        "gate_up": jnp.pad(gla_gate_up[0], ((0, 0), (0, LANES - GLA_RANK), (0, 0))),
        "gate_bias": gla_gate_bias[0], "gla_norm": gla_norm[0],
        "odd_w_out": odd_w_out[0].astype(bf16), "final_norm": final_norm,
    }


def kernel(x_prompt, x_sample, even_norm, even_w_in, even_mu_prev, even_mu_next, rwkv_w0, rwkv_w_up,
           rwkv_a0, rwkv_a_up, rwkv_k_k, rwkv_k_a, rwkv_r_k, rwkv_gn_w, rwkv_gn_b, even_w_out,
           odd_norm, odd_w_in, gla_gate_up, gla_gate_bias, gla_norm, odd_w_out, final_norm):
    p = _prepare(even_norm, even_w_in, even_mu_prev, even_mu_next, rwkv_w0, rwkv_w_up, rwkv_a0,
                 rwkv_a_up, rwkv_k_k, rwkv_k_a, rwkv_r_k, rwkv_gn_w, rwkv_gn_b, even_w_out,
                 odd_norm, odd_w_in, gla_gate_up, gla_gate_bias, gla_norm, odd_w_out, final_norm)
    return (_trunk(x_prompt, p), _trunk(x_sample, p))
```

```python
import functools
import math

import jax
import jax.numpy as jnp
from jax import lax
from jax.experimental import pallas as pl
from jax.experimental.pallas import tpu as pltpu

f32 = jnp.float32
bf16 = jnp.bfloat16

D_MODEL = 1024
RMS_EPS = 1e-6

RW_HEADS = 8
RW_DH = 64
RW_WIDTH = RW_HEADS * RW_DH
RW_LORA = 64
RW_DECAY_SCALE = 0.6065306597126334
RW_GN_EPS = 64e-5
AT_HEADS = 8
AT_DH = 64
AT_WIDTH = AT_HEADS * AT_DH
AT_SIDE = 64
AT_DILATIONS = (1, 4, 16)
ROPE_THETA = 10000.0
EVEN_SHIFT = 3 * RW_WIDTH + 2 * RW_LORA
EVEN_COLS = EVEN_SHIFT + RW_WIDTH + 4 * AT_WIDTH

GLA_HEADS = 4
GLA_KEY = 512
GLA_VAL = 1024
GLA_HK = GLA_KEY // GLA_HEADS
GLA_HV = GLA_VAL // GLA_HEADS
GLA_RANK = 16
GLA_GATE_NORM = 16.0

CHUNK = 64
LANES = 128
SUBLANES = 8
MXU_COLS = 256
ROW_BLOCK = 512
VMEM_LIMIT = 56 * 1024 * 1024

_NT = (((1,), (1,)), ((), ()))
_TN = (((0,), (0,)), ((), ()))


def _dot(a, b):
    return jnp.dot(a, b, preferred_element_type=f32)


def _dot_nt(a, b):
    return lax.dot_general(a, b, _NT, preferred_element_type=f32)


def _dot_tn(a, b):
    return lax.dot_general(a, b, _TN, preferred_element_type=f32)


def _split2(x):
    hi = x.astype(bf16)
    lo = (x - hi.astype(f32)).astype(bf16)
    return hi, lo


def _dot_seg(x, e):
    xb = x.astype(bf16)
    return jnp.concatenate([_dot(xb[:, c:c + LANES], e) for c in range(0, x.shape[1], LANES)], axis=1)


def _cumsum_rows(tri, x):
    hi, lo = _split2(x)
    return _dot(tri, hi) + _dot(tri, lo)


def _sigmoid(x):
    return 0.5 * (jnp.tanh(0.5 * x) + 1.0)


def _silu(x):
    h = 0.5 * x
    return h + h * jnp.tanh(h)


def _rms_rows(x, w):
    return x * lax.rsqrt(jnp.mean(x * x, axis=-1, keepdims=True) + RMS_EPS) * w


def _cparams(*sem):
    return pltpu.CompilerParams(dimension_semantics=sem, vmem_limit_bytes=VMEM_LIMIT)


def _rope_partner(x):
    half = AT_DH // 2
    lane = lax.broadcasted_iota(jnp.int32, x.shape, 1)
    return jnp.where((lane & half) == 0, pltpu.roll(x, LANES - half, 1), pltpu.roll(x, half, 1))


AT_FOLD = AT_DILATIONS[1]
assert AT_DILATIONS == (1, AT_FOLD, AT_FOLD * AT_FOLD)
AT_Q_DILS = AT_DILATIONS[1:]
AT_KV_DILS = AT_DILATIONS
AT_GROUPS = 3 * AT_WIDTH // LANES


def _even_in_kernel(x_ref, nw_ref, w_ref, cos_ref, sin_ref, *refs):
    ps_ref, ga_ref, gb_ref = refs[:3]
    nq, nkv = len(AT_Q_DILS), len(AT_KV_DILS)
    q_refs = refs[3:3 + nq]
    k_refs = refs[3 + nq:3 + nq + nkv]
    v_refs = refs[3 + nq + nkv:3 + nq + 2 * nkv]
    stage = refs[3 + nq + 2 * nkv:]
    tmp_refs, fold_refs = stage[:AT_GROUPS], stage[AT_GROUPS:]
    tm = x_ref.shape[0]
    xn = _rms_rows(x_ref[...], nw_ref[...]).astype(bf16)
    cos = cos_ref[...]
    sin = sin_ref[...]

    def plain(ref, off):
        def put(t, _):
            ref[:, off:off + LANES] = t
        return put

    def spread(outs, off, rotary, mul=None):
        def put(t, slot):
            if rotary:
                t = t * cos + _rope_partner(t) * sin
            if mul is not None:
                t = t * mul
            nat, fold = tmp_refs[slot], fold_refs[slot % len(fold_refs)]
            nat[...] = t
            per = tm // AT_FOLD
            by_dil = dict((dil, ref) for ref, dil in outs)
            if 1 in by_dil:
                by_dil[1][0, :, off:off + LANES] = t.astype(bf16)
            for r in range(AT_FOLD):
                x = nat[pl.ds(r, per, stride=AT_FOLD), :]
                by_dil[AT_FOLD][r, :, off:off + LANES] = x.astype(bf16)
                fold[r * per:(r + 1) * per, :] = x
            wide = AT_FOLD * AT_FOLD
            for r in range(wide):
                rows = pl.ds((r % AT_FOLD) * per + r // AT_FOLD, tm // wide, stride=AT_FOLD)
                by_dil[wide][r, :, off:off + LANES] = fold[rows, :].astype(bf16)
        return put

    cols = range(0, AT_WIDTH, LANES)
    dests = ([plain(ps_ref, c) for c in range(0, EVEN_SHIFT, LANES)]
             + [plain(ga_ref, c) for c in range(0, RW_WIDTH, LANES)]
             + [spread(list(zip(q_refs, AT_Q_DILS)), c, True, AT_DH ** -0.5) for c in cols]
             + [spread(list(zip(k_refs, AT_KV_DILS)), c, True) for c in cols]
             + [spread(list(zip(v_refs, AT_KV_DILS)), c, False) for c in cols]
             + [plain(gb_ref, c) for c in cols])
    first_at = (EVEN_SHIFT + RW_WIDTH) // LANES
    tile = 2 * MXU_COLS
    for c0 in range(0, EVEN_COLS, tile):
        c1 = min(c0 + tile, EVEN_COLS)
        acc = _dot(xn, w_ref[:, c0:c1])
        for j in range((c1 - c0) // LANES):
            g = c0 // LANES + j
            dests[g](acc[:, j * LANES:(j + 1) * LANES], (g - first_at) % AT_GROUPS)


def _rope_tables(seq):
    inv = ROPE_THETA ** (-jnp.arange(0, AT_DH, 2, dtype=f32) / AT_DH)
    ang = jnp.arange(seq, dtype=f32)[:, None] * inv[None, :]
    cos, sin = jnp.cos(ang), jnp.sin(ang)
    return (jnp.concatenate([cos, cos, cos, cos], axis=-1),
            jnp.concatenate([-sin, sin, -sin, sin], axis=-1))


def _even_in(x2d, seq, norm_w, w_in_bf16, block_rows=ROW_BLOCK):
    m = x2d.shape[0]
    tm = block_rows
    per_seq = seq // tm
    bsz = m // seq
    cos, sin = _rope_tables(seq)
    row = lambda i: (i, 0)
    full = lambda i: (0, 0)
    tab = lambda i: (i % per_seq, 0)
    res = lambda i: (i // per_seq, 0, i % per_seq, 0)
    dils = AT_Q_DILS + AT_KV_DILS + AT_KV_DILS
    outs = pl.pallas_call(
        _even_in_kernel,
        grid=(m // tm,),
        in_specs=[
            pl.BlockSpec((tm, D_MODEL), row),
            pl.BlockSpec((1, D_MODEL), full),
            pl.BlockSpec((D_MODEL, EVEN_COLS), full),
            pl.BlockSpec((tm, LANES), tab),
            pl.BlockSpec((tm, LANES), tab),
        ],
        out_specs=[
            pl.BlockSpec((tm, EVEN_SHIFT), row),
            pl.BlockSpec((tm, RW_WIDTH), row),
            pl.BlockSpec((tm, AT_WIDTH), row),
        ] + [pl.BlockSpec((None, d, tm // d, AT_WIDTH), res) for d in dils],
        out_shape=[
            jax.ShapeDtypeStruct((m, EVEN_SHIFT), f32),
            jax.ShapeDtypeStruct((m, RW_WIDTH), f32),
            jax.ShapeDtypeStruct((m, AT_WIDTH), f32),
        ] + [jax.ShapeDtypeStruct((bsz, d, seq // d, AT_WIDTH), bf16) for d in dils],
        scratch_shapes=[pltpu.VMEM((tm, LANES), f32)] * (AT_GROUPS + AT_FOLD),
        compiler_params=_cparams("parallel"),
        name="even_in",
    )(x2d, norm_w.reshape(1, D_MODEL), w_in_bf16, cos, sin)
    nq, nkv = len(AT_Q_DILS), len(AT_KV_DILS)
    return (outs[0], outs[1], outs[2], outs[3:3 + nq], outs[3 + nq:3 + nq + nkv],
            outs[3 + nq + nkv:])


ODD_PAD_COLS = 2 * GLA_KEY + 2 * GLA_VAL + LANES


def _mid_kernel(x_ref, ya_ref, yb_ref, wo_ref, nw_ref, wi_ref,
                x1_ref, q_ref, k_ref, v_ref, g_ref, gl_ref):
    x1 = (x_ref[...] + _dot(ya_ref[...], wo_ref[0:RW_WIDTH, :])
          + _dot(yb_ref[...], wo_ref[RW_WIDTH:RW_WIDTH + AT_WIDTH, :]))
    x1_ref[...] = x1
    xn = _rms_rows(x1, nw_ref[...]).astype(bf16)
    c = 0
    for ref, width in ((q_ref, GLA_KEY), (k_ref, GLA_KEY), (v_ref, GLA_VAL), (g_ref, GLA_VAL),
                       (gl_ref, LANES)):
        for c0 in range(0, width, 512):
            c1 = min(c0 + 512, width)
            ref[:, c0:c1] = _dot(xn, wi_ref[:, c + c0:c + c1]).astype(ref.dtype)
        c += width


def _mid(x2d, ya, yb, w_out_bf16, norm_w, w_in_pad_bf16, block_rows=ROW_BLOCK):
    m = x2d.shape[0]
    tm = block_rows
    row = lambda i: (i, 0)
    full = lambda i: (0, 0)
    widths = (D_MODEL, GLA_KEY, GLA_KEY, GLA_VAL, GLA_VAL, LANES)
    return pl.pallas_call(
        _mid_kernel,
        grid=(m // tm,),
        in_specs=[
            pl.BlockSpec((tm, D_MODEL), row),
            pl.BlockSpec((tm, RW_WIDTH), row),
            pl.BlockSpec((tm, AT_WIDTH), row),
            pl.BlockSpec((RW_WIDTH + AT_WIDTH, D_MODEL), full),
            pl.BlockSpec((1, D_MODEL), full),
            pl.BlockSpec((D_MODEL, ODD_PAD_COLS), full),
        ],
        out_specs=[pl.BlockSpec((tm, w), row) for w in widths],
        out_shape=[jax.ShapeDtypeStruct((m, w), bf16 if i == 3 else f32)
                   for i, w in enumerate(widths)],
        compiler_params=_cparams("parallel"),
        name="mid",
    )(x2d, ya, yb, w_out_bf16, norm_w.reshape(1, D_MODEL), w_in_pad_bf16)


def _causal_mask(reverse):
    t = lax.broadcasted_iota(jnp.int32, (CHUNK, CHUNK), 0)
    s = lax.broadcasted_iota(jnp.int32, (CHUNK, CHUNK), 1)
    return s >= t if reverse else s <= t


def _row_to_col(row):
    n = row.shape[1]
    eye = (lax.broadcasted_iota(jnp.int32, (n, n), 0) == lax.broadcasted_iota(jnp.int32, (n, n), 1))
    return jnp.sum(jnp.where(eye, jnp.broadcast_to(row, (n, n)), 0.0), axis=1, keepdims=True)


def _gla_kernel(*refs, reverse, final, block_rows):
    if final:
        (q_ref, k_ref, v_ref, gl_ref, gup_ref, gb_ref, gate_ref, of_ref, nw_ref, x1_ref, wo_ref,
         fnw_ref, o_ref, st_ref, y_ref) = refs
    else:
        (q_ref, k_ref, v_ref, gl_ref, gup_ref, gb_ref, o_ref, st_ref) = refs
    d = 1 if reverse else 0

    @pl.when(pl.program_id(1) == 0)
    def _():
        st_ref[...] = jnp.zeros_like(st_ref)

    incl = _causal_mask(reverse)
    tri = incl.astype(bf16)
    gup = gup_ref[d]
    gbias = gb_ref[d:d + 1, :]
    nchunks = block_rows // CHUNK
    last = 0 if reverse else CHUNK - 1

    x = _dot(gl_ref[...].astype(bf16), gup) + gbias
    g = (jnp.minimum(x, 0.0) - jnp.log(1.0 + jnp.exp(-jnp.abs(x)))) / GLA_GATE_NORM
    chunk_rows = [slice(c * CHUNK, (c + 1) * CHUNK) for c in range(nchunks)]
    bcum = jnp.concatenate([_cumsum_rows(tri, g[rs]) for rs in chunk_rows], axis=0)
    b_last = [bcum[c * CHUNK + last:c * CHUNK + last + 1, :] for c in range(nchunks)]
    b_end = jnp.concatenate([jnp.broadcast_to(b, (CHUNK, GLA_KEY)) for b in b_last], axis=0)
    q = q_ref[...] * (GLA_HK ** -0.5)
    k = k_ref[...]
    q_dec = (q * jnp.exp(bcum)).astype(bf16)
    k_dec = (k * jnp.exp(-bcum)).astype(bf16)
    k_end = (k * jnp.exp(b_end - bcum)).astype(bf16)

    order = list(reversed(range(nchunks))) if reverse else list(range(nchunks))
    klanes = [slice(h * GLA_HK, (h + 1) * GLA_HK) for h in range(GLA_HEADS)]
    vlanes = [slice(h * GLA_HV, (h + 1) * GLA_HV) for h in range(GLA_HEADS)]
    units = [(c, h) for c in order for h in range(GLA_HEADS)]
    v = [v_ref[chunk_rows[c], vlanes[h]].astype(bf16) for c, h in units]
    att = [jnp.where(incl, _dot_nt(q_dec[chunk_rows[c], klanes[h]], k_dec[chunk_rows[c], klanes[h]]),
                     0.0).astype(bf16) for c, h in units]
    o_intra = [_dot(att[i], v[i]) for i in range(len(units))]
    kv = [_dot_tn(k_end[chunk_rows[c], klanes[h]], v[i]) for i, (c, h) in enumerate(units)]
    dcol = [_row_to_col(jnp.exp(b_last[c][:, klanes[h]])) for c, h in units]

    state = [st_ref[h] for h in range(GLA_HEADS)]
    for i, (c, h) in enumerate(units):
        rows = chunk_rows[c]
        o = o_intra[i] + _dot(q_dec[rows, klanes[h]], state[h].astype(bf16))
        state[h] = state[h] * dcol[i] + kv[i]
        if final:
            o = o + of_ref[rows, vlanes[h]]
            o = o * lax.rsqrt(jnp.mean(o * o, axis=-1, keepdims=True) + RMS_EPS) * nw_ref[...]
            y_ref[rows, vlanes[h]] = (o * _silu(gate_ref[rows, vlanes[h]])).astype(y_ref.dtype)
        else:
            o_ref[rows, vlanes[h]] = o
    for h in range(GLA_HEADS):
        st_ref[h] = state[h]
    if final:
        x2 = x1_ref[...] + _dot(y_ref[...], wo_ref[...])
        o_ref[...] = _rms_rows(x2, fnw_ref[...])


def _gla_scan(q, k, v, gl, gate_up_pad, gate_bias, *, reverse, gate=None, o_fwd=None, norm_w=None,
              x1=None, w_out_bf16=None, final_norm=None, block_rows=ROW_BLOCK):
    bsz, seq, _ = q.shape
    tb = block_rows
    nb = seq // tb
    final = reverse
    blk = (lambda b, j: (b, nb - 1 - j, 0)) if reverse else (lambda b, j: (b, j, 0))
    full2 = lambda b, j: (0, 0)
    full3 = lambda b, j: (0, 0, 0)
    in_specs = [
        pl.BlockSpec((None, tb, GLA_KEY), blk),
        pl.BlockSpec((None, tb, GLA_KEY), blk),
        pl.BlockSpec((None, tb, GLA_VAL), blk),
        pl.BlockSpec((None, tb, LANES), blk),
        pl.BlockSpec((2, LANES, GLA_KEY), full3),
        pl.BlockSpec((2, GLA_KEY), full2),
    ]
    args = [q, k, v, gl, gate_up_pad.astype(bf16), gate_bias]
    if final:
        in_specs += [
            pl.BlockSpec((None, tb, GLA_VAL), blk),
            pl.BlockSpec((None, tb, GLA_VAL), blk),
            pl.BlockSpec((1, GLA_HV), full2),
            pl.BlockSpec((None, tb, D_MODEL), blk),
            pl.BlockSpec((GLA_VAL, D_MODEL), full2),
            pl.BlockSpec((1, D_MODEL), full2),
        ]
        args += [gate, o_fwd, norm_w.reshape(1, GLA_HV), x1, w_out_bf16,
                 final_norm.reshape(1, D_MODEL)]
    scratch = [pltpu.VMEM((GLA_HEADS, GLA_HK, GLA_HV), f32)]
    if final:
        scratch.append(pltpu.VMEM((tb, GLA_VAL), bf16))
    return pl.pallas_call(
        functools.partial(_gla_kernel, reverse=reverse, final=final, block_rows=tb),
        grid=(bsz, nb),
        in_specs=in_specs,
        out_specs=pl.BlockSpec((None, tb, D_MODEL if final else GLA_VAL), blk),
        out_shape=jax.ShapeDtypeStruct((bsz, seq, D_MODEL if final else GLA_VAL), f32),
        scratch_shapes=scratch,
        compiler_params=_cparams("parallel", "arbitrary"),
        name="gla_bwd" if reverse else "gla_fwd",
    )(*args)


AT_NEG = -1e30
AT_QBLK = 128
AT_KBLK = 256
AT_UNROLL = 8


def _attn_kernel(q4, q16, k1, k4, k16, v1, v4, v16, g_ref, o_ref, m_s, l_s, acc_s, bias_s, tmp_s,
                 *, seq):
    lane = lax.broadcasted_iota(jnp.int32, (1, LANES), 1)
    left = lane < AT_DH
    fold_len = seq // AT_FOLD

    def fill_bias(offs, tq, nk, deltas):
        for d, delta in enumerate(deltas):
            bias_s[d, 0:tq, 0:nk] = jnp.where(jnp.abs(offs + delta) <= AT_SIDE, 0.0, AT_NEG)

    def run_blocks(nblocks, load_unit):
        def body(it, carry):
            units = [load_unit(it * AT_UNROLL + u) for u in range(AT_UNROLL)]
            heads = [(u, first) for u in range(AT_UNROLL) for first in (True, False)]
            scores = []
            for u, first in heads:
                qb, kb, _, bias, _, _ = units[u]
                qh = jnp.where(left if first else ~left, qb, jnp.zeros_like(qb))
                scores.append(_dot_nt(qh, kb) + bias)
            maxes = [jnp.max(s, axis=1, keepdims=True) for s in scores]
            probs = [jnp.exp(s - mh) for s, mh in zip(scores, maxes)]
            sums = [jnp.sum(p, axis=1, keepdims=True) for p in probs]
            outs = [_dot(p.astype(bf16), units[u][2]) for p, (u, _) in zip(probs, heads)]
            olds = [None if units[u][5] else
                    [(m_s[idx, :], l_s[idx, :], acc_s[idx, :]) for idx, _ in units[u][4]]
                    for u in range(AT_UNROLL)]
            for u in range(AT_UNROLL):
                m_new = jnp.where(left, maxes[2 * u], maxes[2 * u + 1])
                l_new = jnp.where(left, sums[2 * u], sums[2 * u + 1])
                a_new = jnp.where(left, outs[2 * u], outs[2 * u + 1])
                for pi, (idx, rows) in enumerate(units[u][4]):
                    mp, lp, ap = m_new[rows], l_new[rows], a_new[rows]
                    if units[u][5]:
                        m_s[idx, :] = mp
                        l_s[idx, :] = lp
                        acc_s[idx, :] = ap
                    else:
                        m_old, l_old, a_old = olds[u][pi]
                        m = jnp.maximum(m_old, mp)
                        w_old = jnp.exp(m_old - m)
                        w_new = jnp.exp(mp - m)
                        m_s[idx, :] = m
                        l_s[idx, :] = l_old * w_old + lp * w_new
                        acc_s[idx, :] = a_old * w_old + ap * w_new
            return carry

        lax.fori_loop(0, nblocks // AT_UNROLL, body, 0)

    def window(m0, sub, tq, nk):
        ks = jnp.clip(m0 - AT_SIDE, 0, sub - nk)
        return ks, (m0 - ks) // AT_SIDE

    tq, nk = AT_QBLK, AT_KBLK
    per = tq // AT_FOLD
    qi = lax.broadcasted_iota(jnp.int32, (tq, nk), 0)
    kj = lax.broadcasted_iota(jnp.int32, (tq, nk), 1)
    fill_bias((qi // per) + AT_FOLD * (qi % per) - kj, tq, nk, (0, AT_SIDE, 2 * AT_SIDE))

    def unit_d1(i):
        t0 = i * tq
        ks, bi = window(t0, seq, tq, nk)
        qrow = pl.ds(pl.multiple_of(i * per, per), per)
        qb = jnp.concatenate([q4[rho, qrow, :] for rho in range(AT_FOLD)], axis=0)
        krow = pl.ds(pl.multiple_of(ks, AT_SIDE), nk)
        state = [(pl.ds(pl.multiple_of(rho * fold_len + i * per, per), per),
                  slice(rho * per, (rho + 1) * per)) for rho in range(AT_FOLD)]
        return qb, k1[0, krow, :], v1[0, krow, :], bias_s[bi], state, True

    run_blocks(seq // tq, unit_d1)

    sub = fold_len
    fill_bias(qi - kj, tq, nk, (0, AT_SIDE, 2 * AT_SIDE))
    nblk = sub // tq

    def unit_fold(i):
        r = i // nblk
        m0 = (i % nblk) * tq
        ks, bi = window(m0, sub, tq, nk)
        qrow = pl.ds(pl.multiple_of(m0, tq), tq)
        krow = pl.ds(pl.multiple_of(ks, AT_SIDE), nk)
        state = [(pl.ds(pl.multiple_of(r * sub + m0, tq), tq), slice(0, tq))]
        return q4[r, qrow, :], k4[r, krow, :], v4[r, krow, :], bias_s[bi], state, False

    run_blocks(AT_FOLD * nblk, unit_fold)

    dil = AT_DILATIONS[-1]
    sub = seq // dil
    tq, nk = min(AT_QBLK, sub), min(AT_KBLK, sub)
    fill_bias((qi - kj)[0:tq, 0:nk], tq, nk, (0, AT_SIDE, nk - tq))
    nblk = sub // tq

    def unit_wide(i):
        r = i // nblk
        m0 = (i % nblk) * tq
        ks, bi = window(m0, sub, tq, nk)
        qrow = pl.ds(pl.multiple_of(m0, AT_SIDE), tq)
        krow = pl.ds(pl.multiple_of(ks, AT_SIDE), nk)
        srow = (r % AT_FOLD) * fold_len + r // AT_FOLD + (dil // AT_FOLD) * m0
        state = [(pl.ds(srow, tq, stride=dil // AT_FOLD), slice(0, tq))]
        return (q16[r, qrow, :], k16[r, krow, :], v16[r, krow, :], bias_s[bi, 0:tq, 0:nk], state,
                False)

    run_blocks(dil * nblk, unit_wide)

    rows = tmp_s.shape[0]
    per_fold = rows // AT_FOLD

    def finish(i, carry):
        for rho in range(AT_FOLD):
            idx = pl.ds(pl.multiple_of(rho * fold_len + i * per_fold, per_fold), per_fold)
            tmp_s[pl.ds(rho, per_fold, stride=AT_FOLD), :] = acc_s[idx, :] / l_s[idx, :]
        out = pl.ds(pl.multiple_of(i * rows, rows), rows)
        o_ref[out, :] = (tmp_s[...] * _silu(g_ref[out, :])).astype(o_ref.dtype)
        return carry

    lax.fori_loop(0, seq // rows, finish, 0)


def _attention(qs, ks, vs, gate):
    bsz, seq, _ = gate.shape
    pairs = AT_WIDTH // LANES
    once = pl.Buffered(1)
    col = lambda b, p: (b, 0, p)
    res = lambda b, p: (b, 0, 0, p)
    copies = list(qs) + list(ks) + list(vs)
    return pl.pallas_call(
        functools.partial(_attn_kernel, seq=seq),
        grid=(bsz, pairs),
        in_specs=[pl.BlockSpec((None, t.shape[1], t.shape[2], LANES), res) for t in copies]
        + [pl.BlockSpec((None, seq, LANES), col, pipeline_mode=once)],
        out_specs=pl.BlockSpec((None, seq, LANES), col),
        out_shape=jax.ShapeDtypeStruct((bsz, seq, AT_WIDTH), bf16),
        scratch_shapes=([pltpu.VMEM((seq, LANES), f32)] * 3
                        + [pltpu.VMEM((3, AT_QBLK, AT_KBLK), f32),
                           pltpu.VMEM((min(seq, 512), LANES), f32)]),
        compiler_params=_cparams("parallel", "parallel"),
        name="dilated_attn",
    )(*copies, gate)


RW_GROUP = 2
RW_GW = RW_GROUP * RW_DH
RW_NGROUPS = RW_HEADS // RW_GROUP
RW_WAVE = 4


def _head_stack(x, masks):
    return jnp.concatenate([jnp.where(mk, x, jnp.zeros_like(x)) for mk in masks], axis=0)


def _pair_t(x):
    xt = x.T
    return jnp.concatenate([xt[h * RW_DH:(h + 1) * RW_DH] for h in range(RW_GROUP)], axis=1)


def _unit_lower_inverse_stages(w, src, dst, masks):
    steps = int(math.log2(CHUNK)) - 1

    def start():
        n, lanes = w[src][0].shape
        eye = (lax.broadcasted_iota(jnp.int32, (n, lanes), 1) % n
               == lax.broadcasted_iota(jnp.int32, (n, lanes), 0)).astype(f32)
        w["_t"] = [eye + a for a in w[src]]
        qs = [a.astype(bf16) for a in w[src]]
        w["_q"] = [_dot(q, _head_stack(q, masks)).astype(bf16) for q in qs]

    def double():
        n = w[src][0].shape[0]
        both = [_dot(jnp.concatenate([q, t.astype(bf16)], axis=0), _head_stack(q, masks))
                for q, t in zip(w["_q"], w["_t"])]
        w["_q"] = [x[:n].astype(bf16) for x in both]
        w["_t"] = [t + x[n:] for t, x in zip(w["_t"], both)]

    def finish():
        w[dst] = [_head_stack((t + _dot(t.astype(bf16), _head_stack(q, masks))).astype(bf16), masks)
                  for q, t in zip(w["_q"], w["_t"])]

    return [start] + [double] * (steps - 1) + [finish]


def _rwkv_kernel(*refs, reverse, final, block_rows, nblocks):
    (ps_ref, prev_ref, next_ref, mup_ref, mun_ref, w0_ref, wup_ref, a0_ref, aup_ref,
     kk_ref, ka_ref, seg_ref) = refs[:12]
    if final:
        rk_ref, gnw_ref, gnb_ref, ga_ref, yf_ref, o_ref, sh_ref, st_ref = refs[12:]
    else:
        o_ref, sh_ref, st_ref = refs[12:]
    d = 1 if reverse else 0
    tb = block_rows
    j = pl.program_id(1)
    blk = (nblocks - 1 - j) if reverse else j

    @pl.when(j == 0)
    def _():
        st_ref[...] = jnp.zeros_like(st_ref)

    rid = lax.broadcasted_iota(jnp.int32, (tb, 1), 0)
    has_prev = (blk > 0).astype(f32)
    has_next = (blk < nblocks - 1).astype(f32)
    for c0 in range(0, EVEN_SHIFT, 256):
        c1 = min(c0 + 256, EVEN_SHIFT)
        x = ps_ref[:, c0:c1]
        before = prev_ref[SUBLANES - 1:SUBLANES, c0:c1] * has_prev
        after = next_ref[0:1, c0:c1] * has_next
        prv = jnp.where(rid == 0, before, pltpu.roll(x, 1, 0))
        nxt = jnp.where(rid == tb - 1, after, pltpu.roll(x, tb - 1, 0))
        sh_ref[:, c0:c1] = x + mup_ref[:, c0:c1] * (prv - x) + mun_ref[:, c0:c1] * (nxt - x)

    tri = _causal_mask(reverse).astype(bf16)
    gi = lax.broadcasted_iota(jnp.int32, (2 * CHUNK, 2 * RW_GW), 0)
    gj = lax.broadcasted_iota(jnp.int32, (2 * CHUNK, 2 * RW_GW), 1)
    ahead = (gi % CHUNK) - (gj % CHUNK)
    ahead = -ahead if reverse else ahead
    keep = ahead < jnp.where(gj < RW_GW, 0, 1)
    lane = lax.broadcasted_iota(jnp.int32, (1, 2 * RW_GW), 1)
    masks2 = [(lane // RW_DH) % RW_GROUP == h for h in range(RW_GROUP)]
    masks = [mk[:, 0:RW_GW] for mk in masks2]
    seg = seg_ref[...]
    last = 0 if reverse else CHUNK - 1
    nchunks = tb // CHUNK
    w_lo, a_lo = 3 * RW_WIDTH, 3 * RW_WIDTH + RW_LORA

    def lr_gate(a_lat, dd):
        return _sigmoid(a0_ref[dd:dd + 1, :] + _dot(a_lat.astype(bf16), aup_ref[dd]))

    r = sh_ref[:, 0:RW_WIDTH]
    k = sh_ref[:, RW_WIDTH:2 * RW_WIDTH]
    v = sh_ref[:, 2 * RW_WIDTH:3 * RW_WIDTH]
    w_lat = sh_ref[:, w_lo:w_lo + RW_LORA]
    a_lat = sh_ref[:, a_lo:a_lo + RW_LORA]
    w_log = -RW_DECAY_SCALE * _sigmoid(
        w0_ref[d:d + 1, :] + _dot(jnp.tanh(w_lat).astype(bf16), wup_ref[d]))
    lr = lr_gate(a_lat, d)
    kk = k * kk_ref[...]
    kk = kk * lax.rsqrt(jnp.maximum(_dot_seg(kk * kk, seg), 1e-24))
    k_dir = k * (1.0 + (lr - 1.0) * ka_ref[...])
    chunk_rows = [slice(c * CHUNK, (c + 1) * CHUNK) for c in range(nchunks)]
    cum = jnp.concatenate([_cumsum_rows(tri, w_log[rs]) for rs in chunk_rows], axis=0)
    grow = jnp.exp(-cum)
    a_t = -kk * jnp.exp(cum - w_log)
    b_t = kk * lr * grow
    k_t = k_dir * grow
    r_t = r * jnp.exp(cum)
    a_b, b_b, k_b, r_b = (t.astype(bf16) for t in (a_t, b_t, k_t, r_t))

    order = list(reversed(range(nchunks))) if reverse else list(range(nchunks))
    lanes = [slice(g * RW_GW, (g + 1) * RW_GW) for g in range(RW_NGROUPS)]
    state = [st_ref[g] for g in range(RW_NGROUPS)]
    y_rows = {}

    def wave_stages(chunks):
        units = [(c, g) for c in chunks for g in range(RW_NGROUPS)]
        n = range(len(units))
        w = {}

        def stacked():
            sl = lambda x, i: x[chunk_rows[units[i][0]], lanes[units[i][1]]]
            w["bk"] = [jnp.concatenate([sl(b_b, i), sl(k_b, i)], axis=0) for i in n]
            w["ar_bd"] = [jnp.concatenate([_head_stack(sl(a_b, i), masks),
                                           _head_stack(sl(r_b, i), masks)], axis=0) for i in n]
            w["b_bd"] = [_head_stack(sl(b_b, i), masks) for i in n]
            w["k_bd"] = [_head_stack(sl(k_b, i), masks) for i in n]
            w["a_t"] = [_pair_t(sl(a_t, i)).astype(bf16) for i in n]
            w["r_bd"] = [_head_stack(_pair_t(sl(r_t, i)).astype(bf16), masks) for i in n]
            w["v_t"] = [_pair_t(sl(v, i)).astype(bf16) for i in n]

        def grams():
            gram = [jnp.where(keep, _dot_nt(w["bk"][i], w["ar_bd"][i]), 0.0) for i in n]
            w["ab_t"] = [t[0:CHUNK, 0:RW_GW] for t in gram]
            w["ak_rk_bd"] = [_head_stack(t[CHUNK:, :].astype(bf16), masks2) for t in gram]
            w["rb_b_bd"] = [jnp.concatenate([_head_stack(gram[i][0:CHUNK, RW_GW:].astype(bf16), masks),
                                             w["b_bd"][i]], axis=1) for i in n]

        inverse = _unit_lower_inverse_stages(w, "ab_t", "t_inv_bd", masks)

        def value_products():
            x = [_dot(w["v_t"][i], w["ak_rk_bd"][i]) for i in n]
            w["av_t"] = [t[:, 0:RW_GW].astype(bf16) for t in x]
            w["y_local_t"] = [t[:, RW_GW:] for t in x]
            w["vk"] = [_dot(w["v_t"][i], w["k_bd"][i]) for i in n]

        def apply_inverse():
            x = [_dot(jnp.concatenate([w["av_t"][i], w["a_t"][i]], axis=0), w["t_inv_bd"][i])
                 for i in n]
            w["z_t"] = [t[0:RW_DH] for t in x]
            w["wm_r_bd"] = [jnp.concatenate([_head_stack(x[i][RW_DH:].astype(bf16), masks),
                                             w["r_bd"][i]], axis=1) for i in n]

        def recur(pos_c, c):
            def new_state():
                p_end = jnp.exp(cum[c * CHUNK + last:c * CHUNK + last + 1, :])
                w["y_t"] = []
                for g in range(RW_NGROUPS):
                    i = pos_c * RW_NGROUPS + g
                    s0 = state[g]
                    x = _dot(s0.astype(bf16), w["wm_r_bd"][i])
                    u_t = (x[:, 0:RW_GW] + w["z_t"][i]).astype(bf16)
                    x2 = _dot(u_t, w["rb_b_bd"][i])
                    state[g] = (s0 + x2[:, RW_GW:] + w["vk"][i]) * p_end[:, lanes[g]]
                    w["y_t"].append(x[:, RW_GW:] + x2[:, 0:RW_GW] + w["y_local_t"][i])

            def outputs():
                y_rows[c] = jnp.concatenate([_pair_t(t) for t in w["y_t"]], axis=1)

            return [new_state, outputs]

        independent = [stacked, grams] + inverse + [value_products, apply_inverse]
        recurrence = [f for pos_c, c in enumerate(chunks) for f in recur(pos_c, c)]
        return independent, recurrence

    waves = [order[i:i + RW_WAVE] for i in range(0, nchunks, RW_WAVE)]
    pending = []
    for chunks in waves:
        independent, recurrence = wave_stages(chunks)
        every = max(1, len(independent) // (len(pending) + 1)) if pending else 0
        for si, stage in enumerate(independent):
            stage()
            if pending and (si + 1) % every == 0:
                pending.pop(0)()
        while pending:
            pending.pop(0)()
        pending = recurrence
    while pending:
        pending.pop(0)()
    for g in range(RW_NGROUPS):
        st_ref[g] = state[g]
    y_all = jnp.concatenate([y_rows[c] for c in range(nchunks)], axis=0)

    if not final:
        o_ref[...] = y_all
    else:
        y_all = y_all + yf_ref[...]
        mean = _dot_seg(y_all, seg) * (1.0 / RW_DH)
        cen = y_all - mean
        var = _dot_seg(cen * cen, seg) * (1.0 / RW_DH)
        yn = cen * lax.rsqrt(var + RW_GN_EPS) * gnw_ref[...] + gnb_ref[...]
        k_other = k * (1.0 + (lr_gate(a_lat, 1 - d) - 1.0) * ka_ref[...])
        rk = _dot_seg(r * (k_dir + k_other) * rk_ref[...], seg)
        o_ref[...] = ((yn + rk * v) * _silu(ga_ref[...])).astype(o_ref.dtype)


def _segment_ones():
    h = jnp.arange(LANES) // RW_DH
    return (h[:, None] == h[None, :]).astype(bf16)


def _rwkv_scan(ps, mu_prev, mu_next, w0, w_up, a0, a_up, k_k, k_a, *, reverse,
               r_k=None, gn_w=None, gn_b=None, gate=None, y_fwd=None, block_rows=2 * ROW_BLOCK):
    bsz, seq, _ = ps.shape
    tb = block_rows
    nb = seq // tb
    final = reverse
    halo = tb // SUBLANES
    pos = (lambda j: nb - 1 - j) if reverse else (lambda j: j)
    blk = lambda b, j: (b, pos(j), 0)
    prev = lambda b, j: (b, jnp.maximum(pos(j) * halo - 1, 0), 0)
    nxt = lambda b, j: (b, jnp.minimum((pos(j) + 1) * halo, seq // SUBLANES - 1), 0)
    full2 = lambda b, j: (0, 0)
    full3 = lambda b, j: (0, 0, 0)
    vec = lambda n: pl.BlockSpec((1, n), full2)
    in_specs = [
        pl.BlockSpec((None, tb, EVEN_SHIFT), blk),
        pl.BlockSpec((None, SUBLANES, EVEN_SHIFT), prev),
        pl.BlockSpec((None, SUBLANES, EVEN_SHIFT), nxt),
        vec(EVEN_SHIFT), vec(EVEN_SHIFT),
        pl.BlockSpec((2, RW_WIDTH), full2),
        pl.BlockSpec((2, RW_LORA, RW_WIDTH), full3),
        pl.BlockSpec((2, RW_WIDTH), full2),
        pl.BlockSpec((2, RW_LORA, RW_WIDTH), full3),
        vec(RW_WIDTH), vec(RW_WIDTH),
        pl.BlockSpec((LANES, LANES), full2),
    ]
    row = lambda t: t.reshape(1, -1)
    args = [ps, ps, ps, row(mu_prev), row(mu_next), w0, w_up.astype(bf16), a0, a_up.astype(bf16),
            row(k_k), row(k_a),
            _segment_ones()]
    if final:
        in_specs += [vec(RW_WIDTH), vec(RW_WIDTH), vec(RW_WIDTH),
                     pl.BlockSpec((None, tb, RW_WIDTH), blk),
                     pl.BlockSpec((None, tb, RW_WIDTH), blk)]
        args += [row(r_k), row(gn_w), row(gn_b), gate, y_fwd]
    return pl.pallas_call(
        functools.partial(_rwkv_kernel, reverse=reverse, final=final, block_rows=tb, nblocks=nb),
        grid=(bsz, nb),
        in_specs=in_specs,
        out_specs=pl.BlockSpec((None, tb, RW_WIDTH), blk),
        out_shape=jax.ShapeDtypeStruct((bsz, seq, RW_WIDTH), bf16 if final else f32),
        scratch_shapes=[pltpu.VMEM((tb, EVEN_SHIFT), f32),
                        pltpu.VMEM((RW_NGROUPS, RW_DH, RW_GW), f32)],
        compiler_params=_cparams("parallel", "arbitrary"),
        name="rwkv_bwd" if reverse else "rwkv_fwd",
    )(*args)


def _trunk(x, p):
    bsz, seq, _ = x.shape
    tokens = bsz * seq
    x2d = x.reshape(tokens, D_MODEL)
    seq3 = lambda t: t.reshape(bsz, seq, t.shape[-1])
    flat = lambda t: t.reshape(tokens, t.shape[-1])

    ps, ga, gb, qs, ks, vs = _even_in(x2d, seq, p["even_norm"], p["even_w_in"])
    rw = (seq3(ps), p["mu_prev"], p["mu_next"], p["w0"], p["w_up"], p["a0"], p["a_up"],
          p["k_k"], p["k_a"])
    y_fwd = _rwkv_scan(*rw, reverse=False)
    ya = _rwkv_scan(*rw, reverse=True, r_k=p["r_k"], gn_w=p["gn_w"], gn_b=p["gn_b"],
                    gate=seq3(ga), y_fwd=y_fwd)
    yb = _attention(qs, ks, vs, seq3(gb))

    x1, q, k, v, gate, gate_lat = _mid(x2d, flat(ya), flat(yb), p["even_w_out"], p["odd_norm"],
                                       p["odd_w_in"])
    gla = (seq3(q), seq3(k), seq3(v), seq3(gate_lat), p["gate_up"], p["gate_bias"])
    o_fwd = _gla_scan(*gla, reverse=False)
    return _gla_scan(*gla, reverse=True, gate=seq3(gate), o_fwd=o_fwd, norm_w=p["gla_norm"],
                     x1=seq3(x1), w_out_bf16=p["odd_w_out"], final_norm=p["final_norm"])


def _prepare(even_norm, even_w_in, even_mu_prev, even_mu_next, rwkv_w0, rwkv_w_up, rwkv_a0,
             rwkv_a_up, rwkv_k_k, rwkv_k_a, rwkv_r_k, rwkv_gn_w, rwkv_gn_b, even_w_out, odd_norm,
             odd_w_in, gla_gate_up, gla_gate_bias, gla_norm, odd_w_out, final_norm):
    wi = odd_w_in[0]
    lat0 = 2 * GLA_KEY + GLA_VAL
    lat = jnp.pad(wi[:, lat0:lat0 + GLA_RANK], ((0, 0), (0, LANES - GLA_RANK)))
    odd_in = jnp.concatenate([wi[:, :lat0], wi[:, lat0 + GLA_RANK:], lat], axis=1)
    return {
        "even_norm": even_norm[0], "even_w_in": even_w_in[0].astype(bf16),
        "mu_prev": even_mu_prev[0], "mu_next": even_mu_next[0],
        "w0": rwkv_w0[0], "w_up": rwkv_w_up[0], "a0": rwkv_a0[0], "a_up": rwkv_a_up[0],
        "k_k": rwkv_k_k[0], "k_a": rwkv_k_a[0], "r_k": rwkv_r_k[0],
        "gn_w": rwkv_gn_w[0], "gn_b": rwkv_gn_b[0],
        "even_w_out": even_w_out[0].astype(bf16),
        "odd_norm": odd_norm[0], "odd_w_in": odd_in.astype(bf16),
        "gate_up": jnp.pad(gla_gate_up[0], ((0, 0), (0, LANES - GLA_RANK), (0, 0))),
        "gate_bias": gla_gate_bias[0], "gla_norm": gla_norm[0],
        "odd_w_out": odd_w_out[0].astype(bf16), "final_norm": final_norm,
    }


def kernel(x_prompt, x_sample, even_norm, even_w_in, even_mu_prev, even_mu_next, rwkv_w0, rwkv_w_up,
           rwkv_a0, rwkv_a_up, rwkv_k_k, rwkv_k_a, rwkv_r_k, rwkv_gn_w, rwkv_gn_b, even_w_out,
           odd_norm, odd_w_in, gla_gate_up, gla_gate_bias, gla_norm, odd_w_out, final_norm):
    p = _prepare(even_norm, even_w_in, even_mu_prev, even_mu_next, rwkv_w0, rwkv_w_up, rwkv_a0,
                 rwkv_a_up, rwkv_k_k, rwkv_k_a, rwkv_r_k, rwkv_gn_w, rwkv_gn_b, even_w_out,
                 odd_norm, odd_w_in, gla_gate_up, gla_gate_bias, gla_norm, odd_w_out, final_norm)
    return (_trunk(x_prompt, p), _trunk(x_sample, p))
```

```python
import functools
import math

import jax
import jax.numpy as jnp
from jax import lax
from jax.experimental import pallas as pl
from jax.experimental.pallas import tpu as pltpu

f32 = jnp.float32
bf16 = jnp.bfloat16

D_MODEL = 1024
RMS_EPS = 1e-6

RW_HEADS = 8
RW_DH = 64
RW_WIDTH = RW_HEADS * RW_DH
RW_LORA = 64
RW_DECAY_SCALE = 0.6065306597126334
RW_GN_EPS = 64e-5
AT_HEADS = 8
AT_DH = 64
AT_WIDTH = AT_HEADS * AT_DH
AT_SIDE = 64
AT_DILATIONS = (1, 4, 16)
ROPE_THETA = 10000.0
EVEN_SHIFT = 3 * RW_WIDTH + 2 * RW_LORA
EVEN_COLS = EVEN_SHIFT + RW_WIDTH + 4 * AT_WIDTH

GLA_HEADS = 4
GLA_KEY = 512
GLA_VAL = 1024
GLA_HK = GLA_KEY // GLA_HEADS
GLA_HV = GLA_VAL // GLA_HEADS
GLA_RANK = 16
GLA_GATE_NORM = 16.0

CHUNK = 64
LANES = 128
SUBLANES = 8
MXU_COLS = 256
ROW_BLOCK = 512
VMEM_LIMIT = 56 * 1024 * 1024

_NT = (((1,), (1,)), ((), ()))
_TN = (((0,), (0,)), ((), ()))


def _dot(a, b):
    return jnp.dot(a, b, preferred_element_type=f32)


def _dot_nt(a, b):
    return lax.dot_general(a, b, _NT, preferred_element_type=f32)


def _dot_tn(a, b):
    return lax.dot_general(a, b, _TN, preferred_element_type=f32)


def _split2(x):
    hi = x.astype(bf16)
    lo = (x - hi.astype(f32)).astype(bf16)
    return hi, lo


def _dot_seg(x, e):
    xb = x.astype(bf16)
    return jnp.concatenate([_dot(xb[:, c:c + LANES], e) for c in range(0, x.shape[1], LANES)], axis=1)


def _cumsum_rows(tri, x):
    hi, lo = _split2(x)
    return _dot(tri, hi) + _dot(tri, lo)


def _sigmoid(x):
    return 0.5 * (jnp.tanh(0.5 * x) + 1.0)


def _silu(x):
    h = 0.5 * x
    return h + h * jnp.tanh(h)


def _rms_rows(x, w):
    return x * lax.rsqrt(jnp.mean(x * x, axis=-1, keepdims=True) + RMS_EPS) * w


def _cparams(*sem):
    return pltpu.CompilerParams(dimension_semantics=sem, vmem_limit_bytes=VMEM_LIMIT)


def _rope_partner(x):
    half = AT_DH // 2
    lane = lax.broadcasted_iota(jnp.int32, x.shape, 1)
    return jnp.where((lane & half) == 0, pltpu.roll(x, LANES - half, 1), pltpu.roll(x, half, 1))


AT_FOLD = AT_DILATIONS[1]
assert AT_DILATIONS == (1, AT_FOLD, AT_FOLD * AT_FOLD)
AT_Q_DILS = AT_DILATIONS[1:]
AT_KV_DILS = AT_DILATIONS
AT_GROUPS = 3 * AT_WIDTH // LANES


def _even_in_kernel(x_ref, xprev_ref, xnext_ref, nw_ref, w_ref, cos_ref, sin_ref, mup_ref, mun_ref,
                    *refs, blocks_per_seq):
    ps_ref, ga_ref, gb_ref = refs[:3]
    nq, nkv = len(AT_Q_DILS), len(AT_KV_DILS)
    q_refs = refs[3:3 + nq]
    k_refs = refs[3 + nq:3 + nq + nkv]
    v_refs = refs[3 + nq + nkv:3 + nq + 2 * nkv]
    stage = refs[3 + nq + 2 * nkv:]
    tmp_refs, fold_refs = stage[:AT_GROUPS], stage[AT_GROUPS:]
    tm = x_ref.shape[0]
    xn = _rms_rows(x_ref[...], nw_ref[...]).astype(bf16)
    cos = cos_ref[...]
    sin = sin_ref[...]

    pos = pl.program_id(0) % blocks_per_seq
    halo = jnp.concatenate([xprev_ref[...], xnext_ref[...]], axis=0)
    halo_p = _dot(_rms_rows(halo, nw_ref[...]).astype(bf16), w_ref[:, 0:EVEN_SHIFT])
    before = halo_p[SUBLANES - 1:SUBLANES, :] * (pos > 0).astype(f32)
    after = halo_p[SUBLANES:SUBLANES + 1, :] * (pos < blocks_per_seq - 1).astype(f32)
    rid = lax.broadcasted_iota(jnp.int32, (tm, 1), 0)

    def plain(ref, off):
        def put(t, _):
            ref[:, off:off + LANES] = t
        return put

    def shifted(ref, off):
        def put(t, _):
            cs = slice(off, off + LANES)
            prv = jnp.where(rid == 0, before[:, cs], pltpu.roll(t, 1, 0))
            nxt = jnp.where(rid == tm - 1, after[:, cs], pltpu.roll(t, tm - 1, 0))
            ref[:, cs] = t + mup_ref[:, cs] * (prv - t) + mun_ref[:, cs] * (nxt - t)
        return put

    def spread(outs, off, rotary, mul=None):
        def put(t, slot):
            if rotary:
                t = t * cos + _rope_partner(t) * sin
            if mul is not None:
                t = t * mul
            nat, fold = tmp_refs[slot], fold_refs[slot % len(fold_refs)]
            nat[...] = t
            per = tm // AT_FOLD
            by_dil = dict((dil, ref) for ref, dil in outs)
            if 1 in by_dil:
                by_dil[1][0, :, off:off + LANES] = t.astype(bf16)
            for r in range(AT_FOLD):
                x = nat[pl.ds(r, per, stride=AT_FOLD), :]
                by_dil[AT_FOLD][r, :, off:off + LANES] = x.astype(bf16)
                fold[r * per:(r + 1) * per, :] = x
            wide = AT_FOLD * AT_FOLD
            for r in range(wide):
                rows = pl.ds((r % AT_FOLD) * per + r // AT_FOLD, tm // wide, stride=AT_FOLD)
                by_dil[wide][r, :, off:off + LANES] = fold[rows, :].astype(bf16)
        return put

    cols = range(0, AT_WIDTH, LANES)
    dests = ([shifted(ps_ref, c) for c in range(0, EVEN_SHIFT, LANES)]
             + [plain(ga_ref, c) for c in range(0, RW_WIDTH, LANES)]
             + [spread(list(zip(q_refs, AT_Q_DILS)), c, True, AT_DH ** -0.5) for c in cols]
             + [spread(list(zip(k_refs, AT_KV_DILS)), c, True) for c in cols]
             + [spread(list(zip(v_refs, AT_KV_DILS)), c, False) for c in cols]
             + [plain(gb_ref, c) for c in cols])
    first_at = (EVEN_SHIFT + RW_WIDTH) // LANES
    tile = 2 * MXU_COLS
    for c0 in range(0, EVEN_COLS, tile):
        c1 = min(c0 + tile, EVEN_COLS)
        acc = _dot(xn, w_ref[:, c0:c1])
        for j in range((c1 - c0) // LANES):
            g = c0 // LANES + j
            dests[g](acc[:, j * LANES:(j + 1) * LANES], (g - first_at) % AT_GROUPS)


def _rope_tables(seq):
    inv = ROPE_THETA ** (-jnp.arange(0, AT_DH, 2, dtype=f32) / AT_DH)
    ang = jnp.arange(seq, dtype=f32)[:, None] * inv[None, :]
    cos, sin = jnp.cos(ang), jnp.sin(ang)
    return (jnp.concatenate([cos, cos, cos, cos], axis=-1),
            jnp.concatenate([-sin, sin, -sin, sin], axis=-1))


def _even_in(x2d, seq, norm_w, w_in_bf16, mu_prev, mu_next, block_rows=ROW_BLOCK):
    m = x2d.shape[0]
    tm = block_rows
    per_seq = seq // tm
    bsz = m // seq
    cos, sin = _rope_tables(seq)
    row = lambda i: (i, 0)
    full = lambda i: (0, 0)
    tab = lambda i: (i % per_seq, 0)
    res = lambda i: (i // per_seq, 0, i % per_seq, 0)
    dils = AT_Q_DILS + AT_KV_DILS + AT_KV_DILS
    halo = tm // SUBLANES
    prev = lambda i: (jnp.maximum(i * halo - 1, 0), 0)
    nxt = lambda i: (jnp.minimum((i + 1) * halo, m // SUBLANES - 1), 0)
    outs = pl.pallas_call(
        functools.partial(_even_in_kernel, blocks_per_seq=per_seq),
        grid=(m // tm,),
        in_specs=[
            pl.BlockSpec((tm, D_MODEL), row),
            pl.BlockSpec((SUBLANES, D_MODEL), prev),
            pl.BlockSpec((SUBLANES, D_MODEL), nxt),
            pl.BlockSpec((1, D_MODEL), full),
            pl.BlockSpec((D_MODEL, EVEN_COLS), full),
            pl.BlockSpec((tm, LANES), tab),
            pl.BlockSpec((tm, LANES), tab),
            pl.BlockSpec((1, EVEN_SHIFT), full),
            pl.BlockSpec((1, EVEN_SHIFT), full),
        ],
        out_specs=[
            pl.BlockSpec((tm, EVEN_SHIFT), row),
            pl.BlockSpec((tm, RW_WIDTH), row),
            pl.BlockSpec((tm, AT_WIDTH), row),
        ] + [pl.BlockSpec((None, d, tm // d, AT_WIDTH), res) for d in dils],
        out_shape=[
            jax.ShapeDtypeStruct((m, EVEN_SHIFT), f32),
            jax.ShapeDtypeStruct((m, RW_WIDTH), f32),
            jax.ShapeDtypeStruct((m, AT_WIDTH), f32),
        ] + [jax.ShapeDtypeStruct((bsz, d, seq // d, AT_WIDTH), bf16) for d in dils],
        scratch_shapes=[pltpu.VMEM((tm, LANES), f32)] * (AT_GROUPS + AT_FOLD),
        compiler_params=_cparams("parallel"),
        name="even_in",
    )(x2d, x2d, x2d, norm_w.reshape(1, D_MODEL), w_in_bf16, cos, sin,
      mu_prev.reshape(1, EVEN_SHIFT), mu_next.reshape(1, EVEN_SHIFT))
    nq, nkv = len(AT_Q_DILS), len(AT_KV_DILS)
    return (outs[0], outs[1], outs[2], outs[3:3 + nq], outs[3 + nq:3 + nq + nkv],
            outs[3 + nq + nkv:])


ODD_PAD_COLS = 2 * GLA_KEY + 2 * GLA_VAL + LANES


def _mid_kernel(x_ref, ya_ref, yb_ref, wo_ref, nw_ref, wi_ref,
                x1_ref, q_ref, k_ref, v_ref, g_ref, gl_ref):
    x1 = (x_ref[...] + _dot(ya_ref[...], wo_ref[0:RW_WIDTH, :])
          + _dot(yb_ref[...], wo_ref[RW_WIDTH:RW_WIDTH + AT_WIDTH, :]))
    x1_ref[...] = x1
    xn = _rms_rows(x1, nw_ref[...]).astype(bf16)
    c = 0
    for ref, width in ((q_ref, GLA_KEY), (k_ref, GLA_KEY), (v_ref, GLA_VAL), (g_ref, GLA_VAL),
                       (gl_ref, LANES)):
        for c0 in range(0, width, 512):
            c1 = min(c0 + 512, width)
            ref[:, c0:c1] = _dot(xn, wi_ref[:, c + c0:c + c1]).astype(ref.dtype)
        c += width


def _mid(x2d, ya, yb, w_out_bf16, norm_w, w_in_pad_bf16, block_rows=ROW_BLOCK):
    m = x2d.shape[0]
    tm = block_rows
    row = lambda i: (i, 0)
    full = lambda i: (0, 0)
    widths = (D_MODEL, GLA_KEY, GLA_KEY, GLA_VAL, GLA_VAL, LANES)
    return pl.pallas_call(
        _mid_kernel,
        grid=(m // tm,),
        in_specs=[
            pl.BlockSpec((tm, D_MODEL), row),
            pl.BlockSpec((tm, RW_WIDTH), row),
            pl.BlockSpec((tm, AT_WIDTH), row),
            pl.BlockSpec((RW_WIDTH + AT_WIDTH, D_MODEL), full),
            pl.BlockSpec((1, D_MODEL), full),
            pl.BlockSpec((D_MODEL, ODD_PAD_COLS), full),
        ],
        out_specs=[pl.BlockSpec((tm, w), row) for w in widths],
        out_shape=[jax.ShapeDtypeStruct((m, w), bf16 if i == 3 else f32)
                   for i, w in enumerate(widths)],
        compiler_params=_cparams("parallel"),
        name="mid",
    )(x2d, ya, yb, w_out_bf16, norm_w.reshape(1, D_MODEL), w_in_pad_bf16)


def _causal_mask(reverse):
    t = lax.broadcasted_iota(jnp.int32, (CHUNK, CHUNK), 0)
    s = lax.broadcasted_iota(jnp.int32, (CHUNK, CHUNK), 1)
    return s >= t if reverse else s <= t


def _row_to_col(row):
    n = row.shape[1]
    eye = (lax.broadcasted_iota(jnp.int32, (n, n), 0) == lax.broadcasted_iota(jnp.int32, (n, n), 1))
    return jnp.sum(jnp.where(eye, jnp.broadcast_to(row, (n, n)), 0.0), axis=1, keepdims=True)


def _gla_kernel(*refs, reverse, final, block_rows):
    if final:
        (q_ref, k_ref, v_ref, gl_ref, gup_ref, gb_ref, gate_ref, of_ref, nw_ref, x1_ref, wo_ref,
         fnw_ref, o_ref, st_ref, y_ref) = refs
    else:
        (q_ref, k_ref, v_ref, gl_ref, gup_ref, gb_ref, o_ref, st_ref) = refs
    d = 1 if reverse else 0

    @pl.when(pl.program_id(1) == 0)
    def _():
        st_ref[...] = jnp.zeros_like(st_ref)

    incl = _causal_mask(reverse)
    tri = incl.astype(bf16)
    gup = gup_ref[d]
    gbias = gb_ref[d:d + 1, :]
    nchunks = block_rows // CHUNK
    last = 0 if reverse else CHUNK - 1

    x = _dot(gl_ref[...].astype(bf16), gup) + gbias
    g = (jnp.minimum(x, 0.0) - jnp.log(1.0 + jnp.exp(-jnp.abs(x)))) / GLA_GATE_NORM
    chunk_rows = [slice(c * CHUNK, (c + 1) * CHUNK) for c in range(nchunks)]
    bcum = jnp.concatenate([_cumsum_rows(tri, g[rs]) for rs in chunk_rows], axis=0)
    b_last = [bcum[c * CHUNK + last:c * CHUNK + last + 1, :] for c in range(nchunks)]
    b_end = jnp.concatenate([jnp.broadcast_to(b, (CHUNK, GLA_KEY)) for b in b_last], axis=0)
    q = q_ref[...] * (GLA_HK ** -0.5)
    k = k_ref[...]
    q_dec = (q * jnp.exp(bcum)).astype(bf16)
    k_dec = (k * jnp.exp(-bcum)).astype(bf16)
    k_end = (k * jnp.exp(b_end - bcum)).astype(bf16)

    order = list(reversed(range(nchunks))) if reverse else list(range(nchunks))
    klanes = [slice(h * GLA_HK, (h + 1) * GLA_HK) for h in range(GLA_HEADS)]
    vlanes = [slice(h * GLA_HV, (h + 1) * GLA_HV) for h in range(GLA_HEADS)]
    units = [(c, h) for c in order for h in range(GLA_HEADS)]
    v = [v_ref[chunk_rows[c], vlanes[h]].astype(bf16) for c, h in units]
    att = [jnp.where(incl, _dot_nt(q_dec[chunk_rows[c], klanes[h]], k_dec[chunk_rows[c], klanes[h]]),
                     0.0).astype(bf16) for c, h in units]
    o_intra = [_dot(att[i], v[i]) for i in range(len(units))]
    kv = [_dot_tn(k_end[chunk_rows[c], klanes[h]], v[i]) for i, (c, h) in enumerate(units)]
    dcol = [_row_to_col(jnp.exp(b_last[c][:, klanes[h]])) for c, h in units]

    state = [st_ref[h] for h in range(GLA_HEADS)]
    for i, (c, h) in enumerate(units):
        rows = chunk_rows[c]
        o = o_intra[i] + _dot(q_dec[rows, klanes[h]], state[h].astype(bf16))
        state[h] = state[h] * dcol[i] + kv[i]
        if final:
            o = o + of_ref[rows, vlanes[h]]
            o = o * lax.rsqrt(jnp.mean(o * o, axis=-1, keepdims=True) + RMS_EPS) * nw_ref[...]
            y_ref[rows, vlanes[h]] = (o * _silu(gate_ref[rows, vlanes[h]])).astype(y_ref.dtype)
        else:
            o_ref[rows, vlanes[h]] = o
    for h in range(GLA_HEADS):
        st_ref[h] = state[h]
    if final:
        x2 = x1_ref[...] + _dot(y_ref[...], wo_ref[...])
        o_ref[...] = _rms_rows(x2, fnw_ref[...])


def _gla_scan(q, k, v, gl, gate_up_pad, gate_bias, *, reverse, gate=None, o_fwd=None, norm_w=None,
              x1=None, w_out_bf16=None, final_norm=None, block_rows=ROW_BLOCK):
    bsz, seq, _ = q.shape
    tb = block_rows
    nb = seq // tb
    final = reverse
    blk = (lambda b, j: (b, nb - 1 - j, 0)) if reverse else (lambda b, j: (b, j, 0))
    full2 = lambda b, j: (0, 0)
    full3 = lambda b, j: (0, 0, 0)
    in_specs = [
        pl.BlockSpec((None, tb, GLA_KEY), blk),
        pl.BlockSpec((None, tb, GLA_KEY), blk),
        pl.BlockSpec((None, tb, GLA_VAL), blk),
        pl.BlockSpec((None, tb, LANES), blk),
        pl.BlockSpec((2, LANES, GLA_KEY), full3),
        pl.BlockSpec((2, GLA_KEY), full2),
    ]
    args = [q, k, v, gl, gate_up_pad.astype(bf16), gate_bias]
    if final:
        in_specs += [
            pl.BlockSpec((None, tb, GLA_VAL), blk),
            pl.BlockSpec((None, tb, GLA_VAL), blk),
            pl.BlockSpec((1, GLA_HV), full2),
            pl.BlockSpec((None, tb, D_MODEL), blk),
            pl.BlockSpec((GLA_VAL, D_MODEL), full2),
            pl.BlockSpec((1, D_MODEL), full2),
        ]
        args += [gate, o_fwd, norm_w.reshape(1, GLA_HV), x1, w_out_bf16,
                 final_norm.reshape(1, D_MODEL)]
    scratch = [pltpu.VMEM((GLA_HEADS, GLA_HK, GLA_HV), f32)]
    if final:
        scratch.append(pltpu.VMEM((tb, GLA_VAL), bf16))
    return pl.pallas_call(
        functools.partial(_gla_kernel, reverse=reverse, final=final, block_rows=tb),
        grid=(bsz, nb),
        in_specs=in_specs,
        out_specs=pl.BlockSpec((None, tb, D_MODEL if final else GLA_VAL), blk),
        out_shape=jax.ShapeDtypeStruct((bsz, seq, D_MODEL if final else GLA_VAL), f32),
        scratch_shapes=scratch,
        compiler_params=_cparams("parallel", "arbitrary"),
        name="gla_bwd" if reverse else "gla_fwd",
    )(*args)


AT_NEG = -1e30
AT_QBLK = 128
AT_KBLK = 256
AT_UNROLL = 8


def _attn_kernel(q4, q16, k1, k4, k16, v1, v4, v16, g_ref, o_ref, m_s, l_s, acc_s, bias_s, tmp_s,
                 *, seq):
    lane = lax.broadcasted_iota(jnp.int32, (1, LANES), 1)
    left = lane < AT_DH
    fold_len = seq // AT_FOLD

    def fill_bias(offs, tq, nk, deltas):
        for d, delta in enumerate(deltas):
            bias_s[d, 0:tq, 0:nk] = jnp.where(jnp.abs(offs + delta) <= AT_SIDE, 0.0, AT_NEG)

    def run_blocks(nblocks, load_unit):
        def body(it, carry):
            units = [load_unit(it * AT_UNROLL + u) for u in range(AT_UNROLL)]
            heads = [(u, first) for u in range(AT_UNROLL) for first in (True, False)]
            scores = []
            for u, first in heads:
                qb, kb, _, bias, _, _ = units[u]
                qh = jnp.where(left if first else ~left, qb, jnp.zeros_like(qb))
                scores.append(_dot_nt(qh, kb) + bias)
            maxes = [jnp.max(s, axis=1, keepdims=True) for s in scores]
            probs = [jnp.exp(s - mh) for s, mh in zip(scores, maxes)]
            sums = [jnp.sum(p, axis=1, keepdims=True) for p in probs]
            outs = [_dot(p.astype(bf16), units[u][2]) for p, (u, _) in zip(probs, heads)]
            olds = [None if units[u][5] else
                    [(m_s[idx, :], l_s[idx, :], acc_s[idx, :]) for idx, _ in units[u][4]]
                    for u in range(AT_UNROLL)]
            for u in range(AT_UNROLL):
                m_new = jnp.where(left, maxes[2 * u], maxes[2 * u + 1])
                l_new = jnp.where(left, sums[2 * u], sums[2 * u + 1])
                a_new = jnp.where(left, outs[2 * u], outs[2 * u + 1])
                for pi, (idx, rows) in enumerate(units[u][4]):
                    mp, lp, ap = m_new[rows], l_new[rows], a_new[rows]
                    if units[u][5]:
                        m_s[idx, :] = mp
                        l_s[idx, :] = lp
                        acc_s[idx, :] = ap
                    else:
                        m_old, l_old, a_old = olds[u][pi]
                        m = jnp.maximum(m_old, mp)
                        w_old = jnp.exp(m_old - m)
                        w_new = jnp.exp(mp - m)
                        m_s[idx, :] = m
                        l_s[idx, :] = l_old * w_old + lp * w_new
                        acc_s[idx, :] = a_old * w_old + ap * w_new
            return carry

        lax.fori_loop(0, nblocks // AT_UNROLL, body, 0)

    def window(m0, sub, tq, nk):
        ks = jnp.clip(m0 - AT_SIDE, 0, sub - nk)
        return ks, (m0 - ks) // AT_SIDE

    tq, nk = AT_QBLK, AT_KBLK
    per = tq // AT_FOLD
    qi = lax.broadcasted_iota(jnp.int32, (tq, nk), 0)
    kj = lax.broadcasted_iota(jnp.int32, (tq, nk), 1)
    fill_bias((qi // per) + AT_FOLD * (qi % per) - kj, tq, nk, (0, AT_SIDE, 2 * AT_SIDE))

    def unit_d1(i):
        t0 = i * tq
        ks, bi = window(t0, seq, tq, nk)
        qrow = pl.ds(pl.multiple_of(i * per, per), per)
        qb = jnp.concatenate([q4[rho, qrow, :] for rho in range(AT_FOLD)], axis=0)
        krow = pl.ds(pl.multiple_of(ks, AT_SIDE), nk)
        state = [(pl.ds(pl.multiple_of(rho * fold_len + i * per, per), per),
                  slice(rho * per, (rho + 1) * per)) for rho in range(AT_FOLD)]
        return qb, k1[0, krow, :], v1[0, krow, :], bias_s[bi], state, True

    run_blocks(seq // tq, unit_d1)

    sub = fold_len
    fill_bias(qi - kj, tq, nk, (0, AT_SIDE, 2 * AT_SIDE))
    nblk = sub // tq

    def unit_fold(i):
        r = i // nblk
        m0 = (i % nblk) * tq
        ks, bi = window(m0, sub, tq, nk)
        qrow = pl.ds(pl.multiple_of(m0, tq), tq)
        krow = pl.ds(pl.multiple_of(ks, AT_SIDE), nk)
        state = [(pl.ds(pl.multiple_of(r * sub + m0, tq), tq), slice(0, tq))]
        return q4[r, qrow, :], k4[r, krow, :], v4[r, krow, :], bias_s[bi], state, False

    run_blocks(AT_FOLD * nblk, unit_fold)

    dil = AT_DILATIONS[-1]
    sub = seq // dil
    tq, nk = min(AT_QBLK, sub), min(AT_KBLK, sub)
    fill_bias((qi - kj)[0:tq, 0:nk], tq, nk, (0, AT_SIDE, nk - tq))
    nblk = sub // tq

    def unit_wide(i):
        r = i // nblk
        m0 = (i % nblk) * tq
        ks, bi = window(m0, sub, tq, nk)
        qrow = pl.ds(pl.multiple_of(m0, AT_SIDE), tq)
        krow = pl.ds(pl.multiple_of(ks, AT_SIDE), nk)
        srow = (r % AT_FOLD) * fold_len + r // AT_FOLD + (dil // AT_FOLD) * m0
        state = [(pl.ds(srow, tq, stride=dil // AT_FOLD), slice(0, tq))]
        return (q16[r, qrow, :], k16[r, krow, :], v16[r, krow, :], bias_s[bi, 0:tq, 0:nk], state,
                False)

    run_blocks(dil * nblk, unit_wide)

    rows = tmp_s.shape[0]
    per_fold = rows // AT_FOLD

    def finish(i, carry):
        for rho in range(AT_FOLD):
            idx = pl.ds(pl.multiple_of(rho * fold_len + i * per_fold, per_fold), per_fold)
            tmp_s[pl.ds(rho, per_fold, stride=AT_FOLD), :] = acc_s[idx, :] / l_s[idx, :]
        out = pl.ds(pl.multiple_of(i * rows, rows), rows)
        o_ref[out, :] = (tmp_s[...] * _silu(g_ref[out, :])).astype(o_ref.dtype)
        return carry

    lax.fori_loop(0, seq // rows, finish, 0)


def _attention(qs, ks, vs, gate):
    bsz, seq, _ = gate.shape
    pairs = AT_WIDTH // LANES
    once = pl.Buffered(1)
    col = lambda b, p: (b, 0, p)
    res = lambda b, p: (b, 0, 0, p)
    copies = list(qs) + list(ks) + list(vs)
    return pl.pallas_call(
        functools.partial(_attn_kernel, seq=seq),
        grid=(bsz, pairs),
        in_specs=[pl.BlockSpec((None, t.shape[1], t.shape[2], LANES), res) for t in copies]
        + [pl.BlockSpec((None, seq, LANES), col, pipeline_mode=once)],
        out_specs=pl.BlockSpec((None, seq, LANES), col),
        out_shape=jax.ShapeDtypeStruct((bsz, seq, AT_WIDTH), bf16),
        scratch_shapes=([pltpu.VMEM((seq, LANES), f32)] * 3
                        + [pltpu.VMEM((3, AT_QBLK, AT_KBLK), f32),
                           pltpu.VMEM((min(seq, 512), LANES), f32)]),
        compiler_params=_cparams("parallel", "parallel"),
        name="dilated_attn",
    )(*copies, gate)


RW_GROUP = 2
RW_GW = RW_GROUP * RW_DH
RW_NGROUPS = RW_HEADS // RW_GROUP
RW_WAVE = 4


def _head_stack(x, masks):
    return jnp.concatenate([jnp.where(mk, x, jnp.zeros_like(x)) for mk in masks], axis=0)


def _pair_t(x):
    xt = x.T
    return jnp.concatenate([xt[h * RW_DH:(h + 1) * RW_DH] for h in range(RW_GROUP)], axis=1)


def _unit_lower_inverse_stages(w, src, dst, masks):
    steps = int(math.log2(CHUNK)) - 1

    def start():
        n, lanes = w[src][0].shape
        eye = (lax.broadcasted_iota(jnp.int32, (n, lanes), 1) % n
               == lax.broadcasted_iota(jnp.int32, (n, lanes), 0)).astype(f32)
        w["_t"] = [eye + a for a in w[src]]
        qs = [a.astype(bf16) for a in w[src]]
        w["_q"] = [_dot(q, _head_stack(q, masks)).astype(bf16) for q in qs]

    def double():
        n = w[src][0].shape[0]
        both = [_dot(jnp.concatenate([q, t.astype(bf16)], axis=0), _head_stack(q, masks))
                for q, t in zip(w["_q"], w["_t"])]
        w["_q"] = [x[:n].astype(bf16) for x in both]
        w["_t"] = [t + x[n:] for t, x in zip(w["_t"], both)]

    def finish():
        w[dst] = [_head_stack((t + _dot(t.astype(bf16), _head_stack(q, masks))).astype(bf16), masks)
                  for q, t in zip(w["_q"], w["_t"])]

    return [start] + [double] * (steps - 1) + [finish]


def _rwkv_kernel(*refs, reverse, final, block_rows):
    sh_ref, w0_ref, wup_ref, a0_ref, aup_ref, kk_ref, ka_ref, seg_ref = refs[:8]
    if final:
        rk_ref, gnw_ref, gnb_ref, ga_ref, yf_ref, o_ref, st_ref = refs[8:]
    else:
        o_ref, st_ref = refs[8:]
    d = 1 if reverse else 0
    tb = block_rows

    @pl.when(pl.program_id(1) == 0)
    def _():
        st_ref[...] = jnp.zeros_like(st_ref)

    tri = _causal_mask(reverse).astype(bf16)
    gi = lax.broadcasted_iota(jnp.int32, (2 * CHUNK, 2 * RW_GW), 0)
    gj = lax.broadcasted_iota(jnp.int32, (2 * CHUNK, 2 * RW_GW), 1)
    ahead = (gi % CHUNK) - (gj % CHUNK)
    ahead = -ahead if reverse else ahead
    keep = ahead < jnp.where(gj < RW_GW, 0, 1)
    lane = lax.broadcasted_iota(jnp.int32, (1, 2 * RW_GW), 1)
    masks2 = [(lane // RW_DH) % RW_GROUP == h for h in range(RW_GROUP)]
    masks = [mk[:, 0:RW_GW] for mk in masks2]
    seg = seg_ref[...]
    last = 0 if reverse else CHUNK - 1
    nchunks = tb // CHUNK
    w_lo, a_lo = 3 * RW_WIDTH, 3 * RW_WIDTH + RW_LORA

    def lr_gate(a_lat, dd):
        return _sigmoid(a0_ref[dd:dd + 1, :] + _dot(a_lat.astype(bf16), aup_ref[dd]))

    r = sh_ref[:, 0:RW_WIDTH]
    k = sh_ref[:, RW_WIDTH:2 * RW_WIDTH]
    v = sh_ref[:, 2 * RW_WIDTH:3 * RW_WIDTH]
    w_lat = sh_ref[:, w_lo:w_lo + RW_LORA]
    a_lat = sh_ref[:, a_lo:a_lo + RW_LORA]
    w_log = -RW_DECAY_SCALE * _sigmoid(
        w0_ref[d:d + 1, :] + _dot(jnp.tanh(w_lat).astype(bf16), wup_ref[d]))
    lr = lr_gate(a_lat, d)
    kk = k * kk_ref[...]
    kk = kk * lax.rsqrt(jnp.maximum(_dot_seg(kk * kk, seg), 1e-24))
    k_dir = k * (1.0 + (lr - 1.0) * ka_ref[...])
    chunk_rows = [slice(c * CHUNK, (c + 1) * CHUNK) for c in range(nchunks)]
    cum = jnp.concatenate([_cumsum_rows(tri, w_log[rs]) for rs in chunk_rows], axis=0)
    grow = jnp.exp(-cum)
    a_t = -kk * jnp.exp(cum - w_log)
    b_t = kk * lr * grow
    k_t = k_dir * grow
    r_t = r * jnp.exp(cum)
    a_b, b_b, k_b, r_b = (t.astype(bf16) for t in (a_t, b_t, k_t, r_t))

    order = list(reversed(range(nchunks))) if reverse else list(range(nchunks))
    lanes = [slice(g * RW_GW, (g + 1) * RW_GW) for g in range(RW_NGROUPS)]
    state = [st_ref[g] for g in range(RW_NGROUPS)]
    y_rows = {}

    def wave_stages(chunks):
        units = [(c, g) for c in chunks for g in range(RW_NGROUPS)]
        n = range(len(units))
        w = {}

        def stacked():
            sl = lambda x, i: x[chunk_rows[units[i][0]], lanes[units[i][1]]]
            w["bk"] = [jnp.concatenate([sl(b_b, i), sl(k_b, i)], axis=0) for i in n]
            w["ar_bd"] = [jnp.concatenate([_head_stack(sl(a_b, i), masks),
                                           _head_stack(sl(r_b, i), masks)], axis=0) for i in n]
            w["b_bd"] = [_head_stack(sl(b_b, i), masks) for i in n]
            w["k_bd"] = [_head_stack(sl(k_b, i), masks) for i in n]
            w["a_t"] = [_pair_t(sl(a_t, i)).astype(bf16) for i in n]
            w["r_bd"] = [_head_stack(_pair_t(sl(r_t, i)).astype(bf16), masks) for i in n]
            w["v_t"] = [_pair_t(sl(v, i)).astype(bf16) for i in n]

        def grams():
            gram = [jnp.where(keep, _dot_nt(w["bk"][i], w["ar_bd"][i]), 0.0) for i in n]
            w["ab_t"] = [t[0:CHUNK, 0:RW_GW] for t in gram]
            w["ak_rk_bd"] = [_head_stack(t[CHUNK:, :].astype(bf16), masks2) for t in gram]
            w["rb_b_bd"] = [jnp.concatenate([_head_stack(gram[i][0:CHUNK, RW_GW:].astype(bf16), masks),
                                             w["b_bd"][i]], axis=1) for i in n]

        inverse = _unit_lower_inverse_stages(w, "ab_t", "t_inv_bd", masks)

        def value_products():
            x = [_dot(w["v_t"][i], w["ak_rk_bd"][i]) for i in n]
            w["av_t"] = [t[:, 0:RW_GW].astype(bf16) for t in x]
            w["y_local_t"] = [t[:, RW_GW:] for t in x]
            w["vk"] = [_dot(w["v_t"][i], w["k_bd"][i]) for i in n]

        def apply_inverse():
            x = [_dot(jnp.concatenate([w["av_t"][i], w["a_t"][i]], axis=0), w["t_inv_bd"][i])
                 for i in n]
            w["z_t"] = [t[0:RW_DH] for t in x]
            w["wm_r_bd"] = [jnp.concatenate([_head_stack(x[i][RW_DH:].astype(bf16), masks),
                                             w["r_bd"][i]], axis=1) for i in n]

        def recur(pos_c, c):
            def new_state():
                p_end = jnp.exp(cum[c * CHUNK + last:c * CHUNK + last + 1, :])
                w["y_t"] = []
                for g in range(RW_NGROUPS):
                    i = pos_c * RW_NGROUPS + g
                    s0 = state[g]
                    x = _dot(s0.astype(bf16), w["wm_r_bd"][i])
                    u_t = (x[:, 0:RW_GW] + w["z_t"][i]).astype(bf16)
                    x2 = _dot(u_t, w["rb_b_bd"][i])
                    state[g] = (s0 + x2[:, RW_GW:] + w["vk"][i]) * p_end[:, lanes[g]]
                    w["y_t"].append(x[:, RW_GW:] + x2[:, 0:RW_GW] + w["y_local_t"][i])

            def outputs():
                y_rows[c] = jnp.concatenate([_pair_t(t) for t in w["y_t"]], axis=1)

            return [new_state, outputs]

        independent = [stacked, grams] + inverse + [value_products, apply_inverse]
        recurrence = [f for pos_c, c in enumerate(chunks) for f in recur(pos_c, c)]
        return independent, recurrence

    waves = [order[i:i + RW_WAVE] for i in range(0, nchunks, RW_WAVE)]
    pending = []
    for chunks in waves:
        independent, recurrence = wave_stages(chunks)
        every = max(1, len(independent) // (len(pending) + 1)) if pending else 0
        for si, stage in enumerate(independent):
            stage()
            if pending and (si + 1) % every == 0:
                pending.pop(0)()
        while pending:
            pending.pop(0)()
        pending = recurrence
    while pending:
        pending.pop(0)()
    for g in range(RW_NGROUPS):
        st_ref[g] = state[g]
    y_all = jnp.concatenate([y_rows[c] for c in range(nchunks)], axis=0)

    if not final:
        o_ref[...] = y_all
    else:
        y_all = y_all + yf_ref[...]
        mean = _dot_seg(y_all, seg) * (1.0 / RW_DH)
        cen = y_all - mean
        var = _dot_seg(cen * cen, seg) * (1.0 / RW_DH)
        yn = cen * lax.rsqrt(var + RW_GN_EPS) * gnw_ref[...] + gnb_ref[...]
        k_other = k * (1.0 + (lr_gate(a_lat, 1 - d) - 1.0) * ka_ref[...])
        rk = _dot_seg(r * (k_dir + k_other) * rk_ref[...], seg)
        o_ref[...] = ((yn + rk * v) * _silu(ga_ref[...])).astype(o_ref.dtype)


def _segment_ones():
    h = jnp.arange(LANES) // RW_DH
    return (h[:, None] == h[None, :]).astype(bf16)


def _rwkv_scan(ps, w0, w_up, a0, a_up, k_k, k_a, *, reverse,
               r_k=None, gn_w=None, gn_b=None, gate=None, y_fwd=None, block_rows=2 * ROW_BLOCK):
    bsz, seq, _ = ps.shape
    tb = block_rows
    nb = seq // tb
    final = reverse
    blk = (lambda b, j: (b, nb - 1 - j, 0)) if reverse else (lambda b, j: (b, j, 0))
    full2 = lambda b, j: (0, 0)
    full3 = lambda b, j: (0, 0, 0)
    vec = lambda n: pl.BlockSpec((1, n), full2)
    in_specs = [
        pl.BlockSpec((None, tb, EVEN_SHIFT), blk),
        pl.BlockSpec((2, RW_WIDTH), full2),
        pl.BlockSpec((2, RW_LORA, RW_WIDTH), full3),
        pl.BlockSpec((2, RW_WIDTH), full2),
        pl.BlockSpec((2, RW_LORA, RW_WIDTH), full3),
        vec(RW_WIDTH), vec(RW_WIDTH),
        pl.BlockSpec((LANES, LANES), full2),
    ]
    row = lambda t: t.reshape(1, -1)
    args = [ps, w0, w_up.astype(bf16), a0, a_up.astype(bf16), row(k_k), row(k_a), _segment_ones()]
    if final:
        in_specs += [vec(RW_WIDTH), vec(RW_WIDTH), vec(RW_WIDTH),
                     pl.BlockSpec((None, tb, RW_WIDTH), blk),
                     pl.BlockSpec((None, tb, RW_WIDTH), blk)]
        args += [row(r_k), row(gn_w), row(gn_b), gate, y_fwd]
    return pl.pallas_call(
        functools.partial(_rwkv_kernel, reverse=reverse, final=final, block_rows=tb),
        grid=(bsz, nb),
        in_specs=in_specs,
        out_specs=pl.BlockSpec((None, tb, RW_WIDTH), blk),
        out_shape=jax.ShapeDtypeStruct((bsz, seq, RW_WIDTH), bf16 if final else f32),
        scratch_shapes=[pltpu.VMEM((RW_NGROUPS, RW_DH, RW_GW), f32)],
        compiler_params=_cparams("parallel", "arbitrary"),
        name="rwkv_bwd" if reverse else "rwkv_fwd",
    )(*args)


def _trunk(x, p):
    bsz, seq, _ = x.shape
    tokens = bsz * seq
    x2d = x.reshape(tokens, D_MODEL)
    seq3 = lambda t: t.reshape(bsz, seq, t.shape[-1])
    flat = lambda t: t.reshape(tokens, t.shape[-1])

    ps, ga, gb, qs, ks, vs = _even_in(x2d, seq, p["even_norm"], p["even_w_in"], p["mu_prev"],
                                      p["mu_next"])
    rw = (seq3(ps), p["w0"], p["w_up"], p["a0"], p["a_up"], p["k_k"], p["k_a"])
    y_fwd = _rwkv_scan(*rw, reverse=False)
    ya = _rwkv_scan(*rw, reverse=True, r_k=p["r_k"], gn_w=p["gn_w"], gn_b=p["gn_b"],
                    gate=seq3(ga), y_fwd=y_fwd)
    yb = _attention(qs, ks, vs, seq3(gb))

    x1, q, k, v, gate, gate_lat = _mid(x2d, flat(ya), flat(yb), p["even_w_out"], p["odd_norm"],
                                       p["odd_w_in"])
    gla = (seq3(q), seq3(k), seq3(v), seq3(gate_lat), p["gate_up"], p["gate_bias"])
    o_fwd = _gla_scan(*gla, reverse=False)
    return _gla_scan(*gla, reverse=True, gate=seq3(gate), o_fwd=o_fwd, norm_w=p["gla_norm"],
                     x1=seq3(x1), w_out_bf16=p["odd_w_out"], final_norm=p["final_norm"])


def _prepare(even_norm, even_w_in, even_mu_prev, even_mu_next, rwkv_w0, rwkv_w_up, rwkv_a0,
             rwkv_a_up, rwkv_k_k, rwkv_k_a, rwkv_r_k, rwkv_gn_w, rwkv_gn_b, even_w_out, odd_norm,
             odd_w_in, gla_gate_up, gla_gate_bias, gla_norm, odd_w_out, final_norm):
    wi = odd_w_in[0]
    lat0 = 2 * GLA_KEY + GLA_VAL
    lat = jnp.pad(wi[:, lat0:lat0 + GLA_RANK], ((0, 0), (0, LANES - GLA_RANK)))
    odd_in = jnp.concatenate([wi[:, :lat0], wi[:, lat0 + GLA_RANK:], lat], axis=1)
    return {
        "even_norm": even_norm[0], "even_w_in": even_w_in[0].astype(bf16),
        "mu_prev": even_mu_prev[0], "mu_next": even_mu_next[0],
        "w0": rwkv_w0[0], "w_up": rwkv_w_up[0], "a0": rwkv_a0[0], "a_up": rwkv_a_up[0],
        "k_k": rwkv_k_k[0], "k_a": rwkv_k_a[0], "r_k": rwkv_r_k[0],
        "gn_w": rwkv_gn_w[0], "gn_b": rwkv_gn_b[0],
        "even_w_out": even_w_out[0].astype(bf16),
        "odd_norm": odd_norm[0], "odd_w_in": odd_in.astype(bf16),
        "gate_up": jnp.pad(gla_gate_up[0], ((0, 0), (0, LANES - GLA_RANK), (0, 0))),
        "gate_bias": gla_gate_bias[0], "gla_norm": gla_norm[0],
        "odd_w_out": odd_w_out[0].astype(bf16), "final_norm": final_norm,
    }


def kernel(x_prompt, x_sample, even_norm, even_w_in, even_mu_prev, even_mu_next, rwkv_w0, rwkv_w_up,
           rwkv_a0, rwkv_a_up, rwkv_k_k, rwkv_k_a, rwkv_r_k, rwkv_gn_w, rwkv_gn_b, even_w_out,
           odd_norm, odd_w_in, gla_gate_up, gla_gate_bias, gla_norm, odd_w_out, final_norm):
    p = _prepare(even_norm, even_w_in, even_mu_prev, even_mu_next, rwkv_w0, rwkv_w_up, rwkv_a0,
                 rwkv_a_up, rwkv_k_k, rwkv_k_a, rwkv_r_k, rwkv_gn_w, rwkv_gn_b, even_w_out,
                 odd_norm, odd_w_in, gla_gate_up, gla_gate_bias, gla_norm, odd_w_out, final_norm)
    return (_trunk(x_prompt, p), _trunk(x_sample, p))
```

```python
import functools
import math

import jax
import jax.numpy as jnp
from jax import lax
from jax.experimental import pallas as pl
from jax.experimental.pallas import tpu as pltpu

f32 = jnp.float32
bf16 = jnp.bfloat16

D_MODEL = 1024
RMS_EPS = 1e-6

RW_HEADS = 8
RW_DH = 64
RW_WIDTH = RW_HEADS * RW_DH
RW_LORA = 64
RW_DECAY_SCALE = 0.6065306597126334
RW_GN_EPS = 64e-5
AT_HEADS = 8
AT_DH = 64
AT_WIDTH = AT_HEADS * AT_DH
AT_SIDE = 64
AT_DILATIONS = (1, 4, 16)
ROPE_THETA = 10000.0
EVEN_SHIFT = 3 * RW_WIDTH + 2 * RW_LORA
EVEN_COLS = EVEN_SHIFT + RW_WIDTH + 4 * AT_WIDTH

GLA_HEADS = 4
GLA_KEY = 512
GLA_VAL = 1024
GLA_HK = GLA_KEY // GLA_HEADS
GLA_HV = GLA_VAL // GLA_HEADS
GLA_RANK = 16
GLA_GATE_NORM = 16.0

CHUNK = 64
LANES = 128
SUBLANES = 8
MXU_COLS = 256
ROW_BLOCK = 512
VMEM_LIMIT = 56 * 1024 * 1024

_NT = (((1,), (1,)), ((), ()))
_TN = (((0,), (0,)), ((), ()))


def _dot(a, b):
    return jnp.dot(a, b, preferred_element_type=f32)


def _dot_nt(a, b):
    return lax.dot_general(a, b, _NT, preferred_element_type=f32)


def _dot_tn(a, b):
    return lax.dot_general(a, b, _TN, preferred_element_type=f32)


def _split2(x):
    hi = x.astype(bf16)
    lo = (x - hi.astype(f32)).astype(bf16)
    return hi, lo


def _dot_seg(x, e):
    xb = x.astype(bf16)
    return jnp.concatenate([_dot(xb[:, c:c + LANES], e) for c in range(0, x.shape[1], LANES)], axis=1)


def _cumsum_rows(tri, x):
    hi, lo = _split2(x)
    return _dot(tri, hi) + _dot(tri, lo)


def _sigmoid(x):
    return 0.5 * (jnp.tanh(0.5 * x) + 1.0)


def _silu(x):
    h = 0.5 * x
    return h + h * jnp.tanh(h)


def _rms_rows(x, w):
    return x * lax.rsqrt(jnp.mean(x * x, axis=-1, keepdims=True) + RMS_EPS) * w


def _cparams(*sem):
    return pltpu.CompilerParams(dimension_semantics=sem, vmem_limit_bytes=VMEM_LIMIT)


def _rope_partner(x):
    half = AT_DH // 2
    lane = lax.broadcasted_iota(jnp.int32, x.shape, 1)
    return jnp.where((lane & half) == 0, pltpu.roll(x, LANES - half, 1), pltpu.roll(x, half, 1))


AT_FOLD = AT_DILATIONS[1]
assert AT_DILATIONS == (1, AT_FOLD, AT_FOLD * AT_FOLD)
AT_Q_DILS = AT_DILATIONS[1:]
AT_KV_DILS = AT_DILATIONS
AT_GROUPS = 3 * AT_WIDTH // LANES


def _even_in_kernel(x_ref, xprev_ref, xnext_ref, nw_ref, w_ref, cos_ref, sin_ref, mup_ref, mun_ref,
                    *refs, blocks_per_seq):
    ps_ref, ga_ref, gb_ref = refs[:3]
    nq, nkv = len(AT_Q_DILS), len(AT_KV_DILS)
    q_refs = refs[3:3 + nq]
    k_refs = refs[3 + nq:3 + nq + nkv]
    v_refs = refs[3 + nq + nkv:3 + nq + 2 * nkv]
    stage = refs[3 + nq + 2 * nkv:]
    tmp_refs, fold_refs = stage[:AT_GROUPS], stage[AT_GROUPS:]
    tm = x_ref.shape[0]
    xn = _rms_rows(x_ref[...], nw_ref[...]).astype(bf16)
    cos = cos_ref[...]
    sin = sin_ref[...]

    pos = pl.program_id(0) % blocks_per_seq
    halo = jnp.concatenate([xprev_ref[...], xnext_ref[...]], axis=0)
    xn_halo = jnp.concatenate([xn, _rms_rows(halo, nw_ref[...]).astype(bf16)], axis=0)
    has_prev = (pos > 0).astype(f32)
    has_next = (pos < blocks_per_seq - 1).astype(f32)
    rid = lax.broadcasted_iota(jnp.int32, (tm, 1), 0)

    def plain(ref, off):
        def put(t, _):
            ref[:, off:off + LANES] = t[0:tm]
        return put

    def shifted(ref, off):
        def put(t, _):
            cs = slice(off, off + LANES)
            before = t[tm + SUBLANES - 1:tm + SUBLANES] * has_prev
            after = t[tm + SUBLANES:tm + SUBLANES + 1] * has_next
            t = t[0:tm]
            prv = jnp.where(rid == 0, before, pltpu.roll(t, 1, 0))
            nxt = jnp.where(rid == tm - 1, after, pltpu.roll(t, tm - 1, 0))
            ref[:, cs] = t + mup_ref[:, cs] * (prv - t) + mun_ref[:, cs] * (nxt - t)
        return put

    def spread(outs, off, rotary, mul=None):
        def put(t, slot):
            t = t[0:tm]
            if rotary:
                t = t * cos + _rope_partner(t) * sin
            if mul is not None:
                t = t * mul
            nat, fold = tmp_refs[slot], fold_refs[slot % len(fold_refs)]
            nat[...] = t
            per = tm // AT_FOLD
            by_dil = dict((dil, ref) for ref, dil in outs)
            if 1 in by_dil:
                by_dil[1][0, :, off:off + LANES] = t.astype(bf16)
            for r in range(AT_FOLD):
                x = nat[pl.ds(r, per, stride=AT_FOLD), :]
                by_dil[AT_FOLD][r, :, off:off + LANES] = x.astype(bf16)
                fold[r * per:(r + 1) * per, :] = x
            wide = AT_FOLD * AT_FOLD
            for r in range(wide):
                rows = pl.ds((r % AT_FOLD) * per + r // AT_FOLD, tm // wide, stride=AT_FOLD)
                by_dil[wide][r, :, off:off + LANES] = fold[rows, :].astype(bf16)
        return put

    cols = range(0, AT_WIDTH, LANES)
    dests = ([shifted(ps_ref, c) for c in range(0, EVEN_SHIFT, LANES)]
             + [plain(ga_ref, c) for c in range(0, RW_WIDTH, LANES)]
             + [spread(list(zip(q_refs, AT_Q_DILS)), c, True, AT_DH ** -0.5) for c in cols]
             + [spread(list(zip(k_refs, AT_KV_DILS)), c, True) for c in cols]
             + [spread(list(zip(v_refs, AT_KV_DILS)), c, False) for c in cols]
             + [plain(gb_ref, c) for c in cols])
    first_at = (EVEN_SHIFT + RW_WIDTH) // LANES
    tile = 2 * MXU_COLS
    for c0 in range(0, EVEN_COLS, tile):
        c1 = min(c0 + tile, EVEN_COLS)
        acc = _dot(xn_halo if c0 < EVEN_SHIFT else xn, w_ref[:, c0:c1])
        for j in range((c1 - c0) // LANES):
            g = c0 // LANES + j
            dests[g](acc[:, j * LANES:(j + 1) * LANES], (g - first_at) % AT_GROUPS)


def _rope_tables(seq):
    inv = ROPE_THETA ** (-jnp.arange(0, AT_DH, 2, dtype=f32) / AT_DH)
    ang = jnp.arange(seq, dtype=f32)[:, None] * inv[None, :]
    cos, sin = jnp.cos(ang), jnp.sin(ang)
    return (jnp.concatenate([cos, cos, cos, cos], axis=-1),
            jnp.concatenate([-sin, sin, -sin, sin], axis=-1))


def _even_in(x2d, seq, norm_w, w_in_bf16, mu_prev, mu_next, block_rows=ROW_BLOCK):
    m = x2d.shape[0]
    tm = block_rows
    per_seq = seq // tm
    bsz = m // seq
    cos, sin = _rope_tables(seq)
    row = lambda i: (i, 0)
    full = lambda i: (0, 0)
    tab = lambda i: (i % per_seq, 0)
    res = lambda i: (i // per_seq, 0, i % per_seq, 0)
    dils = AT_Q_DILS + AT_KV_DILS + AT_KV_DILS
    halo = tm // SUBLANES
    prev = lambda i: (jnp.maximum(i * halo - 1, 0), 0)
    nxt = lambda i: (jnp.minimum((i + 1) * halo, m // SUBLANES - 1), 0)
    outs = pl.pallas_call(
        functools.partial(_even_in_kernel, blocks_per_seq=per_seq),
        grid=(m // tm,),
        in_specs=[
            pl.BlockSpec((tm, D_MODEL), row),
            pl.BlockSpec((SUBLANES, D_MODEL), prev),
            pl.BlockSpec((SUBLANES, D_MODEL), nxt),
            pl.BlockSpec((1, D_MODEL), full),
            pl.BlockSpec((D_MODEL, EVEN_COLS), full),
            pl.BlockSpec((tm, LANES), tab),
            pl.BlockSpec((tm, LANES), tab),
            pl.BlockSpec((1, EVEN_SHIFT), full),
            pl.BlockSpec((1, EVEN_SHIFT), full),
        ],
        out_specs=[
            pl.BlockSpec((tm, EVEN_SHIFT), row),
            pl.BlockSpec((tm, RW_WIDTH), row),
            pl.BlockSpec((tm, AT_WIDTH), row),
        ] + [pl.BlockSpec((None, d, tm // d, AT_WIDTH), res) for d in dils],
        out_shape=[
            jax.ShapeDtypeStruct((m, EVEN_SHIFT), f32),
            jax.ShapeDtypeStruct((m, RW_WIDTH), f32),
            jax.ShapeDtypeStruct((m, AT_WIDTH), f32),
        ] + [jax.ShapeDtypeStruct((bsz, d, seq // d, AT_WIDTH), bf16) for d in dils],
        scratch_shapes=[pltpu.VMEM((tm, LANES), f32)] * (AT_GROUPS + AT_FOLD),
        compiler_params=_cparams("parallel"),
        name="even_in",
    )(x2d, x2d, x2d, norm_w.reshape(1, D_MODEL), w_in_bf16, cos, sin,
      mu_prev.reshape(1, EVEN_SHIFT), mu_next.reshape(1, EVEN_SHIFT))
    nq, nkv = len(AT_Q_DILS), len(AT_KV_DILS)
    return (outs[0], outs[1], outs[2], outs[3:3 + nq], outs[3 + nq:3 + nq + nkv],
            outs[3 + nq + nkv:])


ODD_PAD_COLS = 2 * GLA_KEY + 2 * GLA_VAL + LANES


def _mid_kernel(x_ref, ya_ref, yb_ref, wo_ref, nw_ref, wi_ref,
                x1_ref, q_ref, k_ref, v_ref, g_ref, gl_ref):
    x1 = (x_ref[...] + _dot(ya_ref[...], wo_ref[0:RW_WIDTH, :])
          + _dot(yb_ref[...], wo_ref[RW_WIDTH:RW_WIDTH + AT_WIDTH, :]))
    x1_ref[...] = x1
    xn = _rms_rows(x1, nw_ref[...]).astype(bf16)
    c = 0
    for ref, width in ((q_ref, GLA_KEY), (k_ref, GLA_KEY), (v_ref, GLA_VAL), (g_ref, GLA_VAL),
                       (gl_ref, LANES)):
        for c0 in range(0, width, 512):
            c1 = min(c0 + 512, width)
            ref[:, c0:c1] = _dot(xn, wi_ref[:, c + c0:c + c1]).astype(ref.dtype)
        c += width


def _mid(x2d, ya, yb, w_out_bf16, norm_w, w_in_pad_bf16, block_rows=ROW_BLOCK):
    m = x2d.shape[0]
    tm = block_rows
    row = lambda i: (i, 0)
    full = lambda i: (0, 0)
    widths = (D_MODEL, GLA_KEY, GLA_KEY, GLA_VAL, GLA_VAL, LANES)
    return pl.pallas_call(
        _mid_kernel,
        grid=(m // tm,),
        in_specs=[
            pl.BlockSpec((tm, D_MODEL), row),
            pl.BlockSpec((tm, RW_WIDTH), row),
            pl.BlockSpec((tm, AT_WIDTH), row),
            pl.BlockSpec((RW_WIDTH + AT_WIDTH, D_MODEL), full),
            pl.BlockSpec((1, D_MODEL), full),
            pl.BlockSpec((D_MODEL, ODD_PAD_COLS), full),
        ],
        out_specs=[pl.BlockSpec((tm, w), row) for w in widths],
        out_shape=[jax.ShapeDtypeStruct((m, w), bf16 if i == 3 else f32)
                   for i, w in enumerate(widths)],
        compiler_params=_cparams("parallel"),
        name="mid",
    )(x2d, ya, yb, w_out_bf16, norm_w.reshape(1, D_MODEL), w_in_pad_bf16)


def _causal_mask(reverse):
    t = lax.broadcasted_iota(jnp.int32, (CHUNK, CHUNK), 0)
    s = lax.broadcasted_iota(jnp.int32, (CHUNK, CHUNK), 1)
    return s >= t if reverse else s <= t


def _row_to_col(row):
    n = row.shape[1]
    eye = (lax.broadcasted_iota(jnp.int32, (n, n), 0) == lax.broadcasted_iota(jnp.int32, (n, n), 1))
    return jnp.sum(jnp.where(eye, jnp.broadcast_to(row, (n, n)), 0.0), axis=1, keepdims=True)


def _gla_kernel(*refs, reverse, final, block_rows):
    if final:
        (q_ref, k_ref, v_ref, gl_ref, gup_ref, gb_ref, gate_ref, of_ref, nw_ref, x1_ref, wo_ref,
         fnw_ref, o_ref, st_ref, y_ref) = refs
    else:
        (q_ref, k_ref, v_ref, gl_ref, gup_ref, gb_ref, o_ref, st_ref) = refs
    d = 1 if reverse else 0

    @pl.when(pl.program_id(1) == 0)
    def _():
        st_ref[...] = jnp.zeros_like(st_ref)

    incl = _causal_mask(reverse)
    tri = incl.astype(bf16)
    gup = gup_ref[d]
    gbias = gb_ref[d:d + 1, :]
    nchunks = block_rows // CHUNK
    last = 0 if reverse else CHUNK - 1

    x = _dot(gl_ref[...].astype(bf16), gup) + gbias
    g = (jnp.minimum(x, 0.0) - jnp.log(1.0 + jnp.exp(-jnp.abs(x)))) / GLA_GATE_NORM
    chunk_rows = [slice(c * CHUNK, (c + 1) * CHUNK) for c in range(nchunks)]
    bcum = jnp.concatenate([_cumsum_rows(tri, g[rs]) for rs in chunk_rows], axis=0)
    b_last = [bcum[c * CHUNK + last:c * CHUNK + last + 1, :] for c in range(nchunks)]
    b_end = jnp.concatenate([jnp.broadcast_to(b, (CHUNK, GLA_KEY)) for b in b_last], axis=0)
    q = q_ref[...] * (GLA_HK ** -0.5)
    k = k_ref[...]
    q_dec = (q * jnp.exp(bcum)).astype(bf16)
    k_dec = (k * jnp.exp(-bcum)).astype(bf16)
    k_end = (k * jnp.exp(b_end - bcum)).astype(bf16)

    order = list(reversed(range(nchunks))) if reverse else list(range(nchunks))
    klanes = [slice(h * GLA_HK, (h + 1) * GLA_HK) for h in range(GLA_HEADS)]
    vlanes = [slice(h * GLA_HV, (h + 1) * GLA_HV) for h in range(GLA_HEADS)]
    units = [(c, h) for c in order for h in range(GLA_HEADS)]
    v = [v_ref[chunk_rows[c], vlanes[h]].astype(bf16) for c, h in units]
    att = [jnp.where(incl, _dot_nt(q_dec[chunk_rows[c], klanes[h]], k_dec[chunk_rows[c], klanes[h]]),
                     0.0).astype(bf16) for c, h in units]
    o_intra = [_dot(att[i], v[i]) for i in range(len(units))]
    kv = [_dot_tn(k_end[chunk_rows[c], klanes[h]], v[i]) for i, (c, h) in enumerate(units)]
    dcol = [_row_to_col(jnp.exp(b_last[c][:, klanes[h]])) for c, h in units]

    state = [st_ref[h] for h in range(GLA_HEADS)]
    for i, (c, h) in enumerate(units):
        rows = chunk_rows[c]
        o = o_intra[i] + _dot(q_dec[rows, klanes[h]], state[h].astype(bf16))
        state[h] = state[h] * dcol[i] + kv[i]
        if final:
            o = o + of_ref[rows, vlanes[h]]
            o = o * lax.rsqrt(jnp.mean(o * o, axis=-1, keepdims=True) + RMS_EPS) * nw_ref[...]
            y_ref[rows, vlanes[h]] = (o * _silu(gate_ref[rows, vlanes[h]])).astype(y_ref.dtype)
        else:
            o_ref[rows, vlanes[h]] = o
    for h in range(GLA_HEADS):
        st_ref[h] = state[h]
    if final:
        x2 = x1_ref[...] + _dot(y_ref[...], wo_ref[...])
        o_ref[...] = _rms_rows(x2, fnw_ref[...])


def _gla_scan(q, k, v, gl, gate_up_pad, gate_bias, *, reverse, gate=None, o_fwd=None, norm_w=None,
              x1=None, w_out_bf16=None, final_norm=None, block_rows=ROW_BLOCK):
    bsz, seq, _ = q.shape
    tb = block_rows
    nb = seq // tb
    final = reverse
    blk = (lambda b, j: (b, nb - 1 - j, 0)) if reverse else (lambda b, j: (b, j, 0))
    full2 = lambda b, j: (0, 0)
    full3 = lambda b, j: (0, 0, 0)
    in_specs = [
        pl.BlockSpec((None, tb, GLA_KEY), blk),
        pl.BlockSpec((None, tb, GLA_KEY), blk),
        pl.BlockSpec((None, tb, GLA_VAL), blk),
        pl.BlockSpec((None, tb, LANES), blk),
        pl.BlockSpec((2, LANES, GLA_KEY), full3),
        pl.BlockSpec((2, GLA_KEY), full2),
    ]
    args = [q, k, v, gl, gate_up_pad.astype(bf16), gate_bias]
    if final:
        in_specs += [
            pl.BlockSpec((None, tb, GLA_VAL), blk),
            pl.BlockSpec((None, tb, GLA_VAL), blk),
            pl.BlockSpec((1, GLA_HV), full2),
            pl.BlockSpec((None, tb, D_MODEL), blk),
            pl.BlockSpec((GLA_VAL, D_MODEL), full2),
            pl.BlockSpec((1, D_MODEL), full2),
        ]
        args += [gate, o_fwd, norm_w.reshape(1, GLA_HV), x1, w_out_bf16,
                 final_norm.reshape(1, D_MODEL)]
    scratch = [pltpu.VMEM((GLA_HEADS, GLA_HK, GLA_HV), f32)]
    if final:
        scratch.append(pltpu.VMEM((tb, GLA_VAL), bf16))
    return pl.pallas_call(
        functools.partial(_gla_kernel, reverse=reverse, final=final, block_rows=tb),
        grid=(bsz, nb),
        in_specs=in_specs,
        out_specs=pl.BlockSpec((None, tb, D_MODEL if final else GLA_VAL), blk),
        out_shape=jax.ShapeDtypeStruct((bsz, seq, D_MODEL if final else GLA_VAL), f32),
        scratch_shapes=scratch,
        compiler_params=_cparams("parallel", "arbitrary"),
        name="gla_bwd" if reverse else "gla_fwd",
    )(*args)


AT_NEG = -1e30
AT_QBLK = 128
AT_KBLK = 256
AT_UNROLL = 8


def _attn_kernel(q4, q16, k1, k4, k16, v1, v4, v16, g_ref, o_ref, m_s, l_s, acc_s, bias_s, tmp_s,
                 *, seq):
    lane = lax.broadcasted_iota(jnp.int32, (1, LANES), 1)
    left = lane < AT_DH
    fold_len = seq // AT_FOLD

    def fill_bias(offs, tq, nk, deltas):
        for d, delta in enumerate(deltas):
            bias_s[d, 0:tq, 0:nk] = jnp.where(jnp.abs(offs + delta) <= AT_SIDE, 0.0, AT_NEG)

    def run_blocks(nblocks, load_unit):
        def body(it, carry):
            units = [load_unit(it * AT_UNROLL + u) for u in range(AT_UNROLL)]
            heads = [(u, first) for u in range(AT_UNROLL) for first in (True, False)]
            scores = []
            for u, first in heads:
                qb, kb, _, bias, _, _ = units[u]
                qh = jnp.where(left if first else ~left, qb, jnp.zeros_like(qb))
                scores.append(_dot_nt(qh, kb) + bias)
            maxes = [jnp.max(s, axis=1, keepdims=True) for s in scores]
            probs = [jnp.exp(s - mh) for s, mh in zip(scores, maxes)]
            sums = [jnp.sum(p, axis=1, keepdims=True) for p in probs]
            outs = [_dot(p.astype(bf16), units[u][2]) for p, (u, _) in zip(probs, heads)]
            olds = [None if units[u][5] else
                    [(m_s[idx, :], l_s[idx, :], acc_s[idx, :]) for idx, _ in units[u][4]]
                    for u in range(AT_UNROLL)]
            for u in range(AT_UNROLL):
                m_new = jnp.where(left, maxes[2 * u], maxes[2 * u + 1])
                l_new = jnp.where(left, sums[2 * u], sums[2 * u + 1])
                a_new = jnp.where(left, outs[2 * u], outs[2 * u + 1])
                for pi, (idx, rows) in enumerate(units[u][4]):
                    mp, lp, ap = m_new[rows], l_new[rows], a_new[rows]
                    if units[u][5]:
                        m_s[idx, :] = mp
                        l_s[idx, :] = lp
                        acc_s[idx, :] = ap
                    else:
                        m_old, l_old, a_old = olds[u][pi]
                        m = jnp.maximum(m_old, mp)
                        w_old = jnp.exp(m_old - m)
                        w_new = jnp.exp(mp - m)
                        m_s[idx, :] = m
                        l_s[idx, :] = l_old * w_old + lp * w_new
                        acc_s[idx, :] = a_old * w_old + ap * w_new
            return carry

        lax.fori_loop(0, nblocks // AT_UNROLL, body, 0)

    def window(m0, sub, tq, nk):
        ks = jnp.clip(m0 - AT_SIDE, 0, sub - nk)
        return ks, (m0 - ks) // AT_SIDE

    tq, nk = AT_QBLK, AT_KBLK
    per = tq // AT_FOLD
    qi = lax.broadcasted_iota(jnp.int32, (tq, nk), 0)
    kj = lax.broadcasted_iota(jnp.int32, (tq, nk), 1)
    fill_bias((qi // per) + AT_FOLD * (qi % per) - kj, tq, nk, (0, AT_SIDE, 2 * AT_SIDE))

    def unit_d1(i):
        t0 = i * tq
        ks, bi = window(t0, seq, tq, nk)
        qrow = pl.ds(pl.multiple_of(i * per, per), per)
        qb = jnp.concatenate([q4[rho, qrow, :] for rho in range(AT_FOLD)], axis=0)
        krow = pl.ds(pl.multiple_of(ks, AT_SIDE), nk)
        state = [(pl.ds(pl.multiple_of(rho * fold_len + i * per, per), per),
                  slice(rho * per, (rho + 1) * per)) for rho in range(AT_FOLD)]
        return qb, k1[0, krow, :], v1[0, krow, :], bias_s[bi], state, True

    run_blocks(seq // tq, unit_d1)

    sub = fold_len
    fill_bias(qi - kj, tq, nk, (0, AT_SIDE, 2 * AT_SIDE))
    nblk = sub // tq

    def unit_fold(i):
        r = i // nblk
        m0 = (i % nblk) * tq
        ks, bi = window(m0, sub, tq, nk)
        qrow = pl.ds(pl.multiple_of(m0, tq), tq)
        krow = pl.ds(pl.multiple_of(ks, AT_SIDE), nk)
        state = [(pl.ds(pl.multiple_of(r * sub + m0, tq), tq), slice(0, tq))]
        return q4[r, qrow, :], k4[r, krow, :], v4[r, krow, :], bias_s[bi], state, False

    run_blocks(AT_FOLD * nblk, unit_fold)

    dil = AT_DILATIONS[-1]
    sub = seq // dil
    tq, nk = min(AT_QBLK, sub), min(AT_KBLK, sub)
    fill_bias((qi - kj)[0:tq, 0:nk], tq, nk, (0, AT_SIDE, nk - tq))
    nblk = sub // tq

    def unit_wide(i):
        r = i // nblk
        m0 = (i % nblk) * tq
        ks, bi = window(m0, sub, tq, nk)
        qrow = pl.ds(pl.multiple_of(m0, AT_SIDE), tq)
        krow = pl.ds(pl.multiple_of(ks, AT_SIDE), nk)
        srow = (r % AT_FOLD) * fold_len + r // AT_FOLD + (dil // AT_FOLD) * m0
        state = [(pl.ds(srow, tq, stride=dil // AT_FOLD), slice(0, tq))]
        return (q16[r, qrow, :], k16[r, krow, :], v16[r, krow, :], bias_s[bi, 0:tq, 0:nk], state,
                False)

    run_blocks(dil * nblk, unit_wide)

    rows = tmp_s.shape[0]
    per_fold = rows // AT_FOLD

    def finish(i, carry):
        for rho in range(AT_FOLD):
            idx = pl.ds(pl.multiple_of(rho * fold_len + i * per_fold, per_fold), per_fold)
            tmp_s[pl.ds(rho, per_fold, stride=AT_FOLD), :] = acc_s[idx, :] / l_s[idx, :]
        out = pl.ds(pl.multiple_of(i * rows, rows), rows)
        o_ref[out, :] = (tmp_s[...] * _silu(g_ref[out, :])).astype(o_ref.dtype)
        return carry

    lax.fori_loop(0, seq // rows, finish, 0)


def _attention(qs, ks, vs, gate):
    bsz, seq, _ = gate.shape
    pairs = AT_WIDTH // LANES
    once = pl.Buffered(1)
    col = lambda b, p: (b, 0, p)
    res = lambda b, p: (b, 0, 0, p)
    copies = list(qs) + list(ks) + list(vs)
    return pl.pallas_call(
        functools.partial(_attn_kernel, seq=seq),
        grid=(bsz, pairs),
        in_specs=[pl.BlockSpec((None, t.shape[1], t.shape[2], LANES), res) for t in copies]
        + [pl.BlockSpec((None, seq, LANES), col, pipeline_mode=once)],
        out_specs=pl.BlockSpec((None, seq, LANES), col),
        out_shape=jax.ShapeDtypeStruct((bsz, seq, AT_WIDTH), bf16),
        scratch_shapes=([pltpu.VMEM((seq, LANES), f32)] * 3
                        + [pltpu.VMEM((3, AT_QBLK, AT_KBLK), f32),
                           pltpu.VMEM((min(seq, 512), LANES), f32)]),
        compiler_params=_cparams("parallel", "parallel"),
        name="dilated_attn",
    )(*copies, gate)


RW_GROUP = 2
RW_GW = RW_GROUP * RW_DH
RW_NGROUPS = RW_HEADS // RW_GROUP
RW_WAVE = 4


def _head_stack(x, masks):
    return jnp.concatenate([jnp.where(mk, x, jnp.zeros_like(x)) for mk in masks], axis=0)


def _pair_t(x):
    xt = x.T
    return jnp.concatenate([xt[h * RW_DH:(h + 1) * RW_DH] for h in range(RW_GROUP)], axis=1)


def _unit_lower_inverse_stages(w, src, dst, masks):
    steps = int(math.log2(CHUNK)) - 1

    def start():
        n, lanes = w[src][0].shape
        eye = (lax.broadcasted_iota(jnp.int32, (n, lanes), 1) % n
               == lax.broadcasted_iota(jnp.int32, (n, lanes), 0)).astype(f32)
        w["_t"] = [eye + a for a in w[src]]
        qs = [a.astype(bf16) for a in w[src]]
        w["_q"] = [_dot(q, _head_stack(q, masks)).astype(bf16) for q in qs]

    def double():
        n = w[src][0].shape[0]
        both = [_dot(jnp.concatenate([q, t.astype(bf16)], axis=0), _head_stack(q, masks))
                for q, t in zip(w["_q"], w["_t"])]
        w["_q"] = [x[:n].astype(bf16) for x in both]
        w["_t"] = [t + x[n:] for t, x in zip(w["_t"], both)]

    def finish():
        w[dst] = [_head_stack((t + _dot(t.astype(bf16), _head_stack(q, masks))).astype(bf16), masks)
                  for q, t in zip(w["_q"], w["_t"])]

    return [start] + [double] * (steps - 1) + [finish]


def _rwkv_kernel(*refs, reverse, final, block_rows):
    sh_ref, w0_ref, wup_ref, a0_ref, aup_ref, kk_ref, ka_ref, seg_ref = refs[:8]
    if final:
        rk_ref, gnw_ref, gnb_ref, ga_ref, yf_ref, o_ref, st_ref = refs[8:]
    else:
        o_ref, st_ref = refs[8:]
    d = 1 if reverse else 0
    tb = block_rows

    @pl.when(pl.program_id(1) == 0)
    def _():
        st_ref[...] = jnp.zeros_like(st_ref)

    tri = _causal_mask(reverse).astype(bf16)
    gi = lax.broadcasted_iota(jnp.int32, (2 * CHUNK, 2 * RW_GW), 0)
    gj = lax.broadcasted_iota(jnp.int32, (2 * CHUNK, 2 * RW_GW), 1)
    ahead = (gi % CHUNK) - (gj % CHUNK)
    ahead = -ahead if reverse else ahead
    keep = ahead < jnp.where(gj < RW_GW, 0, 1)
    lane = lax.broadcasted_iota(jnp.int32, (1, 2 * RW_GW), 1)
    masks2 = [(lane // RW_DH) % RW_GROUP == h for h in range(RW_GROUP)]
    masks = [mk[:, 0:RW_GW] for mk in masks2]
    seg = seg_ref[...]
    last = 0 if reverse else CHUNK - 1
    nchunks = tb // CHUNK
    w_lo, a_lo = 3 * RW_WIDTH, 3 * RW_WIDTH + RW_LORA

    def lr_gate(a_lat, dd):
        return _sigmoid(a0_ref[dd:dd + 1, :] + _dot(a_lat.astype(bf16), aup_ref[dd]))

    r = sh_ref[:, 0:RW_WIDTH]
    k = sh_ref[:, RW_WIDTH:2 * RW_WIDTH]
    v = sh_ref[:, 2 * RW_WIDTH:3 * RW_WIDTH]
    w_lat = sh_ref[:, w_lo:w_lo + RW_LORA]
    a_lat = sh_ref[:, a_lo:a_lo + RW_LORA]
    w_log = -RW_DECAY_SCALE * _sigmoid(
        w0_ref[d:d + 1, :] + _dot(jnp.tanh(w_lat).astype(bf16), wup_ref[d]))
    lr = lr_gate(a_lat, d)
    kk = k * kk_ref[...]
    kk = kk * lax.rsqrt(jnp.maximum(_dot_seg(kk * kk, seg), 1e-24))
    k_dir = k * (1.0 + (lr - 1.0) * ka_ref[...])
    chunk_rows = [slice(c * CHUNK, (c + 1) * CHUNK) for c in range(nchunks)]
    cum = jnp.concatenate([_cumsum_rows(tri, w_log[rs]) for rs in chunk_rows], axis=0)
    grow = jnp.exp(-cum)
    a_t = -kk * jnp.exp(cum - w_log)
    b_t = kk * lr * grow
    k_t = k_dir * grow
    r_t = r * jnp.exp(cum)
    a_b, b_b, k_b, r_b = (t.astype(bf16) for t in (a_t, b_t, k_t, r_t))

    order = list(reversed(range(nchunks))) if reverse else list(range(nchunks))
    lanes = [slice(g * RW_GW, (g + 1) * RW_GW) for g in range(RW_NGROUPS)]
    state = [st_ref[g] for g in range(RW_NGROUPS)]
    y_rows = {}

    def wave_stages(chunks):
        units = [(c, g) for c in chunks for g in range(RW_NGROUPS)]
        n = range(len(units))
        w = {}

        def stacked():
            sl = lambda x, i: x[chunk_rows[units[i][0]], lanes[units[i][1]]]
            w["bk"] = [jnp.concatenate([sl(b_b, i), sl(k_b, i)], axis=0) for i in n]
            w["ar_bd"] = [jnp.concatenate([_head_stack(sl(a_b, i), masks),
                                           _head_stack(sl(r_b, i), masks)], axis=0) for i in n]
            w["b_bd"] = [_head_stack(sl(b_b, i), masks) for i in n]
            w["k_bd"] = [_head_stack(sl(k_b, i), masks) for i in n]
            w["a_t"] = [_pair_t(sl(a_t, i)).astype(bf16) for i in n]
            w["r_bd"] = [_head_stack(_pair_t(sl(r_t, i)).astype(bf16), masks) for i in n]
            w["v_t"] = [_pair_t(sl(v, i)).astype(bf16) for i in n]

        def grams():
            gram = [jnp.where(keep, _dot_nt(w["bk"][i], w["ar_bd"][i]), 0.0) for i in n]
            w["ab_t"] = [t[0:CHUNK, 0:RW_GW] for t in gram]
            w["ak_rk_bd"] = [_head_stack(t[CHUNK:, :].astype(bf16), masks2) for t in gram]
            w["rb_b_bd"] = [jnp.concatenate([_head_stack(gram[i][0:CHUNK, RW_GW:].astype(bf16), masks),
                                             w["b_bd"][i]], axis=1) for i in n]

        inverse = _unit_lower_inverse_stages(w, "ab_t", "t_inv_bd", masks)

        def value_products():
            x = [_dot(w["v_t"][i], w["ak_rk_bd"][i]) for i in n]
            w["av_t"] = [t[:, 0:RW_GW].astype(bf16) for t in x]
            w["y_local_t"] = [t[:, RW_GW:] for t in x]
            w["vk"] = [_dot(w["v_t"][i], w["k_bd"][i]) for i in n]

        def apply_inverse():
            x = [_dot(jnp.concatenate([w["av_t"][i], w["a_t"][i]], axis=0), w["t_inv_bd"][i])
                 for i in n]
            w["z_t"] = [t[0:RW_DH] for t in x]
            w["wm_r_bd"] = [jnp.concatenate([_head_stack(x[i][RW_DH:].astype(bf16), masks),
                                             w["r_bd"][i]], axis=1) for i in n]

        def recur(pos_c, c):
            def new_state():
                p_end = jnp.exp(cum[c * CHUNK + last:c * CHUNK + last + 1, :])
                w["y_t"] = []
                for g in range(RW_NGROUPS):
                    i = pos_c * RW_NGROUPS + g
                    s0 = state[g]
                    x = _dot(s0.astype(bf16), w["wm_r_bd"][i])
                    u_t = (x[:, 0:RW_GW] + w["z_t"][i]).astype(bf16)
                    x2 = _dot(u_t, w["rb_b_bd"][i])
                    state[g] = (s0 + x2[:, RW_GW:] + w["vk"][i]) * p_end[:, lanes[g]]
                    w["y_t"].append(x[:, RW_GW:] + x2[:, 0:RW_GW] + w["y_local_t"][i])

            def outputs():
                y_rows[c] = jnp.concatenate([_pair_t(t) for t in w["y_t"]], axis=1)

            return [new_state, outputs]

        independent = [stacked, grams] + inverse + [value_products, apply_inverse]
        recurrence = [f for pos_c, c in enumerate(chunks) for f in recur(pos_c, c)]
        return independent, recurrence

    waves = [order[i:i + RW_WAVE] for i in range(0, nchunks, RW_WAVE)]
    pending = []
    for chunks in waves:
        independent, recurrence = wave_stages(chunks)
        every = max(1, len(independent) // (len(pending) + 1)) if pending else 0
        for si, stage in enumerate(independent):
            stage()
            if pending and (si + 1) % every == 0:
                pending.pop(0)()
        while pending:
            pending.pop(0)()
        pending = recurrence
    while pending:
        pending.pop(0)()
    for g in range(RW_NGROUPS):
        st_ref[g] = state[g]
    y_all = jnp.concatenate([y_rows[c] for c in range(nchunks)], axis=0)

    if not final:
        o_ref[...] = y_all
    else:
        y_all = y_all + yf_ref[...]
        mean = _dot_seg(y_all, seg) * (1.0 / RW_DH)
        cen = y_all - mean
        var = _dot_seg(cen * cen, seg) * (1.0 / RW_DH)
        yn = cen * lax.rsqrt(var + RW_GN_EPS) * gnw_ref[...] + gnb_ref[...]
        k_other = k * (1.0 + (lr_gate(a_lat, 1 - d) - 1.0) * ka_ref[...])
        rk = _dot_seg(r * (k_dir + k_other) * rk_ref[...], seg)
        o_ref[...] = ((yn + rk * v) * _silu(ga_ref[...])).astype(o_ref.dtype)


def _segment_ones():
    h = jnp.arange(LANES) // RW_DH
    return (h[:, None] == h[None, :]).astype(bf16)


def _rwkv_scan(ps, w0, w_up, a0, a_up, k_k, k_a, *, reverse,
               r_k=None, gn_w=None, gn_b=None, gate=None, y_fwd=None, block_rows=2 * ROW_BLOCK):
    bsz, seq, _ = ps.shape
    tb = block_rows
    nb = seq // tb
    final = reverse
    blk = (lambda b, j: (b, nb - 1 - j, 0)) if reverse else (lambda b, j: (b, j, 0))
    full2 = lambda b, j: (0, 0)
    full3 = lambda b, j: (0, 0, 0)
    vec = lambda n: pl.BlockSpec((1, n), full2)
    in_specs = [
        pl.BlockSpec((None, tb, EVEN_SHIFT), blk),
        pl.BlockSpec((2, RW_WIDTH), full2),
        pl.BlockSpec((2, RW_LORA, RW_WIDTH), full3),
        pl.BlockSpec((2, RW_WIDTH), full2),
        pl.BlockSpec((2, RW_LORA, RW_WIDTH), full3),
        vec(RW_WIDTH), vec(RW_WIDTH),
        pl.BlockSpec((LANES, LANES), full2),
    ]
    row = lambda t: t.reshape(1, -1)
    args = [ps, w0, w_up.astype(bf16), a0, a_up.astype(bf16), row(k_k), row(k_a), _segment_ones()]
    if final:
        in_specs += [vec(RW_WIDTH), vec(RW_WIDTH), vec(RW_WIDTH),
                     pl.BlockSpec((None, tb, RW_WIDTH), blk),
                     pl.BlockSpec((None, tb, RW_WIDTH), blk)]
        args += [row(r_k), row(gn_w), row(gn_b), gate, y_fwd]
    return pl.pallas_call(
        functools.partial(_rwkv_kernel, reverse=reverse, final=final, block_rows=tb),
        grid=(bsz, nb),
        in_specs=in_specs,
        out_specs=pl.BlockSpec((None, tb, RW_WIDTH), blk),
        out_shape=jax.ShapeDtypeStruct((bsz, seq, RW_WIDTH), bf16 if final else f32),
        scratch_shapes=[pltpu.VMEM((RW_NGROUPS, RW_DH, RW_GW), f32)],
        compiler_params=_cparams("parallel", "arbitrary"),
        name="rwkv_bwd" if reverse else "rwkv_fwd",
    )(*args)


def _trunk(x, p):
    bsz, seq, _ = x.shape
    tokens = bsz * seq
    x2d = x.reshape(tokens, D_MODEL)
    seq3 = lambda t: t.reshape(bsz, seq, t.shape[-1])
    flat = lambda t: t.reshape(tokens, t.shape[-1])

    ps, ga, gb, qs, ks, vs = _even_in(x2d, seq, p["even_norm"], p["even_w_in"], p["mu_prev"],
                                      p["mu_next"])
    rw = (seq3(ps), p["w0"], p["w_up"], p["a0"], p["a_up"], p["k_k"], p["k_a"])
    y_fwd = _rwkv_scan(*rw, reverse=False)
    ya = _rwkv_scan(*rw, reverse=True, r_k=p["r_k"], gn_w=p["gn_w"], gn_b=p["gn_b"],
                    gate=seq3(ga), y_fwd=y_fwd)
    yb = _attention(qs, ks, vs, seq3(gb))

    x1, q, k, v, gate, gate_lat = _mid(x2d, flat(ya), flat(yb), p["even_w_out"], p["odd_norm"],
                                       p["odd_w_in"])
    gla = (seq3(q), seq3(k), seq3(v), seq3(gate_lat), p["gate_up"], p["gate_bias"])
    o_fwd = _gla_scan(*gla, reverse=False)
    return _gla_scan(*gla, reverse=True, gate=seq3(gate), o_fwd=o_fwd, norm_w=p["gla_norm"],
                     x1=seq3(x1), w_out_bf16=p["odd_w_out"], final_norm=p["final_norm"])


def _prepare(even_norm, even_w_in, even_mu_prev, even_mu_next, rwkv_w0, rwkv_w_up, rwkv_a0,
             rwkv_a_up, rwkv_k_k, rwkv_k_a, rwkv_r_k, rwkv_gn_w, rwkv_gn_b, even_w_out, odd_norm,
             odd_w_in, gla_gate_up, gla_gate_bias, gla_norm, odd_w_out, final_norm):
    wi = odd_w_in[0]
    lat0 = 2 * GLA_KEY + GLA_VAL
    lat = jnp.pad(wi[:, lat0:lat0 + GLA_RANK], ((0, 0), (0, LANES - GLA_RANK)))
    odd_in = jnp.concatenate([wi[:, :lat0], wi[:, lat0 + GLA_RANK:], lat], axis=1)
    return {
        "even_norm": even_norm[0], "even_w_in": even_w_in[0].astype(bf16),
        "mu_prev": even_mu_prev[0], "mu_next": even_mu_next[0],
        "w0": rwkv_w0[0], "w_up": rwkv_w_up[0], "a0": rwkv_a0[0], "a_up": rwkv_a_up[0],
        "k_k": rwkv_k_k[0], "k_a": rwkv_k_a[0], "r_k": rwkv_r_k[0],
        "gn_w": rwkv_gn_w[0], "gn_b": rwkv_gn_b[0],
        "even_w_out": even_w_out[0].astype(bf16),
        "odd_norm": odd_norm[0], "odd_w_in": odd_in.astype(bf16),
        "gate_up": jnp.pad(gla_gate_up[0], ((0, 0), (0, LANES - GLA_RANK), (0, 0))),
        "gate_bias": gla_gate_bias[0], "gla_norm": gla_norm[0],
        "odd_w_out": odd_w_out[0].astype(bf16), "final_norm": final_norm,
    }


def kernel(x_prompt, x_sample, even_norm, even_w_in, even_mu_prev, even_mu_next, rwkv_w0, rwkv_w_up,
           rwkv_a0, rwkv_a_up, rwkv_k_k, rwkv_k_a, rwkv_r_k, rwkv_gn_w, rwkv_gn_b, even_w_out,
           odd_norm, odd_w_in, gla_gate_up, gla_gate_bias, gla_norm, odd_w_out, final_norm):
    p = _prepare(even_norm, even_w_in, even_mu_prev, even_mu_next, rwkv_w0, rwkv_w_up, rwkv_a0,
                 rwkv_a_up, rwkv_k_k, rwkv_k_a, rwkv_r_k, rwkv_gn_w, rwkv_gn_b, even_w_out,
                 odd_norm, odd_w_in, gla_gate_up, gla_gate_bias, gla_norm, odd_w_out, final_norm)
    return (_trunk(x_prompt, p), _trunk(x_sample, p))
```

```python
import functools
import math

import jax
import jax.numpy as jnp
from jax import lax
from jax.experimental import pallas as pl
from jax.experimental.pallas import tpu as pltpu

f32 = jnp.float32
bf16 = jnp.bfloat16

D_MODEL = 1024
RMS_EPS = 1e-6

RW_HEADS = 8
RW_DH = 64
RW_WIDTH = RW_HEADS * RW_DH
RW_LORA = 64
RW_DECAY_SCALE = 0.6065306597126334
RW_GN_EPS = 64e-5
AT_HEADS = 8
AT_DH = 64
AT_WIDTH = AT_HEADS * AT_DH
AT_SIDE = 64
AT_DILATIONS = (1, 4, 16)
ROPE_THETA = 10000.0
EVEN_SHIFT = 3 * RW_WIDTH + 2 * RW_LORA
EVEN_COLS = EVEN_SHIFT + RW_WIDTH + 4 * AT_WIDTH

GLA_HEADS = 4
GLA_KEY = 512
GLA_VAL = 1024
GLA_HK = GLA_KEY // GLA_HEADS
GLA_HV = GLA_VAL // GLA_HEADS
GLA_RANK = 16
GLA_GATE_NORM = 16.0

CHUNK = 64
LANES = 128
SUBLANES = 8
MXU_COLS = 256
ROW_BLOCK = 512
VMEM_LIMIT = 56 * 1024 * 1024

_NT = (((1,), (1,)), ((), ()))
_TN = (((0,), (0,)), ((), ()))


def _dot(a, b):
    return jnp.dot(a, b, preferred_element_type=f32)


def _dot_nt(a, b):
    return lax.dot_general(a, b, _NT, preferred_element_type=f32)


def _dot_tn(a, b):
    return lax.dot_general(a, b, _TN, preferred_element_type=f32)


def _split2(x):
    hi = x.astype(bf16)
    lo = (x - hi.astype(f32)).astype(bf16)
    return hi, lo


def _dot_seg(x, e):
    xb = x.astype(bf16)
    return jnp.concatenate([_dot(xb[:, c:c + LANES], e) for c in range(0, x.shape[1], LANES)], axis=1)


def _cumsum_rows(tri, x):
    hi, lo = _split2(x)
    return _dot(tri, hi) + _dot(tri, lo)


def _sigmoid(x):
    return 0.5 * (jnp.tanh(0.5 * x) + 1.0)


def _silu(x):
    h = 0.5 * x
    return h + h * jnp.tanh(h)


def _rms_rows(x, w):
    return x * lax.rsqrt(jnp.mean(x * x, axis=-1, keepdims=True) + RMS_EPS) * w


def _cparams(*sem):
    return pltpu.CompilerParams(dimension_semantics=sem, vmem_limit_bytes=VMEM_LIMIT)


def _rope_partner(x):
    half = AT_DH // 2
    lane = lax.broadcasted_iota(jnp.int32, x.shape, 1)
    return jnp.where((lane & half) == 0, pltpu.roll(x, LANES - half, 1), pltpu.roll(x, half, 1))


AT_FOLD = AT_DILATIONS[1]
assert AT_DILATIONS == (1, AT_FOLD, AT_FOLD * AT_FOLD)
AT_Q_DILS = AT_DILATIONS[1:]
AT_KV_DILS = AT_DILATIONS
AT_GROUPS = 3 * AT_WIDTH // LANES


def _even_in_kernel(x_ref, xprev_ref, xnext_ref, nw_ref, w_ref, cos_ref, sin_ref, mup_ref, mun_ref,
                    *refs, blocks_per_seq):
    ps_ref, ga_ref, gb_ref = refs[:3]
    nq, nkv = len(AT_Q_DILS), len(AT_KV_DILS)
    q_refs = refs[3:3 + nq]
    k_refs = refs[3 + nq:3 + nq + nkv]
    v_refs = refs[3 + nq + nkv:3 + nq + 2 * nkv]
    stage = refs[3 + nq + 2 * nkv:]
    tmp_refs, fold_refs = stage[:AT_GROUPS], stage[AT_GROUPS:]
    tm = x_ref.shape[0]
    xn = _rms_rows(x_ref[...], nw_ref[...]).astype(bf16)
    cos = cos_ref[...]
    sin = sin_ref[...]

    pos = pl.program_id(0) % blocks_per_seq
    halo = jnp.concatenate([xprev_ref[...], xnext_ref[...]], axis=0)
    xn_halo = jnp.concatenate([xn, _rms_rows(halo, nw_ref[...]).astype(bf16)], axis=0)
    has_prev = (pos > 0).astype(f32)
    has_next = (pos < blocks_per_seq - 1).astype(f32)
    rid = lax.broadcasted_iota(jnp.int32, (tm, 1), 0)

    def plain(ref, off):
        def put(t, _):
            ref[:, off:off + LANES] = t[0:tm]
        return put

    def shifted(ref, off):
        def put(t, _):
            cs = slice(off, off + LANES)
            before = t[tm + SUBLANES - 1:tm + SUBLANES] * has_prev
            after = t[tm + SUBLANES:tm + SUBLANES + 1] * has_next
            t = t[0:tm]
            prv = jnp.where(rid == 0, before, pltpu.roll(t, 1, 0))
            nxt = jnp.where(rid == tm - 1, after, pltpu.roll(t, tm - 1, 0))
            ref[:, cs] = t + mup_ref[:, cs] * (prv - t) + mun_ref[:, cs] * (nxt - t)
        return put

    def spread(outs, off, rotary, mul=None):
        def put(t, slot):
            t = t[0:tm]
            if rotary:
                t = t * cos + _rope_partner(t) * sin
            if mul is not None:
                t = t * mul
            nat, fold = tmp_refs[slot], fold_refs[slot % len(fold_refs)]
            nat[...] = t
            per = tm // AT_FOLD
            by_dil = dict((dil, ref) for ref, dil in outs)
            if 1 in by_dil:
                by_dil[1][0, :, off:off + LANES] = t.astype(bf16)
            for r in range(AT_FOLD):
                x = nat[pl.ds(r, per, stride=AT_FOLD), :]
                by_dil[AT_FOLD][r, :, off:off + LANES] = x.astype(bf16)
                fold[r * per:(r + 1) * per, :] = x
            wide = AT_FOLD * AT_FOLD
            for r in range(wide):
                rows = pl.ds((r % AT_FOLD) * per + r // AT_FOLD, tm // wide, stride=AT_FOLD)
                by_dil[wide][r, :, off:off + LANES] = fold[rows, :].astype(bf16)
        return put

    cols = range(0, AT_WIDTH, LANES)
    dests = ([shifted(ps_ref, c) for c in range(0, EVEN_SHIFT, LANES)]
             + [plain(ga_ref, c) for c in range(0, RW_WIDTH, LANES)]
             + [spread(list(zip(q_refs, AT_Q_DILS)), c, True, AT_DH ** -0.5) for c in cols]
             + [spread(list(zip(k_refs, AT_KV_DILS)), c, True) for c in cols]
             + [spread(list(zip(v_refs, AT_KV_DILS)), c, False) for c in cols]
             + [plain(gb_ref, c) for c in cols])
    first_at = (EVEN_SHIFT + RW_WIDTH) // LANES
    tile = 2 * MXU_COLS
    for c0 in range(0, EVEN_COLS, tile):
        c1 = min(c0 + tile, EVEN_COLS)
        acc = _dot(xn_halo if c0 < EVEN_SHIFT else xn, w_ref[:, c0:c1])
        for j in range((c1 - c0) // LANES):
            g = c0 // LANES + j
            dests[g](acc[:, j * LANES:(j + 1) * LANES], (g - first_at) % AT_GROUPS)


def _rope_tables(seq):
    inv = ROPE_THETA ** (-jnp.arange(0, AT_DH, 2, dtype=f32) / AT_DH)
    ang = jnp.arange(seq, dtype=f32)[:, None] * inv[None, :]
    cos, sin = jnp.cos(ang), jnp.sin(ang)
    return (jnp.concatenate([cos, cos, cos, cos], axis=-1),
            jnp.concatenate([-sin, sin, -sin, sin], axis=-1))


def _even_in(x2d, seq, norm_w, w_in_bf16, mu_prev, mu_next, block_rows=ROW_BLOCK):
    m = x2d.shape[0]
    tm = block_rows
    per_seq = seq // tm
    bsz = m // seq
    cos, sin = _rope_tables(seq)
    row = lambda i: (i, 0)
    full = lambda i: (0, 0)
    tab = lambda i: (i % per_seq, 0)
    res = lambda i: (i // per_seq, 0, i % per_seq, 0)
    dils = AT_Q_DILS + AT_KV_DILS + AT_KV_DILS
    halo = tm // SUBLANES
    prev = lambda i: (jnp.maximum(i * halo - 1, 0), 0)
    nxt = lambda i: (jnp.minimum((i + 1) * halo, m // SUBLANES - 1), 0)
    outs = pl.pallas_call(
        functools.partial(_even_in_kernel, blocks_per_seq=per_seq),
        grid=(m // tm,),
        in_specs=[
            pl.BlockSpec((tm, D_MODEL), row),
            pl.BlockSpec((SUBLANES, D_MODEL), prev),
            pl.BlockSpec((SUBLANES, D_MODEL), nxt),
            pl.BlockSpec((1, D_MODEL), full),
            pl.BlockSpec((D_MODEL, EVEN_COLS), full),
            pl.BlockSpec((tm, LANES), tab),
            pl.BlockSpec((tm, LANES), tab),
            pl.BlockSpec((1, EVEN_SHIFT), full),
            pl.BlockSpec((1, EVEN_SHIFT), full),
        ],
        out_specs=[
            pl.BlockSpec((tm, EVEN_SHIFT), row),
            pl.BlockSpec((tm, RW_WIDTH), row),
            pl.BlockSpec((tm, AT_WIDTH), row),
        ] + [pl.BlockSpec((None, d, tm // d, AT_WIDTH), res) for d in dils],
        out_shape=[
            jax.ShapeDtypeStruct((m, EVEN_SHIFT), f32),
            jax.ShapeDtypeStruct((m, RW_WIDTH), f32),
            jax.ShapeDtypeStruct((m, AT_WIDTH), f32),
        ] + [jax.ShapeDtypeStruct((bsz, d, seq // d, AT_WIDTH), bf16) for d in dils],
        scratch_shapes=[pltpu.VMEM((tm, LANES), f32)] * (AT_GROUPS + AT_FOLD),
        compiler_params=_cparams("parallel"),
        name="even_in",
    )(x2d, x2d, x2d, norm_w.reshape(1, D_MODEL), w_in_bf16, cos, sin,
      mu_prev.reshape(1, EVEN_SHIFT), mu_next.reshape(1, EVEN_SHIFT))
    nq, nkv = len(AT_Q_DILS), len(AT_KV_DILS)
    return (outs[0], outs[1], outs[2], outs[3:3 + nq], outs[3 + nq:3 + nq + nkv],
            outs[3 + nq + nkv:])


ODD_PAD_COLS = 2 * GLA_KEY + 2 * GLA_VAL + LANES


def _mid_kernel(x_ref, ya_ref, yb_ref, wo_ref, nw_ref, wi_ref,
                x1_ref, q_ref, k_ref, v_ref, g_ref, gl_ref):
    x1 = (x_ref[...] + _dot(ya_ref[...], wo_ref[0:RW_WIDTH, :])
          + _dot(yb_ref[...], wo_ref[RW_WIDTH:RW_WIDTH + AT_WIDTH, :]))
    x1_ref[...] = x1
    xn = _rms_rows(x1, nw_ref[...]).astype(bf16)
    c = 0
    for ref, width in ((q_ref, GLA_KEY), (k_ref, GLA_KEY), (v_ref, GLA_VAL), (g_ref, GLA_VAL),
                       (gl_ref, LANES)):
        for c0 in range(0, width, 512):
            c1 = min(c0 + 512, width)
            ref[:, c0:c1] = _dot(xn, wi_ref[:, c + c0:c + c1]).astype(ref.dtype)
        c += width


def _mid(x2d, ya, yb, w_out_bf16, norm_w, w_in_pad_bf16, block_rows=2 * ROW_BLOCK):
    m = x2d.shape[0]
    tm = block_rows
    row = lambda i: (i, 0)
    full = lambda i: (0, 0)
    widths = (D_MODEL, GLA_KEY, GLA_KEY, GLA_VAL, GLA_VAL, LANES)
    return pl.pallas_call(
        _mid_kernel,
        grid=(m // tm,),
        in_specs=[
            pl.BlockSpec((tm, D_MODEL), row),
            pl.BlockSpec((tm, RW_WIDTH), row),
            pl.BlockSpec((tm, AT_WIDTH), row),
            pl.BlockSpec((RW_WIDTH + AT_WIDTH, D_MODEL), full, pipeline_mode=pl.Buffered(1)),
            pl.BlockSpec((1, D_MODEL), full),
            pl.BlockSpec((D_MODEL, ODD_PAD_COLS), full, pipeline_mode=pl.Buffered(1)),
        ],
        out_specs=[pl.BlockSpec((tm, w), row) for w in widths],
        out_shape=[jax.ShapeDtypeStruct((m, w), bf16 if i == 3 else f32)
                   for i, w in enumerate(widths)],
        compiler_params=_cparams("parallel"),
        name="mid",
    )(x2d, ya, yb, w_out_bf16, norm_w.reshape(1, D_MODEL), w_in_pad_bf16)


def _causal_mask(reverse):
    t = lax.broadcasted_iota(jnp.int32, (CHUNK, CHUNK), 0)
    s = lax.broadcasted_iota(jnp.int32, (CHUNK, CHUNK), 1)
    return s >= t if reverse else s <= t


def _row_to_col(row):
    n = row.shape[1]
    eye = (lax.broadcasted_iota(jnp.int32, (n, n), 0) == lax.broadcasted_iota(jnp.int32, (n, n), 1))
    return jnp.sum(jnp.where(eye, jnp.broadcast_to(row, (n, n)), 0.0), axis=1, keepdims=True)


def _gla_kernel(*refs, reverse, final, block_rows):
    if final:
        (q_ref, k_ref, v_ref, gl_ref, gup_ref, gb_ref, gate_ref, of_ref, nw_ref, x1_ref, wo_ref,
         fnw_ref, o_ref, st_ref, y_ref) = refs
    else:
        (q_ref, k_ref, v_ref, gl_ref, gup_ref, gb_ref, o_ref, st_ref) = refs
    d = 1 if reverse else 0

    @pl.when(pl.program_id(1) == 0)
    def _():
        st_ref[...] = jnp.zeros_like(st_ref)

    incl = _causal_mask(reverse)
    tri = incl.astype(bf16)
    gup = gup_ref[d]
    gbias = gb_ref[d:d + 1, :]
    nchunks = block_rows // CHUNK
    last = 0 if reverse else CHUNK - 1

    x = _dot(gl_ref[...].astype(bf16), gup) + gbias
    g = (jnp.minimum(x, 0.0) - jnp.log(1.0 + jnp.exp(-jnp.abs(x)))) / GLA_GATE_NORM
    chunk_rows = [slice(c * CHUNK, (c + 1) * CHUNK) for c in range(nchunks)]
    bcum = jnp.concatenate([_cumsum_rows(tri, g[rs]) for rs in chunk_rows], axis=0)
    b_last = [bcum[c * CHUNK + last:c * CHUNK + last + 1, :] for c in range(nchunks)]
    b_end = jnp.concatenate([jnp.broadcast_to(b, (CHUNK, GLA_KEY)) for b in b_last], axis=0)
    q = q_ref[...] * (GLA_HK ** -0.5)
    k = k_ref[...]
    q_dec = (q * jnp.exp(bcum)).astype(bf16)
    k_dec = (k * jnp.exp(-bcum)).astype(bf16)
    k_end = (k * jnp.exp(b_end - bcum)).astype(bf16)

    order = list(reversed(range(nchunks))) if reverse else list(range(nchunks))
    klanes = [slice(h * GLA_HK, (h + 1) * GLA_HK) for h in range(GLA_HEADS)]
    vlanes = [slice(h * GLA_HV, (h + 1) * GLA_HV) for h in range(GLA_HEADS)]
    units = [(c, h) for c in order for h in range(GLA_HEADS)]
    v = [v_ref[chunk_rows[c], vlanes[h]].astype(bf16) for c, h in units]
    att = [jnp.where(incl, _dot_nt(q_dec[chunk_rows[c], klanes[h]], k_dec[chunk_rows[c], klanes[h]]),
                     0.0).astype(bf16) for c, h in units]
    o_intra = [_dot(att[i], v[i]) for i in range(len(units))]
    kv = [_dot_tn(k_end[chunk_rows[c], klanes[h]], v[i]) for i, (c, h) in enumerate(units)]
    dcol = [_row_to_col(jnp.exp(b_last[c][:, klanes[h]])) for c, h in units]

    state = [st_ref[h] for h in range(GLA_HEADS)]
    for i, (c, h) in enumerate(units):
        rows = chunk_rows[c]
        o = o_intra[i] + _dot(q_dec[rows, klanes[h]], state[h].astype(bf16))
        state[h] = state[h] * dcol[i] + kv[i]
        if final:
            o = o + of_ref[rows, vlanes[h]]
            o = o * lax.rsqrt(jnp.mean(o * o, axis=-1, keepdims=True) + RMS_EPS) * nw_ref[...]
            y_ref[rows, vlanes[h]] = (o * _silu(gate_ref[rows, vlanes[h]])).astype(y_ref.dtype)
        else:
            o_ref[rows, vlanes[h]] = o
    for h in range(GLA_HEADS):
        st_ref[h] = state[h]
    if final:
        x2 = x1_ref[...] + _dot(y_ref[...], wo_ref[...])
        o_ref[...] = _rms_rows(x2, fnw_ref[...])


def _gla_scan(q, k, v, gl, gate_up_pad, gate_bias, *, reverse, gate=None, o_fwd=None, norm_w=None,
              x1=None, w_out_bf16=None, final_norm=None, block_rows=ROW_BLOCK):
    bsz, seq, _ = q.shape
    tb = block_rows
    nb = seq // tb
    final = reverse
    blk = (lambda b, j: (b, nb - 1 - j, 0)) if reverse else (lambda b, j: (b, j, 0))
    full2 = lambda b, j: (0, 0)
    full3 = lambda b, j: (0, 0, 0)
    in_specs = [
        pl.BlockSpec((None, tb, GLA_KEY), blk),
        pl.BlockSpec((None, tb, GLA_KEY), blk),
        pl.BlockSpec((None, tb, GLA_VAL), blk),
        pl.BlockSpec((None, tb, LANES), blk),
        pl.BlockSpec((2, LANES, GLA_KEY), full3),
        pl.BlockSpec((2, GLA_KEY), full2),
    ]
    args = [q, k, v, gl, gate_up_pad.astype(bf16), gate_bias]
    if final:
        in_specs += [
            pl.BlockSpec((None, tb, GLA_VAL), blk),
            pl.BlockSpec((None, tb, GLA_VAL), blk),
            pl.BlockSpec((1, GLA_HV), full2),
            pl.BlockSpec((None, tb, D_MODEL), blk),
            pl.BlockSpec((GLA_VAL, D_MODEL), full2),
            pl.BlockSpec((1, D_MODEL), full2),
        ]
        args += [gate, o_fwd, norm_w.reshape(1, GLA_HV), x1, w_out_bf16,
                 final_norm.reshape(1, D_MODEL)]
    scratch = [pltpu.VMEM((GLA_HEADS, GLA_HK, GLA_HV), f32)]
    if final:
        scratch.append(pltpu.VMEM((tb, GLA_VAL), bf16))
    return pl.pallas_call(
        functools.partial(_gla_kernel, reverse=reverse, final=final, block_rows=tb),
        grid=(bsz, nb),
        in_specs=in_specs,
        out_specs=pl.BlockSpec((None, tb, D_MODEL if final else GLA_VAL), blk),
        out_shape=jax.ShapeDtypeStruct((bsz, seq, D_MODEL if final else GLA_VAL), f32),
        scratch_shapes=scratch,
        compiler_params=_cparams("parallel", "arbitrary"),
        name="gla_bwd" if reverse else "gla_fwd",
    )(*args)


AT_NEG = -1e30
AT_QBLK = 128
AT_KBLK = 256
AT_UNROLL = 8


def _attn_kernel(q4, q16, k1, k4, k16, v1, v4, v16, g_ref, o_ref, m_s, l_s, acc_s, bias_s, tmp_s,
                 *, seq):
    lane = lax.broadcasted_iota(jnp.int32, (1, LANES), 1)
    left = lane < AT_DH
    fold_len = seq // AT_FOLD

    def fill_bias(offs, tq, nk, deltas):
        for d, delta in enumerate(deltas):
            bias_s[d, 0:tq, 0:nk] = jnp.where(jnp.abs(offs + delta) <= AT_SIDE, 0.0, AT_NEG)

    def run_blocks(nblocks, load_unit):
        def body(it, carry):
            units = [load_unit(it * AT_UNROLL + u) for u in range(AT_UNROLL)]
            heads = [(u, first) for u in range(AT_UNROLL) for first in (True, False)]
            scores = []
            for u, first in heads:
                qb, kb, _, bias, _, _ = units[u]
                qh = jnp.where(left if first else ~left, qb, jnp.zeros_like(qb))
                scores.append(_dot_nt(qh, kb) + bias)
            maxes = [jnp.max(s, axis=1, keepdims=True) for s in scores]
            probs = [jnp.exp(s - mh) for s, mh in zip(scores, maxes)]
            sums = [jnp.sum(p, axis=1, keepdims=True) for p in probs]
            outs = [_dot(p.astype(bf16), units[u][2]) for p, (u, _) in zip(probs, heads)]
            olds = [None if units[u][5] else
                    [(m_s[idx, :], l_s[idx, :], acc_s[idx, :]) for idx, _ in units[u][4]]
                    for u in range(AT_UNROLL)]
            for u in range(AT_UNROLL):
                m_new = jnp.where(left, maxes[2 * u], maxes[2 * u + 1])
                l_new = jnp.where(left, sums[2 * u], sums[2 * u + 1])
                a_new = jnp.where(left, outs[2 * u], outs[2 * u + 1])
                for pi, (idx, rows) in enumerate(units[u][4]):
                    mp, lp, ap = m_new[rows], l_new[rows], a_new[rows]
                    if units[u][5]:
                        m_s[idx, :] = mp
                        l_s[idx, :] = lp
                        acc_s[idx, :] = ap
                    else:
                        m_old, l_old, a_old = olds[u][pi]
                        m = jnp.maximum(m_old, mp)
                        w_old = jnp.exp(m_old - m)
                        w_new = jnp.exp(mp - m)
                        m_s[idx, :] = m
                        l_s[idx, :] = l_old * w_old + lp * w_new
                        acc_s[idx, :] = a_old * w_old + ap * w_new
            return carry

        lax.fori_loop(0, nblocks // AT_UNROLL, body, 0)

    def window(m0, sub, tq, nk):
        ks = jnp.clip(m0 - AT_SIDE, 0, sub - nk)
        return ks, (m0 - ks) // AT_SIDE

    tq, nk = AT_QBLK, AT_KBLK
    per = tq // AT_FOLD
    qi = lax.broadcasted_iota(jnp.int32, (tq, nk), 0)
    kj = lax.broadcasted_iota(jnp.int32, (tq, nk), 1)
    fill_bias((qi // per) + AT_FOLD * (qi % per) - kj, tq, nk, (0, AT_SIDE, 2 * AT_SIDE))

    def unit_d1(i):
        t0 = i * tq
        ks, bi = window(t0, seq, tq, nk)
        qrow = pl.ds(pl.multiple_of(i * per, per), per)
        qb = jnp.concatenate([q4[rho, qrow, :] for rho in range(AT_FOLD)], axis=0)
        krow = pl.ds(pl.multiple_of(ks, AT_SIDE), nk)
        state = [(pl.ds(pl.multiple_of(rho * fold_len + i * per, per), per),
                  slice(rho * per, (rho + 1) * per)) for rho in range(AT_FOLD)]
        return qb, k1[0, krow, :], v1[0, krow, :], bias_s[bi], state, True

    run_blocks(seq // tq, unit_d1)

    sub = fold_len
    fill_bias(qi - kj, tq, nk, (0, AT_SIDE, 2 * AT_SIDE))
    nblk = sub // tq

    def unit_fold(i):
        r = i // nblk
        m0 = (i % nblk) * tq
        ks, bi = window(m0, sub, tq, nk)
        qrow = pl.ds(pl.multiple_of(m0, tq), tq)
        krow = pl.ds(pl.multiple_of(ks, AT_SIDE), nk)
        state = [(pl.ds(pl.multiple_of(r * sub + m0, tq), tq), slice(0, tq))]
        return q4[r, qrow, :], k4[r, krow, :], v4[r, krow, :], bias_s[bi], state, False

    run_blocks(AT_FOLD * nblk, unit_fold)

    dil = AT_DILATIONS[-1]
    sub = seq // dil
    tq, nk = min(AT_QBLK, sub), min(AT_KBLK, sub)
    fill_bias((qi - kj)[0:tq, 0:nk], tq, nk, (0, AT_SIDE, nk - tq))
    nblk = sub // tq

    def unit_wide(i):
        r = i // nblk
        m0 = (i % nblk) * tq
        ks, bi = window(m0, sub, tq, nk)
        qrow = pl.ds(pl.multiple_of(m0, AT_SIDE), tq)
        krow = pl.ds(pl.multiple_of(ks, AT_SIDE), nk)
        srow = (r % AT_FOLD) * fold_len + r // AT_FOLD + (dil // AT_FOLD) * m0
        state = [(pl.ds(srow, tq, stride=dil // AT_FOLD), slice(0, tq))]
        return (q16[r, qrow, :], k16[r, krow, :], v16[r, krow, :], bias_s[bi, 0:tq, 0:nk], state,
                False)

    run_blocks(dil * nblk, unit_wide)

    rows = tmp_s.shape[0]
    per_fold = rows // AT_FOLD

    def finish(i, carry):
        for rho in range(AT_FOLD):
            idx = pl.ds(pl.multiple_of(rho * fold_len + i * per_fold, per_fold), per_fold)
            tmp_s[pl.ds(rho, per_fold, stride=AT_FOLD), :] = acc_s[idx, :] / l_s[idx, :]
        out = pl.ds(pl.multiple_of(i * rows, rows), rows)
        o_ref[out, :] = (tmp_s[...] * _silu(g_ref[out, :])).astype(o_ref.dtype)
        return carry

    lax.fori_loop(0, seq // rows, finish, 0)


def _attention(qs, ks, vs, gate):
    bsz, seq, _ = gate.shape
    pairs = AT_WIDTH // LANES
    once = pl.Buffered(1)
    col = lambda b, p: (b, 0, p)
    res = lambda b, p: (b, 0, 0, p)
    copies = list(qs) + list(ks) + list(vs)
    return pl.pallas_call(
        functools.partial(_attn_kernel, seq=seq),
        grid=(bsz, pairs),
        in_specs=[pl.BlockSpec((None, t.shape[1], t.shape[2], LANES), res) for t in copies]
        + [pl.BlockSpec((None, seq, LANES), col, pipeline_mode=once)],
        out_specs=pl.BlockSpec((None, seq, LANES), col),
        out_shape=jax.ShapeDtypeStruct((bsz, seq, AT_WIDTH), bf16),
        scratch_shapes=([pltpu.VMEM((seq, LANES), f32)] * 3
                        + [pltpu.VMEM((3, AT_QBLK, AT_KBLK), f32),
                           pltpu.VMEM((min(seq, 512), LANES), f32)]),
        compiler_params=_cparams("parallel", "parallel"),
        name="dilated_attn",
    )(*copies, gate)


RW_GROUP = 2
RW_GW = RW_GROUP * RW_DH
RW_NGROUPS = RW_HEADS // RW_GROUP
RW_WAVE = 4


def _head_stack(x, masks):
    return jnp.concatenate([jnp.where(mk, x, jnp.zeros_like(x)) for mk in masks], axis=0)


def _pair_t(x):
    xt = x.T
    return jnp.concatenate([xt[h * RW_DH:(h + 1) * RW_DH] for h in range(RW_GROUP)], axis=1)


def _unit_lower_inverse_stages(w, src, dst, masks):
    steps = int(math.log2(CHUNK)) - 1

    def start():
        n, lanes = w[src][0].shape
        eye = (lax.broadcasted_iota(jnp.int32, (n, lanes), 1) % n
               == lax.broadcasted_iota(jnp.int32, (n, lanes), 0)).astype(f32)
        w["_t"] = [eye + a for a in w[src]]
        qs = [a.astype(bf16) for a in w[src]]
        w["_q"] = [_dot(q, _head_stack(q, masks)).astype(bf16) for q in qs]

    def double():
        n = w[src][0].shape[0]
        both = [_dot(jnp.concatenate([q, t.astype(bf16)], axis=0), _head_stack(q, masks))
                for q, t in zip(w["_q"], w["_t"])]
        w["_q"] = [x[:n].astype(bf16) for x in both]
        w["_t"] = [t + x[n:] for t, x in zip(w["_t"], both)]

    def finish():
        w[dst] = [_head_stack((t + _dot(t.astype(bf16), _head_stack(q, masks))).astype(bf16), masks)
                  for q, t in zip(w["_q"], w["_t"])]

    return [start] + [double] * (steps - 1) + [finish]


def _rwkv_kernel(*refs, reverse, final, block_rows):
    sh_ref, w0_ref, wup_ref, a0_ref, aup_ref, kk_ref, ka_ref, seg_ref = refs[:8]
    if final:
        rk_ref, gnw_ref, gnb_ref, ga_ref, yf_ref, o_ref, st_ref = refs[8:]
    else:
        o_ref, st_ref = refs[8:]
    d = 1 if reverse else 0
    tb = block_rows

    @pl.when(pl.program_id(1) == 0)
    def _():
        st_ref[...] = jnp.zeros_like(st_ref)

    tri = _causal_mask(reverse).astype(bf16)
    gi = lax.broadcasted_iota(jnp.int32, (2 * CHUNK, 2 * RW_GW), 0)
    gj = lax.broadcasted_iota(jnp.int32, (2 * CHUNK, 2 * RW_GW), 1)
    ahead = (gi % CHUNK) - (gj % CHUNK)
    ahead = -ahead if reverse else ahead
    keep = ahead < jnp.where(gj < RW_GW, 0, 1)
    lane = lax.broadcasted_iota(jnp.int32, (1, 2 * RW_GW), 1)
    masks2 = [(lane // RW_DH) % RW_GROUP == h for h in range(RW_GROUP)]
    masks = [mk[:, 0:RW_GW] for mk in masks2]
    seg = seg_ref[...]
    last = 0 if reverse else CHUNK - 1
    nchunks = tb // CHUNK
    w_lo, a_lo = 3 * RW_WIDTH, 3 * RW_WIDTH + RW_LORA

    def lr_gate(a_lat, dd):
        return _sigmoid(a0_ref[dd:dd + 1, :] + _dot(a_lat.astype(bf16), aup_ref[dd]))

    r = sh_ref[:, 0:RW_WIDTH]
    k = sh_ref[:, RW_WIDTH:2 * RW_WIDTH]
    v = sh_ref[:, 2 * RW_WIDTH:3 * RW_WIDTH]
    w_lat = sh_ref[:, w_lo:w_lo + RW_LORA]
    a_lat = sh_ref[:, a_lo:a_lo + RW_LORA]
    w_log = -RW_DECAY_SCALE * _sigmoid(
        w0_ref[d:d + 1, :] + _dot(jnp.tanh(w_lat).astype(bf16), wup_ref[d]))
    lr = lr_gate(a_lat, d)
    kk = k * kk_ref[...]
    kk = kk * lax.rsqrt(jnp.maximum(_dot_seg(kk * kk, seg), 1e-24))
    k_dir = k * (1.0 + (lr - 1.0) * ka_ref[...])
    chunk_rows = [slice(c * CHUNK, (c + 1) * CHUNK) for c in range(nchunks)]
    cum = jnp.concatenate([_cumsum_rows(tri, w_log[rs]) for rs in chunk_rows], axis=0)
    grow = jnp.exp(-cum)
    a_t = -kk * jnp.exp(cum - w_log)
    b_t = kk * lr * grow
    k_t = k_dir * grow
    r_t = r * jnp.exp(cum)
    a_b, b_b, k_b, r_b = (t.astype(bf16) for t in (a_t, b_t, k_t, r_t))

    order = list(reversed(range(nchunks))) if reverse else list(range(nchunks))
    lanes = [slice(g * RW_GW, (g + 1) * RW_GW) for g in range(RW_NGROUPS)]
    state = [st_ref[g] for g in range(RW_NGROUPS)]
    y_rows = {}

    def wave_stages(chunks):
        units = [(c, g) for c in chunks for g in range(RW_NGROUPS)]
        n = range(len(units))
        w = {}

        def stacked():
            sl = lambda x, i: x[chunk_rows[units[i][0]], lanes[units[i][1]]]
            w["bk"] = [jnp.concatenate([sl(b_b, i), sl(k_b, i)], axis=0) for i in n]
            w["ar_bd"] = [jnp.concatenate([_head_stack(sl(a_b, i), masks),
                                           _head_stack(sl(r_b, i), masks)], axis=0) for i in n]
            w["b_bd"] = [_head_stack(sl(b_b, i), masks) for i in n]
            w["k_bd"] = [_head_stack(sl(k_b, i), masks) for i in n]
            w["a_t"] = [_pair_t(sl(a_t, i)).astype(bf16) for i in n]
            w["r_bd"] = [_head_stack(_pair_t(sl(r_t, i)).astype(bf16), masks) for i in n]
            w["v_t"] = [_pair_t(sl(v, i)).astype(bf16) for i in n]

        def grams():
            gram = [jnp.where(keep, _dot_nt(w["bk"][i], w["ar_bd"][i]), 0.0) for i in n]
            w["ab_t"] = [t[0:CHUNK, 0:RW_GW] for t in gram]
            w["ak_rk_bd"] = [_head_stack(t[CHUNK:, :].astype(bf16), masks2) for t in gram]
            w["rb_b_bd"] = [jnp.concatenate([_head_stack(gram[i][0:CHUNK, RW_GW:].astype(bf16), masks),
                                             w["b_bd"][i]], axis=1) for i in n]

        inverse = _unit_lower_inverse_stages(w, "ab_t", "t_inv_bd", masks)

        def value_products():
            x = [_dot(w["v_t"][i], w["ak_rk_bd"][i]) for i in n]
            w["av_t"] = [t[:, 0:RW_GW].astype(bf16) for t in x]
            w["y_local_t"] = [t[:, RW_GW:] for t in x]
            w["vk"] = [_dot(w["v_t"][i], w["k_bd"][i]) for i in n]

        def apply_inverse():
            x = [_dot(jnp.concatenate([w["av_t"][i], w["a_t"][i]], axis=0), w["t_inv_bd"][i])
                 for i in n]
            w["z_t"] = [t[0:RW_DH] for t in x]
            w["wm_r_bd"] = [jnp.concatenate([_head_stack(x[i][RW_DH:].astype(bf16), masks),
                                             w["r_bd"][i]], axis=1) for i in n]

        def recur(pos_c, c):
            def new_state():
                p_end = jnp.exp(cum[c * CHUNK + last:c * CHUNK + last + 1, :])
                w["y_t"] = []
                for g in range(RW_NGROUPS):
                    i = pos_c * RW_NGROUPS + g
                    s0 = state[g]
                    x = _dot(s0.astype(bf16), w["wm_r_bd"][i])
                    u_t = (x[:, 0:RW_GW] + w["z_t"][i]).astype(bf16)
                    x2 = _dot(u_t, w["rb_b_bd"][i])
                    state[g] = (s0 + x2[:, RW_GW:] + w["vk"][i]) * p_end[:, lanes[g]]
                    w["y_t"].append(x[:, RW_GW:] + x2[:, 0:RW_GW] + w["y_local_t"][i])

            def outputs():
                y_rows[c] = jnp.concatenate([_pair_t(t) for t in w["y_t"]], axis=1)

            return [new_state, outputs]

        independent = [stacked, grams] + inverse + [value_products, apply_inverse]
        recurrence = [f for pos_c, c in enumerate(chunks) for f in recur(pos_c, c)]
        return independent, recurrence

    waves = [order[i:i + RW_WAVE] for i in range(0, nchunks, RW_WAVE)]
    pending = []
    for chunks in waves:
        independent, recurrence = wave_stages(chunks)
        every = max(1, len(independent) // (len(pending) + 1)) if pending else 0
        for si, stage in enumerate(independent):
            stage()
            if pending and (si + 1) % every == 0:
                pending.pop(0)()
        while pending:
            pending.pop(0)()
        pending = recurrence
    while pending:
        pending.pop(0)()
    for g in range(RW_NGROUPS):
        st_ref[g] = state[g]
    y_all = jnp.concatenate([y_rows[c] for c in range(nchunks)], axis=0)

    if not final:
        o_ref[...] = y_all
    else:
        y_all = y_all + yf_ref[...]
        mean = _dot_seg(y_all, seg) * (1.0 / RW_DH)
        cen = y_all - mean
        var = _dot_seg(cen * cen, seg) * (1.0 / RW_DH)
        yn = cen * lax.rsqrt(var + RW_GN_EPS) * gnw_ref[...] + gnb_ref[...]
        k_other = k * (1.0 + (lr_gate(a_lat, 1 - d) - 1.0) * ka_ref[...])
        rk = _dot_seg(r * (k_dir + k_other) * rk_ref[...], seg)
        o_ref[...] = ((yn + rk * v) * _silu(ga_ref[...])).astype(o_ref.dtype)


def _segment_ones():
    h = jnp.arange(LANES) // RW_DH
    return (h[:, None] == h[None, :]).astype(bf16)


def _rwkv_scan(ps, w0, w_up, a0, a_up, k_k, k_a, *, reverse,
               r_k=None, gn_w=None, gn_b=None, gate=None, y_fwd=None, block_rows=2 * ROW_BLOCK):
    bsz, seq, _ = ps.shape
    tb = block_rows
    nb = seq // tb
    final = reverse
    blk = (lambda b, j: (b, nb - 1 - j, 0)) if reverse else (lambda b, j: (b, j, 0))
    full2 = lambda b, j: (0, 0)
    full3 = lambda b, j: (0, 0, 0)
    vec = lambda n: pl.BlockSpec((1, n), full2)
    in_specs = [
        pl.BlockSpec((None, tb, EVEN_SHIFT), blk),
        pl.BlockSpec((2, RW_WIDTH), full2),
        pl.BlockSpec((2, RW_LORA, RW_WIDTH), full3),
        pl.BlockSpec((2, RW_WIDTH), full2),
        pl.BlockSpec((2, RW_LORA, RW_WIDTH), full3),
        vec(RW_WIDTH), vec(RW_WIDTH),
        pl.BlockSpec((LANES, LANES), full2),
    ]
    row = lambda t: t.reshape(1, -1)
    args = [ps, w0, w_up.astype(bf16), a0, a_up.astype(bf16), row(k_k), row(k_a), _segment_ones()]
    if final:
        in_specs += [vec(RW_WIDTH), vec(RW_WIDTH), vec(RW_WIDTH),
                     pl.BlockSpec((None, tb, RW_WIDTH), blk),
                     pl.BlockSpec((None, tb, RW_WIDTH), blk)]
        args += [row(r_k), row(gn_w), row(gn_b), gate, y_fwd]
    return pl.pallas_call(
        functools.partial(_rwkv_kernel, reverse=reverse, final=final, block_rows=tb),
        grid=(bsz, nb),
        in_specs=in_specs,
        out_specs=pl.BlockSpec((None, tb, RW_WIDTH), blk),
        out_shape=jax.ShapeDtypeStruct((bsz, seq, RW_WIDTH), bf16 if final else f32),
        scratch_shapes=[pltpu.VMEM((RW_NGROUPS, RW_DH, RW_GW), f32)],
        compiler_params=_cparams("parallel", "arbitrary"),
        name="rwkv_bwd" if reverse else "rwkv_fwd",
    )(*args)


def _trunk(x, p):
    bsz, seq, _ = x.shape
    tokens = bsz * seq
    x2d = x.reshape(tokens, D_MODEL)
    seq3 = lambda t: t.reshape(bsz, seq, t.shape[-1])
    flat = lambda t: t.reshape(tokens, t.shape[-1])

    ps, ga, gb, qs, ks, vs = _even_in(x2d, seq, p["even_norm"], p["even_w_in"], p["mu_prev"],
                                      p["mu_next"])
    rw = (seq3(ps), p["w0"], p["w_up"], p["a0"], p["a_up"], p["k_k"], p["k_a"])
    y_fwd = _rwkv_scan(*rw, reverse=False)
    ya = _rwkv_scan(*rw, reverse=True, r_k=p["r_k"], gn_w=p["gn_w"], gn_b=p["gn_b"],
                    gate=seq3(ga), y_fwd=y_fwd)
    yb = _attention(qs, ks, vs, seq3(gb))

    x1, q, k, v, gate, gate_lat = _mid(x2d, flat(ya), flat(yb), p["even_w_out"], p["odd_norm"],
                                       p["odd_w_in"])
    gla = (seq3(q), seq3(k), seq3(v), seq3(gate_lat), p["gate_up"], p["gate_bias"])
    o_fwd = _gla_scan(*gla, reverse=False)
    return _gla_scan(*gla, reverse=True, gate=seq3(gate), o_fwd=o_fwd, norm_w=p["gla_norm"],
                     x1=seq3(x1), w_out_bf16=p["odd_w_out"], final_norm=p["final_norm"])


def _prepare(even_norm, even_w_in, even_mu_prev, even_mu_next, rwkv_w0, rwkv_w_up, rwkv_a0,
             rwkv_a_up, rwkv_k_k, rwkv_k_a, rwkv_r_k, rwkv_gn_w, rwkv_gn_b, even_w_out, odd_norm,
             odd_w_in, gla_gate_up, gla_gate_bias, gla_norm, odd_w_out, final_norm):
    wi = odd_w_in[0]
    lat0 = 2 * GLA_KEY + GLA_VAL
    lat = jnp.pad(wi[:, lat0:lat0 + GLA_RANK], ((0, 0), (0, LANES - GLA_RANK)))
    odd_in = jnp.concatenate([wi[:, :lat0], wi[:, lat0 + GLA_RANK:], lat], axis=1)
    return {
        "even_norm": even_norm[0], "even_w_in": even_w_in[0].astype(bf16),
        "mu_prev": even_mu_prev[0], "mu_next": even_mu_next[0],
        "w0": rwkv_w0[0], "w_up": rwkv_w_up[0], "a0": rwkv_a0[0], "a_up": rwkv_a_up[0],
        "k_k": rwkv_k_k[0], "k_a": rwkv_k_a[0], "r_k": rwkv_r_k[0],
        "gn_w": rwkv_gn_w[0], "gn_b": rwkv_gn_b[0],
        "even_w_out": even_w_out[0].astype(bf16),
        "odd_norm": odd_norm[0], "odd_w_in": odd_in.astype(bf16),
        "gate_up": jnp.pad(gla_gate_up[0], ((0, 0), (0, LANES - GLA_RANK), (0, 0))),
        "gate_bias": gla_gate_bias[0], "gla_norm": gla_norm[0],
        "odd_w_out": odd_w_out[0].astype(bf16), "final_norm": final_norm,
    }


def kernel(x_prompt, x_sample, even_norm, even_w_in, even_mu_prev, even_mu_next, rwkv_w0, rwkv_w_up,
           rwkv_a0, rwkv_a_up, rwkv_k_k, rwkv_k_a, rwkv_r_k, rwkv_gn_w, rwkv_gn_b, even_w_out,
           odd_norm, odd_w_in, gla_gate_up, gla_gate_bias, gla_norm, odd_w_out, final_norm):
    p = _prepare(even_norm, even_w_in, even_mu_prev, even_mu_next, rwkv_w0, rwkv_w_up, rwkv_a0,
                 rwkv_a_up, rwkv_k_k, rwkv_k_a, rwkv_r_k, rwkv_gn_w, rwkv_gn_b, even_w_out,
                 odd_norm, odd_w_in, gla_gate_up, gla_gate_bias, gla_norm, odd_w_out, final_norm)
    return (_trunk(x_prompt, p), _trunk(x_sample, p))
```
